```python
import math
import jax, jax.numpy as jnp
from jax import lax
import numpy as np

D_MODEL = 1024
BATCH = 16
SEQ = 2048
DEPTH = 1

EPS = 1e-6
D_RNN = 512
RNN_HEADS = 8
RNN_HEAD_DIM = D_RNN // RNN_HEADS
CONV_WIDTH = 4
LRU_C = 8.0
LRU_A_MIN = 0.9
LRU_A_MAX = 0.999
N_Q_HEADS = 8
N_KV_HEADS = 2
HEAD_DIM = 64
Q_PER_KV = N_Q_HEADS // N_KV_HEADS
D_ATTN = N_Q_HEADS * HEAD_DIM
D_KV = N_KV_HEADS * HEAD_DIM
CMP_BLOCK = 32
CMP_STRIDE = 16
CMP_HIDDEN = 256
SEL_BLOCK = 64
SEL_TOPK = 8
SEL_Q_BLOCK = 64
SEL_FORCE = 1e9
WINDOW = 512
WIN_Q_BLOCK = 128
N_BRANCH = 3
D_MIX = D_RNN + D_ATTN
SPLIT_SIZES = (D_RNN, D_RNN, D_ATTN, D_KV, D_KV, D_KV, D_KV, D_KV, D_KV, D_ATTN, N_BRANCH * N_Q_HEADS)
D_IN = 2 * D_RNN + 2 * D_ATTN + 6 * D_KV + N_BRANCH * N_Q_HEADS

kernel_name = "hybrid_rglru_nsa_parallel_heads"


def rms_norm(x, w):
    xf = x.astype(jnp.float32)
    y = xf * lax.rsqrt(jnp.mean(xf * xf, axis=-1, keepdims=True) + EPS)
    return (y * w.astype(jnp.float32)).astype(x.dtype)


def masked_softmax(s, mask):
    s = jnp.where(mask, s, -jnp.inf)
    m = jnp.max(s, axis=-1, keepdims=True)
    m = jnp.where(jnp.isfinite(m), m, 0.0)
    e = jnp.where(mask, jnp.exp(s - m), 0.0)
    den = jnp.sum(e, axis=-1, keepdims=True)
    return e / jnp.where(den > 0, den, 1.0)


def causal_depthwise_conv(x, w, b):
    S = x.shape[1]
    xp = jnp.pad(x, ((0, 0), (CONV_WIDTH - 1, 0), (0, 0)))
    out = b
    for k in range(CONV_WIDTH):
        out = out + xp[:, k:k + S] * w[k]
    return out


def rg_lru(x, wa, ba, wx, bx, lam):
    B, S, _ = x.shape
    xh = x.reshape(B, S, RNN_HEADS, RNN_HEAD_DIM)
    r = jax.nn.sigmoid(jnp.einsum('bshi,hij->bshj', xh, wa).reshape(B, S, D_RNN) + ba)
    i = jax.nn.sigmoid(jnp.einsum('bshi,hij->bshj', xh, wx).reshape(B, S, D_RNN) + bx)
    log_a = LRU_C * r.astype(jnp.float32) * jax.nn.log_sigmoid(lam.astype(jnp.float32))
    a = jnp.exp(log_a)
    u = jnp.sqrt(-jnp.expm1(2.0 * log_a)) * (i * x).astype(jnp.float32)

    def combine(c1, c2):
        a1, b1 = c1
        a2, b2 = c2
        return a1 * a2, a2 * b1 + b2

    _, h = lax.associative_scan(combine, (a, u), axis=1)
    return h.astype(x.dtype)


def compress_blocks(k, pe, w1, w2):
    B, S, G, dh = k.shape
    n_cmp = (S - CMP_BLOCK) // CMP_STRIDE + 1
    idx = jnp.arange(n_cmp)[:, None] * CMP_STRIDE + jnp.arange(CMP_BLOCK)[None, :]
    blocks = k[:, idx] + pe[None, None, :, None, :]
    flat = blocks.transpose(0, 1, 3, 2, 4).reshape(B, n_cmp, G, CMP_BLOCK * dh)
    return jax.nn.silu(flat @ w1) @ w2


def cmp_sel_overlap(n_cmp, n_sel):
    cs = jnp.arange(n_cmp)[:, None] * CMP_STRIDE
    ss = jnp.arange(n_sel)[None, :] * SEL_BLOCK
    ov = jnp.clip(jnp.minimum(cs + CMP_BLOCK, ss + SEL_BLOCK) - jnp.maximum(cs, ss), 0, None)
    return ov.astype(jnp.float32) / CMP_BLOCK


def to_chunks(a, size):
    B, S = a.shape[:2]
    return jnp.moveaxis(a.reshape((B, S // size, size) + a.shape[2:]), 1, 0)


def selected_attention(q, k, v, idx, valid):
    B, S, G, R, dh = q.shape
    n = idx.shape[-1]
    scale = dh ** -0.5
    kb = k.reshape(B, S // SEL_BLOCK, SEL_BLOCK, G, dh).transpose(0, 3, 1, 2, 4)
    vb = v.reshape(B, S // SEL_BLOCK, SEL_BLOCK, G, dh).transpose(0, 3, 1, 2, 4)
    gather = jax.vmap(jax.vmap(lambda blocks, ids: blocks[ids]))

    def chunk(args):
        q_c, idx_c, valid_c, t_c = args
        Tq = q_c.shape[1]
        ids = idx_c.transpose(0, 2, 1, 3)
        kg = gather(kb, ids)
        vg = gather(vb, ids)
        pos = idx_c[..., None] * SEL_BLOCK + jnp.arange(SEL_BLOCK)
        mask = valid_c[..., None] & (pos <= t_c[None, :, None, None, None])
        s = jnp.einsum('btgrd,bgtnld->btgrnl', q_c, kg).astype(jnp.float32) * scale
        s = s.reshape(B, Tq, G, R, n * SEL_BLOCK)
        p = masked_softmax(s, mask.reshape(B, Tq, G, 1, n * SEL_BLOCK)).reshape(B, Tq, G, R, n, SEL_BLOCK)
        return jnp.einsum('btgrnl,bgtnld->btgrd', p.astype(vg.dtype), vg)

    t_chunks = jnp.arange(S).reshape(S // SEL_Q_BLOCK, SEL_Q_BLOCK)
    out = lax.map(chunk, (to_chunks(q, SEL_Q_BLOCK), to_chunks(idx, SEL_Q_BLOCK),
                          to_chunks(valid, SEL_Q_BLOCK), t_chunks))
    return jnp.moveaxis(out, 0, 1).reshape(B, S, G, R, dh)


def window_attention(q, k, v):
    B, S, G, R, dh = q.shape
    scale = dh ** -0.5
    kp = jnp.pad(k, ((0, 0), (WINDOW, 0), (0, 0), (0, 0)))
    vp = jnp.pad(v, ((0, 0), (WINDOW, 0), (0, 0), (0, 0)))
    band = WIN_Q_BLOCK + WINDOW

    def block(i):
        q0 = i * WIN_Q_BLOCK
        qb = lax.dynamic_slice_in_dim(q, q0, WIN_Q_BLOCK, axis=1)
        kb = lax.dynamic_slice_in_dim(kp, q0, band, axis=1)
        vb = lax.dynamic_slice_in_dim(vp, q0, band, axis=1)
        t = q0 + jnp.arange(WIN_Q_BLOCK)
        s_pos = q0 - WINDOW + jnp.arange(band)
        mask = (s_pos[None, :] <= t[:, None]) & (s_pos[None, :] > t[:, None] - WINDOW) & (s_pos[None, :] >= 0)
        s = jnp.einsum('btgrd,bsgd->btgrs', qb, kb).astype(jnp.float32) * scale
        p = masked_softmax(s, mask[None, :, None, None, :])
        return jnp.einsum('btgrs,bsgd->btgrd', p.astype(vb.dtype), vb)

    out = lax.map(block, jnp.arange(S // WIN_Q_BLOCK))
    return jnp.moveaxis(out, 0, 1).reshape(B, S, G, R, dh)


def nsa_attention(q, k_cmp, v_cmp, k_sel, v_sel, k_win, v_win, br_gate,
                  cmp_k_pe, cmp_k_w1, cmp_k_w2, cmp_v_pe, cmp_v_w1, cmp_v_w2):
    B, S, _ = q.shape
    G, R, dh = N_KV_HEADS, Q_PER_KV, HEAD_DIM
    scale = dh ** -0.5
    q = q.reshape(B, S, G, R, dh)
    kv = lambda a: a.reshape(B, S, G, dh)
    t_pos = jnp.arange(S)
    kc = compress_blocks(kv(k_cmp), cmp_k_pe, cmp_k_w1, cmp_k_w2)
    vc = compress_blocks(kv(v_cmp), cmp_v_pe, cmp_v_w1, cmp_v_w2)
    n_cmp = kc.shape[1]
    cmp_end = jnp.arange(n_cmp) * CMP_STRIDE + CMP_BLOCK - 1
    cmp_mask = cmp_end[None, :] <= t_pos[:, None]
    s_cmp = jnp.einsum('btgrd,bcgd->btgrc', q, kc).astype(jnp.float32) * scale
    p_cmp = masked_softmax(s_cmp, cmp_mask[None, :, None, None, :])
    o_cmp = jnp.einsum('btgrc,bcgd->btgrd', p_cmp.astype(vc.dtype), vc)
    n_sel_blocks = S // SEL_BLOCK
    imp = jnp.einsum('btgrc,cj->btgj', p_cmp, cmp_sel_overlap(n_cmp, n_sel_blocks))
    blk = jnp.arange(n_sel_blocks)[None, :]
    cur = (t_pos // SEL_BLOCK)[:, None]
    forced = (blk == 0) | (blk == cur) | (blk == cur - 1)
    causal_blk = blk * SEL_BLOCK <= t_pos[:, None]
    imp = jnp.where(forced[None, :, None, :], SEL_FORCE, imp)
    imp = jnp.where(causal_blk[None, :, None, :], imp, -SEL_FORCE)
    top_val, top_idx = lax.top_k(imp, min(SEL_TOPK, n_sel_blocks))
    top_valid = top_val > -0.5 * SEL_FORCE
    o_sel = selected_attention(q, kv(k_sel), kv(v_sel), top_idx, top_valid)
    o_win = window_attention(q, kv(k_win), kv(v_win))
    g = jax.nn.sigmoid(br_gate.astype(jnp.float32)).reshape(B, S, N_BRANCH, G, R)[..., None]
    o = g[:, :, 0] * o_cmp + g[:, :, 1] * o_sel + g[:, :, 2] * o_win
    return o.reshape(B, S, D_ATTN).astype(br_gate.dtype)


def setup_inputs(seed: int = 0) -> dict:
    key = jax.random.key(seed)
    ks = jax.random.split(key, 20)
    f32 = jnp.float32
    L = DEPTH

    def nrm(k, shape, scale):
        return jax.random.normal(k, shape, f32) * scale

    u = jax.random.uniform(ks[9], (L, D_RNN), f32, LRU_A_MIN ** (1.0 / LRU_C), LRU_A_MAX ** (1.0 / LRU_C))
    return {
        "x": nrm(ks[0], (BATCH, SEQ, D_MODEL), 1.0),
        "norm1_w": 1.0 + nrm(ks[1], (L, D_MODEL), 0.02),
        "w_in": nrm(ks[2], (L, D_MODEL, D_IN), D_MODEL ** -0.5),
        "conv_w": nrm(ks[3], (L, CONV_WIDTH, D_RNN), CONV_WIDTH ** -0.5),
        "conv_b": nrm(ks[4], (L, D_RNN), 0.01),
        "rg_wa": nrm(ks[5], (L, RNN_HEADS, RNN_HEAD_DIM, RNN_HEAD_DIM), RNN_HEAD_DIM ** -0.5),
        "rg_ba": nrm(ks[6], (L, D_RNN), 0.01),
        "rg_wx": nrm(ks[7], (L, RNN_HEADS, RNN_HEAD_DIM, RNN_HEAD_DIM), RNN_HEAD_DIM ** -0.5),
        "rg_bx": nrm(ks[8], (L, D_RNN), 0.01),
        "rg_lambda": jnp.log(u) - jnp.log1p(-u),
        "cmp_k_pe": nrm(ks[10], (L, CMP_BLOCK, HEAD_DIM), 0.1),
        "cmp_k_w1": nrm(ks[11], (L, CMP_BLOCK * HEAD_DIM, CMP_HIDDEN), (CMP_BLOCK * HEAD_DIM) ** -0.5),
        "cmp_k_w2": nrm(ks[12], (L, CMP_HIDDEN, HEAD_DIM), CMP_HIDDEN ** -0.5),
        "cmp_v_pe": nrm(ks[13], (L, CMP_BLOCK, HEAD_DIM), 0.1),
        "cmp_v_w1": nrm(ks[14], (L, CMP_BLOCK * HEAD_DIM, CMP_HIDDEN), (CMP_BLOCK * HEAD_DIM) ** -0.5),
        "cmp_v_w2": nrm(ks[15], (L, CMP_HIDDEN, HEAD_DIM), CMP_HIDDEN ** -0.5),
        "w_out": nrm(ks[16], (L, D_MIX, D_MODEL), D_MIX ** -0.5),
        "normf_w": 1.0 + nrm(ks[17], (D_MODEL,), 0.02),
    }


def reference(x, norm1_w, w_in, conv_w, conv_b, rg_wa, rg_ba, rg_wx, rg_bx, rg_lambda,
              cmp_k_pe, cmp_k_w1, cmp_k_w2, cmp_v_pe, cmp_v_w1, cmp_v_w2, w_out, normf_w):
    split_at = [int(v) for v in np.cumsum(SPLIT_SIZES)[:-1]]
    for l in range(DEPTH):
        h = rms_norm(x, norm1_w[l])
        proj = h @ w_in[l]
        (rnn_x, rnn_gate, q, k_cmp, v_cmp, k_sel, v_sel, k_win, v_win,
         attn_gate, br_gate) = jnp.split(proj, split_at, axis=-1)
        rnn_in = causal_depthwise_conv(rnn_x, conv_w[l], conv_b[l])
        rnn_out = rg_lru(rnn_in, rg_wa[l], rg_ba[l], rg_wx[l], rg_bx[l], rg_lambda[l]) * jax.nn.silu(rnn_gate)
        attn = nsa_attention(q, k_cmp, v_cmp, k_sel, v_sel, k_win, v_win, br_gate,
                             cmp_k_pe[l], cmp_k_w1[l], cmp_k_w2[l], cmp_v_pe[l], cmp_v_w1[l], cmp_v_w2[l])
        attn_out = attn * jax.nn.silu(attn_gate)
        x = x + jnp.concatenate([rnn_out, attn_out], axis=-1) @ w_out[l]
    return rms_norm(x, normf_w)
```

```python
import functools

import numpy as np
import jax
import jax.numpy as jnp
from jax import lax
from jax.experimental import pallas as pl
from jax.experimental.pallas import tpu as pltpu

F32 = jnp.float32
BF16 = jnp.bfloat16

D_MODEL = 1024
EPS = 1e-6
D_RNN = 512
RNN_HEADS = 8
RNN_HEAD_DIM = D_RNN // RNN_HEADS
CONV_WIDTH = 4
LRU_C = 8.0
N_Q_HEADS = 8
N_KV_HEADS = 2
HEAD_DIM = 64
Q_PER_KV = N_Q_HEADS // N_KV_HEADS
D_ATTN = N_Q_HEADS * HEAD_DIM
D_KV = N_KV_HEADS * HEAD_DIM
CMP_BLOCK = 32
CMP_STRIDE = 16
CMP_HIDDEN = 256
SEL_BLOCK = 64
SEL_TOPK = 8
SEL_FORCE = 1e9
WINDOW = 512
N_BRANCH = 3
D_MIX = D_RNN + D_ATTN
N_GATE_ROWS = 32

TM = 512
TS = 512
TQ = 128
KS = 256
KW = 128
CB = 4
NEG = -1e30
VMEM_LIMIT = 48 * 1024 * 1024

_O_RX, _O_RG, _O_Q = 0, D_RNN, 2 * D_RNN
_O_KC = _O_Q + D_ATTN
_O_VC = _O_KC + D_KV
_O_KS = _O_VC + D_KV
_O_VS = _O_KS + D_KV
_O_KW = _O_VS + D_KV
_O_VW = _O_KW + D_KV
_O_AG = _O_VW + D_KV
_O_BR = _O_AG + D_ATTN
_D_IN = _O_BR + N_BRANCH * N_Q_HEADS


def _silu(x):
    return x * jax.nn.sigmoid(x)


def _inproj_kernel(x_ref, nw_ref, wn_ref, wt_ref,
                   rnnx_ref, rnng_ref, kvc_ref, ksel_ref, kwin_ref, ag_ref,
                   qt_ref, vselt_ref, vwint_ref, brgt_ref):
    x = x_ref[...]
    ms = jnp.mean(x * x, axis=-1, keepdims=True)
    h = ((x * lax.rsqrt(ms + EPS)) * nw_ref[...]).astype(BF16)

    def nat(a, b):
        return jnp.dot(h, wn_ref[:, a:b], preferred_element_type=F32)

    def tr(a, b):
        return lax.dot_general(wt_ref[a:b, :], h, (((1,), (1,)), ((), ())),
                               preferred_element_type=F32)

    rnnx_ref[...] = nat(0, 512)
    rnng_ref[...] = nat(512, 1024)
    kvc_ref[0] = nat(1024, 1152)
    kvc_ref[1] = nat(1152, 1280)
    ksel_ref[...] = nat(1280, 1408).astype(BF16)
    kwin_ref[...] = nat(1408, 1536).astype(BF16)
    ag_ref[...] = nat(1536, 2048)

    qt_ref[...] = (tr(0, 512) * (HEAD_DIM ** -0.5)).astype(BF16)
    vs = tr(512, 640).astype(BF16)
    for c in range(TM // KS):
        vselt_ref[c] = vs[:, c * KS:(c + 1) * KS]
    vw = tr(640, 768).astype(BF16)
    for c in range(TM // KW):
        vwint_ref[c] = vw[:, c * KW:(c + 1) * KW]
    brgt_ref[...] = tr(768, 768 + N_GATE_ROWS)


def _inproj(x2, nw, wn, wt, B, S):
    T = B * S
    ns = S // TM
    grid = (B, ns)
    row = lambda b, s: (b * ns + s, 0)
    out_shape = (
        jax.ShapeDtypeStruct((T, D_RNN), F32),
        jax.ShapeDtypeStruct((T, D_RNN), F32),
        jax.ShapeDtypeStruct((2, T, D_KV), F32),
        jax.ShapeDtypeStruct((T, D_KV), BF16),
        jax.ShapeDtypeStruct((T, D_KV), BF16),
        jax.ShapeDtypeStruct((T, D_ATTN), F32),
        jax.ShapeDtypeStruct((B, ns, D_ATTN, TM), BF16),
        jax.ShapeDtypeStruct((B, S // KS, D_KV, KS), BF16),
        jax.ShapeDtypeStruct((B, S // KW, D_KV, KW), BF16),
        jax.ShapeDtypeStruct((B, ns, N_GATE_ROWS, TM), F32),
    )
    out_specs = (
        pl.BlockSpec((TM, D_RNN), row),
        pl.BlockSpec((TM, D_RNN), row),
        pl.BlockSpec((2, TM, D_KV), lambda b, s: (0, b * ns + s, 0)),
        pl.BlockSpec((TM, D_KV), row),
        pl.BlockSpec((TM, D_KV), row),
        pl.BlockSpec((TM, D_ATTN), row),
        pl.BlockSpec((None, None, D_ATTN, TM), lambda b, s: (b, s, 0, 0)),
        pl.BlockSpec((None, TM // KS, D_KV, KS), lambda b, s: (b, s, 0, 0)),
        pl.BlockSpec((None, TM // KW, D_KV, KW), lambda b, s: (b, s, 0, 0)),
        pl.BlockSpec((None, None, N_GATE_ROWS, TM), lambda b, s: (b, s, 0, 0)),
    )
    return pl.pallas_call(
        _inproj_kernel,
        grid=grid,
        in_specs=[
            pl.BlockSpec((TM, D_MODEL), row),
            pl.BlockSpec((1, D_MODEL), lambda b, s: (0, 0)),
            pl.BlockSpec(wn.shape, lambda b, s: (0, 0)),
            pl.BlockSpec(wt.shape, lambda b, s: (0, 0)),
        ],
        out_specs=out_specs,
        out_shape=out_shape,
        compiler_params=pltpu.CompilerParams(
            dimension_semantics=("arbitrary", "arbitrary"), vmem_limit_bytes=VMEM_LIMIT),
        name="inproj",
    )(x2, nw, wn, wt)


def _rglru_kernel(x_ref, gate_ref, cw_ref, cb_ref, wg_ref, ba_ref, bx_ref, lam_ref,
                  out_ref, cbuf, a_s, u_s, h_s, hlast):
    sc = pl.program_id(1)

    @pl.when(sc == 0)
    def _():
        cbuf[0:8, :] = jnp.zeros((8, D_RNN), F32)
        hlast[...] = jnp.zeros((1, D_RNN), F32)

    x = x_ref[...]
    cbuf[8:8 + TS, :] = x
    y = cb_ref[...]
    for k in range(CONV_WIDTH):
        off = 8 - (CONV_WIDTH - 1) + k
        y = y + cbuf[off:off + TS, :] * cw_ref[k:k + 1, :]
    cbuf[0:8, :] = x[TS - 8:TS, :]

    yb = y.astype(BF16)
    half = D_RNN // 2
    pre = [jnp.dot(yb[:, hh * half:(hh + 1) * half], wg_ref[hh], preferred_element_type=F32)
           for hh in range(2)]
    pre_a = jnp.concatenate([pre[0][:, :half], pre[1][:, :half]], axis=1)
    pre_x = jnp.concatenate([pre[0][:, half:], pre[1][:, half:]], axis=1)
    r = jax.nn.sigmoid(pre_a + ba_ref[...])
    i = jax.nn.sigmoid(pre_x + bx_ref[...])
    lam = lam_ref[...]
    lsig = jnp.minimum(lam, 0.0) - jnp.log1p(jnp.exp(-jnp.abs(lam)))
    log_a = LRU_C * r * lsig
    a = jnp.exp(log_a)
    u = jnp.sqrt(1.0 - a * a) * (i * y)

    rowmod = lax.broadcasted_iota(jnp.int32, (TS, D_RNN), 0) & 7
    for d in (1, 2, 4):
        a_sh = pltpu.roll(a, d, 0)
        u_sh = pltpu.roll(u, d, 0)
        valid = rowmod >= d
        u = jnp.where(valid, a * u_sh + u, u)
        a = jnp.where(valid, a * a_sh, a)
    a_s[...] = a
    u_s[...] = u

    def body(j, carry):
        blk = pl.multiple_of(j * 8, 8)
        hb = u_s[pl.ds(blk, 8), :] + a_s[pl.ds(blk, 8), :] * carry
        h_s[pl.ds(blk, 8), :] = hb
        return hb[7:8, :]

    hlast[...] = lax.fori_loop(0, TS // 8, body, hlast[...], unroll=8)
    out_ref[...] = (h_s[...] * _silu(gate_ref[...])).astype(out_ref.dtype)


def _rglru(rnn_x, rnn_gate, cw, cb, wg, ba, bx, lam, B, S):
    T = B * S
    ns = S // TS
    row = lambda b, s: (b * ns + s, 0)
    const2 = lambda b, s: (0, 0)
    return pl.pallas_call(
        _rglru_kernel,
        grid=(B, ns),
        in_specs=[
            pl.BlockSpec((TS, D_RNN), row),
            pl.BlockSpec((TS, D_RNN), row),
            pl.BlockSpec((CONV_WIDTH, D_RNN), const2),
            pl.BlockSpec((1, D_RNN), const2),
            pl.BlockSpec(wg.shape, lambda b, s: (0, 0, 0)),
            pl.BlockSpec((1, D_RNN), const2),
            pl.BlockSpec((1, D_RNN), const2),
            pl.BlockSpec((1, D_RNN), const2),
        ],
        out_specs=pl.BlockSpec((TS, D_RNN), row),
        out_shape=jax.ShapeDtypeStruct((T, D_RNN), BF16),
        scratch_shapes=[
            pltpu.VMEM((TS + 8, D_RNN), F32),
            pltpu.VMEM((TS, D_RNN), F32),
            pltpu.VMEM((TS, D_RNN), F32),
            pltpu.VMEM((TS, D_RNN), F32),
            pltpu.VMEM((1, D_RNN), F32),
        ],
        compiler_params=pltpu.CompilerParams(
            dimension_semantics=("arbitrary", "arbitrary"), vmem_limit_bytes=VMEM_LIMIT),
        name="rglru",
    )(rnn_x, rnn_gate, cw, cb, wg, ba, bx, lam)


def _compress_kernel(x_ref, pe_ref, wbig_ref, w2_ref, out_ref):
    nrow = x_ref.shape[1]
    nchunk = nrow // CB
    x = x_ref[0]
    xa = (x + pe_ref[0, 0:1, :]).astype(BF16)
    xb = (x + pe_ref[0, 1:2, :]).astype(BF16)
    nh = N_KV_HEADS * CMP_HIDDEN
    first = jnp.dot(xa, wbig_ref[0, :, :nh], preferred_element_type=F32)
    second = jnp.dot(xb, wbig_ref[0, :, nh:], preferred_element_type=F32)
    hid = _silu(first + pltpu.roll(second, nrow - 1, 0))
    out = jnp.dot(hid.astype(BF16), w2_ref[0], preferred_element_type=F32)
    c_idx = lax.broadcasted_iota(jnp.int32, out.shape, 0) & (nchunk - 1)
    out_ref[0] = jnp.where(c_idx < nchunk - 1, out, 0.0)


def _compress(xc, pe2, wbig, w2bd, B, S):
    nchunk = S // CMP_STRIDE
    nrow = CB * nchunk
    width = CMP_STRIDE * D_KV
    kv = lambda i, j: (i, 0, 0)
    return pl.pallas_call(
        _compress_kernel,
        grid=(2, B // CB),
        in_specs=[
            pl.BlockSpec((1, nrow, width), lambda i, j: (i, j, 0)),
            pl.BlockSpec((1, 2, width), kv),
            pl.BlockSpec((1,) + wbig.shape[1:], kv),
            pl.BlockSpec((1,) + w2bd.shape[1:], kv),
        ],
        out_specs=pl.BlockSpec((1, nrow, D_KV), lambda i, j: (i, j, 0)),
        out_shape=jax.ShapeDtypeStruct((2, B * nchunk, D_KV), F32),
        compiler_params=pltpu.CompilerParams(
            dimension_semantics=("arbitrary", "arbitrary"), vmem_limit_bytes=VMEM_LIMIT),
        name="compress",
    )(xc, pe2, wbig, w2bd)


def _nsa_kernel(qt_ref, cmp_ref, ksel_ref, vselt_ref, kwin_ref, vwint_ref, brgt_ref, ag_ref,
                ovt_ref, out_ref, m_ref, l_ref, acc_ref, pen_ref):
    qi = pl.program_id(1)
    q0 = qi * TQ
    nl = Q_PER_KV * TQ
    nblk = pen_ref.shape[0]
    ncmp = cmp_ref.shape[1]

    gates = jax.nn.sigmoid(brgt_ref[...])

    def tile4(a):
        return jnp.concatenate([a] * Q_PER_KV, axis=1)

    def init_state():
        m_ref[...] = jnp.full((1, nl), NEG, F32)
        l_ref[...] = jnp.zeros((1, nl), F32)
        acc_ref[...] = jnp.zeros((HEAD_DIM, nl), F32)

    def attend(k_tile, vt_tile, qpad, pen):
        s = jnp.dot(k_tile, qpad, preferred_element_type=F32)
        if pen is not None:
            s = s + pen
        m_prev = m_ref[...]
        m_new = jnp.maximum(m_prev, jnp.max(s, axis=0, keepdims=True))
        alpha = jnp.exp(m_prev - m_new)
        p = jnp.exp(s - m_new)
        l_ref[...] = alpha * l_ref[...] + jnp.sum(p, axis=0, keepdims=True)
        acc_ref[...] = alpha * acc_ref[...] + jnp.dot(vt_tile, p.astype(BF16),
                                                      preferred_element_type=F32)
        m_ref[...] = m_new

    def finish():
        return acc_ref[...] * (1.0 / l_ref[...])

    group_out = []
    for g in range(N_KV_HEADS):
        gs = slice(g * HEAD_DIM, (g + 1) * HEAD_DIM)
        zpad = jnp.zeros((HEAD_DIM, TQ), BF16)
        cols = []
        for r in range(Q_PER_KV):
            hq = g * Q_PER_KV + r
            qh = qt_ref[hq * HEAD_DIM:(hq + 1) * HEAD_DIM, :]
            cols.append(jnp.concatenate([qh, zpad] if g == 0 else [zpad, qh], axis=0))
        qpad = jnp.concatenate(cols, axis=1)

        kc = cmp_ref[0].astype(BF16)
        vct = cmp_ref[1].T[gs, :].astype(BF16)
        s = jnp.dot(kc, qpad, preferred_element_type=F32)
        c_row = lax.broadcasted_iota(jnp.int32, (ncmp, TQ), 0)
        t_lane = q0 + lax.broadcasted_iota(jnp.int32, (ncmp, TQ), 1)
        cvis = c_row * CMP_STRIDE + (CMP_BLOCK - 1) <= t_lane
        s = s + tile4(jnp.where(cvis, 0.0, NEG))
        m = jnp.max(s, axis=0, keepdims=True)
        m = jnp.where(m > 0.5 * NEG, m, 0.0)
        e = jnp.exp(s - m)
        den = jnp.sum(e, axis=0, keepdims=True)
        p = e * (1.0 / jnp.where(den > 0.0, den, 1.0))
        o_cmp = jnp.dot(vct, p.astype(BF16), preferred_element_type=F32)

        psum = p[:, 0:TQ]
        for r in range(1, Q_PER_KV):
            psum = psum + p[:, r * TQ:(r + 1) * TQ]
        imp = jnp.dot(ovt_ref[...], psum, preferred_element_type=F32,
                      precision=lax.Precision.HIGHEST)
        j_row = lax.broadcasted_iota(jnp.int32, (nblk, TQ), 0)
        t_blk = q0 + lax.broadcasted_iota(jnp.int32, (nblk, TQ), 1)
        cur = jnp.right_shift(t_blk, SEL_BLOCK.bit_length() - 1)
        forced = (j_row == 0) | (j_row == cur) | (j_row == cur - 1)
        val = jnp.where(forced, SEL_FORCE, imp)
        val = jnp.where(j_row * SEL_BLOCK <= t_blk, val, -SEL_FORCE)
        rank = jnp.zeros((nblk, TQ), F32)
        for i in range(nblk):
            vi = jnp.broadcast_to(val[i:i + 1, :], (nblk, TQ))
            rank = rank + jnp.where(j_row > i, jnp.where(vi >= val, 1.0, 0.0),
                                    jnp.where(vi > val, 1.0, 0.0))
        chosen = (rank < float(min(SEL_TOPK, nblk))) & (val > -0.5 * SEL_FORCE)
        selpen = jnp.where(chosen, 0.0, NEG)
        for j in range(nblk):
            pen_ref[j] = jnp.broadcast_to(selpen[j:j + 1, :], (8, TQ))

        init_state()
        bpt = KS // SEL_BLOCK

        def sel_body(kt, carry):
            k_tile = ksel_ref[pl.ds(pl.multiple_of(kt * KS, KS), KS), :]
            vt_tile = vselt_ref[kt, gs, :]
            rows = [jnp.concatenate([pen_ref[kt * bpt + jj]] * (SEL_BLOCK // 8), axis=0)
                    for jj in range(bpt)]
            pen = jnp.concatenate(rows, axis=0)
            key = kt * KS + lax.broadcasted_iota(jnp.int32, (KS, TQ), 0)
            t_q = q0 + lax.broadcasted_iota(jnp.int32, (KS, TQ), 1)
            pen = pen + jnp.where(key <= t_q, 0.0, NEG)
            attend(k_tile, vt_tile, qpad, tile4(pen))
            return carry

        lax.fori_loop(0, (q0 + TQ + KS - 1) // KS, sel_body, 0)
        o_sel = finish()

        init_state()
        row_w = lax.broadcasted_iota(jnp.int32, (KW, TQ), 0)
        lane_w = lax.broadcasted_iota(jnp.int32, (KW, TQ), 1)
        n_back = WINDOW // KW
        for dk in range(n_back + 1):
            if dk == 0:
                pen = tile4(jnp.where(row_w <= lane_w, 0.0, NEG))
            elif dk == n_back:
                pen = tile4(jnp.where(row_w > lane_w, 0.0, NEG))
            else:
                pen = None

            @pl.when(qi - dk >= 0)
            def _(dk=dk, pen=pen):
                kt = qi - dk
                k_tile = kwin_ref[pl.ds(pl.multiple_of(kt * KW, KW), KW), :]
                attend(k_tile, vwint_ref[kt, gs, :], qpad, pen)

        o_win = finish()

        heads = []
        for r in range(Q_PER_KV):
            hq = g * Q_PER_KV + r
            ls = slice(r * TQ, (r + 1) * TQ)
            o = (gates[hq:hq + 1, :] * o_cmp[:, ls]
                 + gates[N_Q_HEADS + hq:N_Q_HEADS + hq + 1, :] * o_sel[:, ls]
                 + gates[2 * N_Q_HEADS + hq:2 * N_Q_HEADS + hq + 1, :] * o_win[:, ls])
            heads.append(o)
        group_out.append(jnp.concatenate(heads, axis=0).T)

    attn = jnp.concatenate(group_out, axis=1)
    out_ref[...] = (attn * _silu(ag_ref[...])).astype(out_ref.dtype)


def _nsa(qt, cmp_kv, ksel, vselt, kwin, vwint, brgt, ag, ovt, B, S):
    T = B * S
    nq = S // TQ
    qpm = TM // TQ
    nblk = S // SEL_BLOCK
    ncmp = S // CMP_STRIDE
    return pl.pallas_call(
        _nsa_kernel,
        grid=(B, nq),
        in_specs=[
            pl.BlockSpec((None, None, D_ATTN, TQ), lambda b, q: (b, q // qpm, 0, q % qpm)),
            pl.BlockSpec((2, ncmp, D_KV), lambda b, q: (0, b, 0)),
            pl.BlockSpec((S, D_KV), lambda b, q: (b, 0)),
            pl.BlockSpec((None, S // KS, D_KV, KS), lambda b, q: (b, 0, 0, 0)),
            pl.BlockSpec((S, D_KV), lambda b, q: (b, 0)),
            pl.BlockSpec((None, S // KW, D_KV, KW), lambda b, q: (b, 0, 0, 0)),
            pl.BlockSpec((None, None, N_GATE_ROWS, TQ), lambda b, q: (b, q // qpm, 0, q % qpm)),
            pl.BlockSpec((TQ, D_ATTN), lambda b, q: (b * nq + q, 0)),
            pl.BlockSpec(ovt.shape, lambda b, q: (0, 0)),
        ],
        out_specs=pl.BlockSpec((TQ, D_ATTN), lambda b, q: (b * nq + q, 0)),
        out_shape=jax.ShapeDtypeStruct((T, D_ATTN), BF16),
        scratch_shapes=[
            pltpu.VMEM((1, Q_PER_KV * TQ), F32),
            pltpu.VMEM((1, Q_PER_KV * TQ), F32),
            pltpu.VMEM((HEAD_DIM, Q_PER_KV * TQ), F32),
            pltpu.VMEM((nblk, 8, TQ), F32),
        ],
        compiler_params=pltpu.CompilerParams(
            dimension_semantics=("arbitrary", "arbitrary"), vmem_limit_bytes=VMEM_LIMIT),
        name="nsa",
    )(qt, cmp_kv, ksel, vselt, kwin, vwint, brgt, ag, ovt)


def _outproj_kernel(x_ref, rnn_ref, attn_ref, wo_ref, nfw_ref, out_ref, *, final_norm):
    y = x_ref[...]
    y = y + jnp.dot(rnn_ref[...], wo_ref[0:D_RNN, :], preferred_element_type=F32)
    y = y + jnp.dot(attn_ref[...], wo_ref[D_RNN:D_MIX, :], preferred_element_type=F32)
    if final_norm:
        ms = jnp.mean(y * y, axis=-1, keepdims=True)
        y = (y * lax.rsqrt(ms + EPS)) * nfw_ref[...]
    out_ref[...] = y


def _outproj(x2, rnn_out, attn_out, wo, nfw, final_norm):
    T = x2.shape[0]
    row = lambda i: (i, 0)
    return pl.pallas_call(
        functools.partial(_outproj_kernel, final_norm=final_norm),
        grid=(T // TM,),
        in_specs=[
            pl.BlockSpec((TM, D_MODEL), row),
            pl.BlockSpec((TM, D_RNN), row),
            pl.BlockSpec((TM, D_ATTN), row),
            pl.BlockSpec((D_MIX, D_MODEL), lambda i: (0, 0)),
            pl.BlockSpec((1, D_MODEL), lambda i: (0, 0)),
        ],
        out_specs=pl.BlockSpec((TM, D_MODEL), row),
        out_shape=jax.ShapeDtypeStruct((T, D_MODEL), F32),
        compiler_params=pltpu.CompilerParams(
            dimension_semantics=("arbitrary",), vmem_limit_bytes=VMEM_LIMIT),
        name="outproj",
    )(x2, rnn_out, attn_out, wo, nfw)


def _block_diag_halves(wa, wx):
    eye = jnp.eye(RNN_HEADS, dtype=wa.dtype)
    full = lambda w: jnp.einsum('hij,hk->hikj', w, eye).reshape(D_RNN, D_RNN)
    fa, fx = full(wa), full(wx)
    half = D_RNN // 2
    return jnp.stack([
        jnp.concatenate([fa[s:s + half, s:s + half], fx[s:s + half, s:s + half]], axis=1)
        for s in (0, half)]).astype(BF16)


def _compress_weights(pe, w1, w2):
    eye = jnp.eye(N_KV_HEADS, dtype=w1.dtype)
    w1r = w1.reshape(2, CMP_STRIDE, HEAD_DIM, CMP_HIDDEN)
    wbig = jnp.einsum('hldn,ge->lgdhen', w1r, eye).reshape(
        CMP_STRIDE * D_KV, 2 * N_KV_HEADS * CMP_HIDDEN)
    w2bd = jnp.einsum('nd,ge->gned', w2, eye).reshape(N_KV_HEADS * CMP_HIDDEN, D_KV)
    per = pe.reshape(2, CMP_STRIDE, 1, HEAD_DIM)
    pe2 = jnp.broadcast_to(per, (2, CMP_STRIDE, N_KV_HEADS, HEAD_DIM)).reshape(2, CMP_STRIDE * D_KV)
    return pe2, wbig.astype(BF16), w2bd.astype(BF16)


def _overlap_t(ncmp_pad, nblk):
    cs = np.arange(ncmp_pad)[None, :] * CMP_STRIDE
    ss = np.arange(nblk)[:, None] * SEL_BLOCK
    ov = np.clip(np.minimum(cs + CMP_BLOCK, ss + SEL_BLOCK) - np.maximum(cs, ss), 0, None)
    return jnp.asarray(ov.astype(np.float32) / CMP_BLOCK)


def kernel(x, norm1_w, w_in, conv_w, conv_b, rg_wa, rg_ba, rg_wx, rg_bx, rg_lambda,
           cmp_k_pe, cmp_k_w1, cmp_k_w2, cmp_v_pe, cmp_v_w1, cmp_v_w2, w_out, normf_w):
    B, S, D = x.shape
    assert D == D_MODEL and w_in.shape[-1] == _D_IN
    assert S % TM == 0 and S % TS == 0 and B % CB == 0 and KW == TQ and WINDOW % KW == 0
    depth = w_in.shape[0]
    T = B * S
    ovt = _overlap_t(S // CMP_STRIDE, S // SEL_BLOCK)
    x2 = x.reshape(T, D)
    for l in range(depth):
        w = w_in[l]
        wn = jnp.concatenate([w[:, _O_RX:_O_Q], w[:, _O_KC:_O_KS], w[:, _O_KS:_O_VS],
                              w[:, _O_KW:_O_VW], w[:, _O_AG:_O_BR]], axis=1).astype(BF16)
        wt = jnp.concatenate([w[:, _O_Q:_O_KC], w[:, _O_VS:_O_KW], w[:, _O_VW:_O_AG],
                              w[:, _O_BR:_D_IN],
                              jnp.zeros((D, N_GATE_ROWS - N_BRANCH * N_Q_HEADS), w.dtype)],
                             axis=1).T.astype(BF16)
        (rnn_x, rnn_gate, kvc, ksel, kwin, ag, qt, vselt, vwint, brgt) = _inproj(
            x2, norm1_w[l].reshape(1, D), wn, wt, B, S)

        rnn_out = _rglru(rnn_x, rnn_gate, conv_w[l], conv_b[l].reshape(1, D_RNN),
                         _block_diag_halves(rg_wa[l], rg_wx[l]),
                         rg_ba[l].reshape(1, D_RNN), rg_bx[l].reshape(1, D_RNN),
                         rg_lambda[l].reshape(1, D_RNN), B, S)

        pk, wbk, w2k = _compress_weights(cmp_k_pe[l], cmp_k_w1[l], cmp_k_w2[l])
        pv, wbv, w2v = _compress_weights(cmp_v_pe[l], cmp_v_w1[l], cmp_v_w2[l])
        xc = kvc.reshape(2, T // CMP_STRIDE, CMP_STRIDE * D_KV)
        cmp_kv = _compress(xc, jnp.stack([pk, pv]), jnp.stack([wbk, wbv]),
                           jnp.stack([w2k, w2v]), B, S)

        attn_out = _nsa(qt, cmp_kv, ksel, vselt, kwin, vwint, brgt, ag, ovt, B, S)

        x2 = _outproj(x2, rnn_out, attn_out, w_out[l].astype(BF16),
                      normf_w.reshape(1, D), final_norm=(l == depth - 1))
    return x2.reshape(B, S, D)
```

```python
import functools

import numpy as np
import jax
import jax.numpy as jnp
from jax import lax
from jax.experimental import pallas as pl
from jax.experimental.pallas import tpu as pltpu

F32 = jnp.float32
BF16 = jnp.bfloat16

D_MODEL = 1024
EPS = 1e-6
D_RNN = 512
RNN_HEADS = 8
RNN_HEAD_DIM = D_RNN // RNN_HEADS
CONV_WIDTH = 4
LRU_C = 8.0
N_Q_HEADS = 8
N_KV_HEADS = 2
HEAD_DIM = 64
Q_PER_KV = N_Q_HEADS // N_KV_HEADS
D_ATTN = N_Q_HEADS * HEAD_DIM
D_KV = N_KV_HEADS * HEAD_DIM
CMP_BLOCK = 32
CMP_STRIDE = 16
CMP_HIDDEN = 256
SEL_BLOCK = 64
SEL_TOPK = 8
SEL_FORCE = 1e9
WINDOW = 512
N_BRANCH = 3
D_MIX = D_RNN + D_ATTN
N_GATE_ROWS = 32

TM = 512
TS = 512
TQ = 128
KS = 256
KW = 128
VC = 128
CB = 4
NEG = -1e30
VMEM_LIMIT = 48 * 1024 * 1024

_O_RX, _O_RG, _O_Q = 0, D_RNN, 2 * D_RNN
_O_KC = _O_Q + D_ATTN
_O_VC = _O_KC + D_KV
_O_KS = _O_VC + D_KV
_O_VS = _O_KS + D_KV
_O_KW = _O_VS + D_KV
_O_VW = _O_KW + D_KV
_O_AG = _O_VW + D_KV
_O_BR = _O_AG + D_ATTN
_D_IN = _O_BR + N_BRANCH * N_Q_HEADS


def _silu(x):
    return x * jax.nn.sigmoid(x)


def _inproj_kernel(x_ref, nw_ref, wn_ref, wt_ref,
                   rnnx_ref, rnng_ref, kvc_ref, ksel_ref, kwin_ref, ag_ref,
                   qt_ref, vselt_ref, vwint_ref, brgt_ref):
    x = x_ref[...]
    ms = jnp.mean(x * x, axis=-1, keepdims=True)
    h = ((x * lax.rsqrt(ms + EPS)) * nw_ref[...]).astype(BF16)

    def nat(a, b):
        return jnp.dot(h, wn_ref[:, a:b], preferred_element_type=F32)

    def tr(a, b):
        return lax.dot_general(wt_ref[a:b, :], h, (((1,), (1,)), ((), ())),
                               preferred_element_type=F32)

    rnnx_ref[...] = nat(0, 512)
    rnng_ref[...] = nat(512, 1024)
    kvc_ref[0] = nat(1024, 1152)
    kvc_ref[1] = nat(1152, 1280)
    ksel_ref[...] = nat(1280, 1408).astype(BF16)
    kwin_ref[...] = nat(1408, 1536).astype(BF16)
    ag_ref[...] = nat(1536, 2048)

    qt_ref[...] = (tr(0, 512) * (HEAD_DIM ** -0.5)).astype(BF16)
    vs = tr(512, 640).astype(BF16)
    for c in range(TM // VC):
        vselt_ref[c] = vs[:, c * VC:(c + 1) * VC]
    vw = tr(640, 768).astype(BF16)
    for c in range(TM // VC):
        vwint_ref[c] = vw[:, c * VC:(c + 1) * VC]
    brgt_ref[...] = tr(768, 768 + N_GATE_ROWS)


def _inproj(x2, nw, wn, wt, B, S):
    T = B * S
    ns = S // TM
    grid = (B, ns)
    row = lambda b, s: (b * ns + s, 0)
    out_shape = (
        jax.ShapeDtypeStruct((T, D_RNN), F32),
        jax.ShapeDtypeStruct((T, D_RNN), F32),
        jax.ShapeDtypeStruct((2, T, D_KV), F32),
        jax.ShapeDtypeStruct((T, D_KV), BF16),
        jax.ShapeDtypeStruct((T, D_KV), BF16),
        jax.ShapeDtypeStruct((T, D_ATTN), F32),
        jax.ShapeDtypeStruct((B, ns, D_ATTN, TM), BF16),
        jax.ShapeDtypeStruct((B, S // VC, D_KV, VC), BF16),
        jax.ShapeDtypeStruct((B, S // VC, D_KV, VC), BF16),
        jax.ShapeDtypeStruct((B, ns, N_GATE_ROWS, TM), F32),
    )
    out_specs = (
        pl.BlockSpec((TM, D_RNN), row),
        pl.BlockSpec((TM, D_RNN), row),
        pl.BlockSpec((2, TM, D_KV), lambda b, s: (0, b * ns + s, 0)),
        pl.BlockSpec((TM, D_KV), row),
        pl.BlockSpec((TM, D_KV), row),
        pl.BlockSpec((TM, D_ATTN), row),
        pl.BlockSpec((None, None, D_ATTN, TM), lambda b, s: (b, s, 0, 0)),
        pl.BlockSpec((None, TM // VC, D_KV, VC), lambda b, s: (b, s, 0, 0)),
        pl.BlockSpec((None, TM // VC, D_KV, VC), lambda b, s: (b, s, 0, 0)),
        pl.BlockSpec((None, None, N_GATE_ROWS, TM), lambda b, s: (b, s, 0, 0)),
    )
    return pl.pallas_call(
        _inproj_kernel,
        grid=grid,
        in_specs=[
            pl.BlockSpec((TM, D_MODEL), row),
            pl.BlockSpec((1, D_MODEL), lambda b, s: (0, 0)),
            pl.BlockSpec(wn.shape, lambda b, s: (0, 0)),
            pl.BlockSpec(wt.shape, lambda b, s: (0, 0)),
        ],
        out_specs=out_specs,
        out_shape=out_shape,
        compiler_params=pltpu.CompilerParams(
            dimension_semantics=("arbitrary", "arbitrary"), vmem_limit_bytes=VMEM_LIMIT),
        name="inproj",
    )(x2, nw, wn, wt)


def _rglru_kernel(x_ref, gate_ref, cw_ref, cb_ref, wg_ref, ba_ref, bx_ref, lam_ref,
                  out_ref, cbuf, a_s, u_s, h_s, hlast):
    sc = pl.program_id(1)

    @pl.when(sc == 0)
    def _():
        cbuf[0:8, :] = jnp.zeros((8, D_RNN), F32)
        hlast[...] = jnp.zeros((1, D_RNN), F32)

    x = x_ref[...]
    cbuf[8:8 + TS, :] = x
    y = cb_ref[...]
    for k in range(CONV_WIDTH):
        off = 8 - (CONV_WIDTH - 1) + k
        y = y + cbuf[off:off + TS, :] * cw_ref[k:k + 1, :]
    cbuf[0:8, :] = x[TS - 8:TS, :]

    yb = y.astype(BF16)
    half = D_RNN // 2
    pre = [jnp.dot(yb[:, hh * half:(hh + 1) * half], wg_ref[hh], preferred_element_type=F32)
           for hh in range(2)]
    pre_a = jnp.concatenate([pre[0][:, :half], pre[1][:, :half]], axis=1)
    pre_x = jnp.concatenate([pre[0][:, half:], pre[1][:, half:]], axis=1)
    r = jax.nn.sigmoid(pre_a + ba_ref[...])
    i = jax.nn.sigmoid(pre_x + bx_ref[...])
    lam = lam_ref[...]
    lsig = jnp.minimum(lam, 0.0) - jnp.log1p(jnp.exp(-jnp.abs(lam)))
    log_a = LRU_C * r * lsig
    a = jnp.exp(log_a)
    u = jnp.sqrt(1.0 - a * a) * (i * y)

    rowmod = lax.broadcasted_iota(jnp.int32, (TS, D_RNN), 0) & 7
    for d in (1, 2, 4):
        a_sh = pltpu.roll(a, d, 0)
        u_sh = pltpu.roll(u, d, 0)
        valid = rowmod >= d
        u = jnp.where(valid, a * u_sh + u, u)
        a = jnp.where(valid, a * a_sh, a)
    a_s[...] = a
    u_s[...] = u

    def body(j, carry):
        blk = pl.multiple_of(j * 8, 8)
        hb = u_s[pl.ds(blk, 8), :] + a_s[pl.ds(blk, 8), :] * carry
        h_s[pl.ds(blk, 8), :] = hb
        return hb[7:8, :]

    hlast[...] = lax.fori_loop(0, TS // 8, body, hlast[...], unroll=8)
    out_ref[...] = (h_s[...] * _silu(gate_ref[...])).astype(out_ref.dtype)


def _rglru(rnn_x, rnn_gate, cw, cb, wg, ba, bx, lam, B, S):
    T = B * S
    ns = S // TS
    row = lambda b, s: (b * ns + s, 0)
    const2 = lambda b, s: (0, 0)
    return pl.pallas_call(
        _rglru_kernel,
        grid=(B, ns),
        in_specs=[
            pl.BlockSpec((TS, D_RNN), row),
            pl.BlockSpec((TS, D_RNN), row),
            pl.BlockSpec((CONV_WIDTH, D_RNN), const2),
            pl.BlockSpec((1, D_RNN), const2),
            pl.BlockSpec(wg.shape, lambda b, s: (0, 0, 0)),
            pl.BlockSpec((1, D_RNN), const2),
            pl.BlockSpec((1, D_RNN), const2),
            pl.BlockSpec((1, D_RNN), const2),
        ],
        out_specs=pl.BlockSpec((TS, D_RNN), row),
        out_shape=jax.ShapeDtypeStruct((T, D_RNN), BF16),
        scratch_shapes=[
            pltpu.VMEM((TS + 8, D_RNN), F32),
            pltpu.VMEM((TS, D_RNN), F32),
            pltpu.VMEM((TS, D_RNN), F32),
            pltpu.VMEM((TS, D_RNN), F32),
            pltpu.VMEM((1, D_RNN), F32),
        ],
        compiler_params=pltpu.CompilerParams(
            dimension_semantics=("arbitrary", "arbitrary"), vmem_limit_bytes=VMEM_LIMIT),
        name="rglru",
    )(rnn_x, rnn_gate, cw, cb, wg, ba, bx, lam)


def _compress_kernel(x_ref, pe_ref, wbig_ref, w2_ref, out_ref):
    nrow = x_ref.shape[1]
    nchunk = nrow // CB
    x = x_ref[0]
    xa = (x + pe_ref[0, 0:1, :]).astype(BF16)
    xb = (x + pe_ref[0, 1:2, :]).astype(BF16)
    nh = N_KV_HEADS * CMP_HIDDEN
    first = jnp.dot(xa, wbig_ref[0, :, :nh], preferred_element_type=F32)
    second = jnp.dot(xb, wbig_ref[0, :, nh:], preferred_element_type=F32)
    hid = _silu(first + pltpu.roll(second, nrow - 1, 0))
    out = jnp.dot(hid.astype(BF16), w2_ref[0], preferred_element_type=F32)
    c_idx = lax.broadcasted_iota(jnp.int32, out.shape, 0) & (nchunk - 1)
    out_ref[0] = jnp.where(c_idx < nchunk - 1, out, 0.0)


def _compress(xc, pe2, wbig, w2bd, B, S):
    nchunk = S // CMP_STRIDE
    nrow = CB * nchunk
    width = CMP_STRIDE * D_KV
    kv = lambda i, j: (i, 0, 0)
    return pl.pallas_call(
        _compress_kernel,
        grid=(2, B // CB),
        in_specs=[
            pl.BlockSpec((1, nrow, width), lambda i, j: (i, j, 0)),
            pl.BlockSpec((1, 2, width), kv),
            pl.BlockSpec((1,) + wbig.shape[1:], kv),
            pl.BlockSpec((1,) + w2bd.shape[1:], kv),
        ],
        out_specs=pl.BlockSpec((1, nrow, D_KV), lambda i, j: (i, j, 0)),
        out_shape=jax.ShapeDtypeStruct((2, B * nchunk, D_KV), F32),
        compiler_params=pltpu.CompilerParams(
            dimension_semantics=("arbitrary", "arbitrary"), vmem_limit_bytes=VMEM_LIMIT),
        name="compress",
    )(xc, pe2, wbig, w2bd)


def _nsa_kernel(qt_ref, cmp_ref, ksel_ref, vselt_ref, kwin_ref, vwint_ref, brgt_ref, ag_ref,
                ovt_ref, out_ref, qpad_ref, pen_ref, o_ref, m_ref, l_ref, acc_ref):
    qi = pl.program_id(1)
    q0 = qi * TQ
    nl = Q_PER_KV * TQ
    nblk = pen_ref.shape[1]
    ncmp = cmp_ref.shape[1]
    groups = range(N_KV_HEADS)
    gsl = [slice(g * HEAD_DIM, (g + 1) * HEAD_DIM) for g in groups]

    def tile4(a):
        return jnp.concatenate([a] * Q_PER_KV, axis=1)

    for g in groups:
        zpad = jnp.zeros((HEAD_DIM, TQ), BF16)
        cols = []
        for r in range(Q_PER_KV):
            hq = g * Q_PER_KV + r
            qh = qt_ref[hq * HEAD_DIM:(hq + 1) * HEAD_DIM, :]
            cols.append(jnp.concatenate([qh, zpad] if g == 0 else [zpad, qh], axis=0))
        qpad = jnp.concatenate(cols, axis=1)
        qpad_ref[g] = qpad

        kc = cmp_ref[0].astype(BF16)
        vct = cmp_ref[1].T[gsl[g], :].astype(BF16)
        s = jnp.dot(kc, qpad, preferred_element_type=F32)
        c_row = lax.broadcasted_iota(jnp.int32, (ncmp, TQ), 0)
        t_lane = q0 + lax.broadcasted_iota(jnp.int32, (ncmp, TQ), 1)
        cvis = c_row * CMP_STRIDE + (CMP_BLOCK - 1) <= t_lane
        s = s + tile4(jnp.where(cvis, 0.0, NEG))
        m = jnp.max(s, axis=0, keepdims=True)
        m = jnp.where(m > 0.5 * NEG, m, 0.0)
        e = jnp.exp(s - m)
        den = jnp.sum(e, axis=0, keepdims=True)
        p = e * (1.0 / jnp.where(den > 0.0, den, 1.0))
        o_ref[0, g] = jnp.dot(vct, p.astype(BF16), preferred_element_type=F32)

        psum = p[:, 0:TQ]
        for r in range(1, Q_PER_KV):
            psum = psum + p[:, r * TQ:(r + 1) * TQ]
        imp = jnp.dot(ovt_ref[...], psum, preferred_element_type=F32,
                      precision=lax.Precision.HIGHEST)
        j_row = lax.broadcasted_iota(jnp.int32, (nblk, TQ), 0)
        t_blk = q0 + lax.broadcasted_iota(jnp.int32, (nblk, TQ), 1)
        cur = jnp.right_shift(t_blk, SEL_BLOCK.bit_length() - 1)
        forced = (j_row == 0) | (j_row == cur) | (j_row == cur - 1)
        val = jnp.where(forced, SEL_FORCE, imp)
        val = jnp.where(j_row * SEL_BLOCK <= t_blk, val, -SEL_FORCE)
        rank = jnp.zeros((nblk, TQ), F32)
        for i in range(nblk):
            vi = jnp.broadcast_to(val[i:i + 1, :], (nblk, TQ))
            rank = rank + jnp.where(j_row > i, jnp.where(vi >= val, 1.0, 0.0),
                                    jnp.where(vi > val, 1.0, 0.0))
        chosen = (rank < float(min(SEL_TOPK, nblk))) & (val > -0.5 * SEL_FORCE)
        selpen = jnp.where(chosen, 0.0, NEG)
        for j in range(nblk):
            pen_ref[g, j] = jnp.broadcast_to(selpen[j:j + 1, :], (8, TQ))

    m_ref[...] = jnp.full(m_ref.shape, NEG, F32)
    l_ref[...] = jnp.zeros(l_ref.shape, F32)
    acc_ref[...] = jnp.zeros(acc_ref.shape, F32)
    bpt = KS // SEL_BLOCK
    cpt = KS // VC

    def sel_tile(kt, diagonal):
        k_tile = ksel_ref[pl.ds(pl.multiple_of(kt * KS, KS), KS), :]
        for g in groups:
            s = jnp.dot(k_tile, qpad_ref[g], preferred_element_type=F32)
            m_prev = m_ref[g]
            if diagonal:
                rows = [jnp.concatenate([pen_ref[g, kt * bpt + jj]] * (SEL_BLOCK // 8), axis=0)
                        for jj in range(bpt)]
                pen = jnp.concatenate(rows, axis=0)
                key = kt * KS + lax.broadcasted_iota(jnp.int32, (KS, TQ), 0)
                t_q = q0 + lax.broadcasted_iota(jnp.int32, (KS, TQ), 1)
                s = s + tile4(pen + jnp.where(key <= t_q, 0.0, NEG))
                m_new = jnp.maximum(m_prev, jnp.max(s, axis=0, keepdims=True))
                p = jnp.exp(s - m_new)
            else:
                sb = [s[jj * SEL_BLOCK:(jj + 1) * SEL_BLOCK, :] for jj in range(bpt)]
                bias = [tile4(pen_ref[g, kt * bpt + jj][0:1, :]) for jj in range(bpt)]
                m_new = m_prev
                for jj in range(bpt):
                    m_new = jnp.maximum(m_new, jnp.max(sb[jj], axis=0, keepdims=True) + bias[jj])
                p = jnp.concatenate([jnp.exp(sb[jj] + (bias[jj] - m_new)) for jj in range(bpt)],
                                    axis=0)
            alpha = jnp.exp(m_prev - m_new)
            l_ref[g] = alpha * l_ref[g] + jnp.sum(p, axis=0, keepdims=True)
            vt = jnp.concatenate([vselt_ref[kt * cpt + c, gsl[g], :] for c in range(cpt)], axis=1)
            acc_ref[g] = alpha * acc_ref[g] + jnp.dot(vt, p.astype(BF16),
                                                      preferred_element_type=F32)
            m_ref[g] = m_new

    n_full = q0 // KS

    def sel_body(kt, carry):
        sel_tile(kt, diagonal=False)
        return carry

    lax.fori_loop(0, n_full, sel_body, 0)
    sel_tile(n_full, diagonal=True)
    for g in groups:
        o_ref[1, g] = acc_ref[g] * (1.0 / l_ref[g])

    wrows = WINDOW + TQ
    kt0 = jnp.maximum(qi - WINDOW // KW, 0)
    k_win = kwin_ref[pl.ds(pl.multiple_of(kt0 * KW, KW), wrows), :]
    key = kt0 * KW + lax.broadcasted_iota(jnp.int32, (wrows, TQ), 0)
    t_q = q0 + lax.broadcasted_iota(jnp.int32, (wrows, TQ), 1)
    wpen = tile4(jnp.where(key <= t_q, jnp.where(key > t_q - WINDOW, 0.0, NEG), NEG))
    for g in groups:
        s = jnp.dot(k_win, qpad_ref[g], preferred_element_type=F32) + wpen
        m = jnp.max(s, axis=0, keepdims=True)
        p = jnp.exp(s - m)
        den = jnp.sum(p, axis=0, keepdims=True)
        vt = jnp.concatenate([vwint_ref[kt0 + c, gsl[g], :] for c in range(wrows // KW)], axis=1)
        o_ref[2, g] = jnp.dot(vt, p.astype(BF16), preferred_element_type=F32) * (1.0 / den)

    gates = jax.nn.sigmoid(brgt_ref[...])
    group_out = []
    for g in groups:
        heads = []
        for r in range(Q_PER_KV):
            hq = g * Q_PER_KV + r
            ls = slice(r * TQ, (r + 1) * TQ)
            o = gates[hq:hq + 1, :] * o_ref[0, g, :, ls]
            for n in range(1, N_BRANCH):
                o = o + gates[n * N_Q_HEADS + hq:n * N_Q_HEADS + hq + 1, :] * o_ref[n, g, :, ls]
            heads.append(o)
        group_out.append(jnp.concatenate(heads, axis=0).T)

    attn = jnp.concatenate(group_out, axis=1)
    out_ref[...] = (attn * _silu(ag_ref[...])).astype(out_ref.dtype)


def _nsa(qt, cmp_kv, ksel, vselt, kwin, vwint, brgt, ag, ovt, B, S):
    T = B * S
    nq = S // TQ
    qpm = TM // TQ
    nblk = S // SEL_BLOCK
    ncmp = S // CMP_STRIDE
    nl = Q_PER_KV * TQ
    return pl.pallas_call(
        _nsa_kernel,
        grid=(B, nq),
        in_specs=[
            pl.BlockSpec((None, None, D_ATTN, TQ), lambda b, q: (b, q // qpm, 0, q % qpm)),
            pl.BlockSpec((2, ncmp, D_KV), lambda b, q: (0, b, 0)),
            pl.BlockSpec((S, D_KV), lambda b, q: (b, 0)),
            pl.BlockSpec((None, S // VC, D_KV, VC), lambda b, q: (b, 0, 0, 0)),
            pl.BlockSpec((S, D_KV), lambda b, q: (b, 0)),
            pl.BlockSpec((None, S // VC, D_KV, VC), lambda b, q: (b, 0, 0, 0)),
            pl.BlockSpec((None, None, N_GATE_ROWS, TQ), lambda b, q: (b, q // qpm, 0, q % qpm)),
            pl.BlockSpec((TQ, D_ATTN), lambda b, q: (b * nq + q, 0)),
            pl.BlockSpec(ovt.shape, lambda b, q: (0, 0)),
        ],
        out_specs=pl.BlockSpec((TQ, D_ATTN), lambda b, q: (b * nq + q, 0)),
        out_shape=jax.ShapeDtypeStruct((T, D_ATTN), BF16),
        scratch_shapes=[
            pltpu.VMEM((N_KV_HEADS, D_KV, nl), BF16),
            pltpu.VMEM((N_KV_HEADS, nblk, 8, TQ), F32),
            pltpu.VMEM((N_BRANCH, N_KV_HEADS, HEAD_DIM, nl), F32),
            pltpu.VMEM((N_KV_HEADS, 1, nl), F32),
            pltpu.VMEM((N_KV_HEADS, 1, nl), F32),
            pltpu.VMEM((N_KV_HEADS, HEAD_DIM, nl), F32),
        ],
        compiler_params=pltpu.CompilerParams(
            dimension_semantics=("arbitrary", "arbitrary"), vmem_limit_bytes=VMEM_LIMIT),
        name="nsa",
    )(qt, cmp_kv, ksel, vselt, kwin, vwint, brgt, ag, ovt)


def _outproj_kernel(x_ref, rnn_ref, attn_ref, wo_ref, nfw_ref, out_ref, *, final_norm):
    y = x_ref[...]
    y = y + jnp.dot(rnn_ref[...], wo_ref[0:D_RNN, :], preferred_element_type=F32)
    y = y + jnp.dot(attn_ref[...], wo_ref[D_RNN:D_MIX, :], preferred_element_type=F32)
    if final_norm:
        ms = jnp.mean(y * y, axis=-1, keepdims=True)
        y = (y * lax.rsqrt(ms + EPS)) * nfw_ref[...]
    out_ref[...] = y


def _outproj(x2, rnn_out, attn_out, wo, nfw, final_norm):
    T = x2.shape[0]
    row = lambda i: (i, 0)
    return pl.pallas_call(
        functools.partial(_outproj_kernel, final_norm=final_norm),
        grid=(T // TM,),
        in_specs=[
            pl.BlockSpec((TM, D_MODEL), row),
            pl.BlockSpec((TM, D_RNN), row),
            pl.BlockSpec((TM, D_ATTN), row),
            pl.BlockSpec((D_MIX, D_MODEL), lambda i: (0, 0)),
            pl.BlockSpec((1, D_MODEL), lambda i: (0, 0)),
        ],
        out_specs=pl.BlockSpec((TM, D_MODEL), row),
        out_shape=jax.ShapeDtypeStruct((T, D_MODEL), F32),
        compiler_params=pltpu.CompilerParams(
            dimension_semantics=("arbitrary",), vmem_limit_bytes=VMEM_LIMIT),
        name="outproj",
    )(x2, rnn_out, attn_out, wo, nfw)


def _block_diag_halves(wa, wx):
    eye = jnp.eye(RNN_HEADS, dtype=wa.dtype)
    full = lambda w: jnp.einsum('hij,hk->hikj', w, eye).reshape(D_RNN, D_RNN)
    fa, fx = full(wa), full(wx)
    half = D_RNN // 2
    return jnp.stack([
        jnp.concatenate([fa[s:s + half, s:s + half], fx[s:s + half, s:s + half]], axis=1)
        for s in (0, half)]).astype(BF16)


def _compress_weights(pe, w1, w2):
    eye = jnp.eye(N_KV_HEADS, dtype=w1.dtype)
    w1r = w1.reshape(2, CMP_STRIDE, HEAD_DIM, CMP_HIDDEN)
    wbig = jnp.einsum('hldn,ge->lgdhen', w1r, eye).reshape(
        CMP_STRIDE * D_KV, 2 * N_KV_HEADS * CMP_HIDDEN)
    w2bd = jnp.einsum('nd,ge->gned', w2, eye).reshape(N_KV_HEADS * CMP_HIDDEN, D_KV)
    per = pe.reshape(2, CMP_STRIDE, 1, HEAD_DIM)
    pe2 = jnp.broadcast_to(per, (2, CMP_STRIDE, N_KV_HEADS, HEAD_DIM)).reshape(2, CMP_STRIDE * D_KV)
    return pe2, wbig.astype(BF16), w2bd.astype(BF16)


def _overlap_t(ncmp_pad, nblk):
    cs = np.arange(ncmp_pad)[None, :] * CMP_STRIDE
    ss = np.arange(nblk)[:, None] * SEL_BLOCK
    ov = np.clip(np.minimum(cs + CMP_BLOCK, ss + SEL_BLOCK) - np.maximum(cs, ss), 0, None)
    return jnp.asarray(ov.astype(np.float32) / CMP_BLOCK)


def kernel(x, norm1_w, w_in, conv_w, conv_b, rg_wa, rg_ba, rg_wx, rg_bx, rg_lambda,
           cmp_k_pe, cmp_k_w1, cmp_k_w2, cmp_v_pe, cmp_v_w1, cmp_v_w2, w_out, normf_w):
    B, S, D = x.shape
    assert D == D_MODEL and w_in.shape[-1] == _D_IN
    assert S % TM == 0 and S % TS == 0 and B % CB == 0
    assert KW == TQ == VC and WINDOW % KW == 0 and KS % TQ == 0 and S >= WINDOW + TQ
    depth = w_in.shape[0]
    T = B * S
    ovt = _overlap_t(S // CMP_STRIDE, S // SEL_BLOCK)
    x2 = x.reshape(T, D)
    for l in range(depth):
        w = w_in[l]
        wn = jnp.concatenate([w[:, _O_RX:_O_Q], w[:, _O_KC:_O_KS], w[:, _O_KS:_O_VS],
                              w[:, _O_KW:_O_VW], w[:, _O_AG:_O_BR]], axis=1).astype(BF16)
        wt = jnp.concatenate([w[:, _O_Q:_O_KC], w[:, _O_VS:_O_KW], w[:, _O_VW:_O_AG],
                              w[:, _O_BR:_D_IN],
                              jnp.zeros((D, N_GATE_ROWS - N_BRANCH * N_Q_HEADS), w.dtype)],
                             axis=1).T.astype(BF16)
        (rnn_x, rnn_gate, kvc, ksel, kwin, ag, qt, vselt, vwint, brgt) = _inproj(
            x2, norm1_w[l].reshape(1, D), wn, wt, B, S)

        rnn_out = _rglru(rnn_x, rnn_gate, conv_w[l], conv_b[l].reshape(1, D_RNN),
                         _block_diag_halves(rg_wa[l], rg_wx[l]),
                         rg_ba[l].reshape(1, D_RNN), rg_bx[l].reshape(1, D_RNN),
                         rg_lambda[l].reshape(1, D_RNN), B, S)

        pk, wbk, w2k = _compress_weights(cmp_k_pe[l], cmp_k_w1[l], cmp_k_w2[l])
        pv, wbv, w2v = _compress_weights(cmp_v_pe[l], cmp_v_w1[l], cmp_v_w2[l])
        xc = kvc.reshape(2, T // CMP_STRIDE, CMP_STRIDE * D_KV)
        cmp_kv = _compress(xc, jnp.stack([pk, pv]), jnp.stack([wbk, wbv]),
                           jnp.stack([w2k, w2v]), B, S)

        attn_out = _nsa(qt, cmp_kv, ksel, vselt, kwin, vwint, brgt, ag, ovt, B, S)

        x2 = _outproj(x2, rnn_out, attn_out, w_out[l].astype(BF16),
                      normf_w.reshape(1, D), final_norm=(l == depth - 1))
    return x2.reshape(B, S, D)
```

```python
import functools

import numpy as np
import jax
import jax.numpy as jnp
from jax import lax
from jax.experimental import pallas as pl
from jax.experimental.pallas import tpu as pltpu

F32 = jnp.float32
BF16 = jnp.bfloat16

D_MODEL = 1024
EPS = 1e-6
D_RNN = 512
RNN_HEADS = 8
RNN_HEAD_DIM = D_RNN // RNN_HEADS
CONV_WIDTH = 4
LRU_C = 8.0
N_Q_HEADS = 8
N_KV_HEADS = 2
HEAD_DIM = 64
Q_PER_KV = N_Q_HEADS // N_KV_HEADS
D_ATTN = N_Q_HEADS * HEAD_DIM
D_KV = N_KV_HEADS * HEAD_DIM
CMP_BLOCK = 32
CMP_STRIDE = 16
CMP_HIDDEN = 256
SEL_BLOCK = 64
SEL_TOPK = 8
SEL_FORCE = 1e9
WINDOW = 512
N_BRANCH = 3
D_MIX = D_RNN + D_ATTN
N_GATE_ROWS = 32

TM = 512
TS = 512
TQ = 128
KS = 512
KW = 128
VC = 128
CB = 4
NEG = -1e30
LOG2E = 1.4426950408889634
ONES_ROWS = 16
VMEM_LIMIT = 48 * 1024 * 1024

_O_RX, _O_RG, _O_Q = 0, D_RNN, 2 * D_RNN
_O_KC = _O_Q + D_ATTN
_O_VC = _O_KC + D_KV
_O_KS = _O_VC + D_KV
_O_VS = _O_KS + D_KV
_O_KW = _O_VS + D_KV
_O_VW = _O_KW + D_KV
_O_AG = _O_VW + D_KV
_O_BR = _O_AG + D_ATTN
_D_IN = _O_BR + N_BRANCH * N_Q_HEADS


def _silu(x):
    return x * jax.nn.sigmoid(x)


def _inproj_kernel(x_ref, nw_ref, wn_ref, wt_ref,
                   rnnx_ref, rnng_ref, kvc_ref, ksel_ref, kwin_ref, ag_ref,
                   qt_ref, vselt_ref, vwint_ref, brgt_ref):
    x = x_ref[...]
    ms = jnp.mean(x * x, axis=-1, keepdims=True)
    h = ((x * lax.rsqrt(ms + EPS)) * nw_ref[...]).astype(BF16)

    def nat(a, b):
        return jnp.dot(h, wn_ref[:, a:b], preferred_element_type=F32)

    def tr(a, b):
        return lax.dot_general(wt_ref[a:b, :], h, (((1,), (1,)), ((), ())),
                               preferred_element_type=F32)

    rnnx_ref[...] = nat(0, 512)
    rnng_ref[...] = nat(512, 1024)
    kvc_ref[0] = nat(1024, 1152)
    kvc_ref[1] = nat(1152, 1280)
    ksel_ref[...] = nat(1280, 1408).astype(BF16)
    kwin_ref[...] = nat(1408, 1536).astype(BF16)
    ag_ref[...] = nat(1536, 2048)

    qt_ref[...] = (tr(0, 512) * (HEAD_DIM ** -0.5 * LOG2E)).astype(BF16)
    vs = tr(512, 640).astype(BF16)
    for c in range(TM // VC):
        vselt_ref[c] = vs[:, c * VC:(c + 1) * VC]
    vw = tr(640, 768).astype(BF16)
    for c in range(TM // VC):
        vwint_ref[c] = vw[:, c * VC:(c + 1) * VC]
    brgt_ref[...] = tr(768, 768 + N_GATE_ROWS)


def _inproj(x2, nw, wn, wt, B, S):
    T = B * S
    ns = S // TM
    grid = (B, ns)
    row = lambda b, s: (b * ns + s, 0)
    out_shape = (
        jax.ShapeDtypeStruct((T, D_RNN), F32),
        jax.ShapeDtypeStruct((T, D_RNN), F32),
        jax.ShapeDtypeStruct((2, T, D_KV), F32),
        jax.ShapeDtypeStruct((T, D_KV), BF16),
        jax.ShapeDtypeStruct((T, D_KV), BF16),
        jax.ShapeDtypeStruct((T, D_ATTN), F32),
        jax.ShapeDtypeStruct((B, ns, D_ATTN, TM), BF16),
        jax.ShapeDtypeStruct((B, S // VC, D_KV, VC), BF16),
        jax.ShapeDtypeStruct((B, S // VC, D_KV, VC), BF16),
        jax.ShapeDtypeStruct((B, ns, N_GATE_ROWS, TM), F32),
    )
    out_specs = (
        pl.BlockSpec((TM, D_RNN), row),
        pl.BlockSpec((TM, D_RNN), row),
        pl.BlockSpec((2, TM, D_KV), lambda b, s: (0, b * ns + s, 0)),
        pl.BlockSpec((TM, D_KV), row),
        pl.BlockSpec((TM, D_KV), row),
        pl.BlockSpec((TM, D_ATTN), row),
        pl.BlockSpec((None, None, D_ATTN, TM), lambda b, s: (b, s, 0, 0)),
        pl.BlockSpec((None, TM // VC, D_KV, VC), lambda b, s: (b, s, 0, 0)),
        pl.BlockSpec((None, TM // VC, D_KV, VC), lambda b, s: (b, s, 0, 0)),
        pl.BlockSpec((None, None, N_GATE_ROWS, TM), lambda b, s: (b, s, 0, 0)),
    )
    return pl.pallas_call(
        _inproj_kernel,
        grid=grid,
        in_specs=[
            pl.BlockSpec((TM, D_MODEL), row),
            pl.BlockSpec((1, D_MODEL), lambda b, s: (0, 0)),
            pl.BlockSpec(wn.shape, lambda b, s: (0, 0)),
            pl.BlockSpec(wt.shape, lambda b, s: (0, 0)),
        ],
        out_specs=out_specs,
        out_shape=out_shape,
        compiler_params=pltpu.CompilerParams(
            dimension_semantics=("arbitrary", "arbitrary"), vmem_limit_bytes=VMEM_LIMIT),
        name="inproj",
    )(x2, nw, wn, wt)


def _rglru_kernel(x_ref, gate_ref, cw_ref, cb_ref, wg_ref, ba_ref, bx_ref, lam_ref,
                  out_ref, cbuf, a_s, u_s, h_s, hlast):
    sc = pl.program_id(1)

    @pl.when(sc == 0)
    def _():
        cbuf[0:8, :] = jnp.zeros((8, D_RNN), F32)
        hlast[...] = jnp.zeros((1, D_RNN), F32)

    x = x_ref[...]
    cbuf[8:8 + TS, :] = x
    y = cb_ref[...]
    for k in range(CONV_WIDTH):
        off = 8 - (CONV_WIDTH - 1) + k
        y = y + cbuf[off:off + TS, :] * cw_ref[k:k + 1, :]
    cbuf[0:8, :] = x[TS - 8:TS, :]

    yb = y.astype(BF16)
    half = D_RNN // 2
    pre = [jnp.dot(yb[:, hh * half:(hh + 1) * half], wg_ref[hh], preferred_element_type=F32)
           for hh in range(2)]
    pre_a = jnp.concatenate([pre[0][:, :half], pre[1][:, :half]], axis=1)
    pre_x = jnp.concatenate([pre[0][:, half:], pre[1][:, half:]], axis=1)
    r = jax.nn.sigmoid(pre_a + ba_ref[...])
    i = jax.nn.sigmoid(pre_x + bx_ref[...])
    lam = lam_ref[...]
    lsig = jnp.minimum(lam, 0.0) - jnp.log1p(jnp.exp(-jnp.abs(lam)))
    log_a = LRU_C * r * lsig
    a = jnp.exp(log_a)
    u = jnp.sqrt(1.0 - a * a) * (i * y)

    rowmod = lax.broadcasted_iota(jnp.int32, (TS, D_RNN), 0) & 7
    for d in (1, 2, 4):
        a_sh = pltpu.roll(a, d, 0)
        u_sh = pltpu.roll(u, d, 0)
        valid = rowmod >= d
        u = jnp.where(valid, a * u_sh + u, u)
        a = jnp.where(valid, a * a_sh, a)
    a_s[...] = a
    u_s[...] = u

    def body(j, carry):
        blk = pl.multiple_of(j * 8, 8)
        hb = u_s[pl.ds(blk, 8), :] + a_s[pl.ds(blk, 8), :] * carry
        h_s[pl.ds(blk, 8), :] = hb
        return hb[7:8, :]

    hlast[...] = lax.fori_loop(0, TS // 8, body, hlast[...], unroll=8)
    out_ref[...] = (h_s[...] * _silu(gate_ref[...])).astype(out_ref.dtype)


def _rglru(rnn_x, rnn_gate, cw, cb, wg, ba, bx, lam, B, S):
    T = B * S
    ns = S // TS
    row = lambda b, s: (b * ns + s, 0)
    const2 = lambda b, s: (0, 0)
    return pl.pallas_call(
        _rglru_kernel,
        grid=(B, ns),
        in_specs=[
            pl.BlockSpec((TS, D_RNN), row),
            pl.BlockSpec((TS, D_RNN), row),
            pl.BlockSpec((CONV_WIDTH, D_RNN), const2),
            pl.BlockSpec((1, D_RNN), const2),
            pl.BlockSpec(wg.shape, lambda b, s: (0, 0, 0)),
            pl.BlockSpec((1, D_RNN), const2),
            pl.BlockSpec((1, D_RNN), const2),
            pl.BlockSpec((1, D_RNN), const2),
        ],
        out_specs=pl.BlockSpec((TS, D_RNN), row),
        out_shape=jax.ShapeDtypeStruct((T, D_RNN), BF16),
        scratch_shapes=[
            pltpu.VMEM((TS + 8, D_RNN), F32),
            pltpu.VMEM((TS, D_RNN), F32),
            pltpu.VMEM((TS, D_RNN), F32),
            pltpu.VMEM((TS, D_RNN), F32),
            pltpu.VMEM((1, D_RNN), F32),
        ],
        compiler_params=pltpu.CompilerParams(
            dimension_semantics=("arbitrary", "arbitrary"), vmem_limit_bytes=VMEM_LIMIT),
        name="rglru",
    )(rnn_x, rnn_gate, cw, cb, wg, ba, bx, lam)


def _compress_kernel(x_ref, pe_ref, wbig_ref, w2_ref, out_ref):
    nrow = x_ref.shape[1]
    nchunk = nrow // CB
    x = x_ref[0]
    xa = (x + pe_ref[0, 0:1, :]).astype(BF16)
    xb = (x + pe_ref[0, 1:2, :]).astype(BF16)
    nh = N_KV_HEADS * CMP_HIDDEN
    first = jnp.dot(xa, wbig_ref[0, :, :nh], preferred_element_type=F32)
    second = jnp.dot(xb, wbig_ref[0, :, nh:], preferred_element_type=F32)
    hid = _silu(first + pltpu.roll(second, nrow - 1, 0))
    out = jnp.dot(hid.astype(BF16), w2_ref[0], preferred_element_type=F32)
    c_idx = lax.broadcasted_iota(jnp.int32, out.shape, 0) & (nchunk - 1)
    out_ref[0] = jnp.where(c_idx < nchunk - 1, out, 0.0)


def _compress(xc, pe2, wbig, w2bd, B, S):
    nchunk = S // CMP_STRIDE
    nrow = CB * nchunk
    width = CMP_STRIDE * D_KV
    kv = lambda i, j: (i, 0, 0)
    return pl.pallas_call(
        _compress_kernel,
        grid=(2, B // CB),
        in_specs=[
            pl.BlockSpec((1, nrow, width), lambda i, j: (i, j, 0)),
            pl.BlockSpec((1, 2, width), kv),
            pl.BlockSpec((1,) + wbig.shape[1:], kv),
            pl.BlockSpec((1,) + w2bd.shape[1:], kv),
        ],
        out_specs=pl.BlockSpec((1, nrow, D_KV), lambda i, j: (i, j, 0)),
        out_shape=jax.ShapeDtypeStruct((2, B * nchunk, D_KV), F32),
        compiler_params=pltpu.CompilerParams(
            dimension_semantics=("arbitrary", "arbitrary"), vmem_limit_bytes=VMEM_LIMIT),
        name="compress",
    )(xc, pe2, wbig, w2bd)


def _nsa_kernel(qt_ref, cmp_ref, ksel_ref, vselt_ref, kwin_ref, vwint_ref, brgt_ref, ag_ref,
                ovt_ref, out_ref, qpad_ref, pen_ref, o_ref, m_ref, acc_ref):
    qi = pl.program_id(1)
    q0 = qi * TQ
    nl = Q_PER_KV * TQ
    nblk = pen_ref.shape[1]
    ncmp = cmp_ref.shape[1]
    groups = range(N_KV_HEADS)
    gsl = [slice(g * HEAD_DIM, (g + 1) * HEAD_DIM) for g in groups]

    def tile4(a):
        return jnp.concatenate([a] * Q_PER_KV, axis=1)

    for g in groups:
        zpad = jnp.zeros((HEAD_DIM, TQ), BF16)
        cols = []
        for r in range(Q_PER_KV):
            hq = g * Q_PER_KV + r
            qh = qt_ref[hq * HEAD_DIM:(hq + 1) * HEAD_DIM, :]
            cols.append(jnp.concatenate([qh, zpad] if g == 0 else [zpad, qh], axis=0))
        qpad = jnp.concatenate(cols, axis=1)
        qpad_ref[g] = qpad

        kc = cmp_ref[0].astype(BF16)
        vct = cmp_ref[1].T[gsl[g], :].astype(BF16)
        s = jnp.dot(kc, qpad, preferred_element_type=F32)
        c_row = lax.broadcasted_iota(jnp.int32, (ncmp, TQ), 0)
        t_lane = q0 + lax.broadcasted_iota(jnp.int32, (ncmp, TQ), 1)
        cvis = c_row * CMP_STRIDE + (CMP_BLOCK - 1) <= t_lane
        s = s + tile4(jnp.where(cvis, 0.0, NEG))
        m = jnp.max(s, axis=0, keepdims=True)
        m = jnp.where(m > 0.5 * NEG, m, 0.0)
        e = jnp.exp2(s - m)
        den = jnp.sum(e, axis=0, keepdims=True)
        p = e * (1.0 / jnp.where(den > 0.0, den, 1.0))
        o_ref[0, g] = jnp.dot(vct, p.astype(BF16), preferred_element_type=F32)

        psum = p[:, 0:TQ]
        for r in range(1, Q_PER_KV):
            psum = psum + p[:, r * TQ:(r + 1) * TQ]
        imp = jnp.dot(ovt_ref[...], psum, preferred_element_type=F32,
                      precision=lax.Precision.HIGHEST)
        j_row = lax.broadcasted_iota(jnp.int32, (nblk, TQ), 0)
        t_blk = q0 + lax.broadcasted_iota(jnp.int32, (nblk, TQ), 1)
        cur = jnp.right_shift(t_blk, SEL_BLOCK.bit_length() - 1)
        forced = (j_row == 0) | (j_row == cur) | (j_row == cur - 1)
        val = jnp.where(forced, SEL_FORCE, imp)
        val = jnp.where(j_row * SEL_BLOCK <= t_blk, val, -SEL_FORCE)
        rank = jnp.zeros((nblk, TQ), F32)
        for i in range(nblk):
            vi = jnp.broadcast_to(val[i:i + 1, :], (nblk, TQ))
            rank = rank + jnp.where(j_row > i, jnp.where(vi >= val, 1.0, 0.0),
                                    jnp.where(vi > val, 1.0, 0.0))
        chosen = (rank < float(min(SEL_TOPK, nblk))) & (val > -0.5 * SEL_FORCE)
        selpen = jnp.where(chosen, 0.0, NEG)
        for j in range(nblk):
            pen_ref[g, j] = jnp.broadcast_to(selpen[j:j + 1, :], (8, TQ))

    m_ref[...] = jnp.full(m_ref.shape, NEG, F32)
    acc_ref[...] = jnp.zeros(acc_ref.shape, F32)
    ones = jnp.ones((ONES_ROWS, VC), BF16)

    def vt_aug(v_ref, c0, n, g):
        vt = jnp.concatenate([v_ref[c0 + c, gsl[g], :] for c in range(n)], axis=1)
        return jnp.concatenate([vt, jnp.concatenate([ones] * n, axis=1)], axis=0)

    def normalise(res):
        return res[0:HEAD_DIM, :] * (1.0 / res[HEAD_DIM:HEAD_DIM + 1, :])
    bpt = KS // SEL_BLOCK
    cpt = KS // VC

    def sel_tile(kt, diagonal):
        k_tile = ksel_ref[pl.ds(pl.multiple_of(kt * KS, KS), KS), :]
        for g in groups:
            s = jnp.dot(k_tile, qpad_ref[g], preferred_element_type=F32)
            m_prev = m_ref[g]
            if diagonal:
                rows = [jnp.concatenate([pen_ref[g, kt * bpt + jj]] * (SEL_BLOCK // 8), axis=0)
                        for jj in range(bpt)]
                pen = jnp.concatenate(rows, axis=0)
                key = kt * KS + lax.broadcasted_iota(jnp.int32, (KS, TQ), 0)
                t_q = q0 + lax.broadcasted_iota(jnp.int32, (KS, TQ), 1)
                s = s + tile4(pen + jnp.where(key <= t_q, 0.0, NEG))
                m_new = jnp.maximum(m_prev, jnp.max(s, axis=0, keepdims=True))
                p = jnp.exp2(s - m_new)
            else:
                sb = [s[jj * SEL_BLOCK:(jj + 1) * SEL_BLOCK, :] for jj in range(bpt)]
                bias = [tile4(pen_ref[g, kt * bpt + jj][0:1, :]) for jj in range(bpt)]
                m_new = m_prev
                for jj in range(bpt):
                    m_new = jnp.maximum(m_new, jnp.max(sb[jj], axis=0, keepdims=True) + bias[jj])
                p = jnp.concatenate([jnp.exp2(sb[jj] + (bias[jj] - m_new)) for jj in range(bpt)],
                                    axis=0)
            alpha = jnp.exp2(m_prev - m_new)
            acc_ref[g] = alpha * acc_ref[g] + jnp.dot(vt_aug(vselt_ref, kt * cpt, cpt, g),
                                                      p.astype(BF16), preferred_element_type=F32)
            m_ref[g] = m_new

    n_full = q0 // KS

    def sel_body(kt, carry):
        sel_tile(kt, diagonal=False)
        return carry

    lax.fori_loop(0, n_full, sel_body, 0)
    sel_tile(n_full, diagonal=True)
    for g in groups:
        o_ref[1, g] = normalise(acc_ref[g])

    wrows = WINDOW + TQ
    kt0 = jnp.maximum(qi - WINDOW // KW, 0)
    k_win = kwin_ref[pl.ds(pl.multiple_of(kt0 * KW, KW), wrows), :]
    key = kt0 * KW + lax.broadcasted_iota(jnp.int32, (wrows, TQ), 0)
    t_q = q0 + lax.broadcasted_iota(jnp.int32, (wrows, TQ), 1)
    wpen = tile4(jnp.where(key <= t_q, jnp.where(key > t_q - WINDOW, 0.0, NEG), NEG))
    for g in groups:
        s = jnp.dot(k_win, qpad_ref[g], preferred_element_type=F32) + wpen
        m = jnp.max(s, axis=0, keepdims=True)
        p = jnp.exp2(s - m).astype(BF16)
        o_ref[2, g] = normalise(jnp.dot(vt_aug(vwint_ref, kt0, wrows // VC, g), p,
                                        preferred_element_type=F32))

    gates = jax.nn.sigmoid(brgt_ref[...])
    group_out = []
    for g in groups:
        heads = []
        for r in range(Q_PER_KV):
            hq = g * Q_PER_KV + r
            ls = slice(r * TQ, (r + 1) * TQ)
            o = gates[hq:hq + 1, :] * o_ref[0, g, :, ls]
            for n in range(1, N_BRANCH):
                o = o + gates[n * N_Q_HEADS + hq:n * N_Q_HEADS + hq + 1, :] * o_ref[n, g, :, ls]
            heads.append(o)
        group_out.append(jnp.concatenate(heads, axis=0).T)

    attn = jnp.concatenate(group_out, axis=1)
    out_ref[...] = (attn * _silu(ag_ref[...])).astype(out_ref.dtype)


def _nsa(qt, cmp_kv, ksel, vselt, kwin, vwint, brgt, ag, ovt, B, S):
    T = B * S
    nq = S // TQ
    qpm = TM // TQ
    nblk = S // SEL_BLOCK
    ncmp = S // CMP_STRIDE
    nl = Q_PER_KV * TQ
    return pl.pallas_call(
        _nsa_kernel,
        grid=(B, nq),
        in_specs=[
            pl.BlockSpec((None, None, D_ATTN, TQ), lambda b, q: (b, q // qpm, 0, q % qpm)),
            pl.BlockSpec((2, ncmp, D_KV), lambda b, q: (0, b, 0)),
            pl.BlockSpec((S, D_KV), lambda b, q: (b, 0)),
            pl.BlockSpec((None, S // VC, D_KV, VC), lambda b, q: (b, 0, 0, 0)),
            pl.BlockSpec((S, D_KV), lambda b, q: (b, 0)),
            pl.BlockSpec((None, S // VC, D_KV, VC), lambda b, q: (b, 0, 0, 0)),
            pl.BlockSpec((None, None, N_GATE_ROWS, TQ), lambda b, q: (b, q // qpm, 0, q % qpm)),
            pl.BlockSpec((TQ, D_ATTN), lambda b, q: (b * nq + q, 0)),
            pl.BlockSpec(ovt.shape, lambda b, q: (0, 0)),
        ],
        out_specs=pl.BlockSpec((TQ, D_ATTN), lambda b, q: (b * nq + q, 0)),
        out_shape=jax.ShapeDtypeStruct((T, D_ATTN), BF16),
        scratch_shapes=[
            pltpu.VMEM((N_KV_HEADS, D_KV, nl), BF16),
            pltpu.VMEM((N_KV_HEADS, nblk, 8, TQ), F32),
            pltpu.VMEM((N_BRANCH, N_KV_HEADS, HEAD_DIM, nl), F32),
            pltpu.VMEM((N_KV_HEADS, 1, nl), F32),
            pltpu.VMEM((N_KV_HEADS, HEAD_DIM + ONES_ROWS, nl), F32),
        ],
        compiler_params=pltpu.CompilerParams(
            dimension_semantics=("arbitrary", "arbitrary"), vmem_limit_bytes=VMEM_LIMIT),
        name="nsa",
    )(qt, cmp_kv, ksel, vselt, kwin, vwint, brgt, ag, ovt)


def _outproj_kernel(x_ref, rnn_ref, attn_ref, wo_ref, nfw_ref, out_ref, *, final_norm):
    y = x_ref[...]
    y = y + jnp.dot(rnn_ref[...], wo_ref[0:D_RNN, :], preferred_element_type=F32)
    y = y + jnp.dot(attn_ref[...], wo_ref[D_RNN:D_MIX, :], preferred_element_type=F32)
    if final_norm:
        ms = jnp.mean(y * y, axis=-1, keepdims=True)
        y = (y * lax.rsqrt(ms + EPS)) * nfw_ref[...]
    out_ref[...] = y


def _outproj(x2, rnn_out, attn_out, wo, nfw, final_norm):
    T = x2.shape[0]
    row = lambda i: (i, 0)
    return pl.pallas_call(
        functools.partial(_outproj_kernel, final_norm=final_norm),
        grid=(T // TM,),
        in_specs=[
            pl.BlockSpec((TM, D_MODEL), row),
            pl.BlockSpec((TM, D_RNN), row),
            pl.BlockSpec((TM, D_ATTN), row),
            pl.BlockSpec((D_MIX, D_MODEL), lambda i: (0, 0)),
            pl.BlockSpec((1, D_MODEL), lambda i: (0, 0)),
        ],
        out_specs=pl.BlockSpec((TM, D_MODEL), row),
        out_shape=jax.ShapeDtypeStruct((T, D_MODEL), F32),
        compiler_params=pltpu.CompilerParams(
            dimension_semantics=("arbitrary",), vmem_limit_bytes=VMEM_LIMIT),
        name="outproj",
    )(x2, rnn_out, attn_out, wo, nfw)


def _block_diag_halves(wa, wx):
    eye = jnp.eye(RNN_HEADS, dtype=wa.dtype)
    full = lambda w: jnp.einsum('hij,hk->hikj', w, eye).reshape(D_RNN, D_RNN)
    fa, fx = full(wa), full(wx)
    half = D_RNN // 2
    return jnp.stack([
        jnp.concatenate([fa[s:s + half, s:s + half], fx[s:s + half, s:s + half]], axis=1)
        for s in (0, half)]).astype(BF16)


def _compress_weights(pe, w1, w2):
    eye = jnp.eye(N_KV_HEADS, dtype=w1.dtype)
    w1r = w1.reshape(2, CMP_STRIDE, HEAD_DIM, CMP_HIDDEN)
    wbig = jnp.einsum('hldn,ge->lgdhen', w1r, eye).reshape(
        CMP_STRIDE * D_KV, 2 * N_KV_HEADS * CMP_HIDDEN)
    w2bd = jnp.einsum('nd,ge->gned', w2, eye).reshape(N_KV_HEADS * CMP_HIDDEN, D_KV)
    per = pe.reshape(2, CMP_STRIDE, 1, HEAD_DIM)
    pe2 = jnp.broadcast_to(per, (2, CMP_STRIDE, N_KV_HEADS, HEAD_DIM)).reshape(2, CMP_STRIDE * D_KV)
    return pe2, wbig.astype(BF16), w2bd.astype(BF16)


def _overlap_t(ncmp_pad, nblk):
    cs = np.arange(ncmp_pad)[None, :] * CMP_STRIDE
    ss = np.arange(nblk)[:, None] * SEL_BLOCK
    ov = np.clip(np.minimum(cs + CMP_BLOCK, ss + SEL_BLOCK) - np.maximum(cs, ss), 0, None)
    return jnp.asarray(ov.astype(np.float32) / CMP_BLOCK)


def kernel(x, norm1_w, w_in, conv_w, conv_b, rg_wa, rg_ba, rg_wx, rg_bx, rg_lambda,
           cmp_k_pe, cmp_k_w1, cmp_k_w2, cmp_v_pe, cmp_v_w1, cmp_v_w2, w_out, normf_w):
    B, S, D = x.shape
    assert D == D_MODEL and w_in.shape[-1] == _D_IN
    assert S % TM == 0 and S % TS == 0 and B % CB == 0
    assert KW == TQ == VC and WINDOW % KW == 0 and KS % TQ == 0 and S >= WINDOW + TQ
    depth = w_in.shape[0]
    T = B * S
    ovt = _overlap_t(S // CMP_STRIDE, S // SEL_BLOCK)
    x2 = x.reshape(T, D)
    for l in range(depth):
        w = w_in[l]
        wn = jnp.concatenate([w[:, _O_RX:_O_Q], w[:, _O_KC:_O_KS], w[:, _O_KS:_O_VS],
                              w[:, _O_KW:_O_VW], w[:, _O_AG:_O_BR]], axis=1).astype(BF16)
        wt = jnp.concatenate([w[:, _O_Q:_O_KC], w[:, _O_VS:_O_KW], w[:, _O_VW:_O_AG],
                              w[:, _O_BR:_D_IN],
                              jnp.zeros((D, N_GATE_ROWS - N_BRANCH * N_Q_HEADS), w.dtype)],
                             axis=1).T.astype(BF16)
        (rnn_x, rnn_gate, kvc, ksel, kwin, ag, qt, vselt, vwint, brgt) = _inproj(
            x2, norm1_w[l].reshape(1, D), wn, wt, B, S)

        rnn_out = _rglru(rnn_x, rnn_gate, conv_w[l], conv_b[l].reshape(1, D_RNN),
                         _block_diag_halves(rg_wa[l], rg_wx[l]),
                         rg_ba[l].reshape(1, D_RNN), rg_bx[l].reshape(1, D_RNN),
                         rg_lambda[l].reshape(1, D_RNN), B, S)

        pk, wbk, w2k = _compress_weights(cmp_k_pe[l], cmp_k_w1[l], cmp_k_w2[l])
        pv, wbv, w2v = _compress_weights(cmp_v_pe[l], cmp_v_w1[l], cmp_v_w2[l])
        xc = kvc.reshape(2, T // CMP_STRIDE, CMP_STRIDE * D_KV)
        cmp_kv = _compress(xc, jnp.stack([pk, pv]), jnp.stack([wbk, wbv]),
                           jnp.stack([w2k, w2v]), B, S)

        attn_out = _nsa(qt, cmp_kv, ksel, vselt, kwin, vwint, brgt, ag, ovt, B, S)

        x2 = _outproj(x2, rnn_out, attn_out, w_out[l].astype(BF16),
                      normf_w.reshape(1, D), final_norm=(l == depth - 1))
    return x2.reshape(B, S, D)
```

```python
import functools

import numpy as np
import jax
import jax.numpy as jnp
from jax import lax
from jax.experimental import pallas as pl
from jax.experimental.pallas import tpu as pltpu

F32 = jnp.float32
BF16 = jnp.bfloat16

D_MODEL = 1024
EPS = 1e-6
D_RNN = 512
RNN_HEADS = 8
RNN_HEAD_DIM = D_RNN // RNN_HEADS
CONV_WIDTH = 4
LRU_C = 8.0
N_Q_HEADS = 8
N_KV_HEADS = 2
HEAD_DIM = 64
Q_PER_KV = N_Q_HEADS // N_KV_HEADS
D_ATTN = N_Q_HEADS * HEAD_DIM
D_KV = N_KV_HEADS * HEAD_DIM
CMP_BLOCK = 32
CMP_STRIDE = 16
CMP_HIDDEN = 256
SEL_BLOCK = 64
SEL_TOPK = 8
SEL_FORCE = 1e9
WINDOW = 512
N_BRANCH = 3
D_MIX = D_RNN + D_ATTN
N_GATE_ROWS = 32

TM = 512
TS = 512
TQ = 128
KS = 512
KW = 128
VC = 128
CB = 4
NEG = -1e30
LOG2E = 1.4426950408889634
ONES_ROWS = 16
VMEM_LIMIT = 48 * 1024 * 1024

_O_RX, _O_RG, _O_Q = 0, D_RNN, 2 * D_RNN
_O_KC = _O_Q + D_ATTN
_O_VC = _O_KC + D_KV
_O_KS = _O_VC + D_KV
_O_VS = _O_KS + D_KV
_O_KW = _O_VS + D_KV
_O_VW = _O_KW + D_KV
_O_AG = _O_VW + D_KV
_O_BR = _O_AG + D_ATTN
_D_IN = _O_BR + N_BRANCH * N_Q_HEADS


def _silu(x):
    return x * jax.nn.sigmoid(x)


def _inproj_kernel(x_ref, nw_ref, wn_ref, wt_ref,
                   rnnx_ref, rnng_ref, kvc_ref, ksel_ref, kwin_ref, ag_ref,
                   qt_ref, vselt_ref, vwint_ref, brgt_ref):
    x = x_ref[...]
    ms = jnp.mean(x * x, axis=-1, keepdims=True)
    h = ((x * lax.rsqrt(ms + EPS)) * nw_ref[...]).astype(BF16)

    def nat(a, b):
        return jnp.dot(h, wn_ref[:, a:b], preferred_element_type=F32)

    def tr(a, b):
        return lax.dot_general(wt_ref[a:b, :], h, (((1,), (1,)), ((), ())),
                               preferred_element_type=F32)

    rnnx_ref[...] = nat(0, 512)
    rnng_ref[...] = nat(512, 1024)
    kvc_ref[0] = nat(1024, 1152)
    kvc_ref[1] = nat(1152, 1280)
    ksel_ref[...] = nat(1280, 1408).astype(BF16)
    kwin_ref[...] = nat(1408, 1536).astype(BF16)
    ag_ref[...] = nat(1536, 2048)

    qt_ref[...] = (tr(0, 512) * (HEAD_DIM ** -0.5 * LOG2E)).astype(BF16)
    vs = tr(512, 640).astype(BF16)
    for c in range(TM // VC):
        vselt_ref[c] = vs[:, c * VC:(c + 1) * VC]
    vw = tr(640, 768).astype(BF16)
    for c in range(TM // VC):
        vwint_ref[c] = vw[:, c * VC:(c + 1) * VC]
    brgt_ref[...] = tr(768, 768 + N_GATE_ROWS)


def _inproj(x2, nw, wn, wt, B, S):
    T = B * S
    ns = S // TM
    grid = (B, ns)
    row = lambda b, s: (b * ns + s, 0)
    out_shape = (
        jax.ShapeDtypeStruct((T, D_RNN), F32),
        jax.ShapeDtypeStruct((T, D_RNN), F32),
        jax.ShapeDtypeStruct((2, T, D_KV), F32),
        jax.ShapeDtypeStruct((T, D_KV), BF16),
        jax.ShapeDtypeStruct((T, D_KV), BF16),
        jax.ShapeDtypeStruct((T, D_ATTN), F32),
        jax.ShapeDtypeStruct((B, ns, D_ATTN, TM), BF16),
        jax.ShapeDtypeStruct((B, S // VC, D_KV, VC), BF16),
        jax.ShapeDtypeStruct((B, S // VC, D_KV, VC), BF16),
        jax.ShapeDtypeStruct((B, ns, N_GATE_ROWS, TM), F32),
    )
    out_specs = (
        pl.BlockSpec((TM, D_RNN), row),
        pl.BlockSpec((TM, D_RNN), row),
        pl.BlockSpec((2, TM, D_KV), lambda b, s: (0, b * ns + s, 0)),
        pl.BlockSpec((TM, D_KV), row),
        pl.BlockSpec((TM, D_KV), row),
        pl.BlockSpec((TM, D_ATTN), row),
        pl.BlockSpec((None, None, D_ATTN, TM), lambda b, s: (b, s, 0, 0)),
        pl.BlockSpec((None, TM // VC, D_KV, VC), lambda b, s: (b, s, 0, 0)),
        pl.BlockSpec((None, TM // VC, D_KV, VC), lambda b, s: (b, s, 0, 0)),
        pl.BlockSpec((None, None, N_GATE_ROWS, TM), lambda b, s: (b, s, 0, 0)),
    )
    return pl.pallas_call(
        _inproj_kernel,
        grid=grid,
        in_specs=[
            pl.BlockSpec((TM, D_MODEL), row),
            pl.BlockSpec((1, D_MODEL), lambda b, s: (0, 0)),
            pl.BlockSpec(wn.shape, lambda b, s: (0, 0)),
            pl.BlockSpec(wt.shape, lambda b, s: (0, 0)),
        ],
        out_specs=out_specs,
        out_shape=out_shape,
        compiler_params=pltpu.CompilerParams(
            dimension_semantics=("arbitrary", "arbitrary"), vmem_limit_bytes=VMEM_LIMIT),
        name="inproj",
    )(x2, nw, wn, wt)


def _rglru_kernel(x_ref, gate_ref, cw_ref, cb_ref, wg_ref, ba_ref, bx_ref, lam_ref,
                  out_ref, cbuf, a_s, u_s, h_s, hlast):
    sc = pl.program_id(1)

    @pl.when(sc == 0)
    def _():
        cbuf[0:8, :] = jnp.zeros((8, D_RNN), F32)
        hlast[...] = jnp.zeros((1, D_RNN), F32)

    x = x_ref[...]
    cbuf[8:8 + TS, :] = x
    y = cb_ref[...]
    for k in range(CONV_WIDTH):
        off = 8 - (CONV_WIDTH - 1) + k
        y = y + cbuf[off:off + TS, :] * cw_ref[k:k + 1, :]
    cbuf[0:8, :] = x[TS - 8:TS, :]

    yb = y.astype(BF16)
    half = D_RNN // 2
    pre = [jnp.dot(yb[:, hh * half:(hh + 1) * half], wg_ref[hh], preferred_element_type=F32)
           for hh in range(2)]
    pre_a = jnp.concatenate([pre[0][:, :half], pre[1][:, :half]], axis=1)
    pre_x = jnp.concatenate([pre[0][:, half:], pre[1][:, half:]], axis=1)
    r = jax.nn.sigmoid(pre_a + ba_ref[...])
    i = jax.nn.sigmoid(pre_x + bx_ref[...])
    lam = lam_ref[...]
    lsig = jnp.minimum(lam, 0.0) - jnp.log1p(jnp.exp(-jnp.abs(lam)))
    log_a = LRU_C * r * lsig
    a = jnp.exp(log_a)
    u = jnp.sqrt(1.0 - a * a) * (i * y)

    rowmod = lax.broadcasted_iota(jnp.int32, (TS, D_RNN), 0) & 7
    for d in (1, 2, 4):
        a_sh = pltpu.roll(a, d, 0)
        u_sh = pltpu.roll(u, d, 0)
        valid = rowmod >= d
        u = jnp.where(valid, a * u_sh + u, u)
        a = jnp.where(valid, a * a_sh, a)
    a_s[...] = a
    u_s[...] = u

    def body(j, carry):
        blk = pl.multiple_of(j * 8, 8)
        hb = u_s[pl.ds(blk, 8), :] + a_s[pl.ds(blk, 8), :] * carry
        h_s[pl.ds(blk, 8), :] = hb
        return hb[7:8, :]

    hlast[...] = lax.fori_loop(0, TS // 8, body, hlast[...], unroll=8)
    out_ref[...] = (h_s[...] * _silu(gate_ref[...])).astype(out_ref.dtype)


def _rglru(rnn_x, rnn_gate, cw, cb, wg, ba, bx, lam, B, S):
    T = B * S
    ns = S // TS
    row = lambda b, s: (b * ns + s, 0)
    const2 = lambda b, s: (0, 0)
    return pl.pallas_call(
        _rglru_kernel,
        grid=(B, ns),
        in_specs=[
            pl.BlockSpec((TS, D_RNN), row),
            pl.BlockSpec((TS, D_RNN), row),
            pl.BlockSpec((CONV_WIDTH, D_RNN), const2),
            pl.BlockSpec((1, D_RNN), const2),
            pl.BlockSpec(wg.shape, lambda b, s: (0, 0, 0)),
            pl.BlockSpec((1, D_RNN), const2),
            pl.BlockSpec((1, D_RNN), const2),
            pl.BlockSpec((1, D_RNN), const2),
        ],
        out_specs=pl.BlockSpec((TS, D_RNN), row),
        out_shape=jax.ShapeDtypeStruct((T, D_RNN), BF16),
        scratch_shapes=[
            pltpu.VMEM((TS + 8, D_RNN), F32),
            pltpu.VMEM((TS, D_RNN), F32),
            pltpu.VMEM((TS, D_RNN), F32),
            pltpu.VMEM((TS, D_RNN), F32),
            pltpu.VMEM((1, D_RNN), F32),
        ],
        compiler_params=pltpu.CompilerParams(
            dimension_semantics=("arbitrary", "arbitrary"), vmem_limit_bytes=VMEM_LIMIT),
        name="rglru",
    )(rnn_x, rnn_gate, cw, cb, wg, ba, bx, lam)


def _compress_kernel(x_ref, pe_ref, wbig_ref, w2_ref, out_ref):
    nrow = x_ref.shape[1]
    nchunk = nrow // CB
    x = x_ref[0]
    xa = (x + pe_ref[0, 0:1, :]).astype(BF16)
    xb = (x + pe_ref[0, 1:2, :]).astype(BF16)
    nh = N_KV_HEADS * CMP_HIDDEN
    first = jnp.dot(xa, wbig_ref[0, :, :nh], preferred_element_type=F32)
    second = jnp.dot(xb, wbig_ref[0, :, nh:], preferred_element_type=F32)
    hid = _silu(first + pltpu.roll(second, nrow - 1, 0))
    out = jnp.dot(hid.astype(BF16), w2_ref[0], preferred_element_type=F32)
    c_idx = lax.broadcasted_iota(jnp.int32, out.shape, 0) & (nchunk - 1)
    out_ref[0] = jnp.where(c_idx < nchunk - 1, out, 0.0)


def _compress(xc, pe2, wbig, w2bd, B, S):
    nchunk = S // CMP_STRIDE
    nrow = CB * nchunk
    width = CMP_STRIDE * D_KV
    kv = lambda i, j: (i, 0, 0)
    return pl.pallas_call(
        _compress_kernel,
        grid=(2, B // CB),
        in_specs=[
            pl.BlockSpec((1, nrow, width), lambda i, j: (i, j, 0)),
            pl.BlockSpec((1, 2, width), kv),
            pl.BlockSpec((1,) + wbig.shape[1:], kv),
            pl.BlockSpec((1,) + w2bd.shape[1:], kv),
        ],
        out_specs=pl.BlockSpec((1, nrow, D_KV), lambda i, j: (i, j, 0)),
        out_shape=jax.ShapeDtypeStruct((2, B * nchunk, D_KV), F32),
        compiler_params=pltpu.CompilerParams(
            dimension_semantics=("arbitrary", "arbitrary"), vmem_limit_bytes=VMEM_LIMIT),
        name="compress",
    )(xc, pe2, wbig, w2bd)


def _nsa_kernel(qt_ref, cmp_ref, ksel_ref, vselt_ref, kwin_ref, vwint_ref, brgt_ref, ag_ref,
                ovt_ref, out_ref, qpad_ref, pen_ref, o_ref, m_ref, acc_ref, s_ref):
    qi = pl.program_id(1)
    q0 = qi * TQ
    nl = Q_PER_KV * TQ
    nblk = pen_ref.shape[1]
    ncmp = cmp_ref.shape[1]
    groups = range(N_KV_HEADS)
    gsl = [slice(g * HEAD_DIM, (g + 1) * HEAD_DIM) for g in groups]

    def tile4(a):
        return jnp.concatenate([a] * Q_PER_KV, axis=1)

    kc = cmp_ref[0].astype(BF16)
    s_cmp = []
    for g in groups:
        zpad = jnp.zeros((HEAD_DIM, TQ), BF16)
        cols = []
        for r in range(Q_PER_KV):
            hq = g * Q_PER_KV + r
            qh = qt_ref[hq * HEAD_DIM:(hq + 1) * HEAD_DIM, :]
            cols.append(jnp.concatenate([qh, zpad] if g == 0 else [zpad, qh], axis=0))
        qpad = jnp.concatenate(cols, axis=1)
        qpad_ref[g] = qpad
        s_cmp.append(jnp.dot(kc, qpad, preferred_element_type=F32))

    def sel_scores(kt, slot):
        k_tile = ksel_ref[pl.ds(pl.multiple_of(kt * KS, KS), KS), :]
        for g in groups:
            s_ref[slot, g] = jnp.dot(k_tile, qpad_ref[g], preferred_element_type=F32)

    n_full = q0 // KS
    sel_scores(0, n_full & 1)

    c_row = lax.broadcasted_iota(jnp.int32, (ncmp, TQ), 0)
    t_lane = q0 + lax.broadcasted_iota(jnp.int32, (ncmp, TQ), 1)
    cpen = tile4(jnp.where(c_row * CMP_STRIDE + (CMP_BLOCK - 1) <= t_lane, 0.0, NEG))
    vc_t = cmp_ref[1].T
    for g in groups:
        vct = vc_t[gsl[g], :].astype(BF16)
        s = s_cmp[g] + cpen
        m = jnp.max(s, axis=0, keepdims=True)
        m = jnp.where(m > 0.5 * NEG, m, 0.0)
        e = jnp.exp2(s - m)
        den = jnp.sum(e, axis=0, keepdims=True)
        p = e * (1.0 / jnp.where(den > 0.0, den, 1.0))
        o_ref[0, g] = jnp.dot(vct, p.astype(BF16), preferred_element_type=F32)

        psum = p[:, 0:TQ]
        for r in range(1, Q_PER_KV):
            psum = psum + p[:, r * TQ:(r + 1) * TQ]
        imp = jnp.dot(ovt_ref[...], psum, preferred_element_type=F32,
                      precision=lax.Precision.HIGHEST)
        j_row = lax.broadcasted_iota(jnp.int32, (nblk, TQ), 0)
        t_blk = q0 + lax.broadcasted_iota(jnp.int32, (nblk, TQ), 1)
        cur = jnp.right_shift(t_blk, SEL_BLOCK.bit_length() - 1)
        forced = (j_row == 0) | (j_row == cur) | (j_row == cur - 1)
        val = jnp.where(forced, SEL_FORCE, imp)
        val = jnp.where(j_row * SEL_BLOCK <= t_blk, val, -SEL_FORCE)
        rank = jnp.zeros((nblk, TQ), F32)
        for i in range(nblk):
            vi = jnp.broadcast_to(val[i:i + 1, :], (nblk, TQ))
            rank = rank + jnp.where(j_row > i, jnp.where(vi >= val, 1.0, 0.0),
                                    jnp.where(vi > val, 1.0, 0.0))
        chosen = (rank < float(min(SEL_TOPK, nblk))) & (val > -0.5 * SEL_FORCE)
        selpen = jnp.where(chosen, 0.0, NEG)
        for j in range(nblk):
            pen_ref[g, j] = jnp.broadcast_to(selpen[j:j + 1, :], (8, TQ))

    m_ref[...] = jnp.full(m_ref.shape, NEG, F32)
    acc_ref[...] = jnp.zeros(acc_ref.shape, F32)
    ones = jnp.ones((ONES_ROWS, VC), BF16)

    def vt_aug(v_ref, c0, n, g):
        vt = jnp.concatenate([v_ref[c0 + c, gsl[g], :] for c in range(n)], axis=1)
        return jnp.concatenate([vt, jnp.concatenate([ones] * n, axis=1)], axis=0)

    def normalise(res):
        return res[0:HEAD_DIM, :] * (1.0 / res[HEAD_DIM:HEAD_DIM + 1, :])
    bpt = KS // SEL_BLOCK
    cpt = KS // VC

    def sel_update(kt, slot, diagonal):
        probs, alphas = [], []
        for g in groups:
            m_prev = m_ref[g]
            if diagonal:
                rows = [jnp.concatenate([pen_ref[g, kt * bpt + jj]] * (SEL_BLOCK // 8), axis=0)
                        for jj in range(bpt)]
                pen = jnp.concatenate(rows, axis=0)
                key = kt * KS + lax.broadcasted_iota(jnp.int32, (KS, TQ), 0)
                t_q = q0 + lax.broadcasted_iota(jnp.int32, (KS, TQ), 1)
                s = s_ref[slot, g] + tile4(pen + jnp.where(key <= t_q, 0.0, NEG))
                m_new = jnp.maximum(m_prev, jnp.max(s, axis=0, keepdims=True))
                p = jnp.exp2(s - m_new)
            else:
                sb = [s_ref[slot, g, jj * SEL_BLOCK:(jj + 1) * SEL_BLOCK, :] for jj in range(bpt)]
                bias = [tile4(pen_ref[g, kt * bpt + jj][0:1, :]) for jj in range(bpt)]
                m_new = m_prev
                for jj in range(bpt):
                    m_new = jnp.maximum(m_new, jnp.max(sb[jj], axis=0, keepdims=True) + bias[jj])
                p = jnp.concatenate([jnp.exp2(sb[jj] + (bias[jj] - m_new)) for jj in range(bpt)],
                                    axis=0)
            probs.append(p.astype(BF16))
            alphas.append(jnp.exp2(m_prev - m_new))
            m_ref[g] = m_new
        for g in groups:
            acc_ref[g] = alphas[g] * acc_ref[g] + jnp.dot(
                vt_aug(vselt_ref, kt * cpt, cpt, g), probs[g], preferred_element_type=F32)

    def sel_body(kt, carry):
        for slot in (0, 1):
            @pl.when(((n_full - kt) & 1) == slot)
            def _(slot=slot):
                sel_scores(kt + 1, 1 - slot)
                sel_update(kt, slot, diagonal=False)
        return carry

    lax.fori_loop(0, n_full, sel_body, 0)

    wrows = WINDOW + TQ
    kt0 = jnp.maximum(qi - WINDOW // KW, 0)
    k_win = kwin_ref[pl.ds(pl.multiple_of(kt0 * KW, KW), wrows), :]
    s_win = [jnp.dot(k_win, qpad_ref[g], preferred_element_type=F32) for g in groups]

    sel_update(n_full, 0, diagonal=True)
    for g in groups:
        o_ref[1, g] = normalise(acc_ref[g])

    key = kt0 * KW + lax.broadcasted_iota(jnp.int32, (wrows, TQ), 0)
    t_q = q0 + lax.broadcasted_iota(jnp.int32, (wrows, TQ), 1)
    wpen = tile4(jnp.where(key <= t_q, jnp.where(key > t_q - WINDOW, 0.0, NEG), NEG))
    p_win = []
    for g in groups:
        s = s_win[g] + wpen
        m = jnp.max(s, axis=0, keepdims=True)
        p_win.append(jnp.exp2(s - m).astype(BF16))
    for g in groups:
        o_ref[2, g] = normalise(jnp.dot(vt_aug(vwint_ref, kt0, wrows // VC, g), p_win[g],
                                        preferred_element_type=F32))

    gates = jax.nn.sigmoid(brgt_ref[...])
    group_out = []
    for g in groups:
        heads = []
        for r in range(Q_PER_KV):
            hq = g * Q_PER_KV + r
            ls = slice(r * TQ, (r + 1) * TQ)
            o = gates[hq:hq + 1, :] * o_ref[0, g, :, ls]
            for n in range(1, N_BRANCH):
                o = o + gates[n * N_Q_HEADS + hq:n * N_Q_HEADS + hq + 1, :] * o_ref[n, g, :, ls]
            heads.append(o)
        group_out.append(jnp.concatenate(heads, axis=0).T)

    attn = jnp.concatenate(group_out, axis=1)
    out_ref[...] = (attn * _silu(ag_ref[...])).astype(out_ref.dtype)


def _nsa(qt, cmp_kv, ksel, vselt, kwin, vwint, brgt, ag, ovt, B, S):
    T = B * S
    nq = S // TQ
    qpm = TM // TQ
    nblk = S // SEL_BLOCK
    ncmp = S // CMP_STRIDE
    nl = Q_PER_KV * TQ
    return pl.pallas_call(
        _nsa_kernel,
        grid=(B, nq),
        in_specs=[
            pl.BlockSpec((None, None, D_ATTN, TQ), lambda b, q: (b, q // qpm, 0, q % qpm)),
            pl.BlockSpec((2, ncmp, D_KV), lambda b, q: (0, b, 0)),
            pl.BlockSpec((S, D_KV), lambda b, q: (b, 0)),
            pl.BlockSpec((None, S // VC, D_KV, VC), lambda b, q: (b, 0, 0, 0)),
            pl.BlockSpec((S, D_KV), lambda b, q: (b, 0)),
            pl.BlockSpec((None, S // VC, D_KV, VC), lambda b, q: (b, 0, 0, 0)),
            pl.BlockSpec((None, None, N_GATE_ROWS, TQ), lambda b, q: (b, q // qpm, 0, q % qpm)),
            pl.BlockSpec((TQ, D_ATTN), lambda b, q: (b * nq + q, 0)),
            pl.BlockSpec(ovt.shape, lambda b, q: (0, 0)),
        ],
        out_specs=pl.BlockSpec((TQ, D_ATTN), lambda b, q: (b * nq + q, 0)),
        out_shape=jax.ShapeDtypeStruct((T, D_ATTN), BF16),
        scratch_shapes=[
            pltpu.VMEM((N_KV_HEADS, D_KV, nl), BF16),
            pltpu.VMEM((N_KV_HEADS, nblk, 8, TQ), F32),
            pltpu.VMEM((N_BRANCH, N_KV_HEADS, HEAD_DIM, nl), F32),
            pltpu.VMEM((N_KV_HEADS, 1, nl), F32),
            pltpu.VMEM((N_KV_HEADS, HEAD_DIM + ONES_ROWS, nl), F32),
            pltpu.VMEM((2, N_KV_HEADS, KS, nl), F32),
        ],
        compiler_params=pltpu.CompilerParams(
            dimension_semantics=("arbitrary", "arbitrary"), vmem_limit_bytes=VMEM_LIMIT),
        name="nsa",
    )(qt, cmp_kv, ksel, vselt, kwin, vwint, brgt, ag, ovt)


def _outproj_kernel(x_ref, rnn_ref, attn_ref, wo_ref, nfw_ref, out_ref, *, final_norm):
    y = x_ref[...]
    y = y + jnp.dot(rnn_ref[...], wo_ref[0:D_RNN, :], preferred_element_type=F32)
    y = y + jnp.dot(attn_ref[...], wo_ref[D_RNN:D_MIX, :], preferred_element_type=F32)
    if final_norm:
        ms = jnp.mean(y * y, axis=-1, keepdims=True)
        y = (y * lax.rsqrt(ms + EPS)) * nfw_ref[...]
    out_ref[...] = y


def _outproj(x2, rnn_out, attn_out, wo, nfw, final_norm):
    T = x2.shape[0]
    row = lambda i: (i, 0)
    return pl.pallas_call(
        functools.partial(_outproj_kernel, final_norm=final_norm),
        grid=(T // TM,),
        in_specs=[
            pl.BlockSpec((TM, D_MODEL), row),
            pl.BlockSpec((TM, D_RNN), row),
            pl.BlockSpec((TM, D_ATTN), row),
            pl.BlockSpec((D_MIX, D_MODEL), lambda i: (0, 0)),
            pl.BlockSpec((1, D_MODEL), lambda i: (0, 0)),
        ],
        out_specs=pl.BlockSpec((TM, D_MODEL), row),
        out_shape=jax.ShapeDtypeStruct((T, D_MODEL), F32),
        compiler_params=pltpu.CompilerParams(
            dimension_semantics=("arbitrary",), vmem_limit_bytes=VMEM_LIMIT),
        name="outproj",
    )(x2, rnn_out, attn_out, wo, nfw)


def _block_diag_halves(wa, wx):
    eye = jnp.eye(RNN_HEADS, dtype=wa.dtype)
    full = lambda w: jnp.einsum('hij,hk->hikj', w, eye).reshape(D_RNN, D_RNN)
    fa, fx = full(wa), full(wx)
    half = D_RNN // 2
    return jnp.stack([
        jnp.concatenate([fa[s:s + half, s:s + half], fx[s:s + half, s:s + half]], axis=1)
        for s in (0, half)]).astype(BF16)


def _compress_weights(pe, w1, w2):
    eye = jnp.eye(N_KV_HEADS, dtype=w1.dtype)
    w1r = w1.reshape(2, CMP_STRIDE, HEAD_DIM, CMP_HIDDEN)
    wbig = jnp.einsum('hldn,ge->lgdhen', w1r, eye).reshape(
        CMP_STRIDE * D_KV, 2 * N_KV_HEADS * CMP_HIDDEN)
    w2bd = jnp.einsum('nd,ge->gned', w2, eye).reshape(N_KV_HEADS * CMP_HIDDEN, D_KV)
    per = pe.reshape(2, CMP_STRIDE, 1, HEAD_DIM)
    pe2 = jnp.broadcast_to(per, (2, CMP_STRIDE, N_KV_HEADS, HEAD_DIM)).reshape(2, CMP_STRIDE * D_KV)
    return pe2, wbig.astype(BF16), w2bd.astype(BF16)


def _overlap_t(ncmp_pad, nblk):
    cs = np.arange(ncmp_pad)[None, :] * CMP_STRIDE
    ss = np.arange(nblk)[:, None] * SEL_BLOCK
    ov = np.clip(np.minimum(cs + CMP_BLOCK, ss + SEL_BLOCK) - np.maximum(cs, ss), 0, None)
    return jnp.asarray(ov.astype(np.float32) / CMP_BLOCK)


def kernel(x, norm1_w, w_in, conv_w, conv_b, rg_wa, rg_ba, rg_wx, rg_bx, rg_lambda,
           cmp_k_pe, cmp_k_w1, cmp_k_w2, cmp_v_pe, cmp_v_w1, cmp_v_w2, w_out, normf_w):
    B, S, D = x.shape
    assert D == D_MODEL and w_in.shape[-1] == _D_IN
    assert S % TM == 0 and S % TS == 0 and B % CB == 0
    assert KW == TQ == VC and WINDOW % KW == 0 and KS % TQ == 0 and S >= WINDOW + TQ
    depth = w_in.shape[0]
    T = B * S
    ovt = _overlap_t(S // CMP_STRIDE, S // SEL_BLOCK)
    x2 = x.reshape(T, D)
    for l in range(depth):
        w = w_in[l]
        wn = jnp.concatenate([w[:, _O_RX:_O_Q], w[:, _O_KC:_O_KS], w[:, _O_KS:_O_VS],
                              w[:, _O_KW:_O_VW], w[:, _O_AG:_O_BR]], axis=1).astype(BF16)
        wt = jnp.concatenate([w[:, _O_Q:_O_KC], w[:, _O_VS:_O_KW], w[:, _O_VW:_O_AG],
                              w[:, _O_BR:_D_IN],
                              jnp.zeros((D, N_GATE_ROWS - N_BRANCH * N_Q_HEADS), w.dtype)],
                             axis=1).T.astype(BF16)
        (rnn_x, rnn_gate, kvc, ksel, kwin, ag, qt, vselt, vwint, brgt) = _inproj(
            x2, norm1_w[l].reshape(1, D), wn, wt, B, S)

        rnn_out = _rglru(rnn_x, rnn_gate, conv_w[l], conv_b[l].reshape(1, D_RNN),
                         _block_diag_halves(rg_wa[l], rg_wx[l]),
                         rg_ba[l].reshape(1, D_RNN), rg_bx[l].reshape(1, D_RNN),
                         rg_lambda[l].reshape(1, D_RNN), B, S)

        pk, wbk, w2k = _compress_weights(cmp_k_pe[l], cmp_k_w1[l], cmp_k_w2[l])
        pv, wbv, w2v = _compress_weights(cmp_v_pe[l], cmp_v_w1[l], cmp_v_w2[l])
        xc = kvc.reshape(2, T // CMP_STRIDE, CMP_STRIDE * D_KV)
        cmp_kv = _compress(xc, jnp.stack([pk, pv]), jnp.stack([wbk, wbv]),
                           jnp.stack([w2k, w2v]), B, S)

        attn_out = _nsa(qt, cmp_kv, ksel, vselt, kwin, vwint, brgt, ag, ovt, B, S)

        x2 = _outproj(x2, rnn_out, attn_out, w_out[l].astype(BF16),
                      normf_w.reshape(1, D), final_norm=(l == depth - 1))
    return x2.reshape(B, S, D)
```

```python
import functools

import numpy as np
import jax
import jax.numpy as jnp
from jax import lax
from jax.experimental import pallas as pl
from jax.experimental.pallas import tpu as pltpu

F32 = jnp.float32
BF16 = jnp.bfloat16

D_MODEL = 1024
EPS = 1e-6
D_RNN = 512
RNN_HEADS = 8
RNN_HEAD_DIM = D_RNN // RNN_HEADS
CONV_WIDTH = 4
LRU_C = 8.0
N_Q_HEADS = 8
N_KV_HEADS = 2
HEAD_DIM = 64
Q_PER_KV = N_Q_HEADS // N_KV_HEADS
D_ATTN = N_Q_HEADS * HEAD_DIM
D_KV = N_KV_HEADS * HEAD_DIM
CMP_BLOCK = 32
CMP_STRIDE = 16
CMP_HIDDEN = 256
SEL_BLOCK = 64
SEL_TOPK = 8
SEL_FORCE = 1e9
WINDOW = 512
N_BRANCH = 3
D_MIX = D_RNN + D_ATTN
N_GATE_ROWS = 32

TM = 512
TS = 512
TQ = 256
KS = 512
VC = 128
CB = 4
NEG = -1e30
LOG2E = 1.4426950408889634
ONES_ROWS = 16
VMEM_LIMIT = 48 * 1024 * 1024

_O_RX, _O_RG, _O_Q = 0, D_RNN, 2 * D_RNN
_O_KC = _O_Q + D_ATTN
_O_VC = _O_KC + D_KV
_O_KS = _O_VC + D_KV
_O_VS = _O_KS + D_KV
_O_KW = _O_VS + D_KV
_O_VW = _O_KW + D_KV
_O_AG = _O_VW + D_KV
_O_BR = _O_AG + D_ATTN
_D_IN = _O_BR + N_BRANCH * N_Q_HEADS


def _silu(x):
    return x * jax.nn.sigmoid(x)


def _inproj_kernel(x_ref, nw_ref, wn_ref, wt_ref,
                   rnnx_ref, rnng_ref, kvc_ref, ksel_ref, kwin_ref, ag_ref,
                   qt_ref, vselt_ref, vwint_ref, brgt_ref):
    x = x_ref[...]
    ms = jnp.mean(x * x, axis=-1, keepdims=True)
    h = ((x * lax.rsqrt(ms + EPS)) * nw_ref[...]).astype(BF16)

    def nat(a, b):
        return jnp.dot(h, wn_ref[:, a:b], preferred_element_type=F32)

    def tr(a, b):
        return lax.dot_general(wt_ref[a:b, :], h, (((1,), (1,)), ((), ())),
                               preferred_element_type=F32)

    rnnx_ref[...] = nat(0, 512)
    rnng_ref[...] = nat(512, 1024)
    kvc_ref[0] = nat(1024, 1152)
    kvc_ref[1] = nat(1152, 1280)
    ksel_ref[...] = nat(1280, 1408).astype(BF16)
    kwin_ref[...] = nat(1408, 1536).astype(BF16)
    ag_ref[...] = nat(1536, 2048)

    qt_ref[...] = (tr(0, 512) * (HEAD_DIM ** -0.5 * LOG2E)).astype(BF16)
    vs = tr(512, 640).astype(BF16)
    for c in range(TM // VC):
        vselt_ref[c] = vs[:, c * VC:(c + 1) * VC]
    vw = tr(640, 768).astype(BF16)
    for c in range(TM // VC):
        vwint_ref[c] = vw[:, c * VC:(c + 1) * VC]
    brgt_ref[...] = tr(768, 768 + N_GATE_ROWS)


def _inproj(x2, nw, wn, wt, B, S):
    T = B * S
    ns = S // TM
    grid = (B, ns)
    row = lambda b, s: (b * ns + s, 0)
    out_shape = (
        jax.ShapeDtypeStruct((T, D_RNN), F32),
        jax.ShapeDtypeStruct((T, D_RNN), F32),
        jax.ShapeDtypeStruct((2, T, D_KV), F32),
        jax.ShapeDtypeStruct((T, D_KV), BF16),
        jax.ShapeDtypeStruct((T, D_KV), BF16),
        jax.ShapeDtypeStruct((T, D_ATTN), F32),
        jax.ShapeDtypeStruct((B, ns, D_ATTN, TM), BF16),
        jax.ShapeDtypeStruct((B, S // VC, D_KV, VC), BF16),
        jax.ShapeDtypeStruct((B, S // VC, D_KV, VC), BF16),
        jax.ShapeDtypeStruct((B, ns, N_GATE_ROWS, TM), F32),
    )
    out_specs = (
        pl.BlockSpec((TM, D_RNN), row),
        pl.BlockSpec((TM, D_RNN), row),
        pl.BlockSpec((2, TM, D_KV), lambda b, s: (0, b * ns + s, 0)),
        pl.BlockSpec((TM, D_KV), row),
        pl.BlockSpec((TM, D_KV), row),
        pl.BlockSpec((TM, D_ATTN), row),
        pl.BlockSpec((None, None, D_ATTN, TM), lambda b, s: (b, s, 0, 0)),
        pl.BlockSpec((None, TM // VC, D_KV, VC), lambda b, s: (b, s, 0, 0)),
        pl.BlockSpec((None, TM // VC, D_KV, VC), lambda b, s: (b, s, 0, 0)),
        pl.BlockSpec((None, None, N_GATE_ROWS, TM), lambda b, s: (b, s, 0, 0)),
    )
    return pl.pallas_call(
        _inproj_kernel,
        grid=grid,
        in_specs=[
            pl.BlockSpec((TM, D_MODEL), row),
            pl.BlockSpec((1, D_MODEL), lambda b, s: (0, 0)),
            pl.BlockSpec(wn.shape, lambda b, s: (0, 0)),
            pl.BlockSpec(wt.shape, lambda b, s: (0, 0)),
        ],
        out_specs=out_specs,
        out_shape=out_shape,
        compiler_params=pltpu.CompilerParams(
            dimension_semantics=("arbitrary", "arbitrary"), vmem_limit_bytes=VMEM_LIMIT),
        name="inproj",
    )(x2, nw, wn, wt)


def _rglru_kernel(x_ref, gate_ref, cw_ref, cb_ref, wg_ref, ba_ref, bx_ref, lam_ref,
                  out_ref, cbuf, a_s, u_s, h_s, hlast):
    sc = pl.program_id(1)

    @pl.when(sc == 0)
    def _():
        cbuf[0:8, :] = jnp.zeros((8, D_RNN), F32)
        hlast[...] = jnp.zeros((1, D_RNN), F32)

    x = x_ref[...]
    cbuf[8:8 + TS, :] = x
    y = cb_ref[...]
    for k in range(CONV_WIDTH):
        off = 8 - (CONV_WIDTH - 1) + k
        y = y + cbuf[off:off + TS, :] * cw_ref[k:k + 1, :]
    cbuf[0:8, :] = x[TS - 8:TS, :]

    yb = y.astype(BF16)
    half = D_RNN // 2
    pre = [jnp.dot(yb[:, hh * half:(hh + 1) * half], wg_ref[hh], preferred_element_type=F32)
           for hh in range(2)]
    pre_a = jnp.concatenate([pre[0][:, :half], pre[1][:, :half]], axis=1)
    pre_x = jnp.concatenate([pre[0][:, half:], pre[1][:, half:]], axis=1)
    r = jax.nn.sigmoid(pre_a + ba_ref[...])
    i = jax.nn.sigmoid(pre_x + bx_ref[...])
    lam = lam_ref[...]
    lsig = jnp.minimum(lam, 0.0) - jnp.log1p(jnp.exp(-jnp.abs(lam)))
    log_a = LRU_C * r * lsig
    a = jnp.exp(log_a)
    u = jnp.sqrt(1.0 - a * a) * (i * y)

    rowmod = lax.broadcasted_iota(jnp.int32, (TS, D_RNN), 0) & 7
    for d in (1, 2, 4):
        a_sh = pltpu.roll(a, d, 0)
        u_sh = pltpu.roll(u, d, 0)
        valid = rowmod >= d
        u = jnp.where(valid, a * u_sh + u, u)
        a = jnp.where(valid, a * a_sh, a)
    a_s[...] = a
    u_s[...] = u

    def body(j, carry):
        blk = pl.multiple_of(j * 8, 8)
        hb = u_s[pl.ds(blk, 8), :] + a_s[pl.ds(blk, 8), :] * carry
        h_s[pl.ds(blk, 8), :] = hb
        return hb[7:8, :]

    hlast[...] = lax.fori_loop(0, TS // 8, body, hlast[...], unroll=8)
    out_ref[...] = (h_s[...] * _silu(gate_ref[...])).astype(out_ref.dtype)


def _rglru(rnn_x, rnn_gate, cw, cb, wg, ba, bx, lam, B, S):
    T = B * S
    ns = S // TS
    row = lambda b, s: (b * ns + s, 0)
    const2 = lambda b, s: (0, 0)
    return pl.pallas_call(
        _rglru_kernel,
        grid=(B, ns),
        in_specs=[
            pl.BlockSpec((TS, D_RNN), row),
            pl.BlockSpec((TS, D_RNN), row),
            pl.BlockSpec((CONV_WIDTH, D_RNN), const2),
            pl.BlockSpec((1, D_RNN), const2),
            pl.BlockSpec(wg.shape, lambda b, s: (0, 0, 0)),
            pl.BlockSpec((1, D_RNN), const2),
            pl.BlockSpec((1, D_RNN), const2),
            pl.BlockSpec((1, D_RNN), const2),
        ],
        out_specs=pl.BlockSpec((TS, D_RNN), row),
        out_shape=jax.ShapeDtypeStruct((T, D_RNN), BF16),
        scratch_shapes=[
            pltpu.VMEM((TS + 8, D_RNN), F32),
            pltpu.VMEM((TS, D_RNN), F32),
            pltpu.VMEM((TS, D_RNN), F32),
            pltpu.VMEM((TS, D_RNN), F32),
            pltpu.VMEM((1, D_RNN), F32),
        ],
        compiler_params=pltpu.CompilerParams(
            dimension_semantics=("arbitrary", "arbitrary"), vmem_limit_bytes=VMEM_LIMIT),
        name="rglru",
    )(rnn_x, rnn_gate, cw, cb, wg, ba, bx, lam)


def _compress_kernel(x_ref, pe_ref, wbig_ref, w2_ref, out_ref):
    nrow = x_ref.shape[1]
    nchunk = nrow // CB
    x = x_ref[0]
    xa = (x + pe_ref[0, 0:1, :]).astype(BF16)
    xb = (x + pe_ref[0, 1:2, :]).astype(BF16)
    nh = N_KV_HEADS * CMP_HIDDEN
    first = jnp.dot(xa, wbig_ref[0, :, :nh], preferred_element_type=F32)
    second = jnp.dot(xb, wbig_ref[0, :, nh:], preferred_element_type=F32)
    hid = _silu(first + pltpu.roll(second, nrow - 1, 0))
    out = jnp.dot(hid.astype(BF16), w2_ref[0], preferred_element_type=F32)
    c_idx = lax.broadcasted_iota(jnp.int32, out.shape, 0) & (nchunk - 1)
    out_ref[0] = jnp.where(c_idx < nchunk - 1, out, 0.0)


def _compress(xc, pe2, wbig, w2bd, B, S):
    nchunk = S // CMP_STRIDE
    nrow = CB * nchunk
    width = CMP_STRIDE * D_KV
    kv = lambda i, j: (i, 0, 0)
    return pl.pallas_call(
        _compress_kernel,
        grid=(2, B // CB),
        in_specs=[
            pl.BlockSpec((1, nrow, width), lambda i, j: (i, j, 0)),
            pl.BlockSpec((1, 2, width), kv),
            pl.BlockSpec((1,) + wbig.shape[1:], kv),
            pl.BlockSpec((1,) + w2bd.shape[1:], kv),
        ],
        out_specs=pl.BlockSpec((1, nrow, D_KV), lambda i, j: (i, j, 0)),
        out_shape=jax.ShapeDtypeStruct((2, B * nchunk, D_KV), F32),
        compiler_params=pltpu.CompilerParams(
            dimension_semantics=("arbitrary", "arbitrary"), vmem_limit_bytes=VMEM_LIMIT),
        name="compress",
    )(xc, pe2, wbig, w2bd)


def _nsa_kernel(qt_ref, cmp_ref, ksel_ref, vselt_ref, kwin_ref, vwint_ref, brgt_ref, ag_ref,
                ovt_ref, out_ref, qpad_ref, pen_ref, o_ref, m_ref, acc_ref, s_ref):
    qi = pl.program_id(1)
    q0 = qi * TQ
    nl = Q_PER_KV * TQ
    nblk = pen_ref.shape[1]
    ncmp = cmp_ref.shape[1]
    groups = range(N_KV_HEADS)
    gsl = [slice(g * HEAD_DIM, (g + 1) * HEAD_DIM) for g in groups]

    def tile4(a):
        return jnp.concatenate([a] * Q_PER_KV, axis=1)

    kc = cmp_ref[0].astype(BF16)
    s_cmp = []
    for g in groups:
        zpad = jnp.zeros((HEAD_DIM, TQ), BF16)
        cols = []
        for r in range(Q_PER_KV):
            hq = g * Q_PER_KV + r
            qh = qt_ref[hq * HEAD_DIM:(hq + 1) * HEAD_DIM, :]
            cols.append(jnp.concatenate([qh, zpad] if g == 0 else [zpad, qh], axis=0))
        qpad = jnp.concatenate(cols, axis=1)
        qpad_ref[g] = qpad
        s_cmp.append(jnp.dot(kc, qpad, preferred_element_type=F32))

    def sel_scores(kt, slot):
        k_tile = ksel_ref[pl.ds(pl.multiple_of(kt * KS, KS), KS), :]
        for g in groups:
            s_ref[slot, g] = jnp.dot(k_tile, qpad_ref[g], preferred_element_type=F32)

    wrows = WINDOW + TQ
    w0 = pl.multiple_of(jnp.maximum(q0 - WINDOW, 0), TQ)
    k_win = kwin_ref[pl.ds(w0, wrows), :]
    s_win = [jnp.dot(k_win, qpad_ref[g], preferred_element_type=F32) for g in groups]

    n_full = q0 // KS
    sel_scores(0, n_full & 1)

    ones = jnp.ones((ONES_ROWS, VC), BF16)

    def vt_aug(v_ref, c0, n, g):
        vt = jnp.concatenate([v_ref[c0 + c, gsl[g], :] for c in range(n)], axis=1)
        return jnp.concatenate([vt, jnp.concatenate([ones] * n, axis=1)], axis=0)

    def normalise(res):
        return res[0:HEAD_DIM, :] * (1.0 / res[HEAD_DIM:HEAD_DIM + 1, :])

    c_row = lax.broadcasted_iota(jnp.int32, (ncmp, TQ), 0)
    t_lane = q0 + lax.broadcasted_iota(jnp.int32, (ncmp, TQ), 1)
    cpen = tile4(jnp.where(c_row * CMP_STRIDE + (CMP_BLOCK - 1) <= t_lane, 0.0, NEG))
    vc_t = cmp_ref[1].T
    for g in groups:
        vct = vc_t[gsl[g], :].astype(BF16)
        s = s_cmp[g] + cpen
        m = jnp.max(s, axis=0, keepdims=True)
        m = jnp.where(m > 0.5 * NEG, m, 0.0)
        e = jnp.exp2(s - m)
        den = jnp.sum(e, axis=0, keepdims=True)
        p = e * (1.0 / jnp.where(den > 0.0, den, 1.0))
        o_ref[0, g] = jnp.dot(vct, p.astype(BF16), preferred_element_type=F32)

        psum = p[:, 0:TQ]
        for r in range(1, Q_PER_KV):
            psum = psum + p[:, r * TQ:(r + 1) * TQ]
        imp = jnp.dot(ovt_ref[...], psum, preferred_element_type=F32,
                      precision=lax.Precision.HIGHEST)
        j_row = lax.broadcasted_iota(jnp.int32, (nblk, TQ), 0)
        t_blk = q0 + lax.broadcasted_iota(jnp.int32, (nblk, TQ), 1)
        cur = jnp.right_shift(t_blk, SEL_BLOCK.bit_length() - 1)
        forced = (j_row == 0) | (j_row == cur) | (j_row == cur - 1)
        val = jnp.where(forced, SEL_FORCE, imp)
        val = jnp.where(j_row * SEL_BLOCK <= t_blk, val, -SEL_FORCE)
        rank = jnp.zeros((nblk, TQ), F32)
        for i in range(nblk):
            vi = jnp.broadcast_to(val[i:i + 1, :], (nblk, TQ))
            rank = rank + jnp.where(j_row > i, jnp.where(vi >= val, 1.0, 0.0),
                                    jnp.where(vi > val, 1.0, 0.0))
        chosen = (rank < float(min(SEL_TOPK, nblk))) & (val > -0.5 * SEL_FORCE)
        selpen = jnp.where(chosen, 0.0, NEG)
        for j in range(nblk):
            pen_ref[g, j] = jnp.broadcast_to(selpen[j:j + 1, :], (8, TQ))

    key = w0 + lax.broadcasted_iota(jnp.int32, (wrows, TQ), 0)
    t_q = q0 + lax.broadcasted_iota(jnp.int32, (wrows, TQ), 1)
    wpen = tile4(jnp.where(key <= t_q, jnp.where(key > t_q - WINDOW, 0.0, NEG), NEG))
    p_win = []
    for g in groups:
        s = s_win[g] + wpen
        m = jnp.max(s, axis=0, keepdims=True)
        p_win.append(jnp.exp2(s - m).astype(BF16))
    for g in groups:
        o_ref[2, g] = normalise(jnp.dot(vt_aug(vwint_ref, w0 // VC, wrows // VC, g), p_win[g],
                                        preferred_element_type=F32))

    m_ref[...] = jnp.full(m_ref.shape, NEG, F32)
    acc_ref[...] = jnp.zeros(acc_ref.shape, F32)
    bpt = KS // SEL_BLOCK
    cpt = KS // VC

    def sel_update(kt, slot, diagonal):
        probs, alphas = [], []
        for g in groups:
            m_prev = m_ref[g]
            if diagonal:
                rows = [jnp.concatenate([pen_ref[g, kt * bpt + jj]] * (SEL_BLOCK // 8), axis=0)
                        for jj in range(bpt)]
                pen = jnp.concatenate(rows, axis=0)
                key = kt * KS + lax.broadcasted_iota(jnp.int32, (KS, TQ), 0)
                t_q = q0 + lax.broadcasted_iota(jnp.int32, (KS, TQ), 1)
                s = s_ref[slot, g] + tile4(pen + jnp.where(key <= t_q, 0.0, NEG))
                m_new = jnp.maximum(m_prev, jnp.max(s, axis=0, keepdims=True))
                p = jnp.exp2(s - m_new)
            else:
                sb = [s_ref[slot, g, jj * SEL_BLOCK:(jj + 1) * SEL_BLOCK, :] for jj in range(bpt)]
                bias = [tile4(pen_ref[g, kt * bpt + jj][0:1, :]) for jj in range(bpt)]
                m_new = m_prev
                for jj in range(bpt):
                    m_new = jnp.maximum(m_new, jnp.max(sb[jj], axis=0, keepdims=True) + bias[jj])
                p = jnp.concatenate([jnp.exp2(sb[jj] + (bias[jj] - m_new)) for jj in range(bpt)],
                                    axis=0)
            probs.append(p.astype(BF16))
            alphas.append(jnp.exp2(m_prev - m_new))
            m_ref[g] = m_new
        for g in groups:
            acc_ref[g] = alphas[g] * acc_ref[g] + jnp.dot(
                vt_aug(vselt_ref, kt * cpt, cpt, g), probs[g], preferred_element_type=F32)

    def sel_body(kt, carry):
        for slot in (0, 1):
            @pl.when(((n_full - kt) & 1) == slot)
            def _(slot=slot):
                sel_scores(kt + 1, 1 - slot)
                sel_update(kt, slot, diagonal=False)
        return carry

    lax.fori_loop(0, n_full, sel_body, 0)
    sel_update(n_full, 0, diagonal=True)
    for g in groups:
        o_ref[1, g] = normalise(acc_ref[g])

    gates = jax.nn.sigmoid(brgt_ref[...])
    group_out = []
    for g in groups:
        heads = []
        for r in range(Q_PER_KV):
            hq = g * Q_PER_KV + r
            ls = slice(r * TQ, (r + 1) * TQ)
            o = gates[hq:hq + 1, :] * o_ref[0, g, :, ls]
            for n in range(1, N_BRANCH):
                o = o + gates[n * N_Q_HEADS + hq:n * N_Q_HEADS + hq + 1, :] * o_ref[n, g, :, ls]
            heads.append(o)
        group_out.append(jnp.concatenate(heads, axis=0).T)

    attn = jnp.concatenate(group_out, axis=1)
    out_ref[...] = (attn * _silu(ag_ref[...])).astype(out_ref.dtype)


def _nsa(qt, cmp_kv, ksel, vselt, kwin, vwint, brgt, ag, ovt, B, S):
    T = B * S
    nq = S // TQ
    qpm = TM // TQ
    nblk = S // SEL_BLOCK
    ncmp = S // CMP_STRIDE
    nl = Q_PER_KV * TQ
    return pl.pallas_call(
        _nsa_kernel,
        grid=(B, nq),
        in_specs=[
            pl.BlockSpec((None, None, D_ATTN, TQ), lambda b, q: (b, q // qpm, 0, q % qpm)),
            pl.BlockSpec((2, ncmp, D_KV), lambda b, q: (0, b, 0)),
            pl.BlockSpec((S, D_KV), lambda b, q: (b, 0)),
            pl.BlockSpec((None, S // VC, D_KV, VC), lambda b, q: (b, 0, 0, 0)),
            pl.BlockSpec((S, D_KV), lambda b, q: (b, 0)),
            pl.BlockSpec((None, S // VC, D_KV, VC), lambda b, q: (b, 0, 0, 0)),
            pl.BlockSpec((None, None, N_GATE_ROWS, TQ), lambda b, q: (b, q // qpm, 0, q % qpm)),
            pl.BlockSpec((TQ, D_ATTN), lambda b, q: (b * nq + q, 0)),
            pl.BlockSpec(ovt.shape, lambda b, q: (0, 0)),
        ],
        out_specs=pl.BlockSpec((TQ, D_ATTN), lambda b, q: (b * nq + q, 0)),
        out_shape=jax.ShapeDtypeStruct((T, D_ATTN), BF16),
        scratch_shapes=[
            pltpu.VMEM((N_KV_HEADS, D_KV, nl), BF16),
            pltpu.VMEM((N_KV_HEADS, nblk, 8, TQ), F32),
            pltpu.VMEM((N_BRANCH, N_KV_HEADS, HEAD_DIM, nl), F32),
            pltpu.VMEM((N_KV_HEADS, 1, nl), F32),
            pltpu.VMEM((N_KV_HEADS, HEAD_DIM + ONES_ROWS, nl), F32),
            pltpu.VMEM((2, N_KV_HEADS, KS, nl), F32),
        ],
        compiler_params=pltpu.CompilerParams(
            dimension_semantics=("arbitrary", "arbitrary"), vmem_limit_bytes=VMEM_LIMIT),
        name="nsa",
    )(qt, cmp_kv, ksel, vselt, kwin, vwint, brgt, ag, ovt)


def _outproj_kernel(x_ref, rnn_ref, attn_ref, wo_ref, nfw_ref, out_ref, *, final_norm):
    y = x_ref[...]
    y = y + jnp.dot(rnn_ref[...], wo_ref[0:D_RNN, :], preferred_element_type=F32)
    y = y + jnp.dot(attn_ref[...], wo_ref[D_RNN:D_MIX, :], preferred_element_type=F32)
    if final_norm:
        ms = jnp.mean(y * y, axis=-1, keepdims=True)
        y = (y * lax.rsqrt(ms + EPS)) * nfw_ref[...]
    out_ref[...] = y


def _outproj(x2, rnn_out, attn_out, wo, nfw, final_norm):
    T = x2.shape[0]
    row = lambda i: (i, 0)
    return pl.pallas_call(
        functools.partial(_outproj_kernel, final_norm=final_norm),
        grid=(T // TM,),
        in_specs=[
            pl.BlockSpec((TM, D_MODEL), row),
            pl.BlockSpec((TM, D_RNN), row),
            pl.BlockSpec((TM, D_ATTN), row),
            pl.BlockSpec((D_MIX, D_MODEL), lambda i: (0, 0)),
            pl.BlockSpec((1, D_MODEL), lambda i: (0, 0)),
        ],
        out_specs=pl.BlockSpec((TM, D_MODEL), row),
        out_shape=jax.ShapeDtypeStruct((T, D_MODEL), F32),
        compiler_params=pltpu.CompilerParams(
            dimension_semantics=("arbitrary",), vmem_limit_bytes=VMEM_LIMIT),
        name="outproj",
    )(x2, rnn_out, attn_out, wo, nfw)


def _block_diag_halves(wa, wx):
    eye = jnp.eye(RNN_HEADS, dtype=wa.dtype)
    full = lambda w: jnp.einsum('hij,hk->hikj', w, eye).reshape(D_RNN, D_RNN)
    fa, fx = full(wa), full(wx)
    half = D_RNN // 2
    return jnp.stack([
        jnp.concatenate([fa[s:s + half, s:s + half], fx[s:s + half, s:s + half]], axis=1)
        for s in (0, half)]).astype(BF16)


def _compress_weights(pe, w1, w2):
    eye = jnp.eye(N_KV_HEADS, dtype=w1.dtype)
    w1r = w1.reshape(2, CMP_STRIDE, HEAD_DIM, CMP_HIDDEN)
    wbig = jnp.einsum('hldn,ge->lgdhen', w1r, eye).reshape(
        CMP_STRIDE * D_KV, 2 * N_KV_HEADS * CMP_HIDDEN)
    w2bd = jnp.einsum('nd,ge->gned', w2, eye).reshape(N_KV_HEADS * CMP_HIDDEN, D_KV)
    per = pe.reshape(2, CMP_STRIDE, 1, HEAD_DIM)
    pe2 = jnp.broadcast_to(per, (2, CMP_STRIDE, N_KV_HEADS, HEAD_DIM)).reshape(2, CMP_STRIDE * D_KV)
    return pe2, wbig.astype(BF16), w2bd.astype(BF16)


def _overlap_t(ncmp_pad, nblk):
    cs = np.arange(ncmp_pad)[None, :] * CMP_STRIDE
    ss = np.arange(nblk)[:, None] * SEL_BLOCK
    ov = np.clip(np.minimum(cs + CMP_BLOCK, ss + SEL_BLOCK) - np.maximum(cs, ss), 0, None)
    return jnp.asarray(ov.astype(np.float32) / CMP_BLOCK)


def kernel(x, norm1_w, w_in, conv_w, conv_b, rg_wa, rg_ba, rg_wx, rg_bx, rg_lambda,
           cmp_k_pe, cmp_k_w1, cmp_k_w2, cmp_v_pe, cmp_v_w1, cmp_v_w2, w_out, normf_w):
    B, S, D = x.shape
    assert D == D_MODEL and w_in.shape[-1] == _D_IN
    assert S % TM == 0 and S % TS == 0 and B % CB == 0
    assert TQ % VC == 0 and WINDOW % TQ == 0 and KS % TQ == 0 and TM % TQ == 0
    assert S % KS == 0 and S >= WINDOW + TQ
    depth = w_in.shape[0]
    T = B * S
    ovt = _overlap_t(S // CMP_STRIDE, S // SEL_BLOCK)
    x2 = x.reshape(T, D)
    for l in range(depth):
        w = w_in[l]
        wn = jnp.concatenate([w[:, _O_RX:_O_Q], w[:, _O_KC:_O_KS], w[:, _O_KS:_O_VS],
                              w[:, _O_KW:_O_VW], w[:, _O_AG:_O_BR]], axis=1).astype(BF16)
        wt = jnp.concatenate([w[:, _O_Q:_O_KC], w[:, _O_VS:_O_KW], w[:, _O_VW:_O_AG],
                              w[:, _O_BR:_D_IN],
                              jnp.zeros((D, N_GATE_ROWS - N_BRANCH * N_Q_HEADS), w.dtype)],
                             axis=1).T.astype(BF16)
        (rnn_x, rnn_gate, kvc, ksel, kwin, ag, qt, vselt, vwint, brgt) = _inproj(
            x2, norm1_w[l].reshape(1, D), wn, wt, B, S)

        rnn_out = _rglru(rnn_x, rnn_gate, conv_w[l], conv_b[l].reshape(1, D_RNN),
                         _block_diag_halves(rg_wa[l], rg_wx[l]),
                         rg_ba[l].reshape(1, D_RNN), rg_bx[l].reshape(1, D_RNN),
                         rg_lambda[l].reshape(1, D_RNN), B, S)

        pk, wbk, w2k = _compress_weights(cmp_k_pe[l], cmp_k_w1[l], cmp_k_w2[l])
        pv, wbv, w2v = _compress_weights(cmp_v_pe[l], cmp_v_w1[l], cmp_v_w2[l])
        xc = kvc.reshape(2, T // CMP_STRIDE, CMP_STRIDE * D_KV)
        cmp_kv = _compress(xc, jnp.stack([pk, pv]), jnp.stack([wbk, wbv]),
                           jnp.stack([w2k, w2v]), B, S)

        attn_out = _nsa(qt, cmp_kv, ksel, vselt, kwin, vwint, brgt, ag, ovt, B, S)

        x2 = _outproj(x2, rnn_out, attn_out, w_out[l].astype(BF16),
                      normf_w.reshape(1, D), final_norm=(l == depth - 1))
    return x2.reshape(B, S, D)
```

```python
import functools

import numpy as np
import jax
import jax.numpy as jnp
from jax import lax
from jax.experimental import pallas as pl
from jax.experimental.pallas import tpu as pltpu

F32 = jnp.float32
BF16 = jnp.bfloat16

D_MODEL = 1024
EPS = 1e-6
D_RNN = 512
RNN_HEADS = 8
RNN_HEAD_DIM = D_RNN // RNN_HEADS
CONV_WIDTH = 4
LRU_C = 8.0
N_Q_HEADS = 8
N_KV_HEADS = 2
HEAD_DIM = 64
Q_PER_KV = N_Q_HEADS // N_KV_HEADS
D_ATTN = N_Q_HEADS * HEAD_DIM
D_KV = N_KV_HEADS * HEAD_DIM
CMP_BLOCK = 32
CMP_STRIDE = 16
CMP_HIDDEN = 256
SEL_BLOCK = 64
SEL_TOPK = 8
SEL_FORCE = 1e9
WINDOW = 512
N_BRANCH = 3
D_MIX = D_RNN + D_ATTN
N_GATE_ROWS = 32

LANE = 128
SCAN_CHUNKS = 8

TM = 512
TS = 512
TQ = 256
KS = 512
VC = 128
CB = 4
NEG = -1e30
LOG2E = 1.4426950408889634
ONES_ROWS = 16
VMEM_LIMIT = 48 * 1024 * 1024

_O_RX, _O_RG, _O_Q = 0, D_RNN, 2 * D_RNN
_O_KC = _O_Q + D_ATTN
_O_VC = _O_KC + D_KV
_O_KS = _O_VC + D_KV
_O_VS = _O_KS + D_KV
_O_KW = _O_VS + D_KV
_O_VW = _O_KW + D_KV
_O_AG = _O_VW + D_KV
_O_BR = _O_AG + D_ATTN
_D_IN = _O_BR + N_BRANCH * N_Q_HEADS


def _silu(x):
    return x * jax.nn.sigmoid(x)


def _inproj_kernel(x_ref, nw_ref, wn_ref, wt_ref,
                   rnnx_ref, rnng_ref, kvc_ref, ksel_ref, kwin_ref, ag_ref,
                   qt_ref, vselt_ref, vwint_ref, brgt_ref, kv_scr):
    x = x_ref[...]
    ms = jnp.mean(x * x, axis=-1, keepdims=True)
    h = ((x * lax.rsqrt(ms + EPS)) * nw_ref[...]).astype(BF16)

    def nat(a, b):
        return jnp.dot(h, wn_ref[:, a:b], preferred_element_type=F32)

    def tr(a, b):
        return lax.dot_general(wt_ref[a:b, :], h, (((1,), (1,)), ((), ())),
                               preferred_element_type=F32)

    def store_step_major(dst_ref, val):
        steps = TM // SCAN_CHUNKS
        for l in range(D_RNN // LANE):
            for c in range(SCAN_CHUNKS):
                dst_ref[l, pl.ds(c, steps, stride=SCAN_CHUNKS), :] = (
                    val[c * steps:(c + 1) * steps, l * LANE:(l + 1) * LANE])

    store_step_major(rnnx_ref, nat(0, 512))
    store_step_major(rnng_ref, nat(512, 1024))

    kv = nat(1024, 1280)
    for a in range(2):
        kv_scr[a] = kv[:, a * D_KV:(a + 1) * D_KV]
        for l in range(CMP_STRIDE):
            kvc_ref[a, :, l * D_KV:(l + 1) * D_KV] = (
                kv_scr[a, pl.ds(l, TM // CMP_STRIDE, stride=CMP_STRIDE), :])
    kk = nat(1280, 1536)
    ksel_ref[...] = kk[:, :D_KV].astype(BF16)
    kwin_ref[...] = kk[:, D_KV:].astype(BF16)
    ag_ref[...] = nat(1536, 2048)

    qt_ref[...] = (tr(0, 512) * (HEAD_DIM ** -0.5 * LOG2E)).astype(BF16)
    vs = tr(512, 640).astype(BF16)
    for c in range(TM // VC):
        vselt_ref[c] = vs[:, c * VC:(c + 1) * VC]
    vw = tr(640, 768).astype(BF16)
    for c in range(TM // VC):
        vwint_ref[c] = vw[:, c * VC:(c + 1) * VC]
    brgt_ref[...] = tr(768, 768 + N_GATE_ROWS)


def _inproj(x2, nw, wn, wt, B, S):
    T = B * S
    ns = S // TM
    grid = (B, ns)
    row = lambda b, s: (b * ns + s, 0)
    out_shape = (
        jax.ShapeDtypeStruct((T // TM, D_RNN // LANE, TM, LANE), F32),
        jax.ShapeDtypeStruct((T // TM, D_RNN // LANE, TM, LANE), F32),
        jax.ShapeDtypeStruct((2, T // CMP_STRIDE, CMP_STRIDE * D_KV), F32),
        jax.ShapeDtypeStruct((T, D_KV), BF16),
        jax.ShapeDtypeStruct((T, D_KV), BF16),
        jax.ShapeDtypeStruct((T, D_ATTN), F32),
        jax.ShapeDtypeStruct((B, ns, D_ATTN, TM), BF16),
        jax.ShapeDtypeStruct((B, S // VC, D_KV, VC), BF16),
        jax.ShapeDtypeStruct((B, S // VC, D_KV, VC), BF16),
        jax.ShapeDtypeStruct((B, ns, N_GATE_ROWS, TM), F32),
    )
    slab = pl.BlockSpec((None, D_RNN // LANE, TM, LANE), lambda b, s: (b * ns + s, 0, 0, 0))
    out_specs = (
        slab,
        slab,
        pl.BlockSpec((2, TM // CMP_STRIDE, CMP_STRIDE * D_KV), lambda b, s: (0, b * ns + s, 0)),
        pl.BlockSpec((TM, D_KV), row),
        pl.BlockSpec((TM, D_KV), row),
        pl.BlockSpec((TM, D_ATTN), row),
        pl.BlockSpec((None, None, D_ATTN, TM), lambda b, s: (b, s, 0, 0)),
        pl.BlockSpec((None, TM // VC, D_KV, VC), lambda b, s: (b, s, 0, 0)),
        pl.BlockSpec((None, TM // VC, D_KV, VC), lambda b, s: (b, s, 0, 0)),
        pl.BlockSpec((None, None, N_GATE_ROWS, TM), lambda b, s: (b, s, 0, 0)),
    )
    return pl.pallas_call(
        _inproj_kernel,
        grid=grid,
        in_specs=[
            pl.BlockSpec((TM, D_MODEL), row),
            pl.BlockSpec((1, D_MODEL), lambda b, s: (0, 0)),
            pl.BlockSpec(wn.shape, lambda b, s: (0, 0)),
            pl.BlockSpec(wt.shape, lambda b, s: (0, 0)),
        ],
        out_specs=out_specs,
        out_shape=out_shape,
        scratch_shapes=[pltpu.VMEM((2, TM, D_KV), F32)],
        compiler_params=pltpu.CompilerParams(
            dimension_semantics=("arbitrary", "arbitrary"), vmem_limit_bytes=VMEM_LIMIT),
        name="inproj",
    )(x2, nw, wn, wt)


def _rglru_kernel(x_ref, gate_ref, cw_ref, cb_ref, wg_ref, ba_ref, bx_ref, lam_ref,
                  out_ref, tail_ref, a_s, u_s, h_s, p_s, hlast, o_scr):
    sc = pl.program_id(1)
    nstep = TS // SCAN_CHUNKS
    nslab = D_RNN // LANE
    ntail = (CONV_WIDTH - 1) * SCAN_CHUNKS

    @pl.when(sc == 0)
    def _():
        tail_ref[...] = jnp.zeros(tail_ref.shape, F32)
        hlast[...] = jnp.zeros((1, D_RNN), F32)

    x = jnp.concatenate([x_ref[l] for l in range(nslab)], axis=1)
    gate = jnp.concatenate([gate_ref[l] for l in range(nslab)], axis=1)
    sub = lax.broadcasted_iota(jnp.int32, (SCAN_CHUNKS, D_RNN), 0)
    wraps = []
    for j in range(CONV_WIDTH - 1):
        lo = TS - ntail + j * SCAN_CHUNKS
        cur = pltpu.roll(x[lo:lo + SCAN_CHUNKS, :], 1, 0)
        prev = pltpu.roll(tail_ref[j * SCAN_CHUNKS:(j + 1) * SCAN_CHUNKS, :], 1, 0)
        wraps.append(jnp.where(sub == 0, prev, cur))
    tail_ref[...] = x[TS - ntail:TS, :]
    y = cb_ref[...]
    for k in range(CONV_WIDTH):
        delay = CONV_WIDTH - 1 - k
        if delay == 0:
            xs = x
        else:
            xs = jnp.concatenate(wraps[CONV_WIDTH - 1 - delay:]
                                 + [x[0:TS - delay * SCAN_CHUNKS, :]], axis=0)
        y = y + xs * cw_ref[k:k + 1, :]

    yb = y.astype(BF16)
    half = D_RNN // 2
    pre = [jnp.dot(yb[:, hh * half:(hh + 1) * half], wg_ref[hh], preferred_element_type=F32)
           for hh in range(2)]
    pre_a = jnp.concatenate([pre[0][:, :half], pre[1][:, :half]], axis=1)
    pre_x = jnp.concatenate([pre[0][:, half:], pre[1][:, half:]], axis=1)
    r = jax.nn.sigmoid(pre_a + ba_ref[...])
    i = jax.nn.sigmoid(pre_x + bx_ref[...])
    lam = lam_ref[...]
    lsig = jnp.minimum(lam, 0.0) - jnp.log1p(jnp.exp(-jnp.abs(lam)))
    log_a = LRU_C * r * lsig
    a = jnp.exp(log_a)
    u = jnp.sqrt(1.0 - a * a) * (i * y)

    a_s[...] = a
    u_s[...] = u

    def body(j, carry):
        h, p = carry
        blk = pl.multiple_of(j * SCAN_CHUNKS, SCAN_CHUNKS)
        a_j = a_s[pl.ds(blk, SCAN_CHUNKS), :]
        h = a_j * h + u_s[pl.ds(blk, SCAN_CHUNKS), :]
        p = a_j * p
        h_s[pl.ds(blk, SCAN_CHUNKS), :] = h
        p_s[pl.ds(blk, SCAN_CHUNKS), :] = p
        return h, p

    h_end, p_end = lax.fori_loop(
        0, nstep, body,
        (jnp.zeros((SCAN_CHUNKS, D_RNN), F32), jnp.ones((SCAN_CHUNKS, D_RNN), F32)), unroll=8)
    carry = hlast[...]
    h_in = []
    for c in range(SCAN_CHUNKS):
        h_in.append(carry)
        carry = h_end[c:c + 1, :] + p_end[c:c + 1, :] * carry
    hlast[...] = carry
    h_in = jnp.concatenate([jnp.concatenate(h_in, axis=0)] * nstep, axis=0)
    o = (h_s[...] + p_s[...] * h_in) * _silu(gate)

    for l in range(nslab):
        o_scr[l] = o[:, l * LANE:(l + 1) * LANE]
        for c in range(SCAN_CHUNKS):
            out_ref[c * nstep:(c + 1) * nstep, l * LANE:(l + 1) * LANE] = (
                o_scr[l, pl.ds(c, nstep, stride=SCAN_CHUNKS), :].astype(out_ref.dtype))


def _rglru(rnn_x, rnn_gate, cw, cb, wg, ba, bx, lam, B, S):
    T = B * S
    ns = S // TS
    row = lambda b, s: (b * ns + s, 0)
    const2 = lambda b, s: (0, 0)
    return pl.pallas_call(
        _rglru_kernel,
        grid=(B, ns),
        in_specs=[
            pl.BlockSpec((None, D_RNN // LANE, TS, LANE), lambda b, s: (b * ns + s, 0, 0, 0)),
            pl.BlockSpec((None, D_RNN // LANE, TS, LANE), lambda b, s: (b * ns + s, 0, 0, 0)),
            pl.BlockSpec((CONV_WIDTH, D_RNN), const2),
            pl.BlockSpec((1, D_RNN), const2),
            pl.BlockSpec(wg.shape, lambda b, s: (0, 0, 0)),
            pl.BlockSpec((1, D_RNN), const2),
            pl.BlockSpec((1, D_RNN), const2),
            pl.BlockSpec((1, D_RNN), const2),
        ],
        out_specs=pl.BlockSpec((TS, D_RNN), row),
        out_shape=jax.ShapeDtypeStruct((T, D_RNN), BF16),
        scratch_shapes=[
            pltpu.VMEM(((CONV_WIDTH - 1) * SCAN_CHUNKS, D_RNN), F32),
            pltpu.VMEM((TS, D_RNN), F32),
            pltpu.VMEM((TS, D_RNN), F32),
            pltpu.VMEM((TS, D_RNN), F32),
            pltpu.VMEM((TS, D_RNN), F32),
            pltpu.VMEM((1, D_RNN), F32),
            pltpu.VMEM((D_RNN // LANE, TS, LANE), F32),
        ],
        compiler_params=pltpu.CompilerParams(
            dimension_semantics=("arbitrary", "arbitrary"), vmem_limit_bytes=VMEM_LIMIT),
        name="rglru",
    )(rnn_x, rnn_gate, cw, cb, wg, ba, bx, lam)


def _compress_kernel(x_ref, pe_ref, wbig_ref, w2_ref, out_ref):
    nrow = x_ref.shape[1]
    nchunk = nrow // CB
    x = x_ref[0]
    xa = (x + pe_ref[0, 0:1, :]).astype(BF16)
    xb = (x + pe_ref[0, 1:2, :]).astype(BF16)
    nh = N_KV_HEADS * CMP_HIDDEN
    first = jnp.dot(xa, wbig_ref[0, :, :nh], preferred_element_type=F32)
    second = jnp.dot(xb, wbig_ref[0, :, nh:], preferred_element_type=F32)
    hid = _silu(first + pltpu.roll(second, nrow - 1, 0))
    out = jnp.dot(hid.astype(BF16), w2_ref[0], preferred_element_type=F32)
    c_idx = lax.broadcasted_iota(jnp.int32, out.shape, 0) & (nchunk - 1)
    out_ref[0] = jnp.where(c_idx < nchunk - 1, out, 0.0)


def _compress(xc, pe2, wbig, w2bd, B, S):
    nchunk = S // CMP_STRIDE
    nrow = CB * nchunk
    width = CMP_STRIDE * D_KV
    kv = lambda i, j: (i, 0, 0)
    return pl.pallas_call(
        _compress_kernel,
        grid=(2, B // CB),
        in_specs=[
            pl.BlockSpec((1, nrow, width), lambda i, j: (i, j, 0)),
            pl.BlockSpec((1, 2, width), kv),
            pl.BlockSpec((1,) + wbig.shape[1:], kv),
            pl.BlockSpec((1,) + w2bd.shape[1:], kv),
        ],
        out_specs=pl.BlockSpec((1, nrow, D_KV), lambda i, j: (i, j, 0)),
        out_shape=jax.ShapeDtypeStruct((2, B * nchunk, D_KV), F32),
        compiler_params=pltpu.CompilerParams(
            dimension_semantics=("arbitrary", "arbitrary"), vmem_limit_bytes=VMEM_LIMIT),
        name="compress",
    )(xc, pe2, wbig, w2bd)


def _nsa_kernel(qt_ref, cmp_ref, ksel_ref, vselt_ref, kwin_ref, vwint_ref, brgt_ref, ag_ref,
                ovt_ref, out_ref, qpad_ref, pen_ref, o_ref, m_ref, acc_ref, s_ref):
    qi = pl.program_id(1)
    q0 = qi * TQ
    nl = Q_PER_KV * TQ
    nblk = pen_ref.shape[1]
    ncmp = cmp_ref.shape[1]
    groups = range(N_KV_HEADS)
    gsl = [slice(g * HEAD_DIM, (g + 1) * HEAD_DIM) for g in groups]

    def tile4(a):
        return jnp.concatenate([a] * Q_PER_KV, axis=1)

    kc = cmp_ref[0].astype(BF16)
    s_cmp = []
    for g in groups:
        zpad = jnp.zeros((HEAD_DIM, TQ), BF16)
        cols = []
        for r in range(Q_PER_KV):
            hq = g * Q_PER_KV + r
            qh = qt_ref[hq * HEAD_DIM:(hq + 1) * HEAD_DIM, :]
            cols.append(jnp.concatenate([qh, zpad] if g == 0 else [zpad, qh], axis=0))
        qpad = jnp.concatenate(cols, axis=1)
        qpad_ref[g] = qpad
        s_cmp.append(jnp.dot(kc, qpad, preferred_element_type=F32))

    def sel_scores(kt, slot):
        k_tile = ksel_ref[pl.ds(pl.multiple_of(kt * KS, KS), KS), :]
        for g in groups:
            s_ref[slot, g] = jnp.dot(k_tile, qpad_ref[g], preferred_element_type=F32)

    wrows = WINDOW + TQ
    w0 = pl.multiple_of(jnp.maximum(q0 - WINDOW, 0), TQ)
    k_win = kwin_ref[pl.ds(w0, wrows), :]
    s_win = [jnp.dot(k_win, qpad_ref[g], preferred_element_type=F32) for g in groups]

    n_full = q0 // KS
    sel_scores(0, n_full & 1)

    ones = jnp.ones((ONES_ROWS, VC), BF16)

    def vt_aug(v_ref, c0, n, g):
        vt = jnp.concatenate([v_ref[c0 + c, gsl[g], :] for c in range(n)], axis=1)
        return jnp.concatenate([vt, jnp.concatenate([ones] * n, axis=1)], axis=0)

    def normalise(res):
        return res[0:HEAD_DIM, :] * (1.0 / res[HEAD_DIM:HEAD_DIM + 1, :])

    c_row = lax.broadcasted_iota(jnp.int32, (ncmp, TQ), 0)
    t_lane = q0 + lax.broadcasted_iota(jnp.int32, (ncmp, TQ), 1)
    cpen = tile4(jnp.where(c_row * CMP_STRIDE + (CMP_BLOCK - 1) <= t_lane, 0.0, NEG))
    vc_t = cmp_ref[1].T
    for g in groups:
        vct = vc_t[gsl[g], :].astype(BF16)
        s = s_cmp[g] + cpen
        m = jnp.max(s, axis=0, keepdims=True)
        m = jnp.where(m > 0.5 * NEG, m, 0.0)
        e = jnp.exp2(s - m)
        den = jnp.sum(e, axis=0, keepdims=True)
        p = e * (1.0 / jnp.where(den > 0.0, den, 1.0))
        o_ref[0, g] = jnp.dot(vct, p.astype(BF16), preferred_element_type=F32)

        psum = p[:, 0:TQ]
        for r in range(1, Q_PER_KV):
            psum = psum + p[:, r * TQ:(r + 1) * TQ]
        imp = jnp.dot(ovt_ref[...], psum, preferred_element_type=F32,
                      precision=lax.Precision.HIGHEST)
        j_row = lax.broadcasted_iota(jnp.int32, (nblk, TQ), 0)
        t_blk = q0 + lax.broadcasted_iota(jnp.int32, (nblk, TQ), 1)
        cur = jnp.right_shift(t_blk, SEL_BLOCK.bit_length() - 1)
        forced = (j_row == 0) | (j_row == cur) | (j_row == cur - 1)
        val = jnp.where(forced, SEL_FORCE, imp)
        val = jnp.where(j_row * SEL_BLOCK <= t_blk, val, -SEL_FORCE)
        rank = jnp.zeros((nblk, TQ), F32)
        for i in range(nblk):
            vi = jnp.broadcast_to(val[i:i + 1, :], (nblk, TQ))
            rank = rank + jnp.where(j_row > i, jnp.where(vi >= val, 1.0, 0.0),
                                    jnp.where(vi > val, 1.0, 0.0))
        chosen = (rank < float(min(SEL_TOPK, nblk))) & (val > -0.5 * SEL_FORCE)
        selpen = jnp.where(chosen, 0.0, NEG)
        for j in range(nblk):
            pen_ref[g, j] = jnp.broadcast_to(selpen[j:j + 1, :], (8, TQ))

    key = w0 + lax.broadcasted_iota(jnp.int32, (wrows, TQ), 0)
    t_q = q0 + lax.broadcasted_iota(jnp.int32, (wrows, TQ), 1)
    wpen = tile4(jnp.where(key <= t_q, jnp.where(key > t_q - WINDOW, 0.0, NEG), NEG))
    p_win = []
    for g in groups:
        s = s_win[g] + wpen
        m = jnp.max(s, axis=0, keepdims=True)
        p_win.append(jnp.exp2(s - m).astype(BF16))
    for g in groups:
        o_ref[2, g] = normalise(jnp.dot(vt_aug(vwint_ref, w0 // VC, wrows // VC, g), p_win[g],
                                        preferred_element_type=F32))

    m_ref[...] = jnp.full(m_ref.shape, NEG, F32)
    acc_ref[...] = jnp.zeros(acc_ref.shape, F32)
    bpt = KS // SEL_BLOCK
    cpt = KS // VC

    def sel_update(kt, slot, diagonal):
        probs, alphas = [], []
        for g in groups:
            m_prev = m_ref[g]
            if diagonal:
                rows = [jnp.concatenate([pen_ref[g, kt * bpt + jj]] * (SEL_BLOCK // 8), axis=0)
                        for jj in range(bpt)]
                pen = jnp.concatenate(rows, axis=0)
                key = kt * KS + lax.broadcasted_iota(jnp.int32, (KS, TQ), 0)
                t_q = q0 + lax.broadcasted_iota(jnp.int32, (KS, TQ), 1)
                s = s_ref[slot, g] + tile4(pen + jnp.where(key <= t_q, 0.0, NEG))
                m_new = jnp.maximum(m_prev, jnp.max(s, axis=0, keepdims=True))
                p = jnp.exp2(s - m_new)
            else:
                sb = [s_ref[slot, g, jj * SEL_BLOCK:(jj + 1) * SEL_BLOCK, :] for jj in range(bpt)]
                bias = [tile4(pen_ref[g, kt * bpt + jj][0:1, :]) for jj in range(bpt)]
                m_new = m_prev
                for jj in range(bpt):
                    m_new = jnp.maximum(m_new, jnp.max(sb[jj], axis=0, keepdims=True) + bias[jj])
                p = jnp.concatenate([jnp.exp2(sb[jj] + (bias[jj] - m_new)) for jj in range(bpt)],
                                    axis=0)
            probs.append(p.astype(BF16))
            alphas.append(jnp.exp2(m_prev - m_new))
            m_ref[g] = m_new
        for g in groups:
            acc_ref[g] = alphas[g] * acc_ref[g] + jnp.dot(
                vt_aug(vselt_ref, kt * cpt, cpt, g), probs[g], preferred_element_type=F32)

    def sel_body(kt, carry):
        for slot in (0, 1):
            @pl.when(((n_full - kt) & 1) == slot)
            def _(slot=slot):
                sel_scores(kt + 1, 1 - slot)
                sel_update(kt, slot, diagonal=False)
        return carry

    lax.fori_loop(0, n_full, sel_body, 0)
    sel_update(n_full, 0, diagonal=True)
    for g in groups:
        o_ref[1, g] = normalise(acc_ref[g])

    gates = jax.nn.sigmoid(brgt_ref[...])
    group_out = []
    for g in groups:
        heads = []
        for r in range(Q_PER_KV):
            hq = g * Q_PER_KV + r
            ls = slice(r * TQ, (r + 1) * TQ)
            o = gates[hq:hq + 1, :] * o_ref[0, g, :, ls]
            for n in range(1, N_BRANCH):
                o = o + gates[n * N_Q_HEADS + hq:n * N_Q_HEADS + hq + 1, :] * o_ref[n, g, :, ls]
            heads.append(o)
        group_out.append(jnp.concatenate(heads, axis=0).T)

    attn = jnp.concatenate(group_out, axis=1)
    out_ref[...] = (attn * _silu(ag_ref[...])).astype(out_ref.dtype)


def _nsa(qt, cmp_kv, ksel, vselt, kwin, vwint, brgt, ag, ovt, B, S):
    T = B * S
    nq = S // TQ
    qpm = TM // TQ
    nblk = S // SEL_BLOCK
    ncmp = S // CMP_STRIDE
    nl = Q_PER_KV * TQ
    return pl.pallas_call(
        _nsa_kernel,
        grid=(B, nq),
        in_specs=[
            pl.BlockSpec((None, None, D_ATTN, TQ), lambda b, q: (b, q // qpm, 0, q % qpm)),
            pl.BlockSpec((2, ncmp, D_KV), lambda b, q: (0, b, 0)),
            pl.BlockSpec((S, D_KV), lambda b, q: (b, 0)),
            pl.BlockSpec((None, S // VC, D_KV, VC), lambda b, q: (b, 0, 0, 0)),
            pl.BlockSpec((S, D_KV), lambda b, q: (b, 0)),
            pl.BlockSpec((None, S // VC, D_KV, VC), lambda b, q: (b, 0, 0, 0)),
            pl.BlockSpec((None, None, N_GATE_ROWS, TQ), lambda b, q: (b, q // qpm, 0, q % qpm)),
            pl.BlockSpec((TQ, D_ATTN), lambda b, q: (b * nq + q, 0)),
            pl.BlockSpec(ovt.shape, lambda b, q: (0, 0)),
        ],
        out_specs=pl.BlockSpec((TQ, D_ATTN), lambda b, q: (b * nq + q, 0)),
        out_shape=jax.ShapeDtypeStruct((T, D_ATTN), BF16),
        scratch_shapes=[
            pltpu.VMEM((N_KV_HEADS, D_KV, nl), BF16),
            pltpu.VMEM((N_KV_HEADS, nblk, 8, TQ), F32),
            pltpu.VMEM((N_BRANCH, N_KV_HEADS, HEAD_DIM, nl), F32),
            pltpu.VMEM((N_KV_HEADS, 1, nl), F32),
            pltpu.VMEM((N_KV_HEADS, HEAD_DIM + ONES_ROWS, nl), F32),
            pltpu.VMEM((2, N_KV_HEADS, KS, nl), F32),
        ],
        compiler_params=pltpu.CompilerParams(
            dimension_semantics=("arbitrary", "arbitrary"), vmem_limit_bytes=VMEM_LIMIT),
        name="nsa",
    )(qt, cmp_kv, ksel, vselt, kwin, vwint, brgt, ag, ovt)


def _outproj_kernel(x_ref, rnn_ref, attn_ref, wo_ref, nfw_ref, out_ref, *, final_norm):
    y = x_ref[...]
    y = y + jnp.dot(rnn_ref[...], wo_ref[0:D_RNN, :], preferred_element_type=F32)
    y = y + jnp.dot(attn_ref[...], wo_ref[D_RNN:D_MIX, :], preferred_element_type=F32)
    if final_norm:
        ms = jnp.mean(y * y, axis=-1, keepdims=True)
        y = (y * lax.rsqrt(ms + EPS)) * nfw_ref[...]
    out_ref[...] = y


def _outproj(x2, rnn_out, attn_out, wo, nfw, final_norm):
    T = x2.shape[0]
    row = lambda i: (i, 0)
    return pl.pallas_call(
        functools.partial(_outproj_kernel, final_norm=final_norm),
        grid=(T // TM,),
        in_specs=[
            pl.BlockSpec((TM, D_MODEL), row),
            pl.BlockSpec((TM, D_RNN), row),
            pl.BlockSpec((TM, D_ATTN), row),
            pl.BlockSpec((D_MIX, D_MODEL), lambda i: (0, 0)),
            pl.BlockSpec((1, D_MODEL), lambda i: (0, 0)),
        ],
        out_specs=pl.BlockSpec((TM, D_MODEL), row),
        out_shape=jax.ShapeDtypeStruct((T, D_MODEL), F32),
        compiler_params=pltpu.CompilerParams(
            dimension_semantics=("arbitrary",), vmem_limit_bytes=VMEM_LIMIT),
        name="outproj",
    )(x2, rnn_out, attn_out, wo, nfw)


def _block_diag_halves(wa, wx):
    eye = jnp.eye(RNN_HEADS, dtype=wa.dtype)
    full = lambda w: jnp.einsum('hij,hk->hikj', w, eye).reshape(D_RNN, D_RNN)
    fa, fx = full(wa), full(wx)
    half = D_RNN // 2
    return jnp.stack([
        jnp.concatenate([fa[s:s + half, s:s + half], fx[s:s + half, s:s + half]], axis=1)
        for s in (0, half)]).astype(BF16)


def _compress_weights(pe, w1, w2):
    eye = jnp.eye(N_KV_HEADS, dtype=w1.dtype)
    w1r = w1.reshape(2, CMP_STRIDE, HEAD_DIM, CMP_HIDDEN)
    wbig = jnp.einsum('hldn,ge->lgdhen', w1r, eye).reshape(
        CMP_STRIDE * D_KV, 2 * N_KV_HEADS * CMP_HIDDEN)
    w2bd = jnp.einsum('nd,ge->gned', w2, eye).reshape(N_KV_HEADS * CMP_HIDDEN, D_KV)
    per = pe.reshape(2, CMP_STRIDE, 1, HEAD_DIM)
    pe2 = jnp.broadcast_to(per, (2, CMP_STRIDE, N_KV_HEADS, HEAD_DIM)).reshape(2, CMP_STRIDE * D_KV)
    return pe2, wbig.astype(BF16), w2bd.astype(BF16)


def _overlap_t(ncmp_pad, nblk):
    cs = np.arange(ncmp_pad)[None, :] * CMP_STRIDE
    ss = np.arange(nblk)[:, None] * SEL_BLOCK
    ov = np.clip(np.minimum(cs + CMP_BLOCK, ss + SEL_BLOCK) - np.maximum(cs, ss), 0, None)
    return jnp.asarray(ov.astype(np.float32) / CMP_BLOCK)


def kernel(x, norm1_w, w_in, conv_w, conv_b, rg_wa, rg_ba, rg_wx, rg_bx, rg_lambda,
           cmp_k_pe, cmp_k_w1, cmp_k_w2, cmp_v_pe, cmp_v_w1, cmp_v_w2, w_out, normf_w):
    B, S, D = x.shape
    assert D == D_MODEL and w_in.shape[-1] == _D_IN
    assert S % TM == 0 and TS == TM and B % CB == 0
    assert TQ % VC == 0 and WINDOW % TQ == 0 and KS % TQ == 0 and TM % TQ == 0
    assert S % KS == 0 and S >= WINDOW + TQ
    depth = w_in.shape[0]
    T = B * S
    ovt = _overlap_t(S // CMP_STRIDE, S // SEL_BLOCK)
    x2 = x.reshape(T, D)
    for l in range(depth):
        w = w_in[l]
        wn = jnp.concatenate([w[:, _O_RX:_O_Q], w[:, _O_KC:_O_KS], w[:, _O_KS:_O_VS],
                              w[:, _O_KW:_O_VW], w[:, _O_AG:_O_BR]], axis=1).astype(BF16)
        wt = jnp.concatenate([w[:, _O_Q:_O_KC], w[:, _O_VS:_O_KW], w[:, _O_VW:_O_AG],
                              w[:, _O_BR:_D_IN],
                              jnp.zeros((D, N_GATE_ROWS - N_BRANCH * N_Q_HEADS), w.dtype)],
                             axis=1).T.astype(BF16)
        (rnn_x, rnn_gate, kvc, ksel, kwin, ag, qt, vselt, vwint, brgt) = _inproj(
            x2, norm1_w[l].reshape(1, D), wn, wt, B, S)

        rnn_out = _rglru(rnn_x, rnn_gate, conv_w[l], conv_b[l].reshape(1, D_RNN),
                         _block_diag_halves(rg_wa[l], rg_wx[l]),
                         rg_ba[l].reshape(1, D_RNN), rg_bx[l].reshape(1, D_RNN),
                         rg_lambda[l].reshape(1, D_RNN), B, S)

        pk, wbk, w2k = _compress_weights(cmp_k_pe[l], cmp_k_w1[l], cmp_k_w2[l])
        pv, wbv, w2v = _compress_weights(cmp_v_pe[l], cmp_v_w1[l], cmp_v_w2[l])
        cmp_kv = _compress(kvc, jnp.stack([pk, pv]), jnp.stack([wbk, wbv]),
                           jnp.stack([w2k, w2v]), B, S)

        attn_out = _nsa(qt, cmp_kv, ksel, vselt, kwin, vwint, brgt, ag, ovt, B, S)

        x2 = _outproj(x2, rnn_out, attn_out, w_out[l].astype(BF16),
                      normf_w.reshape(1, D), final_norm=(l == depth - 1))
    return x2.reshape(B, S, D)
```

```python
import functools

import numpy as np
import jax
import jax.numpy as jnp
from jax import lax
from jax.experimental import pallas as pl
from jax.experimental.pallas import tpu as pltpu

F32 = jnp.float32
BF16 = jnp.bfloat16

D_MODEL = 1024
EPS = 1e-6
D_RNN = 512
RNN_HEADS = 8
RNN_HEAD_DIM = D_RNN // RNN_HEADS
CONV_WIDTH = 4
LRU_C = 8.0
N_Q_HEADS = 8
N_KV_HEADS = 2
HEAD_DIM = 64
Q_PER_KV = N_Q_HEADS // N_KV_HEADS
D_ATTN = N_Q_HEADS * HEAD_DIM
D_KV = N_KV_HEADS * HEAD_DIM
CMP_BLOCK = 32
CMP_STRIDE = 16
CMP_HIDDEN = 256
SEL_BLOCK = 64
SEL_TOPK = 8
SEL_FORCE = 1e9
WINDOW = 512
N_BRANCH = 3
D_MIX = D_RNN + D_ATTN
N_GATE_ROWS = 32

LANE = 128
SCAN_CHUNKS = 8

TM = 1024
TS = 1024
TQ = 256
KS = 512
VC = 128
CB = 4
NEG = -1e30
LOG2E = 1.4426950408889634
ONES_ROWS = 16
VMEM_V7X = 64 * 1024 * 1024
VMEM_LIMIT = VMEM_V7X - 8 * 1024 * 1024

_O_RX, _O_RG, _O_Q = 0, D_RNN, 2 * D_RNN
_O_KC = _O_Q + D_ATTN
_O_VC = _O_KC + D_KV
_O_KS = _O_VC + D_KV
_O_VS = _O_KS + D_KV
_O_KW = _O_VS + D_KV
_O_VW = _O_KW + D_KV
_O_AG = _O_VW + D_KV
_O_BR = _O_AG + D_ATTN
_D_IN = _O_BR + N_BRANCH * N_Q_HEADS


def _silu(x):
    return x * jax.nn.sigmoid(x)


def _inproj_kernel(x_ref, nw_ref, wn_ref, wt_ref,
                   rnnx_ref, rnng_ref, kvc_ref, ksel_ref, kwin_ref, ag_ref,
                   qt_ref, vselt_ref, vwint_ref, brgt_ref, kv_scr):
    x = x_ref[...]
    ms = jnp.mean(x * x, axis=-1, keepdims=True)
    h = ((x * lax.rsqrt(ms + EPS)) * nw_ref[...]).astype(BF16)

    def nat(a, b):
        return jnp.dot(h, wn_ref[:, a:b], preferred_element_type=F32)

    def tr(a, b):
        return lax.dot_general(wt_ref[a:b, :], h, (((1,), (1,)), ((), ())),
                               preferred_element_type=F32)

    def store_step_major(dst_ref, val):
        steps = TM // SCAN_CHUNKS
        for l in range(D_RNN // LANE):
            for c in range(SCAN_CHUNKS):
                dst_ref[l, pl.ds(c, steps, stride=SCAN_CHUNKS), :] = (
                    val[c * steps:(c + 1) * steps, l * LANE:(l + 1) * LANE])

    store_step_major(rnnx_ref, nat(0, 512))
    store_step_major(rnng_ref, nat(512, 1024))

    kv = nat(1024, 1280)
    for a in range(2):
        kv_scr[a] = kv[:, a * D_KV:(a + 1) * D_KV]
        for l in range(CMP_STRIDE):
            kvc_ref[a, :, l * D_KV:(l + 1) * D_KV] = (
                kv_scr[a, pl.ds(l, TM // CMP_STRIDE, stride=CMP_STRIDE), :])
    kk = nat(1280, 1536)
    ksel_ref[...] = kk[:, :D_KV].astype(BF16)
    kwin_ref[...] = kk[:, D_KV:].astype(BF16)
    ag_ref[...] = nat(1536, 2048)

    qt_ref[...] = (tr(0, 512) * (HEAD_DIM ** -0.5 * LOG2E)).astype(BF16)
    vs = tr(512, 640).astype(BF16)
    for c in range(TM // VC):
        vselt_ref[c] = vs[:, c * VC:(c + 1) * VC]
    vw = tr(640, 768).astype(BF16)
    for c in range(TM // VC):
        vwint_ref[c] = vw[:, c * VC:(c + 1) * VC]
    brgt_ref[...] = tr(768, 768 + N_GATE_ROWS)


def _inproj(x2, nw, wn, wt, B, S):
    T = B * S
    ns = S // TM
    grid = (B, ns)
    row = lambda b, s: (b * ns + s, 0)
    out_shape = (
        jax.ShapeDtypeStruct((T // TM, D_RNN // LANE, TM, LANE), F32),
        jax.ShapeDtypeStruct((T // TM, D_RNN // LANE, TM, LANE), F32),
        jax.ShapeDtypeStruct((2, T // CMP_STRIDE, CMP_STRIDE * D_KV), F32),
        jax.ShapeDtypeStruct((T, D_KV), BF16),
        jax.ShapeDtypeStruct((T, D_KV), BF16),
        jax.ShapeDtypeStruct((T, D_ATTN), F32),
        jax.ShapeDtypeStruct((B, ns, D_ATTN, TM), BF16),
        jax.ShapeDtypeStruct((B, S // VC, D_KV, VC), BF16),
        jax.ShapeDtypeStruct((B, S // VC, D_KV, VC), BF16),
        jax.ShapeDtypeStruct((B, ns, N_GATE_ROWS, TM), F32),
    )
    slab = pl.BlockSpec((None, D_RNN // LANE, TM, LANE), lambda b, s: (b * ns + s, 0, 0, 0))
    out_specs = (
        slab,
        slab,
        pl.BlockSpec((2, TM // CMP_STRIDE, CMP_STRIDE * D_KV), lambda b, s: (0, b * ns + s, 0)),
        pl.BlockSpec((TM, D_KV), row),
        pl.BlockSpec((TM, D_KV), row),
        pl.BlockSpec((TM, D_ATTN), row),
        pl.BlockSpec((None, None, D_ATTN, TM), lambda b, s: (b, s, 0, 0)),
        pl.BlockSpec((None, TM // VC, D_KV, VC), lambda b, s: (b, s, 0, 0)),
        pl.BlockSpec((None, TM // VC, D_KV, VC), lambda b, s: (b, s, 0, 0)),
        pl.BlockSpec((None, None, N_GATE_ROWS, TM), lambda b, s: (b, s, 0, 0)),
    )
    return pl.pallas_call(
        _inproj_kernel,
        grid=grid,
        in_specs=[
            pl.BlockSpec((TM, D_MODEL), row),
            pl.BlockSpec((1, D_MODEL), lambda b, s: (0, 0)),
            pl.BlockSpec(wn.shape, lambda b, s: (0, 0)),
            pl.BlockSpec(wt.shape, lambda b, s: (0, 0)),
        ],
        out_specs=out_specs,
        out_shape=out_shape,
        scratch_shapes=[pltpu.VMEM((2, TM, D_KV), F32)],
        compiler_params=pltpu.CompilerParams(
            dimension_semantics=("arbitrary", "arbitrary"), vmem_limit_bytes=VMEM_LIMIT),
        name="inproj",
    )(x2, nw, wn, wt)


def _rglru_kernel(x_ref, gate_ref, cw_ref, cb_ref, wg_ref, ba_ref, bx_ref, lam_ref,
                  out_ref, tail_ref, a_s, u_s, h_s, p_s, hlast, o_scr):
    sc = pl.program_id(1)
    nstep = TS // SCAN_CHUNKS
    nslab = D_RNN // LANE
    ntail = (CONV_WIDTH - 1) * SCAN_CHUNKS

    @pl.when(sc == 0)
    def _():
        tail_ref[...] = jnp.zeros(tail_ref.shape, F32)
        hlast[...] = jnp.zeros((1, D_RNN), F32)

    x = jnp.concatenate([x_ref[l] for l in range(nslab)], axis=1)
    gate = jnp.concatenate([gate_ref[l] for l in range(nslab)], axis=1)
    sub = lax.broadcasted_iota(jnp.int32, (SCAN_CHUNKS, D_RNN), 0)
    wraps = []
    for j in range(CONV_WIDTH - 1):
        lo = TS - ntail + j * SCAN_CHUNKS
        cur = pltpu.roll(x[lo:lo + SCAN_CHUNKS, :], 1, 0)
        prev = pltpu.roll(tail_ref[j * SCAN_CHUNKS:(j + 1) * SCAN_CHUNKS, :], 1, 0)
        wraps.append(jnp.where(sub == 0, prev, cur))
    tail_ref[...] = x[TS - ntail:TS, :]
    y = cb_ref[...]
    for k in range(CONV_WIDTH):
        delay = CONV_WIDTH - 1 - k
        if delay == 0:
            xs = x
        else:
            xs = jnp.concatenate(wraps[CONV_WIDTH - 1 - delay:]
                                 + [x[0:TS - delay * SCAN_CHUNKS, :]], axis=0)
        y = y + xs * cw_ref[k:k + 1, :]

    yb = y.astype(BF16)
    half = D_RNN // 2
    pre = [jnp.dot(yb[:, hh * half:(hh + 1) * half], wg_ref[hh], preferred_element_type=F32)
           for hh in range(2)]
    pre_a = jnp.concatenate([pre[0][:, :half], pre[1][:, :half]], axis=1)
    pre_x = jnp.concatenate([pre[0][:, half:], pre[1][:, half:]], axis=1)
    r = jax.nn.sigmoid(pre_a + ba_ref[...])
    i = jax.nn.sigmoid(pre_x + bx_ref[...])
    lam = lam_ref[...]
    lsig = jnp.minimum(lam, 0.0) - jnp.log1p(jnp.exp(-jnp.abs(lam)))
    log_a = LRU_C * r * lsig
    a = jnp.exp(log_a)
    u = jnp.sqrt(1.0 - a * a) * (i * y)

    a_s[...] = a
    u_s[...] = u

    def body(j, carry):
        h, p = carry
        blk = pl.multiple_of(j * SCAN_CHUNKS, SCAN_CHUNKS)
        a_j = a_s[pl.ds(blk, SCAN_CHUNKS), :]
        h = a_j * h + u_s[pl.ds(blk, SCAN_CHUNKS), :]
        p = a_j * p
        h_s[pl.ds(blk, SCAN_CHUNKS), :] = h
        p_s[pl.ds(blk, SCAN_CHUNKS), :] = p
        return h, p

    h_end, p_end = lax.fori_loop(
        0, nstep, body,
        (jnp.zeros((SCAN_CHUNKS, D_RNN), F32), jnp.ones((SCAN_CHUNKS, D_RNN), F32)), unroll=8)
    carry = hlast[...]
    h_in = []
    for c in range(SCAN_CHUNKS):
        h_in.append(carry)
        carry = h_end[c:c + 1, :] + p_end[c:c + 1, :] * carry
    hlast[...] = carry
    h_in = jnp.concatenate([jnp.concatenate(h_in, axis=0)] * nstep, axis=0)
    o = (h_s[...] + p_s[...] * h_in) * _silu(gate)

    for l in range(nslab):
        o_scr[l] = o[:, l * LANE:(l + 1) * LANE]
        for c in range(SCAN_CHUNKS):
            out_ref[c * nstep:(c + 1) * nstep, l * LANE:(l + 1) * LANE] = (
                o_scr[l, pl.ds(c, nstep, stride=SCAN_CHUNKS), :].astype(out_ref.dtype))


def _rglru(rnn_x, rnn_gate, cw, cb, wg, ba, bx, lam, B, S):
    T = B * S
    ns = S // TS
    row = lambda b, s: (b * ns + s, 0)
    const2 = lambda b, s: (0, 0)
    return pl.pallas_call(
        _rglru_kernel,
        grid=(B, ns),
        in_specs=[
            pl.BlockSpec((None, D_RNN // LANE, TS, LANE), lambda b, s: (b * ns + s, 0, 0, 0)),
            pl.BlockSpec((None, D_RNN // LANE, TS, LANE), lambda b, s: (b * ns + s, 0, 0, 0)),
            pl.BlockSpec((CONV_WIDTH, D_RNN), const2),
            pl.BlockSpec((1, D_RNN), const2),
            pl.BlockSpec(wg.shape, lambda b, s: (0, 0, 0)),
            pl.BlockSpec((1, D_RNN), const2),
            pl.BlockSpec((1, D_RNN), const2),
            pl.BlockSpec((1, D_RNN), const2),
        ],
        out_specs=pl.BlockSpec((TS, D_RNN), row),
        out_shape=jax.ShapeDtypeStruct((T, D_RNN), BF16),
        scratch_shapes=[
            pltpu.VMEM(((CONV_WIDTH - 1) * SCAN_CHUNKS, D_RNN), F32),
            pltpu.VMEM((TS, D_RNN), F32),
            pltpu.VMEM((TS, D_RNN), F32),
            pltpu.VMEM((TS, D_RNN), F32),
            pltpu.VMEM((TS, D_RNN), F32),
            pltpu.VMEM((1, D_RNN), F32),
            pltpu.VMEM((D_RNN // LANE, TS, LANE), F32),
        ],
        compiler_params=pltpu.CompilerParams(
            dimension_semantics=("arbitrary", "arbitrary"), vmem_limit_bytes=VMEM_LIMIT),
        name="rglru",
    )(rnn_x, rnn_gate, cw, cb, wg, ba, bx, lam)


def _compress_kernel(x_ref, pe_ref, wbig_ref, w2_ref, out_ref):
    nrow = x_ref.shape[1]
    nchunk = nrow // CB
    x = x_ref[0]
    xa = (x + pe_ref[0, 0:1, :]).astype(BF16)
    xb = (x + pe_ref[0, 1:2, :]).astype(BF16)
    nh = N_KV_HEADS * CMP_HIDDEN
    first = jnp.dot(xa, wbig_ref[0, :, :nh], preferred_element_type=F32)
    second = jnp.dot(xb, wbig_ref[0, :, nh:], preferred_element_type=F32)
    hid = _silu(first + pltpu.roll(second, nrow - 1, 0))
    out = jnp.dot(hid.astype(BF16), w2_ref[0], preferred_element_type=F32)
    c_idx = lax.broadcasted_iota(jnp.int32, out.shape, 0) & (nchunk - 1)
    out_ref[0] = jnp.where(c_idx < nchunk - 1, out, 0.0)


def _compress(xc, pe2, wbig, w2bd, B, S):
    nchunk = S // CMP_STRIDE
    nrow = CB * nchunk
    width = CMP_STRIDE * D_KV
    kv = lambda i, j: (i, 0, 0)
    return pl.pallas_call(
        _compress_kernel,
        grid=(2, B // CB),
        in_specs=[
            pl.BlockSpec((1, nrow, width), lambda i, j: (i, j, 0)),
            pl.BlockSpec((1, 2, width), kv),
            pl.BlockSpec((1,) + wbig.shape[1:], kv),
            pl.BlockSpec((1,) + w2bd.shape[1:], kv),
        ],
        out_specs=pl.BlockSpec((1, nrow, D_KV), lambda i, j: (i, j, 0)),
        out_shape=jax.ShapeDtypeStruct((2, B * nchunk, D_KV), F32),
        compiler_params=pltpu.CompilerParams(
            dimension_semantics=("arbitrary", "arbitrary"), vmem_limit_bytes=VMEM_LIMIT),
        name="compress",
    )(xc, pe2, wbig, w2bd)


def _nsa_kernel(qt_ref, cmp_ref, ksel_ref, vselt_ref, kwin_ref, vwint_ref, brgt_ref, ag_ref,
                ovt_ref, out_ref, qpad_ref, pen_ref, o_ref, m_ref, acc_ref, s_ref):
    qi = pl.program_id(1)
    q0 = qi * TQ
    nl = Q_PER_KV * TQ
    nblk = pen_ref.shape[1]
    ncmp = cmp_ref.shape[1]
    groups = range(N_KV_HEADS)
    gsl = [slice(g * HEAD_DIM, (g + 1) * HEAD_DIM) for g in groups]

    def tile4(a):
        return jnp.concatenate([a] * Q_PER_KV, axis=1)

    kc = cmp_ref[0].astype(BF16)
    s_cmp = []
    for g in groups:
        zpad = jnp.zeros((HEAD_DIM, TQ), BF16)
        cols = []
        for r in range(Q_PER_KV):
            hq = g * Q_PER_KV + r
            qh = qt_ref[hq * HEAD_DIM:(hq + 1) * HEAD_DIM, :]
            cols.append(jnp.concatenate([qh, zpad] if g == 0 else [zpad, qh], axis=0))
        qpad = jnp.concatenate(cols, axis=1)
        qpad_ref[g] = qpad
        s_cmp.append(jnp.dot(kc, qpad, preferred_element_type=F32))

    def sel_scores(kt, slot):
        k_tile = ksel_ref[pl.ds(pl.multiple_of(kt * KS, KS), KS), :]
        for g in groups:
            s_ref[slot, g] = jnp.dot(k_tile, qpad_ref[g], preferred_element_type=F32)

    wrows = WINDOW + TQ
    w0 = pl.multiple_of(jnp.maximum(q0 - WINDOW, 0), TQ)
    k_win = kwin_ref[pl.ds(w0, wrows), :]
    s_win = [jnp.dot(k_win, qpad_ref[g], preferred_element_type=F32) for g in groups]

    n_full = q0 // KS
    sel_scores(0, n_full & 1)

    ones = jnp.ones((ONES_ROWS, VC), BF16)

    def vt_aug(v_ref, c0, n, g):
        vt = jnp.concatenate([v_ref[c0 + c, gsl[g], :] for c in range(n)], axis=1)
        return jnp.concatenate([vt, jnp.concatenate([ones] * n, axis=1)], axis=0)

    def normalise(res):
        return res[0:HEAD_DIM, :] * (1.0 / res[HEAD_DIM:HEAD_DIM + 1, :])

    c_row = lax.broadcasted_iota(jnp.int32, (ncmp, TQ), 0)
    t_lane = q0 + lax.broadcasted_iota(jnp.int32, (ncmp, TQ), 1)
    cpen = tile4(jnp.where(c_row * CMP_STRIDE + (CMP_BLOCK - 1) <= t_lane, 0.0, NEG))
    vc_t = cmp_ref[1].T
    for g in groups:
        vct = vc_t[gsl[g], :].astype(BF16)
        s = s_cmp[g] + cpen
        m = jnp.max(s, axis=0, keepdims=True)
        m = jnp.where(m > 0.5 * NEG, m, 0.0)
        e = jnp.exp2(s - m)
        den = jnp.sum(e, axis=0, keepdims=True)
        p = e * (1.0 / jnp.where(den > 0.0, den, 1.0))
        o_ref[0, g] = jnp.dot(vct, p.astype(BF16), preferred_element_type=F32)

        psum = p[:, 0:TQ]
        for r in range(1, Q_PER_KV):
            psum = psum + p[:, r * TQ:(r + 1) * TQ]
        imp = jnp.dot(ovt_ref[...], psum, preferred_element_type=F32,
                      precision=lax.Precision.HIGHEST)
        j_row = lax.broadcasted_iota(jnp.int32, (nblk, TQ), 0)
        t_blk = q0 + lax.broadcasted_iota(jnp.int32, (nblk, TQ), 1)
        cur = jnp.right_shift(t_blk, SEL_BLOCK.bit_length() - 1)
        forced = (j_row == 0) | (j_row == cur) | (j_row == cur - 1)
        val = jnp.where(forced, SEL_FORCE, imp)
        val = jnp.where(j_row * SEL_BLOCK <= t_blk, val, -SEL_FORCE)
        rank = jnp.zeros((nblk, TQ), F32)
        for i in range(nblk):
            vi = jnp.broadcast_to(val[i:i + 1, :], (nblk, TQ))
            rank = rank + jnp.where(j_row > i, jnp.where(vi >= val, 1.0, 0.0),
                                    jnp.where(vi > val, 1.0, 0.0))
        chosen = (rank < float(min(SEL_TOPK, nblk))) & (val > -0.5 * SEL_FORCE)
        selpen = jnp.where(chosen, 0.0, NEG)
        for j in range(nblk):
            pen_ref[g, j] = jnp.broadcast_to(selpen[j:j + 1, :], (8, TQ))

    key = w0 + lax.broadcasted_iota(jnp.int32, (wrows, TQ), 0)
    t_q = q0 + lax.broadcasted_iota(jnp.int32, (wrows, TQ), 1)
    wpen = tile4(jnp.where(key <= t_q, jnp.where(key > t_q - WINDOW, 0.0, NEG), NEG))
    p_win = []
    for g in groups:
        s = s_win[g] + wpen
        m = jnp.max(s, axis=0, keepdims=True)
        p_win.append(jnp.exp2(s - m).astype(BF16))
    for g in groups:
        o_ref[2, g] = normalise(jnp.dot(vt_aug(vwint_ref, w0 // VC, wrows // VC, g), p_win[g],
                                        preferred_element_type=F32))

    m_ref[...] = jnp.full(m_ref.shape, NEG, F32)
    acc_ref[...] = jnp.zeros(acc_ref.shape, F32)
    bpt = KS // SEL_BLOCK
    cpt = KS // VC

    def sel_update(kt, slot, diagonal):
        probs, alphas = [], []
        for g in groups:
            m_prev = m_ref[g]
            if diagonal:
                rows = [jnp.concatenate([pen_ref[g, kt * bpt + jj]] * (SEL_BLOCK // 8), axis=0)
                        for jj in range(bpt)]
                pen = jnp.concatenate(rows, axis=0)
                key = kt * KS + lax.broadcasted_iota(jnp.int32, (KS, TQ), 0)
                t_q = q0 + lax.broadcasted_iota(jnp.int32, (KS, TQ), 1)
                s = s_ref[slot, g] + tile4(pen + jnp.where(key <= t_q, 0.0, NEG))
                m_new = jnp.maximum(m_prev, jnp.max(s, axis=0, keepdims=True))
                p = jnp.exp2(s - m_new)
            else:
                sb = [s_ref[slot, g, jj * SEL_BLOCK:(jj + 1) * SEL_BLOCK, :] for jj in range(bpt)]
                bias = [tile4(pen_ref[g, kt * bpt + jj][0:1, :]) for jj in range(bpt)]
                m_new = m_prev
                for jj in range(bpt):
                    m_new = jnp.maximum(m_new, jnp.max(sb[jj], axis=0, keepdims=True) + bias[jj])
                p = jnp.concatenate([jnp.exp2(sb[jj] + (bias[jj] - m_new)) for jj in range(bpt)],
                                    axis=0)
            probs.append(p.astype(BF16))
            alphas.append(jnp.exp2(m_prev - m_new))
            m_ref[g] = m_new
        for g in groups:
            acc_ref[g] = alphas[g] * acc_ref[g] + jnp.dot(
                vt_aug(vselt_ref, kt * cpt, cpt, g), probs[g], preferred_element_type=F32)

    def sel_body(kt, carry):
        for slot in (0, 1):
            @pl.when(((n_full - kt) & 1) == slot)
            def _(slot=slot):
                sel_scores(kt + 1, 1 - slot)
                sel_update(kt, slot, diagonal=False)
        return carry

    lax.fori_loop(0, n_full, sel_body, 0)
    sel_update(n_full, 0, diagonal=True)
    for g in groups:
        o_ref[1, g] = normalise(acc_ref[g])

    gates = jax.nn.sigmoid(brgt_ref[...])
    group_out = []
    for g in groups:
        heads = []
        for r in range(Q_PER_KV):
            hq = g * Q_PER_KV + r
            ls = slice(r * TQ, (r + 1) * TQ)
            o = gates[hq:hq + 1, :] * o_ref[0, g, :, ls]
            for n in range(1, N_BRANCH):
                o = o + gates[n * N_Q_HEADS + hq:n * N_Q_HEADS + hq + 1, :] * o_ref[n, g, :, ls]
            heads.append(o)
        group_out.append(jnp.concatenate(heads, axis=0).T)

    attn = jnp.concatenate(group_out, axis=1)
    out_ref[...] = (attn * _silu(ag_ref[...])).astype(out_ref.dtype)


def _nsa(qt, cmp_kv, ksel, vselt, kwin, vwint, brgt, ag, ovt, B, S):
    T = B * S
    nq = S // TQ
    qpm = TM // TQ
    nblk = S // SEL_BLOCK
    ncmp = S // CMP_STRIDE
    nl = Q_PER_KV * TQ
    return pl.pallas_call(
        _nsa_kernel,
        grid=(B, nq),
        in_specs=[
            pl.BlockSpec((None, None, D_ATTN, TQ), lambda b, q: (b, q // qpm, 0, q % qpm)),
            pl.BlockSpec((2, ncmp, D_KV), lambda b, q: (0, b, 0)),
            pl.BlockSpec((S, D_KV), lambda b, q: (b, 0)),
            pl.BlockSpec((None, S // VC, D_KV, VC), lambda b, q: (b, 0, 0, 0)),
            pl.BlockSpec((S, D_KV), lambda b, q: (b, 0)),
            pl.BlockSpec((None, S // VC, D_KV, VC), lambda b, q: (b, 0, 0, 0)),
            pl.BlockSpec((None, None, N_GATE_ROWS, TQ), lambda b, q: (b, q // qpm, 0, q % qpm)),
            pl.BlockSpec((TQ, D_ATTN), lambda b, q: (b * nq + q, 0)),
            pl.BlockSpec(ovt.shape, lambda b, q: (0, 0)),
        ],
        out_specs=pl.BlockSpec((TQ, D_ATTN), lambda b, q: (b * nq + q, 0)),
        out_shape=jax.ShapeDtypeStruct((T, D_ATTN), BF16),
        scratch_shapes=[
            pltpu.VMEM((N_KV_HEADS, D_KV, nl), BF16),
            pltpu.VMEM((N_KV_HEADS, nblk, 8, TQ), F32),
            pltpu.VMEM((N_BRANCH, N_KV_HEADS, HEAD_DIM, nl), F32),
            pltpu.VMEM((N_KV_HEADS, 1, nl), F32),
            pltpu.VMEM((N_KV_HEADS, HEAD_DIM + ONES_ROWS, nl), F32),
            pltpu.VMEM((2, N_KV_HEADS, KS, nl), F32),
        ],
        compiler_params=pltpu.CompilerParams(
            dimension_semantics=("arbitrary", "arbitrary"), vmem_limit_bytes=VMEM_LIMIT),
        name="nsa",
    )(qt, cmp_kv, ksel, vselt, kwin, vwint, brgt, ag, ovt)


def _outproj_kernel(x_ref, rnn_ref, attn_ref, wo_ref, nfw_ref, out_ref, *, final_norm):
    y = x_ref[...]
    y = y + jnp.dot(rnn_ref[...], wo_ref[0:D_RNN, :], preferred_element_type=F32)
    y = y + jnp.dot(attn_ref[...], wo_ref[D_RNN:D_MIX, :], preferred_element_type=F32)
    if final_norm:
        ms = jnp.mean(y * y, axis=-1, keepdims=True)
        y = (y * lax.rsqrt(ms + EPS)) * nfw_ref[...]
    out_ref[...] = y


def _outproj(x2, rnn_out, attn_out, wo, nfw, final_norm):
    T = x2.shape[0]
    row = lambda i: (i, 0)
    return pl.pallas_call(
        functools.partial(_outproj_kernel, final_norm=final_norm),
        grid=(T // TM,),
        in_specs=[
            pl.BlockSpec((TM, D_MODEL), row),
            pl.BlockSpec((TM, D_RNN), row),
            pl.BlockSpec((TM, D_ATTN), row),
            pl.BlockSpec((D_MIX, D_MODEL), lambda i: (0, 0)),
            pl.BlockSpec((1, D_MODEL), lambda i: (0, 0)),
        ],
        out_specs=pl.BlockSpec((TM, D_MODEL), row),
        out_shape=jax.ShapeDtypeStruct((T, D_MODEL), F32),
        compiler_params=pltpu.CompilerParams(
            dimension_semantics=("arbitrary",), vmem_limit_bytes=VMEM_LIMIT),
        name="outproj",
    )(x2, rnn_out, attn_out, wo, nfw)


def _block_diag_halves(wa, wx):
    eye = jnp.eye(RNN_HEADS, dtype=wa.dtype)
    full = lambda w: jnp.einsum('hij,hk->hikj', w, eye).reshape(D_RNN, D_RNN)
    fa, fx = full(wa), full(wx)
    half = D_RNN // 2
    return jnp.stack([
        jnp.concatenate([fa[s:s + half, s:s + half], fx[s:s + half, s:s + half]], axis=1)
        for s in (0, half)]).astype(BF16)


def _compress_weights(pe, w1, w2):
    eye = jnp.eye(N_KV_HEADS, dtype=w1.dtype)
    w1r = w1.reshape(2, CMP_STRIDE, HEAD_DIM, CMP_HIDDEN)
    wbig = jnp.einsum('hldn,ge->lgdhen', w1r, eye).reshape(
        CMP_STRIDE * D_KV, 2 * N_KV_HEADS * CMP_HIDDEN)
    w2bd = jnp.einsum('nd,ge->gned', w2, eye).reshape(N_KV_HEADS * CMP_HIDDEN, D_KV)
    per = pe.reshape(2, CMP_STRIDE, 1, HEAD_DIM)
    pe2 = jnp.broadcast_to(per, (2, CMP_STRIDE, N_KV_HEADS, HEAD_DIM)).reshape(2, CMP_STRIDE * D_KV)
    return pe2, wbig.astype(BF16), w2bd.astype(BF16)


def _overlap_t(ncmp_pad, nblk):
    cs = np.arange(ncmp_pad)[None, :] * CMP_STRIDE
    ss = np.arange(nblk)[:, None] * SEL_BLOCK
    ov = np.clip(np.minimum(cs + CMP_BLOCK, ss + SEL_BLOCK) - np.maximum(cs, ss), 0, None)
    return jnp.asarray(ov.astype(np.float32) / CMP_BLOCK)


def kernel(x, norm1_w, w_in, conv_w, conv_b, rg_wa, rg_ba, rg_wx, rg_bx, rg_lambda,
           cmp_k_pe, cmp_k_w1, cmp_k_w2, cmp_v_pe, cmp_v_w1, cmp_v_w2, w_out, normf_w):
    B, S, D = x.shape
    assert D == D_MODEL and w_in.shape[-1] == _D_IN
    assert S % TM == 0 and TS == TM and B % CB == 0
    assert TQ % VC == 0 and WINDOW % TQ == 0 and KS % TQ == 0 and TM % TQ == 0
    assert S % KS == 0 and S >= WINDOW + TQ
    depth = w_in.shape[0]
    T = B * S
    ovt = _overlap_t(S // CMP_STRIDE, S // SEL_BLOCK)
    x2 = x.reshape(T, D)
    for l in range(depth):
        w = w_in[l]
        wn = jnp.concatenate([w[:, _O_RX:_O_Q], w[:, _O_KC:_O_KS], w[:, _O_KS:_O_VS],
                              w[:, _O_KW:_O_VW], w[:, _O_AG:_O_BR]], axis=1).astype(BF16)
        wt = jnp.concatenate([w[:, _O_Q:_O_KC], w[:, _O_VS:_O_KW], w[:, _O_VW:_O_AG],
                              w[:, _O_BR:_D_IN],
                              jnp.zeros((D, N_GATE_ROWS - N_BRANCH * N_Q_HEADS), w.dtype)],
                             axis=1).T.astype(BF16)
        (rnn_x, rnn_gate, kvc, ksel, kwin, ag, qt, vselt, vwint, brgt) = _inproj(
            x2, norm1_w[l].reshape(1, D), wn, wt, B, S)

        rnn_out = _rglru(rnn_x, rnn_gate, conv_w[l], conv_b[l].reshape(1, D_RNN),
                         _block_diag_halves(rg_wa[l], rg_wx[l]),
                         rg_ba[l].reshape(1, D_RNN), rg_bx[l].reshape(1, D_RNN),
                         rg_lambda[l].reshape(1, D_RNN), B, S)

        pk, wbk, w2k = _compress_weights(cmp_k_pe[l], cmp_k_w1[l], cmp_k_w2[l])
        pv, wbv, w2v = _compress_weights(cmp_v_pe[l], cmp_v_w1[l], cmp_v_w2[l])
        cmp_kv = _compress(kvc, jnp.stack([pk, pv]), jnp.stack([wbk, wbv]),
                           jnp.stack([w2k, w2v]), B, S)

        attn_out = _nsa(qt, cmp_kv, ksel, vselt, kwin, vwint, brgt, ag, ovt, B, S)

        x2 = _outproj(x2, rnn_out, attn_out, w_out[l].astype(BF16),
                      normf_w.reshape(1, D), final_norm=(l == depth - 1))
    return x2.reshape(B, S, D)
```

```python
import functools

import numpy as np
import jax
import jax.numpy as jnp
from jax import lax
from jax.experimental import pallas as pl
from jax.experimental.pallas import tpu as pltpu

F32 = jnp.float32
BF16 = jnp.bfloat16

D_MODEL = 1024
EPS = 1e-6
D_RNN = 512
RNN_HEADS = 8
RNN_HEAD_DIM = D_RNN // RNN_HEADS
CONV_WIDTH = 4
LRU_C = 8.0
N_Q_HEADS = 8
N_KV_HEADS = 2
HEAD_DIM = 64
Q_PER_KV = N_Q_HEADS // N_KV_HEADS
D_ATTN = N_Q_HEADS * HEAD_DIM
D_KV = N_KV_HEADS * HEAD_DIM
CMP_BLOCK = 32
CMP_STRIDE = 16
CMP_HIDDEN = 256
SEL_BLOCK = 64
SEL_TOPK = 8
SEL_FORCE = 1e9
WINDOW = 512
N_BRANCH = 3
D_MIX = D_RNN + D_ATTN
N_GATE_ROWS = 32

LANE = 128
SCAN_CHUNKS = 8

TM = 1024
TS = 1024
TQ = 256
KS = 512
VC = 128
CB = 4
RNN_ROW_CHUNKS = 4
NEG = -1e30
LOG2E = 1.4426950408889634
ONES_ROWS = 16
VMEM_V7X = 64 * 1024 * 1024
VMEM_LIMIT = VMEM_V7X - 8 * 1024 * 1024

_O_RX, _O_RG, _O_Q = 0, D_RNN, 2 * D_RNN
_O_KC = _O_Q + D_ATTN
_O_VC = _O_KC + D_KV
_O_KS = _O_VC + D_KV
_O_VS = _O_KS + D_KV
_O_KW = _O_VS + D_KV
_O_VW = _O_KW + D_KV
_O_AG = _O_VW + D_KV
_O_BR = _O_AG + D_ATTN
_D_IN = _O_BR + N_BRANCH * N_Q_HEADS


def _silu(x):
    return x * jax.nn.sigmoid(x)


def _inproj_kernel(x_ref, nw_ref, wn_ref, cw_ref, cb_ref, wg_ref, ba_ref, bx_ref, lam_ref,
                   rnn_ref, kvc_ref, ksel_ref, kwin_ref, ag_ref,
                   q_ref, vselt_ref, vwint_ref, brg_ref,
                   kv_scr, rx_scr, rg_scr, tail_ref, a_s, u_s, h_s, p_s, hlast, o_scr):
    @pl.when(pl.program_id(1) == 0)
    def _():
        tail_ref[...] = jnp.zeros(tail_ref.shape, F32)
        hlast[...] = jnp.zeros(hlast.shape, F32)

    x = x_ref[...]
    ms = jnp.mean(x * x, axis=-1, keepdims=True)
    h = ((x * lax.rsqrt(ms + EPS)) * nw_ref[...]).astype(BF16)

    def nat(a, b):
        return jnp.dot(h, wn_ref[:, a:b], preferred_element_type=F32)

    def store_step_major(dst_ref, val):
        steps = TM // SCAN_CHUNKS
        for l in range(D_RNN // LANE):
            for c in range(SCAN_CHUNKS):
                dst_ref[l, pl.ds(c, steps, stride=SCAN_CHUNKS), :] = (
                    val[c * steps:(c + 1) * steps, l * LANE:(l + 1) * LANE])

    store_step_major(rx_scr, nat(0, 512))
    store_step_major(rg_scr, nat(512, 1024))

    def proj_cmp():
        kv = nat(1024, 1280)
        for a in range(2):
            kv_scr[a] = kv[:, a * D_KV:(a + 1) * D_KV]
            for l in range(CMP_STRIDE):
                kvc_ref[a, :, l * D_KV:(l + 1) * D_KV] = (
                    kv_scr[a, pl.ds(l, TM // CMP_STRIDE, stride=CMP_STRIDE), :])

    def proj_k():
        kk = nat(1280, 1536)
        ksel_ref[...] = kk[:, :D_KV].astype(BF16)
        kwin_ref[...] = kk[:, D_KV:].astype(BF16)

    def proj_gate():
        ag_ref[...] = nat(1536, 2048)

    def proj_q():
        q_ref[...] = (nat(2048, 2560) * (HEAD_DIM ** -0.5 * LOG2E)).astype(BF16)

    def proj_v():
        vv = nat(2560, 2816)
        for a, dst in enumerate((vselt_ref, vwint_ref)):
            vt = vv[:, a * D_KV:(a + 1) * D_KV].T.astype(BF16)
            for c in range(TM // VC):
                dst[c] = vt[:, c * VC:(c + 1) * VC]
        brg_ref[...] = nat(2816, 2816 + LANE)

    projections = (proj_gate, proj_q, proj_k, proj_cmp, proj_v)
    wraps = _rglru_wraps(rx_scr, tail_ref)
    rows = TM // RNN_ROW_CHUNKS
    for ci in range(RNN_ROW_CHUNKS):
        _rglru_gates(rx_scr, wraps, ci * rows, (ci + 1) * rows,
                     cw_ref, cb_ref, wg_ref, ba_ref, bx_ref, lam_ref, a_s, u_s)
        for proj in projections[ci::RNN_ROW_CHUNKS]:
            proj()

    _rglru_scan(rg_scr, rnn_ref, a_s, u_s, h_s, p_s, hlast, o_scr)


def _inproj(x2, nw, wn, cw, cb, wg, ba, bx, lam, B, S):
    T = B * S
    ns = S // TM
    grid = (B, ns)
    row = lambda b, s: (b * ns + s, 0)
    const2 = lambda b, s: (0, 0)
    out_shape = (
        jax.ShapeDtypeStruct((T, D_RNN), BF16),
        jax.ShapeDtypeStruct((2, T // CMP_STRIDE, CMP_STRIDE * D_KV), F32),
        jax.ShapeDtypeStruct((T, D_KV), BF16),
        jax.ShapeDtypeStruct((T, D_KV), BF16),
        jax.ShapeDtypeStruct((T, D_ATTN), F32),
        jax.ShapeDtypeStruct((T, D_ATTN), BF16),
        jax.ShapeDtypeStruct((B, S // VC, D_KV, VC), BF16),
        jax.ShapeDtypeStruct((B, S // VC, D_KV, VC), BF16),
        jax.ShapeDtypeStruct((T, LANE), F32),
    )
    out_specs = (
        pl.BlockSpec((TM, D_RNN), row),
        pl.BlockSpec((2, TM // CMP_STRIDE, CMP_STRIDE * D_KV), lambda b, s: (0, b * ns + s, 0)),
        pl.BlockSpec((TM, D_KV), row),
        pl.BlockSpec((TM, D_KV), row),
        pl.BlockSpec((TM, D_ATTN), row),
        pl.BlockSpec((TM, D_ATTN), row),
        pl.BlockSpec((None, TM // VC, D_KV, VC), lambda b, s: (b, s, 0, 0)),
        pl.BlockSpec((None, TM // VC, D_KV, VC), lambda b, s: (b, s, 0, 0)),
        pl.BlockSpec((TM, LANE), row),
    )
    return pl.pallas_call(
        _inproj_kernel,
        grid=grid,
        in_specs=[
            pl.BlockSpec((TM, D_MODEL), row),
            pl.BlockSpec((1, D_MODEL), const2),
            pl.BlockSpec(wn.shape, const2),
            pl.BlockSpec((CONV_WIDTH, D_RNN), const2),
            pl.BlockSpec((1, D_RNN), const2),
            pl.BlockSpec(wg.shape, lambda b, s: (0, 0, 0)),
            pl.BlockSpec((1, D_RNN), const2),
            pl.BlockSpec((1, D_RNN), const2),
            pl.BlockSpec((1, D_RNN), const2),
        ],
        out_specs=out_specs,
        out_shape=out_shape,
        scratch_shapes=[
            pltpu.VMEM((2, TM, D_KV), F32),
            pltpu.VMEM((D_RNN // LANE, TM, LANE), F32),
            pltpu.VMEM((D_RNN // LANE, TM, LANE), F32),
            pltpu.VMEM(((CONV_WIDTH - 1) * SCAN_CHUNKS, D_RNN), F32),
            pltpu.VMEM((TM, D_RNN), F32),
            pltpu.VMEM((TM, D_RNN), F32),
            pltpu.VMEM((TM, D_RNN), F32),
            pltpu.VMEM((TM, D_RNN), F32),
            pltpu.VMEM((1, D_RNN), F32),
            pltpu.VMEM((D_RNN // LANE, TM, LANE), F32),
        ],
        compiler_params=pltpu.CompilerParams(
            dimension_semantics=("arbitrary", "arbitrary"), vmem_limit_bytes=VMEM_LIMIT),
        name="inproj",
    )(x2, nw, wn, cw, cb, wg, ba, bx, lam)


def _slab_rows(x_ref, lo, hi):
    return jnp.concatenate([x_ref[l, lo:hi, :] for l in range(D_RNN // LANE)], axis=1)


def _rglru_wraps(x_ref, tail_ref):
    ntail = (CONV_WIDTH - 1) * SCAN_CHUNKS
    last = _slab_rows(x_ref, TS - ntail, TS)
    sub = lax.broadcasted_iota(jnp.int32, (SCAN_CHUNKS, D_RNN), 0)
    wraps = []
    for j in range(CONV_WIDTH - 1):
        rows = slice(j * SCAN_CHUNKS, (j + 1) * SCAN_CHUNKS)
        cur = pltpu.roll(last[rows, :], 1, 0)
        prev = pltpu.roll(tail_ref[rows, :], 1, 0)
        wraps.append(jnp.where(sub == 0, prev, cur))
    tail_ref[...] = last
    return wraps


def _rglru_gates(x_ref, wraps, lo, hi, cw_ref, cb_ref, wg_ref, ba_ref, bx_ref, lam_ref, a_s, u_s):
    y = cb_ref[...]
    for k in range(CONV_WIDTH):
        off = (CONV_WIDTH - 1 - k) * SCAN_CHUNKS
        if lo >= off:
            xs = _slab_rows(x_ref, lo - off, hi - off)
        else:
            assert lo == 0
            xs = jnp.concatenate(wraps[len(wraps) - off // SCAN_CHUNKS:]
                                 + [_slab_rows(x_ref, 0, hi - off)], axis=0)
        y = y + xs * cw_ref[k:k + 1, :]

    yb = y.astype(BF16)
    half = D_RNN // 2
    pre = [jnp.dot(yb[:, hh * half:(hh + 1) * half], wg_ref[hh], preferred_element_type=F32)
           for hh in range(2)]
    pre_a = jnp.concatenate([pre[0][:, :half], pre[1][:, :half]], axis=1)
    pre_x = jnp.concatenate([pre[0][:, half:], pre[1][:, half:]], axis=1)
    r = jax.nn.sigmoid(pre_a + ba_ref[...])
    i = jax.nn.sigmoid(pre_x + bx_ref[...])
    lam = lam_ref[...]
    lsig = jnp.minimum(lam, 0.0) - jnp.log1p(jnp.exp(-jnp.abs(lam)))
    log_a = LRU_C * r * lsig
    a = jnp.exp(log_a)
    u = jnp.sqrt(1.0 - a * a) * (i * y)
    a_s[lo:hi, :] = a
    u_s[lo:hi, :] = u


def _rglru_scan(gate_ref, out_ref, a_s, u_s, h_s, p_s, hlast, o_scr):
    nstep = TS // SCAN_CHUNKS
    nslab = D_RNN // LANE
    gate = jnp.concatenate([gate_ref[l] for l in range(nslab)], axis=1)

    def body(j, carry):
        h, p = carry
        blk = pl.multiple_of(j * SCAN_CHUNKS, SCAN_CHUNKS)
        a_j = a_s[pl.ds(blk, SCAN_CHUNKS), :]
        h = a_j * h + u_s[pl.ds(blk, SCAN_CHUNKS), :]
        p = a_j * p
        h_s[pl.ds(blk, SCAN_CHUNKS), :] = h
        p_s[pl.ds(blk, SCAN_CHUNKS), :] = p
        return h, p

    h_end, p_end = lax.fori_loop(
        0, nstep, body,
        (jnp.zeros((SCAN_CHUNKS, D_RNN), F32), jnp.ones((SCAN_CHUNKS, D_RNN), F32)), unroll=8)
    carry = hlast[...]
    h_in = []
    for c in range(SCAN_CHUNKS):
        h_in.append(carry)
        carry = h_end[c:c + 1, :] + p_end[c:c + 1, :] * carry
    hlast[...] = carry
    h_in = jnp.concatenate([jnp.concatenate(h_in, axis=0)] * nstep, axis=0)
    o = (h_s[...] + p_s[...] * h_in) * _silu(gate)

    for l in range(nslab):
        o_scr[l] = o[:, l * LANE:(l + 1) * LANE]
        for c in range(SCAN_CHUNKS):
            out_ref[c * nstep:(c + 1) * nstep, l * LANE:(l + 1) * LANE] = (
                o_scr[l, pl.ds(c, nstep, stride=SCAN_CHUNKS), :].astype(out_ref.dtype))


def _compress_kernel(x_ref, pe_ref, wbig_ref, w2_ref, out_ref):
    nrow = x_ref.shape[1]
    nchunk = nrow // CB
    x = x_ref[0]
    xa = (x + pe_ref[0, 0:1, :]).astype(BF16)
    xb = (x + pe_ref[0, 1:2, :]).astype(BF16)
    nh = N_KV_HEADS * CMP_HIDDEN
    first = jnp.dot(xa, wbig_ref[0, :, :nh], preferred_element_type=F32)
    second = jnp.dot(xb, wbig_ref[0, :, nh:], preferred_element_type=F32)
    hid = _silu(first + pltpu.roll(second, nrow - 1, 0))
    out = jnp.dot(hid.astype(BF16), w2_ref[0], preferred_element_type=F32)
    c_idx = lax.broadcasted_iota(jnp.int32, out.shape, 0) & (nchunk - 1)
    out_ref[0] = jnp.where(c_idx < nchunk - 1, out, 0.0)


def _compress(xc, pe2, wbig, w2bd, B, S):
    nchunk = S // CMP_STRIDE
    nrow = CB * nchunk
    width = CMP_STRIDE * D_KV
    kv = lambda i, j: (i, 0, 0)
    return pl.pallas_call(
        _compress_kernel,
        grid=(2, B // CB),
        in_specs=[
            pl.BlockSpec((1, nrow, width), lambda i, j: (i, j, 0)),
            pl.BlockSpec((1, 2, width), kv),
            pl.BlockSpec((1,) + wbig.shape[1:], kv),
            pl.BlockSpec((1,) + w2bd.shape[1:], kv),
        ],
        out_specs=pl.BlockSpec((1, nrow, D_KV), lambda i, j: (i, j, 0)),
        out_shape=jax.ShapeDtypeStruct((2, B * nchunk, D_KV), F32),
        compiler_params=pltpu.CompilerParams(
            dimension_semantics=("arbitrary", "arbitrary"), vmem_limit_bytes=VMEM_LIMIT),
        name="compress",
    )(xc, pe2, wbig, w2bd)


def _nsa_kernel(q_ref, cmp_ref, ksel_ref, vselt_ref, kwin_ref, vwint_ref, brg_ref, ag_ref,
                ovt_ref, out_ref, qpad_ref, pen_ref, o_ref, m_ref, acc_ref, s0_ref, s1_ref):
    qi = pl.program_id(1)
    q0 = qi * TQ
    nl = Q_PER_KV * TQ
    nblk = pen_ref.shape[1]
    ncmp = cmp_ref.shape[1]
    groups = range(N_KV_HEADS)
    gsl = [slice(g * HEAD_DIM, (g + 1) * HEAD_DIM) for g in groups]

    def tile4(a):
        return jnp.concatenate([a] * Q_PER_KV, axis=1)

    kc = cmp_ref[0].astype(BF16)
    q_t = q_ref[...].astype(F32).T.astype(BF16)
    s_cmp = []
    for g in groups:
        zpad = jnp.zeros((HEAD_DIM, TQ), BF16)
        cols = []
        for r in range(Q_PER_KV):
            hq = g * Q_PER_KV + r
            qh = q_t[hq * HEAD_DIM:(hq + 1) * HEAD_DIM, :]
            cols.append(jnp.concatenate([qh, zpad] if g == 0 else [zpad, qh], axis=0))
        qpad = jnp.concatenate(cols, axis=1)
        qpad_ref[g] = qpad
        s_cmp.append(jnp.dot(kc, qpad, preferred_element_type=F32))

    s_bufs = (s0_ref, s1_ref)

    def sel_scores(kt, slot):
        k_tile = ksel_ref[pl.ds(pl.multiple_of(kt * KS, KS), KS), :]
        for g in groups:
            s_bufs[slot][g] = jnp.dot(k_tile, qpad_ref[g], preferred_element_type=F32)

    wrows = WINDOW + TQ
    w0 = pl.multiple_of(jnp.maximum(q0 - WINDOW, 0), TQ)
    k_win = kwin_ref[pl.ds(w0, wrows), :]
    n_full = q0 // KS

    ones = jnp.ones((ONES_ROWS, VC), BF16)

    def vt_aug(v_ref, c0, n, g):
        vt = jnp.concatenate([v_ref[c0 + c, gsl[g], :] for c in range(n)], axis=1)
        return jnp.concatenate([vt, jnp.concatenate([ones] * n, axis=1)], axis=0)

    def normalise(res):
        return res[0:HEAD_DIM, :] * (1.0 / res[HEAD_DIM:HEAD_DIM + 1, :])

    c_row = lax.broadcasted_iota(jnp.int32, (ncmp, TQ), 0)
    t_lane = q0 + lax.broadcasted_iota(jnp.int32, (ncmp, TQ), 1)
    cpen = tile4(jnp.where(c_row * CMP_STRIDE + (CMP_BLOCK - 1) <= t_lane, 0.0, NEG))
    vc_t = cmp_ref[1].T
    for g in groups:
        vct = vc_t[gsl[g], :].astype(BF16)
        s = s_cmp[g] + cpen
        m = jnp.max(s, axis=0, keepdims=True)
        m = jnp.where(m > 0.5 * NEG, m, 0.0)
        e = jnp.exp2(s - m)
        den = jnp.sum(e, axis=0, keepdims=True)
        p = e * (1.0 / jnp.where(den > 0.0, den, 1.0))
        o_ref[0, g] = jnp.dot(vct, p.astype(BF16), preferred_element_type=F32)

        psum = p[:, 0:TQ]
        for r in range(1, Q_PER_KV):
            psum = psum + p[:, r * TQ:(r + 1) * TQ]
        imp = jnp.dot(ovt_ref[...], psum, preferred_element_type=F32,
                      precision=lax.Precision.HIGHEST)
        j_row = lax.broadcasted_iota(jnp.int32, (nblk, TQ), 0)
        t_blk = q0 + lax.broadcasted_iota(jnp.int32, (nblk, TQ), 1)
        cur = jnp.right_shift(t_blk, SEL_BLOCK.bit_length() - 1)
        forced = (j_row == 0) | (j_row == cur) | (j_row == cur - 1)
        val = jnp.where(forced, SEL_FORCE, imp)
        val = jnp.where(j_row * SEL_BLOCK <= t_blk, val, -SEL_FORCE)
        rank = jnp.zeros((nblk, TQ), F32)
        for i in range(nblk):
            vi = jnp.broadcast_to(val[i:i + 1, :], (nblk, TQ))
            rank = rank + jnp.where(j_row > i, jnp.where(vi >= val, 1.0, 0.0),
                                    jnp.where(vi > val, 1.0, 0.0))
        chosen = (rank < float(min(SEL_TOPK, nblk))) & (val > -0.5 * SEL_FORCE)
        selpen = jnp.where(chosen, 0.0, NEG)
        for j in range(nblk):
            pen_ref[g, j] = jnp.broadcast_to(selpen[j:j + 1, :], (8, TQ))

    s_win = [jnp.dot(k_win, qpad_ref[g], preferred_element_type=F32) for g in groups]
    sel_scores(0, 0)
    key = w0 + lax.broadcasted_iota(jnp.int32, (wrows, TQ), 0)
    t_q = q0 + lax.broadcasted_iota(jnp.int32, (wrows, TQ), 1)
    wpen = tile4(jnp.where(key <= t_q, jnp.where(key > t_q - WINDOW, 0.0, NEG), NEG))
    p_win = []
    for g in groups:
        s = s_win[g] + wpen
        m = jnp.max(s, axis=0, keepdims=True)
        p_win.append(jnp.exp2(s - m).astype(BF16))
    for g in groups:
        o_ref[2, g] = normalise(jnp.dot(vt_aug(vwint_ref, w0 // VC, wrows // VC, g), p_win[g],
                                        preferred_element_type=F32))

    m_ref[...] = jnp.full(m_ref.shape, NEG, F32)
    acc_ref[...] = jnp.zeros(acc_ref.shape, F32)
    bpt = KS // SEL_BLOCK
    cpt = KS // VC

    def sel_update(kt, slot, diagonal):
        s_ref = s_bufs[slot]
        probs, alphas = [], []
        for g in groups:
            m_prev = m_ref[g]
            if diagonal:
                rows = [jnp.concatenate([pen_ref[g, kt * bpt + jj]] * (SEL_BLOCK // 8), axis=0)
                        for jj in range(bpt)]
                pen = jnp.concatenate(rows, axis=0)
                key = kt * KS + lax.broadcasted_iota(jnp.int32, (KS, TQ), 0)
                t_q = q0 + lax.broadcasted_iota(jnp.int32, (KS, TQ), 1)
                s = s_ref[g] + tile4(pen + jnp.where(key <= t_q, 0.0, NEG))
                m_new = jnp.maximum(m_prev, jnp.max(s, axis=0, keepdims=True))
                p = jnp.exp2(s - m_new)
            else:
                sb = [s_ref[g, jj * SEL_BLOCK:(jj + 1) * SEL_BLOCK, :] for jj in range(bpt)]
                bias = [tile4(pen_ref[g, kt * bpt + jj][0:1, :]) for jj in range(bpt)]
                m_new = m_prev
                for jj in range(bpt):
                    m_new = jnp.maximum(m_new, jnp.max(sb[jj], axis=0, keepdims=True) + bias[jj])
                p = jnp.concatenate([jnp.exp2(sb[jj] + (bias[jj] - m_new)) for jj in range(bpt)],
                                    axis=0)
            probs.append(p.astype(BF16))
            alphas.append(jnp.exp2(m_prev - m_new))
            m_ref[g] = m_new
        for g in groups:
            acc_ref[g] = alphas[g] * acc_ref[g] + jnp.dot(
                vt_aug(vselt_ref, kt * cpt, cpt, g), probs[g], preferred_element_type=F32)

    def sel_body(kt, carry):
        for slot in (0, 1):
            @pl.when((kt & 1) == slot)
            def _(slot=slot):
                sel_scores(kt + 1, 1 - slot)
                sel_update(kt, slot, diagonal=False)
        return carry

    lax.fori_loop(0, n_full, sel_body, 0)
    for slot in (0, 1):
        @pl.when((n_full & 1) == slot)
        def _(slot=slot):
            sel_update(n_full, slot, diagonal=True)
    for g in groups:
        o_ref[1, g] = normalise(acc_ref[g])

    gates = jax.nn.sigmoid(brg_ref[...].T[0:N_GATE_ROWS, :])
    group_out = []
    for g in groups:
        heads = []
        for r in range(Q_PER_KV):
            hq = g * Q_PER_KV + r
            ls = slice(r * TQ, (r + 1) * TQ)
            o = gates[hq:hq + 1, :] * o_ref[0, g, :, ls]
            for n in range(1, N_BRANCH):
                o = o + gates[n * N_Q_HEADS + hq:n * N_Q_HEADS + hq + 1, :] * o_ref[n, g, :, ls]
            heads.append(o)
        group_out.append(jnp.concatenate(heads, axis=0).T)

    attn = jnp.concatenate(group_out, axis=1)
    out_ref[...] = (attn * _silu(ag_ref[...])).astype(out_ref.dtype)


def _nsa(q, cmp_kv, ksel, vselt, kwin, vwint, brg, ag, ovt, B, S):
    T = B * S
    nq = S // TQ
    nblk = S // SEL_BLOCK
    ncmp = S // CMP_STRIDE
    nl = Q_PER_KV * TQ
    return pl.pallas_call(
        _nsa_kernel,
        grid=(B, nq),
        in_specs=[
            pl.BlockSpec((TQ, D_ATTN), lambda b, q: (b * nq + q, 0)),
            pl.BlockSpec((2, ncmp, D_KV), lambda b, q: (0, b, 0)),
            pl.BlockSpec((S, D_KV), lambda b, q: (b, 0)),
            pl.BlockSpec((None, S // VC, D_KV, VC), lambda b, q: (b, 0, 0, 0)),
            pl.BlockSpec((S, D_KV), lambda b, q: (b, 0)),
            pl.BlockSpec((None, S // VC, D_KV, VC), lambda b, q: (b, 0, 0, 0)),
            pl.BlockSpec((TQ, LANE), lambda b, q: (b * nq + q, 0)),
            pl.BlockSpec((TQ, D_ATTN), lambda b, q: (b * nq + q, 0)),
            pl.BlockSpec(ovt.shape, lambda b, q: (0, 0)),
        ],
        out_specs=pl.BlockSpec((TQ, D_ATTN), lambda b, q: (b * nq + q, 0)),
        out_shape=jax.ShapeDtypeStruct((T, D_ATTN), BF16),
        scratch_shapes=[
            pltpu.VMEM((N_KV_HEADS, D_KV, nl), BF16),
            pltpu.VMEM((N_KV_HEADS, nblk, 8, TQ), F32),
            pltpu.VMEM((N_BRANCH, N_KV_HEADS, HEAD_DIM, nl), F32),
            pltpu.VMEM((N_KV_HEADS, 1, nl), F32),
            pltpu.VMEM((N_KV_HEADS, HEAD_DIM + ONES_ROWS, nl), F32),
            pltpu.VMEM((N_KV_HEADS, KS, nl), F32),
            pltpu.VMEM((N_KV_HEADS, KS, nl), F32),
        ],
        compiler_params=pltpu.CompilerParams(
            dimension_semantics=("arbitrary", "arbitrary"), vmem_limit_bytes=VMEM_LIMIT),
        name="nsa",
    )(q, cmp_kv, ksel, vselt, kwin, vwint, brg, ag, ovt)


def _outproj_kernel(x_ref, rnn_ref, attn_ref, wo_ref, nfw_ref, out_ref, *, final_norm):
    y = x_ref[...]
    y = y + jnp.dot(rnn_ref[...], wo_ref[0:D_RNN, :], preferred_element_type=F32)
    y = y + jnp.dot(attn_ref[...], wo_ref[D_RNN:D_MIX, :], preferred_element_type=F32)
    if final_norm:
        ms = jnp.mean(y * y, axis=-1, keepdims=True)
        y = (y * lax.rsqrt(ms + EPS)) * nfw_ref[...]
    out_ref[...] = y


def _outproj(x2, rnn_out, attn_out, wo, nfw, final_norm):
    T = x2.shape[0]
    row = lambda i: (i, 0)
    return pl.pallas_call(
        functools.partial(_outproj_kernel, final_norm=final_norm),
        grid=(T // TM,),
        in_specs=[
            pl.BlockSpec((TM, D_MODEL), row),
            pl.BlockSpec((TM, D_RNN), row),
            pl.BlockSpec((TM, D_ATTN), row),
            pl.BlockSpec((D_MIX, D_MODEL), lambda i: (0, 0)),
            pl.BlockSpec((1, D_MODEL), lambda i: (0, 0)),
        ],
        out_specs=pl.BlockSpec((TM, D_MODEL), row),
        out_shape=jax.ShapeDtypeStruct((T, D_MODEL), F32),
        compiler_params=pltpu.CompilerParams(
            dimension_semantics=("arbitrary",), vmem_limit_bytes=VMEM_LIMIT),
        name="outproj",
    )(x2, rnn_out, attn_out, wo, nfw)


def _block_diag_halves(wa, wx):
    eye = jnp.eye(RNN_HEADS, dtype=wa.dtype)
    full = lambda w: jnp.einsum('hij,hk->hikj', w, eye).reshape(D_RNN, D_RNN)
    fa, fx = full(wa), full(wx)
    half = D_RNN // 2
    return jnp.stack([
        jnp.concatenate([fa[s:s + half, s:s + half], fx[s:s + half, s:s + half]], axis=1)
        for s in (0, half)]).astype(BF16)


def _compress_weights(pe, w1, w2):
    eye = jnp.eye(N_KV_HEADS, dtype=w1.dtype)
    w1r = w1.reshape(2, CMP_STRIDE, HEAD_DIM, CMP_HIDDEN)
    wbig = jnp.einsum('hldn,ge->lgdhen', w1r, eye).reshape(
        CMP_STRIDE * D_KV, 2 * N_KV_HEADS * CMP_HIDDEN)
    w2bd = jnp.einsum('nd,ge->gned', w2, eye).reshape(N_KV_HEADS * CMP_HIDDEN, D_KV)
    per = pe.reshape(2, CMP_STRIDE, 1, HEAD_DIM)
    pe2 = jnp.broadcast_to(per, (2, CMP_STRIDE, N_KV_HEADS, HEAD_DIM)).reshape(2, CMP_STRIDE * D_KV)
    return pe2, wbig.astype(BF16), w2bd.astype(BF16)


def _overlap_t(ncmp_pad, nblk):
    cs = np.arange(ncmp_pad)[None, :] * CMP_STRIDE
    ss = np.arange(nblk)[:, None] * SEL_BLOCK
    ov = np.clip(np.minimum(cs + CMP_BLOCK, ss + SEL_BLOCK) - np.maximum(cs, ss), 0, None)
    return jnp.asarray(ov.astype(np.float32) / CMP_BLOCK)


def kernel(x, norm1_w, w_in, conv_w, conv_b, rg_wa, rg_ba, rg_wx, rg_bx, rg_lambda,
           cmp_k_pe, cmp_k_w1, cmp_k_w2, cmp_v_pe, cmp_v_w1, cmp_v_w2, w_out, normf_w):
    B, S, D = x.shape
    assert D == D_MODEL and w_in.shape[-1] == _D_IN
    assert S % TM == 0 and TS == TM and B % CB == 0
    assert TQ % VC == 0 and WINDOW % TQ == 0 and KS % TQ == 0 and TM % TQ == 0
    assert S % KS == 0 and S >= WINDOW + TQ
    depth = w_in.shape[0]
    T = B * S
    ovt = _overlap_t(S // CMP_STRIDE, S // SEL_BLOCK)
    x2 = x.reshape(T, D)
    for l in range(depth):
        w = w_in[l]
        wn = jnp.concatenate([w[:, _O_RX:_O_Q], w[:, _O_KC:_O_KS], w[:, _O_KS:_O_VS],
                              w[:, _O_KW:_O_VW], w[:, _O_AG:_O_BR], w[:, _O_Q:_O_KC],
                              w[:, _O_VS:_O_KW], w[:, _O_VW:_O_AG], w[:, _O_BR:_D_IN],
                              jnp.zeros((D, LANE - N_BRANCH * N_Q_HEADS), w.dtype)],
                             axis=1).astype(BF16)
        (rnn_out, kvc, ksel, kwin, ag, q, vselt, vwint, brg) = _inproj(
            x2, norm1_w[l].reshape(1, D), wn,
            conv_w[l], conv_b[l].reshape(1, D_RNN), _block_diag_halves(rg_wa[l], rg_wx[l]),
            rg_ba[l].reshape(1, D_RNN), rg_bx[l].reshape(1, D_RNN),
            rg_lambda[l].reshape(1, D_RNN), B, S)

        pk, wbk, w2k = _compress_weights(cmp_k_pe[l], cmp_k_w1[l], cmp_k_w2[l])
        pv, wbv, w2v = _compress_weights(cmp_v_pe[l], cmp_v_w1[l], cmp_v_w2[l])
        cmp_kv = _compress(kvc, jnp.stack([pk, pv]), jnp.stack([wbk, wbv]),
                           jnp.stack([w2k, w2v]), B, S)

        attn_out = _nsa(q, cmp_kv, ksel, vselt, kwin, vwint, brg, ag, ovt, B, S)

        x2 = _outproj(x2, rnn_out, attn_out, w_out[l].astype(BF16),
                      normf_w.reshape(1, D), final_norm=(l == depth - 1))
    return x2.reshape(B, S, D)
```

```python
import functools

import numpy as np
import jax
import jax.numpy as jnp
from jax import lax
from jax.experimental import pallas as pl
from jax.experimental.pallas import tpu as pltpu

F32 = jnp.float32
BF16 = jnp.bfloat16

D_MODEL = 1024
EPS = 1e-6
D_RNN = 512
RNN_HEADS = 8
RNN_HEAD_DIM = D_RNN // RNN_HEADS
CONV_WIDTH = 4
LRU_C = 8.0
N_Q_HEADS = 8
N_KV_HEADS = 2
HEAD_DIM = 64
Q_PER_KV = N_Q_HEADS // N_KV_HEADS
D_ATTN = N_Q_HEADS * HEAD_DIM
D_KV = N_KV_HEADS * HEAD_DIM
CMP_BLOCK = 32
CMP_STRIDE = 16
CMP_HIDDEN = 256
SEL_BLOCK = 64
SEL_TOPK = 8
SEL_FORCE = 1e9
WINDOW = 512
N_BRANCH = 3
D_MIX = D_RNN + D_ATTN
N_GATE_ROWS = 32

LANE = 128
SCAN_CHUNKS = 8

TM = 1024
TS = 1024
TQ = 256
KS = 512
VC = 128
CB = 4
RNN_ROW_CHUNKS = 4
NEG = -1e30
LOG2E = 1.4426950408889634
ONES_ROWS = 16
VMEM_V7X = 64 * 1024 * 1024
VMEM_LIMIT = VMEM_V7X - 8 * 1024 * 1024

_O_RX, _O_RG, _O_Q = 0, D_RNN, 2 * D_RNN
_O_KC = _O_Q + D_ATTN
_O_VC = _O_KC + D_KV
_O_KS = _O_VC + D_KV
_O_VS = _O_KS + D_KV
_O_KW = _O_VS + D_KV
_O_VW = _O_KW + D_KV
_O_AG = _O_VW + D_KV
_O_BR = _O_AG + D_ATTN
_D_IN = _O_BR + N_BRANCH * N_Q_HEADS


def _silu(x):
    return x * jax.nn.sigmoid(x)


def _inproj_kernel(x_ref, nw_ref, wn_ref, cw_ref, cb_ref, wg_ref, ba_ref, bx_ref, lam_ref,
                   rnn_ref, kvc_ref, ksel_ref, kwin_ref, ag_ref,
                   qt_ref, vselt_ref, vwint_ref, brgt_ref,
                   kv_scr, rx_scr, rg_scr, tail_ref, a_s, u_s, h_s, p_s, hlast, o_scr):
    @pl.when(pl.program_id(1) == 0)
    def _():
        tail_ref[...] = jnp.zeros(tail_ref.shape, F32)
        hlast[...] = jnp.zeros(hlast.shape, F32)

    x = x_ref[...]
    ms = jnp.mean(x * x, axis=-1, keepdims=True)
    h = ((x * lax.rsqrt(ms + EPS)) * nw_ref[...]).astype(BF16)

    def nat(a, b):
        return jnp.dot(h, wn_ref[:, a:b], preferred_element_type=F32)

    def store_step_major(dst_ref, val):
        steps = TM // SCAN_CHUNKS
        for l in range(D_RNN // LANE):
            for c in range(SCAN_CHUNKS):
                dst_ref[l, pl.ds(c, steps, stride=SCAN_CHUNKS), :] = (
                    val[c * steps:(c + 1) * steps, l * LANE:(l + 1) * LANE])

    store_step_major(rx_scr, nat(0, 512))
    store_step_major(rg_scr, nat(512, 1024))

    def proj_cmp():
        kv = nat(1024, 1280)
        for a in range(2):
            kv_scr[a] = kv[:, a * D_KV:(a + 1) * D_KV]
            for l in range(CMP_STRIDE):
                kvc_ref[a, :, l * D_KV:(l + 1) * D_KV] = (
                    kv_scr[a, pl.ds(l, TM // CMP_STRIDE, stride=CMP_STRIDE), :])

    def proj_k():
        kk = nat(1280, 1536)
        ksel_ref[...] = kk[:, :D_KV].astype(BF16)
        kwin_ref[...] = kk[:, D_KV:].astype(BF16)

    def proj_gate():
        ag_ref[...] = nat(1536, 2048)

    def proj_q():
        qt_ref[...] = (nat(2048, 2560) * (HEAD_DIM ** -0.5 * LOG2E)).T.astype(BF16)

    def proj_v():
        vv = nat(2560, 2816)
        for a, dst in enumerate((vselt_ref, vwint_ref)):
            vt = vv[:, a * D_KV:(a + 1) * D_KV].T.astype(BF16)
            for c in range(TM // VC):
                dst[c] = vt[:, c * VC:(c + 1) * VC]
        brgt_ref[...] = nat(2816, 2816 + LANE).T[0:N_GATE_ROWS, :]

    projections = (proj_gate, proj_q, proj_k, proj_cmp, proj_v)
    wraps = _rglru_wraps(rx_scr, tail_ref)
    rows = TM // RNN_ROW_CHUNKS
    for ci in range(RNN_ROW_CHUNKS):
        _rglru_gates(rx_scr, wraps, ci * rows, (ci + 1) * rows,
                     cw_ref, cb_ref, wg_ref, ba_ref, bx_ref, lam_ref, a_s, u_s)
        for proj in projections[ci::RNN_ROW_CHUNKS]:
            proj()

    _rglru_scan(rg_scr, rnn_ref, a_s, u_s, h_s, p_s, hlast, o_scr)


def _inproj(x2, nw, wn, cw, cb, wg, ba, bx, lam, B, S):
    T = B * S
    ns = S // TM
    grid = (B, ns)
    row = lambda b, s: (b * ns + s, 0)
    const2 = lambda b, s: (0, 0)
    out_shape = (
        jax.ShapeDtypeStruct((T, D_RNN), BF16),
        jax.ShapeDtypeStruct((2, T // CMP_STRIDE, CMP_STRIDE * D_KV), F32),
        jax.ShapeDtypeStruct((T, D_KV), BF16),
        jax.ShapeDtypeStruct((T, D_KV), BF16),
        jax.ShapeDtypeStruct((T, D_ATTN), F32),
        jax.ShapeDtypeStruct((B, ns, D_ATTN, TM), BF16),
        jax.ShapeDtypeStruct((B, S // VC, D_KV, VC), BF16),
        jax.ShapeDtypeStruct((B, S // VC, D_KV, VC), BF16),
        jax.ShapeDtypeStruct((B, ns, N_GATE_ROWS, TM), F32),
    )
    out_specs = (
        pl.BlockSpec((TM, D_RNN), row),
        pl.BlockSpec((2, TM // CMP_STRIDE, CMP_STRIDE * D_KV), lambda b, s: (0, b * ns + s, 0)),
        pl.BlockSpec((TM, D_KV), row),
        pl.BlockSpec((TM, D_KV), row),
        pl.BlockSpec((TM, D_ATTN), row),
        pl.BlockSpec((None, None, D_ATTN, TM), lambda b, s: (b, s, 0, 0)),
        pl.BlockSpec((None, TM // VC, D_KV, VC), lambda b, s: (b, s, 0, 0)),
        pl.BlockSpec((None, TM // VC, D_KV, VC), lambda b, s: (b, s, 0, 0)),
        pl.BlockSpec((None, None, N_GATE_ROWS, TM), lambda b, s: (b, s, 0, 0)),
    )
    return pl.pallas_call(
        _inproj_kernel,
        grid=grid,
        in_specs=[
            pl.BlockSpec((TM, D_MODEL), row),
            pl.BlockSpec((1, D_MODEL), const2),
            pl.BlockSpec(wn.shape, const2),
            pl.BlockSpec((CONV_WIDTH, D_RNN), const2),
            pl.BlockSpec((1, D_RNN), const2),
            pl.BlockSpec(wg.shape, lambda b, s: (0, 0, 0)),
            pl.BlockSpec((1, D_RNN), const2),
            pl.BlockSpec((1, D_RNN), const2),
            pl.BlockSpec((1, D_RNN), const2),
        ],
        out_specs=out_specs,
        out_shape=out_shape,
        scratch_shapes=[
            pltpu.VMEM((2, TM, D_KV), F32),
            pltpu.VMEM((D_RNN // LANE, TM, LANE), F32),
            pltpu.VMEM((D_RNN // LANE, TM, LANE), F32),
            pltpu.VMEM(((CONV_WIDTH - 1) * SCAN_CHUNKS, D_RNN), F32),
            pltpu.VMEM((TM, D_RNN), F32),
            pltpu.VMEM((TM, D_RNN), F32),
            pltpu.VMEM((TM, D_RNN), F32),
            pltpu.VMEM((TM, D_RNN), F32),
            pltpu.VMEM((1, D_RNN), F32),
            pltpu.VMEM((D_RNN // LANE, TM, LANE), F32),
        ],
        compiler_params=pltpu.CompilerParams(
            dimension_semantics=("arbitrary", "arbitrary"), vmem_limit_bytes=VMEM_LIMIT),
        name="inproj",
    )(x2, nw, wn, cw, cb, wg, ba, bx, lam)


def _slab_rows(x_ref, lo, hi):
    return jnp.concatenate([x_ref[l, lo:hi, :] for l in range(D_RNN // LANE)], axis=1)


def _rglru_wraps(x_ref, tail_ref):
    ntail = (CONV_WIDTH - 1) * SCAN_CHUNKS
    last = _slab_rows(x_ref, TS - ntail, TS)
    sub = lax.broadcasted_iota(jnp.int32, (SCAN_CHUNKS, D_RNN), 0)
    wraps = []
    for j in range(CONV_WIDTH - 1):
        rows = slice(j * SCAN_CHUNKS, (j + 1) * SCAN_CHUNKS)
        cur = pltpu.roll(last[rows, :], 1, 0)
        prev = pltpu.roll(tail_ref[rows, :], 1, 0)
        wraps.append(jnp.where(sub == 0, prev, cur))
    tail_ref[...] = last
    return wraps


def _rglru_gates(x_ref, wraps, lo, hi, cw_ref, cb_ref, wg_ref, ba_ref, bx_ref, lam_ref, a_s, u_s):
    y = cb_ref[...]
    for k in range(CONV_WIDTH):
        off = (CONV_WIDTH - 1 - k) * SCAN_CHUNKS
        if lo >= off:
            xs = _slab_rows(x_ref, lo - off, hi - off)
        else:
            assert lo == 0
            xs = jnp.concatenate(wraps[len(wraps) - off // SCAN_CHUNKS:]
                                 + [_slab_rows(x_ref, 0, hi - off)], axis=0)
        y = y + xs * cw_ref[k:k + 1, :]

    yb = y.astype(BF16)
    half = D_RNN // 2
    pre = [jnp.dot(yb[:, hh * half:(hh + 1) * half], wg_ref[hh], preferred_element_type=F32)
           for hh in range(2)]
    pre_a = jnp.concatenate([pre[0][:, :half], pre[1][:, :half]], axis=1)
    pre_x = jnp.concatenate([pre[0][:, half:], pre[1][:, half:]], axis=1)
    r = jax.nn.sigmoid(pre_a + ba_ref[...])
    i = jax.nn.sigmoid(pre_x + bx_ref[...])
    lam = lam_ref[...]
    lsig = jnp.minimum(lam, 0.0) - jnp.log1p(jnp.exp(-jnp.abs(lam)))
    log_a = LRU_C * r * lsig
    a = jnp.exp(log_a)
    u = jnp.sqrt(1.0 - a * a) * (i * y)
    a_s[lo:hi, :] = a
    u_s[lo:hi, :] = u


def _rglru_scan(gate_ref, out_ref, a_s, u_s, h_s, p_s, hlast, o_scr):
    nstep = TS // SCAN_CHUNKS
    nslab = D_RNN // LANE
    gate = jnp.concatenate([gate_ref[l] for l in range(nslab)], axis=1)

    def body(j, carry):
        h, p = carry
        blk = pl.multiple_of(j * SCAN_CHUNKS, SCAN_CHUNKS)
        a_j = a_s[pl.ds(blk, SCAN_CHUNKS), :]
        h = a_j * h + u_s[pl.ds(blk, SCAN_CHUNKS), :]
        p = a_j * p
        h_s[pl.ds(blk, SCAN_CHUNKS), :] = h
        p_s[pl.ds(blk, SCAN_CHUNKS), :] = p
        return h, p

    h_end, p_end = lax.fori_loop(
        0, nstep, body,
        (jnp.zeros((SCAN_CHUNKS, D_RNN), F32), jnp.ones((SCAN_CHUNKS, D_RNN), F32)), unroll=8)
    carry = hlast[...]
    h_in = []
    for c in range(SCAN_CHUNKS):
        h_in.append(carry)
        carry = h_end[c:c + 1, :] + p_end[c:c + 1, :] * carry
    hlast[...] = carry
    h_in = jnp.concatenate([jnp.concatenate(h_in, axis=0)] * nstep, axis=0)
    o = (h_s[...] + p_s[...] * h_in) * _silu(gate)

    for l in range(nslab):
        o_scr[l] = o[:, l * LANE:(l + 1) * LANE]
        for c in range(SCAN_CHUNKS):
            out_ref[c * nstep:(c + 1) * nstep, l * LANE:(l + 1) * LANE] = (
                o_scr[l, pl.ds(c, nstep, stride=SCAN_CHUNKS), :].astype(out_ref.dtype))


def _compress_kernel(x_ref, pe_ref, wbig_ref, w2_ref, out_ref):
    nrow = x_ref.shape[1]
    nchunk = nrow // CB
    x = x_ref[0]
    xa = (x + pe_ref[0, 0:1, :]).astype(BF16)
    xb = (x + pe_ref[0, 1:2, :]).astype(BF16)
    nh = N_KV_HEADS * CMP_HIDDEN
    first = jnp.dot(xa, wbig_ref[0, :, :nh], preferred_element_type=F32)
    second = jnp.dot(xb, wbig_ref[0, :, nh:], preferred_element_type=F32)
    hid = _silu(first + pltpu.roll(second, nrow - 1, 0))
    out = jnp.dot(hid.astype(BF16), w2_ref[0], preferred_element_type=F32)
    c_idx = lax.broadcasted_iota(jnp.int32, out.shape, 0) & (nchunk - 1)
    out_ref[0] = jnp.where(c_idx < nchunk - 1, out, 0.0)


def _compress(xc, pe2, wbig, w2bd, B, S):
    nchunk = S // CMP_STRIDE
    nrow = CB * nchunk
    width = CMP_STRIDE * D_KV
    kv = lambda i, j: (i, 0, 0)
    return pl.pallas_call(
        _compress_kernel,
        grid=(2, B // CB),
        in_specs=[
            pl.BlockSpec((1, nrow, width), lambda i, j: (i, j, 0)),
            pl.BlockSpec((1, 2, width), kv),
            pl.BlockSpec((1,) + wbig.shape[1:], kv),
            pl.BlockSpec((1,) + w2bd.shape[1:], kv),
        ],
        out_specs=pl.BlockSpec((1, nrow, D_KV), lambda i, j: (i, j, 0)),
        out_shape=jax.ShapeDtypeStruct((2, B * nchunk, D_KV), F32),
        compiler_params=pltpu.CompilerParams(
            dimension_semantics=("arbitrary", "arbitrary"), vmem_limit_bytes=VMEM_LIMIT),
        name="compress",
    )(xc, pe2, wbig, w2bd)


def _nsa_kernel(qt_ref, cmp_ref, ksel_ref, vselt_ref, kwin_ref, vwint_ref, brgt_ref, ag_ref,
                ovt_ref, out_ref, qpad_ref, pen_ref, o_ref, m_ref, acc_ref, s0_ref, s1_ref):
    qi = pl.program_id(1)
    q0 = qi * TQ
    nl = Q_PER_KV * TQ
    nblk = pen_ref.shape[1]
    ncmp = cmp_ref.shape[1]
    groups = range(N_KV_HEADS)
    gsl = [slice(g * HEAD_DIM, (g + 1) * HEAD_DIM) for g in groups]

    def tile4(a):
        return jnp.concatenate([a] * Q_PER_KV, axis=1)

    kc = cmp_ref[0].astype(BF16)
    s_cmp = []
    for g in groups:
        zpad = jnp.zeros((HEAD_DIM, TQ), BF16)
        cols = []
        for r in range(Q_PER_KV):
            hq = g * Q_PER_KV + r
            qh = qt_ref[hq * HEAD_DIM:(hq + 1) * HEAD_DIM, :]
            cols.append(jnp.concatenate([qh, zpad] if g == 0 else [zpad, qh], axis=0))
        qpad = jnp.concatenate(cols, axis=1)
        qpad_ref[g] = qpad
        s_cmp.append(jnp.dot(kc, qpad, preferred_element_type=F32))

    s_bufs = (s0_ref, s1_ref)

    def sel_scores(kt, slot):
        k_tile = ksel_ref[pl.ds(pl.multiple_of(kt * KS, KS), KS), :]
        for g in groups:
            s_bufs[slot][g] = jnp.dot(k_tile, qpad_ref[g], preferred_element_type=F32)

    wrows = WINDOW + TQ
    w0 = pl.multiple_of(jnp.maximum(q0 - WINDOW, 0), TQ)
    k_win = kwin_ref[pl.ds(w0, wrows), :]
    n_full = q0 // KS

    ones = jnp.ones((ONES_ROWS, VC), BF16)

    def vt_aug(v_ref, c0, n, g):
        vt = jnp.concatenate([v_ref[c0 + c, gsl[g], :] for c in range(n)], axis=1)
        return jnp.concatenate([vt, jnp.concatenate([ones] * n, axis=1)], axis=0)

    def normalise(res):
        return res[0:HEAD_DIM, :] * (1.0 / res[HEAD_DIM:HEAD_DIM + 1, :])

    c_row = lax.broadcasted_iota(jnp.int32, (ncmp, TQ), 0)
    t_lane = q0 + lax.broadcasted_iota(jnp.int32, (ncmp, TQ), 1)
    cpen = tile4(jnp.where(c_row * CMP_STRIDE + (CMP_BLOCK - 1) <= t_lane, 0.0, NEG))
    vc_t = cmp_ref[1].T
    for g in groups:
        vct = vc_t[gsl[g], :].astype(BF16)
        s = s_cmp[g] + cpen
        m = jnp.max(s, axis=0, keepdims=True)
        m = jnp.where(m > 0.5 * NEG, m, 0.0)
        e = jnp.exp2(s - m)
        den = jnp.sum(e, axis=0, keepdims=True)
        p = e * (1.0 / jnp.where(den > 0.0, den, 1.0))
        o_ref[0, g] = jnp.dot(vct, p.astype(BF16), preferred_element_type=F32)

        psum = p[:, 0:TQ]
        for r in range(1, Q_PER_KV):
            psum = psum + p[:, r * TQ:(r + 1) * TQ]
        imp = jnp.dot(ovt_ref[...], psum, preferred_element_type=F32,
                      precision=lax.Precision.HIGHEST)
        j_row = lax.broadcasted_iota(jnp.int32, (nblk, TQ), 0)
        t_blk = q0 + lax.broadcasted_iota(jnp.int32, (nblk, TQ), 1)
        cur = jnp.right_shift(t_blk, SEL_BLOCK.bit_length() - 1)
        forced = (j_row == 0) | (j_row == cur) | (j_row == cur - 1)
        val = jnp.where(forced, SEL_FORCE, imp)
        val = jnp.where(j_row * SEL_BLOCK <= t_blk, val, -SEL_FORCE)
        rank = jnp.zeros((nblk, TQ), F32)
        for i in range(nblk):
            vi = jnp.broadcast_to(val[i:i + 1, :], (nblk, TQ))
            rank = rank + jnp.where(j_row > i, jnp.where(vi >= val, 1.0, 0.0),
                                    jnp.where(vi > val, 1.0, 0.0))
        chosen = (rank < float(min(SEL_TOPK, nblk))) & (val > -0.5 * SEL_FORCE)
        selpen = jnp.where(chosen, 0.0, NEG)
        for j in range(nblk):
            pen_ref[g, j] = jnp.broadcast_to(selpen[j:j + 1, :], (8, TQ))

    s_win = [jnp.dot(k_win, qpad_ref[g], preferred_element_type=F32) for g in groups]
    sel_scores(0, 0)
    key = w0 + lax.broadcasted_iota(jnp.int32, (wrows, TQ), 0)
    t_q = q0 + lax.broadcasted_iota(jnp.int32, (wrows, TQ), 1)
    wpen = tile4(jnp.where(key <= t_q, jnp.where(key > t_q - WINDOW, 0.0, NEG), NEG))
    p_win = []
    for g in groups:
        s = s_win[g] + wpen
        m = jnp.max(s, axis=0, keepdims=True)
        p_win.append(jnp.exp2(s - m).astype(BF16))
    for g in groups:
        o_ref[2, g] = normalise(jnp.dot(vt_aug(vwint_ref, w0 // VC, wrows // VC, g), p_win[g],
                                        preferred_element_type=F32))

    m_ref[...] = jnp.full(m_ref.shape, NEG, F32)
    acc_ref[...] = jnp.zeros(acc_ref.shape, F32)
    bpt = KS // SEL_BLOCK
    cpt = KS // VC

    def sel_update(kt, slot, diagonal):
        s_ref = s_bufs[slot]
        probs, alphas = [], []
        for g in groups:
            m_prev = m_ref[g]
            if diagonal:
                rows = [jnp.concatenate([pen_ref[g, kt * bpt + jj]] * (SEL_BLOCK // 8), axis=0)
                        for jj in range(bpt)]
                pen = jnp.concatenate(rows, axis=0)
                key = kt * KS + lax.broadcasted_iota(jnp.int32, (KS, TQ), 0)
                t_q = q0 + lax.broadcasted_iota(jnp.int32, (KS, TQ), 1)
                s = s_ref[g] + tile4(pen + jnp.where(key <= t_q, 0.0, NEG))
                m_new = jnp.maximum(m_prev, jnp.max(s, axis=0, keepdims=True))
                p = jnp.exp2(s - m_new)
            else:
                sb = [s_ref[g, jj * SEL_BLOCK:(jj + 1) * SEL_BLOCK, :] for jj in range(bpt)]
                bias = [tile4(pen_ref[g, kt * bpt + jj][0:1, :]) for jj in range(bpt)]
                m_new = m_prev
                for jj in range(bpt):
                    m_new = jnp.maximum(m_new, jnp.max(sb[jj], axis=0, keepdims=True) + bias[jj])
                p = jnp.concatenate([jnp.exp2(sb[jj] + (bias[jj] - m_new)) for jj in range(bpt)],
                                    axis=0)
            probs.append(p.astype(BF16))
            alphas.append(jnp.exp2(m_prev - m_new))
            m_ref[g] = m_new
        for g in groups:
            acc_ref[g] = alphas[g] * acc_ref[g] + jnp.dot(
                vt_aug(vselt_ref, kt * cpt, cpt, g), probs[g], preferred_element_type=F32)

    def sel_body(kt, carry):
        for slot in (0, 1):
            @pl.when((kt & 1) == slot)
            def _(slot=slot):
                sel_scores(kt + 1, 1 - slot)
                sel_update(kt, slot, diagonal=False)
        return carry

    lax.fori_loop(0, n_full, sel_body, 0)
    for slot in (0, 1):
        @pl.when((n_full & 1) == slot)
        def _(slot=slot):
            sel_update(n_full, slot, diagonal=True)
    for g in groups:
        o_ref[1, g] = normalise(acc_ref[g])

    gates = jax.nn.sigmoid(brgt_ref[...])
    group_out = []
    for g in groups:
        heads = []
        for r in range(Q_PER_KV):
            hq = g * Q_PER_KV + r
            ls = slice(r * TQ, (r + 1) * TQ)
            o = gates[hq:hq + 1, :] * o_ref[0, g, :, ls]
            for n in range(1, N_BRANCH):
                o = o + gates[n * N_Q_HEADS + hq:n * N_Q_HEADS + hq + 1, :] * o_ref[n, g, :, ls]
            heads.append(o)
        group_out.append(jnp.concatenate(heads, axis=0).T)

    attn = jnp.concatenate(group_out, axis=1)
    out_ref[...] = (attn * _silu(ag_ref[...])).astype(out_ref.dtype)


def _nsa(qt, cmp_kv, ksel, vselt, kwin, vwint, brgt, ag, ovt, B, S):
    T = B * S
    nq = S // TQ
    qpm = TM // TQ
    nblk = S // SEL_BLOCK
    ncmp = S // CMP_STRIDE
    nl = Q_PER_KV * TQ
    return pl.pallas_call(
        _nsa_kernel,
        grid=(B, nq),
        in_specs=[
            pl.BlockSpec((None, None, D_ATTN, TQ), lambda b, q: (b, q // qpm, 0, q % qpm)),
            pl.BlockSpec((2, ncmp, D_KV), lambda b, q: (0, b, 0)),
            pl.BlockSpec((S, D_KV), lambda b, q: (b, 0)),
            pl.BlockSpec((None, S // VC, D_KV, VC), lambda b, q: (b, 0, 0, 0)),
            pl.BlockSpec((S, D_KV), lambda b, q: (b, 0)),
            pl.BlockSpec((None, S // VC, D_KV, VC), lambda b, q: (b, 0, 0, 0)),
            pl.BlockSpec((None, None, N_GATE_ROWS, TQ), lambda b, q: (b, q // qpm, 0, q % qpm)),
            pl.BlockSpec((TQ, D_ATTN), lambda b, q: (b * nq + q, 0)),
            pl.BlockSpec(ovt.shape, lambda b, q: (0, 0)),
        ],
        out_specs=pl.BlockSpec((TQ, D_ATTN), lambda b, q: (b * nq + q, 0)),
        out_shape=jax.ShapeDtypeStruct((T, D_ATTN), BF16),
        scratch_shapes=[
            pltpu.VMEM((N_KV_HEADS, D_KV, nl), BF16),
            pltpu.VMEM((N_KV_HEADS, nblk, 8, TQ), F32),
            pltpu.VMEM((N_BRANCH, N_KV_HEADS, HEAD_DIM, nl), F32),
            pltpu.VMEM((N_KV_HEADS, 1, nl), F32),
            pltpu.VMEM((N_KV_HEADS, HEAD_DIM + ONES_ROWS, nl), F32),
            pltpu.VMEM((N_KV_HEADS, KS, nl), F32),
            pltpu.VMEM((N_KV_HEADS, KS, nl), F32),
        ],
        compiler_params=pltpu.CompilerParams(
            dimension_semantics=("arbitrary", "arbitrary"), vmem_limit_bytes=VMEM_LIMIT),
        name="nsa",
    )(qt, cmp_kv, ksel, vselt, kwin, vwint, brgt, ag, ovt)


def _outproj_kernel(x_ref, rnn_ref, attn_ref, wo_ref, nfw_ref, out_ref, *, final_norm):
    y = x_ref[...]
    y = y + jnp.dot(rnn_ref[...], wo_ref[0:D_RNN, :], preferred_element_type=F32)
    y = y + jnp.dot(attn_ref[...], wo_ref[D_RNN:D_MIX, :], preferred_element_type=F32)
    if final_norm:
        ms = jnp.mean(y * y, axis=-1, keepdims=True)
        y = (y * lax.rsqrt(ms + EPS)) * nfw_ref[...]
    out_ref[...] = y


def _outproj(x2, rnn_out, attn_out, wo, nfw, final_norm):
    T = x2.shape[0]
    row = lambda i: (i, 0)
    return pl.pallas_call(
        functools.partial(_outproj_kernel, final_norm=final_norm),
        grid=(T // TM,),
        in_specs=[
            pl.BlockSpec((TM, D_MODEL), row),
            pl.BlockSpec((TM, D_RNN), row),
            pl.BlockSpec((TM, D_ATTN), row),
            pl.BlockSpec((D_MIX, D_MODEL), lambda i: (0, 0)),
            pl.BlockSpec((1, D_MODEL), lambda i: (0, 0)),
        ],
        out_specs=pl.BlockSpec((TM, D_MODEL), row),
        out_shape=jax.ShapeDtypeStruct((T, D_MODEL), F32),
        compiler_params=pltpu.CompilerParams(
            dimension_semantics=("arbitrary",), vmem_limit_bytes=VMEM_LIMIT),
        name="outproj",
    )(x2, rnn_out, attn_out, wo, nfw)


def _block_diag_halves(wa, wx):
    eye = jnp.eye(RNN_HEADS, dtype=wa.dtype)
    full = lambda w: jnp.einsum('hij,hk->hikj', w, eye).reshape(D_RNN, D_RNN)
    fa, fx = full(wa), full(wx)
    half = D_RNN // 2
    return jnp.stack([
        jnp.concatenate([fa[s:s + half, s:s + half], fx[s:s + half, s:s + half]], axis=1)
        for s in (0, half)]).astype(BF16)


def _compress_weights(pe, w1, w2):
    eye = jnp.eye(N_KV_HEADS, dtype=w1.dtype)
    w1r = w1.reshape(2, CMP_STRIDE, HEAD_DIM, CMP_HIDDEN)
    wbig = jnp.einsum('hldn,ge->lgdhen', w1r, eye).reshape(
        CMP_STRIDE * D_KV, 2 * N_KV_HEADS * CMP_HIDDEN)
    w2bd = jnp.einsum('nd,ge->gned', w2, eye).reshape(N_KV_HEADS * CMP_HIDDEN, D_KV)
    per = pe.reshape(2, CMP_STRIDE, 1, HEAD_DIM)
    pe2 = jnp.broadcast_to(per, (2, CMP_STRIDE, N_KV_HEADS, HEAD_DIM)).reshape(2, CMP_STRIDE * D_KV)
    return pe2, wbig.astype(BF16), w2bd.astype(BF16)


def _overlap_t(ncmp_pad, nblk):
    cs = np.arange(ncmp_pad)[None, :] * CMP_STRIDE
    ss = np.arange(nblk)[:, None] * SEL_BLOCK
    ov = np.clip(np.minimum(cs + CMP_BLOCK, ss + SEL_BLOCK) - np.maximum(cs, ss), 0, None)
    return jnp.asarray(ov.astype(np.float32) / CMP_BLOCK)


def kernel(x, norm1_w, w_in, conv_w, conv_b, rg_wa, rg_ba, rg_wx, rg_bx, rg_lambda,
           cmp_k_pe, cmp_k_w1, cmp_k_w2, cmp_v_pe, cmp_v_w1, cmp_v_w2, w_out, normf_w):
    B, S, D = x.shape
    assert D == D_MODEL and w_in.shape[-1] == _D_IN
    assert S % TM == 0 and TS == TM and B % CB == 0
    assert TQ % VC == 0 and WINDOW % TQ == 0 and KS % TQ == 0 and TM % TQ == 0
    assert S % KS == 0 and S >= WINDOW + TQ
    depth = w_in.shape[0]
    T = B * S
    ovt = _overlap_t(S // CMP_STRIDE, S // SEL_BLOCK)
    x2 = x.reshape(T, D)
    for l in range(depth):
        w = w_in[l]
        wn = jnp.concatenate([w[:, _O_RX:_O_Q], w[:, _O_KC:_O_KS], w[:, _O_KS:_O_VS],
                              w[:, _O_KW:_O_VW], w[:, _O_AG:_O_BR], w[:, _O_Q:_O_KC],
                              w[:, _O_VS:_O_KW], w[:, _O_VW:_O_AG], w[:, _O_BR:_D_IN],
                              jnp.zeros((D, LANE - N_BRANCH * N_Q_HEADS), w.dtype)],
                             axis=1).astype(BF16)
        (rnn_out, kvc, ksel, kwin, ag, qt, vselt, vwint, brgt) = _inproj(
            x2, norm1_w[l].reshape(1, D), wn,
            conv_w[l], conv_b[l].reshape(1, D_RNN), _block_diag_halves(rg_wa[l], rg_wx[l]),
            rg_ba[l].reshape(1, D_RNN), rg_bx[l].reshape(1, D_RNN),
            rg_lambda[l].reshape(1, D_RNN), B, S)

        pk, wbk, w2k = _compress_weights(cmp_k_pe[l], cmp_k_w1[l], cmp_k_w2[l])
        pv, wbv, w2v = _compress_weights(cmp_v_pe[l], cmp_v_w1[l], cmp_v_w2[l])
        cmp_kv = _compress(kvc, jnp.stack([pk, pv]), jnp.stack([wbk, wbv]),
                           jnp.stack([w2k, w2v]), B, S)

        attn_out = _nsa(qt, cmp_kv, ksel, vselt, kwin, vwint, brgt, ag, ovt, B, S)

        x2 = _outproj(x2, rnn_out, attn_out, w_out[l].astype(BF16),
                      normf_w.reshape(1, D), final_norm=(l == depth - 1))
    return x2.reshape(B, S, D)
```

```python
import functools

import numpy as np
import jax
import jax.numpy as jnp
from jax import lax
from jax.experimental import pallas as pl
from jax.experimental.pallas import tpu as pltpu

F32 = jnp.float32
BF16 = jnp.bfloat16

D_MODEL = 1024
EPS = 1e-6
D_RNN = 512
RNN_HEADS = 8
RNN_HEAD_DIM = D_RNN // RNN_HEADS
CONV_WIDTH = 4
LRU_C = 8.0
N_Q_HEADS = 8
N_KV_HEADS = 2
HEAD_DIM = 64
Q_PER_KV = N_Q_HEADS // N_KV_HEADS
D_ATTN = N_Q_HEADS * HEAD_DIM
D_KV = N_KV_HEADS * HEAD_DIM
CMP_BLOCK = 32
CMP_STRIDE = 16
CMP_HIDDEN = 256
SEL_BLOCK = 64
SEL_TOPK = 8
SEL_FORCE = 1e9
WINDOW = 512
N_BRANCH = 3
D_MIX = D_RNN + D_ATTN
N_GATE_ROWS = 32

LANE = 128
SCAN_CHUNKS = 8

TM = 1024
TS = 1024
TQ = 256
KS = 512
VC = 128
CB = 4
NEG = -1e30
LOG2E = 1.4426950408889634
ONES_ROWS = 16
VMEM_V7X = 64 * 1024 * 1024
VMEM_LIMIT = VMEM_V7X - 8 * 1024 * 1024

_O_RX, _O_RG, _O_Q = 0, D_RNN, 2 * D_RNN
_O_KC = _O_Q + D_ATTN
_O_VC = _O_KC + D_KV
_O_KS = _O_VC + D_KV
_O_VS = _O_KS + D_KV
_O_KW = _O_VS + D_KV
_O_VW = _O_KW + D_KV
_O_AG = _O_VW + D_KV
_O_BR = _O_AG + D_ATTN
_D_IN = _O_BR + N_BRANCH * N_Q_HEADS


def _silu(x):
    return x * jax.nn.sigmoid(x)


def _inproj_kernel(x_ref, nw_ref, wn_ref, cw_ref, cb_ref, wg_ref, ba_ref, bx_ref, lam_ref,
                   rnn_ref, kvc_ref, ksel_ref, kwin_ref, ag_ref,
                   qt_ref, vselt_ref, vwint_ref, brgt_ref,
                   kv_scr, rx_scr, rg_scr, tail_ref, a_s, u_s, h_s, p_s, hlast, o_scr):
    @pl.when(pl.program_id(1) == 0)
    def _():
        tail_ref[...] = jnp.zeros(tail_ref.shape, F32)
        hlast[...] = jnp.zeros(hlast.shape, F32)

    x = x_ref[...]
    ms = jnp.mean(x * x, axis=-1, keepdims=True)
    h = ((x * lax.rsqrt(ms + EPS)) * nw_ref[...]).astype(BF16)

    def nat(a, b):
        return jnp.dot(h, wn_ref[:, a:b], preferred_element_type=F32)

    def store_step_major(dst_ref, val):
        steps = TM // SCAN_CHUNKS
        for l in range(D_RNN // LANE):
            for c in range(SCAN_CHUNKS):
                dst_ref[l, pl.ds(c, steps, stride=SCAN_CHUNKS), :] = (
                    val[c * steps:(c + 1) * steps, l * LANE:(l + 1) * LANE])

    store_step_major(rx_scr, nat(_O_RX, _O_RG))
    store_step_major(rg_scr, nat(_O_RG, _O_Q))

    def proj_cmp():
        kv = nat(_O_KC, _O_KS)
        for a in range(2):
            kv_scr[a] = kv[:, a * D_KV:(a + 1) * D_KV]
            for l in range(CMP_STRIDE):
                kvc_ref[a, :, l * D_KV:(l + 1) * D_KV] = (
                    kv_scr[a, pl.ds(l, TM // CMP_STRIDE, stride=CMP_STRIDE), :])

    def proj_kv(lo, k_ref, vt_ref):
        kv = nat(lo, lo + 2 * D_KV)
        k_ref[...] = kv[:, :D_KV].astype(BF16)
        vt = kv[:, D_KV:].T.astype(BF16)
        for c in range(TM // VC):
            vt_ref[c] = vt[:, c * VC:(c + 1) * VC]

    def proj_gates():
        ag_ref[...] = nat(_O_AG, _O_BR)
        brgt_ref[...] = nat(_O_BR, _O_BR + LANE).T[0:N_GATE_ROWS, :]

    def proj_q():
        qt_ref[...] = (nat(_O_Q, _O_KC) * (HEAD_DIM ** -0.5 * LOG2E)).T.astype(BF16)

    projections = ((proj_gates,), (proj_q,),
                   (functools.partial(proj_kv, _O_KS, ksel_ref, vselt_ref),
                    functools.partial(proj_kv, _O_KW, kwin_ref, vwint_ref)),
                   (proj_cmp,))
    wraps = _rglru_wraps(rx_scr, tail_ref)
    rows = TM // len(projections)
    for ci, projs in enumerate(projections):
        _rglru_gates(rx_scr, wraps, ci * rows, (ci + 1) * rows,
                     cw_ref, cb_ref, wg_ref, ba_ref, bx_ref, lam_ref, a_s, u_s)
        for proj in projs:
            proj()

    _rglru_scan(rg_scr, rnn_ref, a_s, u_s, h_s, p_s, hlast, o_scr)


def _inproj(x2, nw, wn, cw, cb, wg, ba, bx, lam, B, S):
    T = B * S
    ns = S // TM
    grid = (B, ns)
    row = lambda b, s: (b * ns + s, 0)
    const2 = lambda b, s: (0, 0)
    out_shape = (
        jax.ShapeDtypeStruct((T, D_RNN), BF16),
        jax.ShapeDtypeStruct((2, T // CMP_STRIDE, CMP_STRIDE * D_KV), F32),
        jax.ShapeDtypeStruct((T, D_KV), BF16),
        jax.ShapeDtypeStruct((T, D_KV), BF16),
        jax.ShapeDtypeStruct((T, D_ATTN), F32),
        jax.ShapeDtypeStruct((B, ns, D_ATTN, TM), BF16),
        jax.ShapeDtypeStruct((B, S // VC, D_KV, VC), BF16),
        jax.ShapeDtypeStruct((B, S // VC, D_KV, VC), BF16),
        jax.ShapeDtypeStruct((B, ns, N_GATE_ROWS, TM), F32),
    )
    out_specs = (
        pl.BlockSpec((TM, D_RNN), row),
        pl.BlockSpec((2, TM // CMP_STRIDE, CMP_STRIDE * D_KV), lambda b, s: (0, b * ns + s, 0)),
        pl.BlockSpec((TM, D_KV), row),
        pl.BlockSpec((TM, D_KV), row),
        pl.BlockSpec((TM, D_ATTN), row),
        pl.BlockSpec((None, None, D_ATTN, TM), lambda b, s: (b, s, 0, 0)),
        pl.BlockSpec((None, TM // VC, D_KV, VC), lambda b, s: (b, s, 0, 0)),
        pl.BlockSpec((None, TM // VC, D_KV, VC), lambda b, s: (b, s, 0, 0)),
        pl.BlockSpec((None, None, N_GATE_ROWS, TM), lambda b, s: (b, s, 0, 0)),
    )
    return pl.pallas_call(
        _inproj_kernel,
        grid=grid,
        in_specs=[
            pl.BlockSpec((TM, D_MODEL), row),
            pl.BlockSpec((1, D_MODEL), const2),
            pl.BlockSpec(wn.shape, const2),
            pl.BlockSpec((CONV_WIDTH, D_RNN), const2),
            pl.BlockSpec((1, D_RNN), const2),
            pl.BlockSpec(wg.shape, lambda b, s: (0, 0, 0)),
            pl.BlockSpec((1, D_RNN), const2),
            pl.BlockSpec((1, D_RNN), const2),
            pl.BlockSpec((1, D_RNN), const2),
        ],
        out_specs=out_specs,
        out_shape=out_shape,
        scratch_shapes=[
            pltpu.VMEM((2, TM, D_KV), F32),
            pltpu.VMEM((D_RNN // LANE, TM, LANE), F32),
            pltpu.VMEM((D_RNN // LANE, TM, LANE), F32),
            pltpu.VMEM(((CONV_WIDTH - 1) * SCAN_CHUNKS, D_RNN), F32),
            pltpu.VMEM((TM, D_RNN), F32),
            pltpu.VMEM((TM, D_RNN), F32),
            pltpu.VMEM((TM, D_RNN), F32),
            pltpu.VMEM((TM, D_RNN), F32),
            pltpu.VMEM((1, D_RNN), F32),
            pltpu.VMEM((D_RNN // LANE, TM, LANE), F32),
        ],
        compiler_params=pltpu.CompilerParams(
            dimension_semantics=("arbitrary", "arbitrary"), vmem_limit_bytes=VMEM_LIMIT),
        name="inproj",
    )(x2, nw, wn, cw, cb, wg, ba, bx, lam)


def _slab_rows(x_ref, lo, hi):
    return jnp.concatenate([x_ref[l, lo:hi, :] for l in range(D_RNN // LANE)], axis=1)


def _rglru_wraps(x_ref, tail_ref):
    ntail = (CONV_WIDTH - 1) * SCAN_CHUNKS
    last = _slab_rows(x_ref, TS - ntail, TS)
    sub = lax.broadcasted_iota(jnp.int32, (SCAN_CHUNKS, D_RNN), 0)
    wraps = []
    for j in range(CONV_WIDTH - 1):
        rows = slice(j * SCAN_CHUNKS, (j + 1) * SCAN_CHUNKS)
        cur = pltpu.roll(last[rows, :], 1, 0)
        prev = pltpu.roll(tail_ref[rows, :], 1, 0)
        wraps.append(jnp.where(sub == 0, prev, cur))
    tail_ref[...] = last
    return wraps


def _rglru_gates(x_ref, wraps, lo, hi, cw_ref, cb_ref, wg_ref, ba_ref, bx_ref, lam_ref, a_s, u_s):
    y = cb_ref[...]
    for k in range(CONV_WIDTH):
        off = (CONV_WIDTH - 1 - k) * SCAN_CHUNKS
        if lo >= off:
            xs = _slab_rows(x_ref, lo - off, hi - off)
        else:
            assert lo == 0
            xs = jnp.concatenate(wraps[len(wraps) - off // SCAN_CHUNKS:]
                                 + [_slab_rows(x_ref, 0, hi - off)], axis=0)
        y = y + xs * cw_ref[k:k + 1, :]

    yb = y.astype(BF16)
    half = D_RNN // 2
    pre = [jnp.dot(yb[:, hh * half:(hh + 1) * half], wg_ref[hh], preferred_element_type=F32)
           for hh in range(2)]
    pre_a = jnp.concatenate([pre[0][:, :half], pre[1][:, :half]], axis=1)
    pre_x = jnp.concatenate([pre[0][:, half:], pre[1][:, half:]], axis=1)
    r = jax.nn.sigmoid(pre_a + ba_ref[...])
    i = jax.nn.sigmoid(pre_x + bx_ref[...])
    lam = lam_ref[...]
    lsig = jnp.minimum(lam, 0.0) - jnp.log1p(jnp.exp(-jnp.abs(lam)))
    log_a = LRU_C * r * lsig
    a = jnp.exp(log_a)
    u = jnp.sqrt(1.0 - a * a) * (i * y)
    a_s[lo:hi, :] = a
    u_s[lo:hi, :] = u


def _rglru_scan(gate_ref, out_ref, a_s, u_s, h_s, p_s, hlast, o_scr):
    nstep = TS // SCAN_CHUNKS
    nslab = D_RNN // LANE
    gate = jnp.concatenate([gate_ref[l] for l in range(nslab)], axis=1)

    def body(j, carry):
        h, p = carry
        blk = pl.multiple_of(j * SCAN_CHUNKS, SCAN_CHUNKS)
        a_j = a_s[pl.ds(blk, SCAN_CHUNKS), :]
        h = a_j * h + u_s[pl.ds(blk, SCAN_CHUNKS), :]
        p = a_j * p
        h_s[pl.ds(blk, SCAN_CHUNKS), :] = h
        p_s[pl.ds(blk, SCAN_CHUNKS), :] = p
        return h, p

    h_end, p_end = lax.fori_loop(
        0, nstep, body,
        (jnp.zeros((SCAN_CHUNKS, D_RNN), F32), jnp.ones((SCAN_CHUNKS, D_RNN), F32)), unroll=8)
    carry = hlast[...]
    h_in = []
    for c in range(SCAN_CHUNKS):
        h_in.append(carry)
        carry = h_end[c:c + 1, :] + p_end[c:c + 1, :] * carry
    hlast[...] = carry
    h_in = jnp.concatenate([jnp.concatenate(h_in, axis=0)] * nstep, axis=0)
    o = (h_s[...] + p_s[...] * h_in) * _silu(gate)

    for l in range(nslab):
        o_scr[l] = o[:, l * LANE:(l + 1) * LANE]
        for c in range(SCAN_CHUNKS):
            out_ref[c * nstep:(c + 1) * nstep, l * LANE:(l + 1) * LANE] = (
                o_scr[l, pl.ds(c, nstep, stride=SCAN_CHUNKS), :].astype(out_ref.dtype))


def _compress_kernel(x_ref, pe_ref, w1k_ref, w1v_ref, w2_ref, out_ref, wbig_ref):
    nrow = x_ref.shape[1]
    nchunk = nrow // CB
    nh = N_KV_HEADS * CMP_HIDDEN

    for which, w1_ref in enumerate((w1k_ref, w1v_ref)):
        @pl.when((pl.program_id(0) == which) & (pl.program_id(1) == 0))
        def _(w1_ref=w1_ref):
            wbig_ref[...] = jnp.zeros(wbig_ref.shape, BF16)
            for half in range(2):
                for l in range(CMP_STRIDE):
                    r0 = (half * CMP_STRIDE + l) * HEAD_DIM
                    blk = w1_ref[r0:r0 + HEAD_DIM, :].astype(BF16)
                    for g in range(N_KV_HEADS):
                        rows = slice(l * D_KV + g * HEAD_DIM, l * D_KV + (g + 1) * HEAD_DIM)
                        cols = slice(half * nh + g * CMP_HIDDEN, half * nh + (g + 1) * CMP_HIDDEN)
                        wbig_ref[rows, cols] = blk

    x = x_ref[0]
    xa = (x + pe_ref[0, 0:1, :]).astype(BF16)
    xb = (x + pe_ref[0, 1:2, :]).astype(BF16)
    first = jnp.dot(xa, wbig_ref[:, :nh], preferred_element_type=F32)
    second = jnp.dot(xb, wbig_ref[:, nh:], preferred_element_type=F32)
    hid = _silu(first + pltpu.roll(second, nrow - 1, 0))
    out = jnp.dot(hid.astype(BF16), w2_ref[0], preferred_element_type=F32)
    c_idx = lax.broadcasted_iota(jnp.int32, out.shape, 0) & (nchunk - 1)
    out_ref[0] = jnp.where(c_idx < nchunk - 1, out, 0.0)


def _compress(xc, pe2, w1k, w1v, w2bd, B, S):
    nchunk = S // CMP_STRIDE
    nrow = CB * nchunk
    width = CMP_STRIDE * D_KV
    kv = lambda i, j: (i, 0, 0)
    return pl.pallas_call(
        _compress_kernel,
        grid=(2, B // CB),
        in_specs=[
            pl.BlockSpec((1, nrow, width), lambda i, j: (i, j, 0)),
            pl.BlockSpec((1, 2, width), kv),
            pl.BlockSpec(w1k.shape, lambda i, j: (0, 0)),
            pl.BlockSpec(w1v.shape, lambda i, j: (0, 0)),
            pl.BlockSpec((1,) + w2bd.shape[1:], kv),
        ],
        out_specs=pl.BlockSpec((1, nrow, D_KV), lambda i, j: (i, j, 0)),
        out_shape=jax.ShapeDtypeStruct((2, B * nchunk, D_KV), F32),
        scratch_shapes=[pltpu.VMEM((width, 2 * N_KV_HEADS * CMP_HIDDEN), BF16)],
        compiler_params=pltpu.CompilerParams(
            dimension_semantics=("arbitrary", "arbitrary"), vmem_limit_bytes=VMEM_LIMIT),
        name="compress",
    )(xc, pe2, w1k, w1v, w2bd)


def _nsa_kernel(qt_ref, cmp_ref, ksel_ref, vselt_ref, kwin_ref, vwint_ref, brgt_ref, ag_ref,
                ovt_ref, out_ref, qpad_ref, pen_ref, o_ref, m_ref, acc_ref, s_ref):
    qi = pl.program_id(1)
    q0 = qi * TQ
    nl = Q_PER_KV * TQ
    nblk = pen_ref.shape[1]
    ncmp = cmp_ref.shape[1]
    groups = range(N_KV_HEADS)
    gsl = [slice(g * HEAD_DIM, (g + 1) * HEAD_DIM) for g in groups]

    def tile4(a):
        return jnp.concatenate([a] * Q_PER_KV, axis=1)

    kc = cmp_ref[0].astype(BF16)
    s_cmp = []
    for g in groups:
        zpad = jnp.zeros((HEAD_DIM, TQ), BF16)
        cols = []
        for r in range(Q_PER_KV):
            hq = g * Q_PER_KV + r
            qh = qt_ref[hq * HEAD_DIM:(hq + 1) * HEAD_DIM, :]
            cols.append(jnp.concatenate([qh, zpad] if g == 0 else [zpad, qh], axis=0))
        qpad = jnp.concatenate(cols, axis=1)
        qpad_ref[g] = qpad
        s_cmp.append(jnp.dot(kc, qpad, preferred_element_type=F32))

    def sel_scores(kt, slot):
        k_tile = ksel_ref[pl.ds(pl.multiple_of(kt * KS, KS), KS), :]
        for g in groups:
            s_ref[slot, g] = jnp.dot(k_tile, qpad_ref[g], preferred_element_type=F32)

    wrows = WINDOW + TQ
    w0 = pl.multiple_of(jnp.maximum(q0 - WINDOW, 0), TQ)
    k_win = kwin_ref[pl.ds(w0, wrows), :]
    s_win = [jnp.dot(k_win, qpad_ref[g], preferred_element_type=F32) for g in groups]

    n_full = q0 // KS
    sel_scores(0, n_full & 1)

    ones = jnp.ones((ONES_ROWS, VC), BF16)

    def vt_aug(v_ref, c0, n, g):
        vt = jnp.concatenate([v_ref[c0 + c, gsl[g], :] for c in range(n)], axis=1)
        return jnp.concatenate([vt, jnp.concatenate([ones] * n, axis=1)], axis=0)

    def normalise(res):
        return res[0:HEAD_DIM, :] * (1.0 / res[HEAD_DIM:HEAD_DIM + 1, :])

    c_row = lax.broadcasted_iota(jnp.int32, (ncmp, TQ), 0)
    t_lane = q0 + lax.broadcasted_iota(jnp.int32, (ncmp, TQ), 1)
    cpen = tile4(jnp.where(c_row * CMP_STRIDE + (CMP_BLOCK - 1) <= t_lane, 0.0, NEG))
    vc_t = cmp_ref[1].T
    for g in groups:
        vct = vc_t[gsl[g], :].astype(BF16)
        s = s_cmp[g] + cpen
        m = jnp.max(s, axis=0, keepdims=True)
        m = jnp.where(m > 0.5 * NEG, m, 0.0)
        e = jnp.exp2(s - m)
        den = jnp.sum(e, axis=0, keepdims=True)
        p = e * (1.0 / jnp.where(den > 0.0, den, 1.0))
        o_ref[0, g] = jnp.dot(vct, p.astype(BF16), preferred_element_type=F32)

        psum = p[:, 0:TQ]
        for r in range(1, Q_PER_KV):
            psum = psum + p[:, r * TQ:(r + 1) * TQ]
        imp = jnp.dot(ovt_ref[...], psum, preferred_element_type=F32,
                      precision=lax.Precision.HIGHEST)
        j_row = lax.broadcasted_iota(jnp.int32, (nblk, TQ), 0)
        t_blk = q0 + lax.broadcasted_iota(jnp.int32, (nblk, TQ), 1)
        cur = jnp.right_shift(t_blk, SEL_BLOCK.bit_length() - 1)
        forced = (j_row == 0) | (j_row == cur) | (j_row == cur - 1)
        val = jnp.where(forced, SEL_FORCE, imp)
        val = jnp.where(j_row * SEL_BLOCK <= t_blk, val, -SEL_FORCE)
        rank = jnp.zeros((nblk, TQ), F32)
        for i in range(nblk):
            vi = jnp.broadcast_to(val[i:i + 1, :], (nblk, TQ))
            rank = rank + jnp.where(j_row > i, jnp.where(vi >= val, 1.0, 0.0),
                                    jnp.where(vi > val, 1.0, 0.0))
        chosen = (rank < float(min(SEL_TOPK, nblk))) & (val > -0.5 * SEL_FORCE)
        selpen = jnp.where(chosen, 0.0, NEG)
        for j in range(nblk):
            pen_ref[g, j] = jnp.broadcast_to(selpen[j:j + 1, :], (8, TQ))

    key = w0 + lax.broadcasted_iota(jnp.int32, (wrows, TQ), 0)
    t_q = q0 + lax.broadcasted_iota(jnp.int32, (wrows, TQ), 1)
    wpen = tile4(jnp.where(key <= t_q, jnp.where(key > t_q - WINDOW, 0.0, NEG), NEG))
    p_win = []
    for g in groups:
        s = s_win[g] + wpen
        m = jnp.max(s, axis=0, keepdims=True)
        p_win.append(jnp.exp2(s - m).astype(BF16))
    for g in groups:
        o_ref[2, g] = normalise(jnp.dot(vt_aug(vwint_ref, w0 // VC, wrows // VC, g), p_win[g],
                                        preferred_element_type=F32))

    m_ref[...] = jnp.full(m_ref.shape, NEG, F32)
    acc_ref[...] = jnp.zeros(acc_ref.shape, F32)
    bpt = KS // SEL_BLOCK
    cpt = KS // VC

    def sel_update(kt, slot, diagonal):
        probs, alphas = [], []
        for g in groups:
            m_prev = m_ref[g]
            if diagonal:
                rows = [jnp.concatenate([pen_ref[g, kt * bpt + jj]] * (SEL_BLOCK // 8), axis=0)
                        for jj in range(bpt)]
                pen = jnp.concatenate(rows, axis=0)
                key = kt * KS + lax.broadcasted_iota(jnp.int32, (KS, TQ), 0)
                t_q = q0 + lax.broadcasted_iota(jnp.int32, (KS, TQ), 1)
                s = s_ref[slot, g] + tile4(pen + jnp.where(key <= t_q, 0.0, NEG))
                m_new = jnp.maximum(m_prev, jnp.max(s, axis=0, keepdims=True))
                p = jnp.exp2(s - m_new)
            else:
                sb = [s_ref[slot, g, jj * SEL_BLOCK:(jj + 1) * SEL_BLOCK, :] for jj in range(bpt)]
                bias = [tile4(pen_ref[g, kt * bpt + jj][0:1, :]) for jj in range(bpt)]
                m_new = m_prev
                for jj in range(bpt):
                    m_new = jnp.maximum(m_new, jnp.max(sb[jj], axis=0, keepdims=True) + bias[jj])
                p = jnp.concatenate([jnp.exp2(sb[jj] + (bias[jj] - m_new)) for jj in range(bpt)],
                                    axis=0)
            probs.append(p.astype(BF16))
            alphas.append(jnp.exp2(m_prev - m_new))
            m_ref[g] = m_new
        for g in groups:
            acc_ref[g] = alphas[g] * acc_ref[g] + jnp.dot(
                vt_aug(vselt_ref, kt * cpt, cpt, g), probs[g], preferred_element_type=F32)

    def sel_body(kt, carry):
        for slot in (0, 1):
            @pl.when(((n_full - kt) & 1) == slot)
            def _(slot=slot):
                sel_scores(kt + 1, 1 - slot)
                sel_update(kt, slot, diagonal=False)
        return carry

    lax.fori_loop(0, n_full, sel_body, 0)
    sel_update(n_full, 0, diagonal=True)
    for g in groups:
        o_ref[1, g] = normalise(acc_ref[g])

    gates = jax.nn.sigmoid(brgt_ref[...])
    group_out = []
    for g in groups:
        heads = []
        for r in range(Q_PER_KV):
            hq = g * Q_PER_KV + r
            ls = slice(r * TQ, (r + 1) * TQ)
            o = gates[hq:hq + 1, :] * o_ref[0, g, :, ls]
            for n in range(1, N_BRANCH):
                o = o + gates[n * N_Q_HEADS + hq:n * N_Q_HEADS + hq + 1, :] * o_ref[n, g, :, ls]
            heads.append(o)
        group_out.append(jnp.concatenate(heads, axis=0).T)

    attn = jnp.concatenate(group_out, axis=1)
    out_ref[...] = (attn * _silu(ag_ref[...])).astype(out_ref.dtype)


def _nsa(qt, cmp_kv, ksel, vselt, kwin, vwint, brgt, ag, ovt, B, S):
    T = B * S
    nq = S // TQ
    qpm = TM // TQ
    nblk = S // SEL_BLOCK
    ncmp = S // CMP_STRIDE
    nl = Q_PER_KV * TQ
    return pl.pallas_call(
        _nsa_kernel,
        grid=(B, nq),
        in_specs=[
            pl.BlockSpec((None, None, D_ATTN, TQ), lambda b, q: (b, q // qpm, 0, q % qpm)),
            pl.BlockSpec((2, ncmp, D_KV), lambda b, q: (0, b, 0)),
            pl.BlockSpec((S, D_KV), lambda b, q: (b, 0)),
            pl.BlockSpec((None, S // VC, D_KV, VC), lambda b, q: (b, 0, 0, 0)),
            pl.BlockSpec((S, D_KV), lambda b, q: (b, 0)),
            pl.BlockSpec((None, S // VC, D_KV, VC), lambda b, q: (b, 0, 0, 0)),
            pl.BlockSpec((None, None, N_GATE_ROWS, TQ), lambda b, q: (b, q // qpm, 0, q % qpm)),
            pl.BlockSpec((TQ, D_ATTN), lambda b, q: (b * nq + q, 0)),
            pl.BlockSpec(ovt.shape, lambda b, q: (0, 0)),
        ],
        out_specs=pl.BlockSpec((TQ, D_ATTN), lambda b, q: (b * nq + q, 0)),
        out_shape=jax.ShapeDtypeStruct((T, D_ATTN), BF16),
        scratch_shapes=[
            pltpu.VMEM((N_KV_HEADS, D_KV, nl), BF16),
            pltpu.VMEM((N_KV_HEADS, nblk, 8, TQ), F32),
            pltpu.VMEM((N_BRANCH, N_KV_HEADS, HEAD_DIM, nl), F32),
            pltpu.VMEM((N_KV_HEADS, 1, nl), F32),
            pltpu.VMEM((N_KV_HEADS, HEAD_DIM + ONES_ROWS, nl), F32),
            pltpu.VMEM((2, N_KV_HEADS, KS, nl), F32),
        ],
        compiler_params=pltpu.CompilerParams(
            dimension_semantics=("arbitrary", "arbitrary"), vmem_limit_bytes=VMEM_LIMIT),
        name="nsa",
    )(qt, cmp_kv, ksel, vselt, kwin, vwint, brgt, ag, ovt)


def _outproj_kernel(x_ref, rnn_ref, attn_ref, wo_ref, nfw_ref, out_ref, wo_bf, *, final_norm):
    @pl.when(pl.program_id(0) == 0)
    def _():
        wo_bf[...] = wo_ref[...].astype(BF16)

    y = x_ref[...]
    y = y + jnp.dot(rnn_ref[...], wo_bf[0:D_RNN, :], preferred_element_type=F32)
    y = y + jnp.dot(attn_ref[...], wo_bf[D_RNN:D_MIX, :], preferred_element_type=F32)
    if final_norm:
        ms = jnp.mean(y * y, axis=-1, keepdims=True)
        y = (y * lax.rsqrt(ms + EPS)) * nfw_ref[...]
    out_ref[...] = y


def _outproj(x2, rnn_out, attn_out, wo, nfw, final_norm):
    T = x2.shape[0]
    row = lambda i: (i, 0)
    return pl.pallas_call(
        functools.partial(_outproj_kernel, final_norm=final_norm),
        grid=(T // TM,),
        in_specs=[
            pl.BlockSpec((TM, D_MODEL), row),
            pl.BlockSpec((TM, D_RNN), row),
            pl.BlockSpec((TM, D_ATTN), row),
            pl.BlockSpec((D_MIX, D_MODEL), lambda i: (0, 0)),
            pl.BlockSpec((1, D_MODEL), lambda i: (0, 0)),
        ],
        out_specs=pl.BlockSpec((TM, D_MODEL), row),
        out_shape=jax.ShapeDtypeStruct((T, D_MODEL), F32),
        scratch_shapes=[pltpu.VMEM((D_MIX, D_MODEL), BF16)],
        compiler_params=pltpu.CompilerParams(
            dimension_semantics=("arbitrary",), vmem_limit_bytes=VMEM_LIMIT),
        name="outproj",
    )(x2, rnn_out, attn_out, wo, nfw)


def _block_diag_halves(wa, wx):
    eye = jnp.eye(RNN_HEADS, dtype=wa.dtype)
    full = lambda w: jnp.einsum('hij,hk->hikj', w, eye).reshape(D_RNN, D_RNN)
    fa, fx = full(wa), full(wx)
    half = D_RNN // 2
    return jnp.stack([
        jnp.concatenate([fa[s:s + half, s:s + half], fx[s:s + half, s:s + half]], axis=1)
        for s in (0, half)]).astype(BF16)


def _compress_weights(pe, w2):
    eye = jnp.eye(N_KV_HEADS, dtype=w2.dtype)
    w2bd = jnp.einsum('nd,ge->gned', w2, eye).reshape(N_KV_HEADS * CMP_HIDDEN, D_KV)
    per = pe.reshape(2, CMP_STRIDE, 1, HEAD_DIM)
    pe2 = jnp.broadcast_to(per, (2, CMP_STRIDE, N_KV_HEADS, HEAD_DIM)).reshape(2, CMP_STRIDE * D_KV)
    return pe2, w2bd.astype(BF16)


def _overlap_t(ncmp_pad, nblk):
    cs = np.arange(ncmp_pad)[None, :] * CMP_STRIDE
    ss = np.arange(nblk)[:, None] * SEL_BLOCK
    ov = np.clip(np.minimum(cs + CMP_BLOCK, ss + SEL_BLOCK) - np.maximum(cs, ss), 0, None)
    return jnp.asarray(ov.astype(np.float32) / CMP_BLOCK)


def kernel(x, norm1_w, w_in, conv_w, conv_b, rg_wa, rg_ba, rg_wx, rg_bx, rg_lambda,
           cmp_k_pe, cmp_k_w1, cmp_k_w2, cmp_v_pe, cmp_v_w1, cmp_v_w2, w_out, normf_w):
    B, S, D = x.shape
    assert D == D_MODEL and w_in.shape[-1] == _D_IN
    assert S % TM == 0 and TS == TM and B % CB == 0
    assert TQ % VC == 0 and WINDOW % TQ == 0 and KS % TQ == 0 and TM % TQ == 0
    assert S % KS == 0 and S >= WINDOW + TQ
    depth = w_in.shape[0]
    T = B * S
    ovt = _overlap_t(S // CMP_STRIDE, S // SEL_BLOCK)
    x2 = x.reshape(T, D)
    for l in range(depth):
        wn = jnp.pad(w_in[l].astype(BF16), ((0, 0), (0, _O_BR + LANE - _D_IN)))
        (rnn_out, kvc, ksel, kwin, ag, qt, vselt, vwint, brgt) = _inproj(
            x2, norm1_w[l].reshape(1, D), wn,
            conv_w[l], conv_b[l].reshape(1, D_RNN), _block_diag_halves(rg_wa[l], rg_wx[l]),
            rg_ba[l].reshape(1, D_RNN), rg_bx[l].reshape(1, D_RNN),
            rg_lambda[l].reshape(1, D_RNN), B, S)

        pk, w2k = _compress_weights(cmp_k_pe[l], cmp_k_w2[l])
        pv, w2v = _compress_weights(cmp_v_pe[l], cmp_v_w2[l])
        cmp_kv = _compress(kvc, jnp.stack([pk, pv]), cmp_k_w1[l], cmp_v_w1[l],
                           jnp.stack([w2k, w2v]), B, S)

        attn_out = _nsa(qt, cmp_kv, ksel, vselt, kwin, vwint, brgt, ag, ovt, B, S)

        x2 = _outproj(x2, rnn_out, attn_out, w_out[l],
                      normf_w.reshape(1, D), final_norm=(l == depth - 1))
    return x2.reshape(B, S, D)
```

```python
import functools

import numpy as np
import jax
import jax.numpy as jnp
from jax import lax
from jax.experimental import pallas as pl
from jax.experimental.pallas import tpu as pltpu

F32 = jnp.float32
BF16 = jnp.bfloat16

D_MODEL = 1024
EPS = 1e-6
D_RNN = 512
RNN_HEADS = 8
RNN_HEAD_DIM = D_RNN // RNN_HEADS
CONV_WIDTH = 4
LRU_C = 8.0
N_Q_HEADS = 8
N_KV_HEADS = 2
HEAD_DIM = 64
Q_PER_KV = N_Q_HEADS // N_KV_HEADS
D_ATTN = N_Q_HEADS * HEAD_DIM
D_KV = N_KV_HEADS * HEAD_DIM
CMP_BLOCK = 32
CMP_STRIDE = 16
CMP_HIDDEN = 256
SEL_BLOCK = 64
SEL_TOPK = 8
SEL_FORCE = 1e9
WINDOW = 512
N_BRANCH = 3
D_MIX = D_RNN + D_ATTN
N_GATE_ROWS = 32

LANE = 128
SCAN_CHUNKS = 8

TM = 1024
TS = 1024
TO = 1024
TQ = 256
KS = 512
VC = 128
CB = 4
NEG = -1e30
LOG2E = 1.4426950408889634
ONES_ROWS = 16
VMEM_V7X = 64 * 1024 * 1024
VMEM_LIMIT = VMEM_V7X - 8 * 1024 * 1024

_O_RX, _O_RG, _O_Q = 0, D_RNN, 2 * D_RNN
_O_KC = _O_Q + D_ATTN
_O_VC = _O_KC + D_KV
_O_KS = _O_VC + D_KV
_O_VS = _O_KS + D_KV
_O_KW = _O_VS + D_KV
_O_VW = _O_KW + D_KV
_O_AG = _O_VW + D_KV
_O_BR = _O_AG + D_ATTN
_D_IN = _O_BR + N_BRANCH * N_Q_HEADS


def _silu(x):
    return x * jax.nn.sigmoid(x)


def _inproj_kernel(x_ref, nw_ref, wn_ref, cw_ref, cb_ref, wg_ref, ba_ref, bx_ref, lam_ref,
                   rnn_ref, kvc_ref, ksel_ref, kwin_ref, ag_ref,
                   qt_ref, vselt_ref, vwint_ref, brgt_ref,
                   kv_scr, rx_scr, rg_scr, tail_ref, a_s, u_s, h_s, p_s, hlast, o_scr):
    @pl.when(pl.program_id(1) == 0)
    def _():
        tail_ref[...] = jnp.zeros(tail_ref.shape, F32)
        hlast[...] = jnp.zeros(hlast.shape, F32)

    x = x_ref[...]
    ms = jnp.mean(x * x, axis=-1, keepdims=True)
    h = ((x * lax.rsqrt(ms + EPS)) * nw_ref[...]).astype(BF16)

    def nat(a, b):
        return jnp.dot(h, wn_ref[:, a:b], preferred_element_type=F32)

    def store_step_major(dst_ref, val):
        steps = TM // SCAN_CHUNKS
        for l in range(D_RNN // LANE):
            for c in range(SCAN_CHUNKS):
                dst_ref[l, pl.ds(c, steps, stride=SCAN_CHUNKS), :] = (
                    val[c * steps:(c + 1) * steps, l * LANE:(l + 1) * LANE])

    store_step_major(rx_scr, nat(_O_RX, _O_RG))
    store_step_major(rg_scr, nat(_O_RG, _O_Q))

    def proj_cmp():
        kv = nat(_O_KC, _O_KS)
        for a in range(2):
            kv_scr[a] = kv[:, a * D_KV:(a + 1) * D_KV]
            for l in range(CMP_STRIDE):
                kvc_ref[a, :, l * D_KV:(l + 1) * D_KV] = (
                    kv_scr[a, pl.ds(l, TM // CMP_STRIDE, stride=CMP_STRIDE), :])

    def proj_kv(lo, k_ref, vt_ref):
        kv = nat(lo, lo + 2 * D_KV)
        k_ref[...] = kv[:, :D_KV].astype(BF16)
        vt = kv[:, D_KV:].T.astype(BF16)
        for c in range(TM // VC):
            vt_ref[c] = vt[:, c * VC:(c + 1) * VC]

    def proj_gates():
        ag_ref[...] = nat(_O_AG, _O_BR)
        brgt_ref[...] = nat(_O_BR, _O_BR + LANE).T[0:N_GATE_ROWS, :]

    def proj_q():
        qt_ref[...] = (nat(_O_Q, _O_KC) * (HEAD_DIM ** -0.5 * LOG2E)).T.astype(BF16)

    projections = ((proj_gates,), (proj_q,),
                   (functools.partial(proj_kv, _O_KS, ksel_ref, vselt_ref),
                    functools.partial(proj_kv, _O_KW, kwin_ref, vwint_ref)),
                   (proj_cmp,))
    wraps = _rglru_wraps(rx_scr, tail_ref)
    rows = TM // len(projections)
    for ci, projs in enumerate(projections):
        _rglru_gates(rx_scr, wraps, ci * rows, (ci + 1) * rows,
                     cw_ref, cb_ref, wg_ref, ba_ref, bx_ref, lam_ref, a_s, u_s)
        for proj in projs:
            proj()

    _rglru_scan(rg_scr, rnn_ref, a_s, u_s, h_s, p_s, hlast, o_scr)


def _inproj(x2, nw, wn, cw, cb, wg, ba, bx, lam, B, S):
    T = B * S
    ns = S // TM
    grid = (B, ns)
    row = lambda b, s: (b * ns + s, 0)
    const2 = lambda b, s: (0, 0)
    out_shape = (
        jax.ShapeDtypeStruct((T, D_RNN), BF16),
        jax.ShapeDtypeStruct((2, T // CMP_STRIDE, CMP_STRIDE * D_KV), F32),
        jax.ShapeDtypeStruct((T, D_KV), BF16),
        jax.ShapeDtypeStruct((T, D_KV), BF16),
        jax.ShapeDtypeStruct((T, D_ATTN), F32),
        jax.ShapeDtypeStruct((B, ns, D_ATTN, TM), BF16),
        jax.ShapeDtypeStruct((B, S // VC, D_KV, VC), BF16),
        jax.ShapeDtypeStruct((B, S // VC, D_KV, VC), BF16),
        jax.ShapeDtypeStruct((B, ns, N_GATE_ROWS, TM), F32),
    )
    out_specs = (
        pl.BlockSpec((TM, D_RNN), row),
        pl.BlockSpec((2, TM // CMP_STRIDE, CMP_STRIDE * D_KV), lambda b, s: (0, b * ns + s, 0)),
        pl.BlockSpec((TM, D_KV), row),
        pl.BlockSpec((TM, D_KV), row),
        pl.BlockSpec((TM, D_ATTN), row),
        pl.BlockSpec((None, None, D_ATTN, TM), lambda b, s: (b, s, 0, 0)),
        pl.BlockSpec((None, TM // VC, D_KV, VC), lambda b, s: (b, s, 0, 0)),
        pl.BlockSpec((None, TM // VC, D_KV, VC), lambda b, s: (b, s, 0, 0)),
        pl.BlockSpec((None, None, N_GATE_ROWS, TM), lambda b, s: (b, s, 0, 0)),
    )
    return pl.pallas_call(
        _inproj_kernel,
        grid=grid,
        in_specs=[
            pl.BlockSpec((TM, D_MODEL), row),
            pl.BlockSpec((1, D_MODEL), const2),
            pl.BlockSpec(wn.shape, const2),
            pl.BlockSpec((CONV_WIDTH, D_RNN), const2),
            pl.BlockSpec((1, D_RNN), const2),
            pl.BlockSpec(wg.shape, lambda b, s: (0, 0, 0)),
            pl.BlockSpec((1, D_RNN), const2),
            pl.BlockSpec((1, D_RNN), const2),
            pl.BlockSpec((1, D_RNN), const2),
        ],
        out_specs=out_specs,
        out_shape=out_shape,
        scratch_shapes=[
            pltpu.VMEM((2, TM, D_KV), F32),
            pltpu.VMEM((D_RNN // LANE, TM, LANE), F32),
            pltpu.VMEM((D_RNN // LANE, TM, LANE), F32),
            pltpu.VMEM(((CONV_WIDTH - 1) * SCAN_CHUNKS, D_RNN), F32),
            pltpu.VMEM((TM, D_RNN), F32),
            pltpu.VMEM((TM, D_RNN), F32),
            pltpu.VMEM((TM, D_RNN), F32),
            pltpu.VMEM((TM, D_RNN), F32),
            pltpu.VMEM((1, D_RNN), F32),
            pltpu.VMEM((D_RNN // LANE, TM, LANE), F32),
        ],
        compiler_params=pltpu.CompilerParams(
            dimension_semantics=("arbitrary", "arbitrary"), vmem_limit_bytes=VMEM_LIMIT),
        name="inproj",
    )(x2, nw, wn, cw, cb, wg, ba, bx, lam)


def _slab_rows(x_ref, lo, hi):
    return jnp.concatenate([x_ref[l, lo:hi, :] for l in range(D_RNN // LANE)], axis=1)


def _rglru_wraps(x_ref, tail_ref):
    ntail = (CONV_WIDTH - 1) * SCAN_CHUNKS
    last = _slab_rows(x_ref, TS - ntail, TS)
    sub = lax.broadcasted_iota(jnp.int32, (SCAN_CHUNKS, D_RNN), 0)
    wraps = []
    for j in range(CONV_WIDTH - 1):
        rows = slice(j * SCAN_CHUNKS, (j + 1) * SCAN_CHUNKS)
        cur = pltpu.roll(last[rows, :], 1, 0)
        prev = pltpu.roll(tail_ref[rows, :], 1, 0)
        wraps.append(jnp.where(sub == 0, prev, cur))
    tail_ref[...] = last
    return wraps


def _rglru_gates(x_ref, wraps, lo, hi, cw_ref, cb_ref, wg_ref, ba_ref, bx_ref, lam_ref, a_s, u_s):
    y = cb_ref[...]
    for k in range(CONV_WIDTH):
        off = (CONV_WIDTH - 1 - k) * SCAN_CHUNKS
        if lo >= off:
            xs = _slab_rows(x_ref, lo - off, hi - off)
        else:
            assert lo == 0
            xs = jnp.concatenate(wraps[len(wraps) - off // SCAN_CHUNKS:]
                                 + [_slab_rows(x_ref, 0, hi - off)], axis=0)
        y = y + xs * cw_ref[k:k + 1, :]

    yb = y.astype(BF16)
    half = D_RNN // 2
    pre = [jnp.dot(yb[:, hh * half:(hh + 1) * half], wg_ref[hh], preferred_element_type=F32)
           for hh in range(2)]
    pre_a = jnp.concatenate([pre[0][:, :half], pre[1][:, :half]], axis=1)
    pre_x = jnp.concatenate([pre[0][:, half:], pre[1][:, half:]], axis=1)
    r = jax.nn.sigmoid(pre_a + ba_ref[...])
    i = jax.nn.sigmoid(pre_x + bx_ref[...])
    lam = lam_ref[...]
    lsig = jnp.minimum(lam, 0.0) - jnp.log1p(jnp.exp(-jnp.abs(lam)))
    log_a = LRU_C * r * lsig
    a = jnp.exp(log_a)
    u = jnp.sqrt(1.0 - a * a) * (i * y)
    a_s[lo:hi, :] = a
    u_s[lo:hi, :] = u


def _rglru_scan(gate_ref, out_ref, a_s, u_s, h_s, p_s, hlast, o_scr):
    nstep = TS // SCAN_CHUNKS
    nslab = D_RNN // LANE
    gate = jnp.concatenate([gate_ref[l] for l in range(nslab)], axis=1)

    def body(j, carry):
        h, p = carry
        blk = pl.multiple_of(j * SCAN_CHUNKS, SCAN_CHUNKS)
        a_j = a_s[pl.ds(blk, SCAN_CHUNKS), :]
        h = a_j * h + u_s[pl.ds(blk, SCAN_CHUNKS), :]
        p = a_j * p
        h_s[pl.ds(blk, SCAN_CHUNKS), :] = h
        p_s[pl.ds(blk, SCAN_CHUNKS), :] = p
        return h, p

    h_end, p_end = lax.fori_loop(
        0, nstep, body,
        (jnp.zeros((SCAN_CHUNKS, D_RNN), F32), jnp.ones((SCAN_CHUNKS, D_RNN), F32)), unroll=8)
    carry = hlast[...]
    h_in = []
    for c in range(SCAN_CHUNKS):
        h_in.append(carry)
        carry = h_end[c:c + 1, :] + p_end[c:c + 1, :] * carry
    hlast[...] = carry
    h_in = jnp.concatenate([jnp.concatenate(h_in, axis=0)] * nstep, axis=0)
    o = (h_s[...] + p_s[...] * h_in) * _silu(gate)

    for l in range(nslab):
        o_scr[l] = o[:, l * LANE:(l + 1) * LANE]
        for c in range(SCAN_CHUNKS):
            out_ref[c * nstep:(c + 1) * nstep, l * LANE:(l + 1) * LANE] = (
                o_scr[l, pl.ds(c, nstep, stride=SCAN_CHUNKS), :].astype(out_ref.dtype))


def _compress_kernel(x_ref, pe_ref, w1k_ref, w1v_ref, w2_ref, out_ref, wbig_ref):
    nrow = x_ref.shape[1]
    nchunk = nrow // CB
    nh = N_KV_HEADS * CMP_HIDDEN

    for which, w1_ref in enumerate((w1k_ref, w1v_ref)):
        @pl.when((pl.program_id(0) == which) & (pl.program_id(1) == 0))
        def _(w1_ref=w1_ref):
            wbig_ref[...] = jnp.zeros(wbig_ref.shape, BF16)
            for half in range(2):
                for l in range(CMP_STRIDE):
                    r0 = (half * CMP_STRIDE + l) * HEAD_DIM
                    blk = w1_ref[r0:r0 + HEAD_DIM, :].astype(BF16)
                    for g in range(N_KV_HEADS):
                        rows = slice(l * D_KV + g * HEAD_DIM, l * D_KV + (g + 1) * HEAD_DIM)
                        cols = slice(half * nh + g * CMP_HIDDEN, half * nh + (g + 1) * CMP_HIDDEN)
                        wbig_ref[rows, cols] = blk

    x = x_ref[0]
    xa = (x + pe_ref[0, 0:1, :]).astype(BF16)
    xb = (x + pe_ref[0, 1:2, :]).astype(BF16)
    first = jnp.dot(xa, wbig_ref[:, :nh], preferred_element_type=F32)
    second = jnp.dot(xb, wbig_ref[:, nh:], preferred_element_type=F32)
    hid = _silu(first + pltpu.roll(second, nrow - 1, 0))
    out = jnp.dot(hid.astype(BF16), w2_ref[0], preferred_element_type=F32)
    c_idx = lax.broadcasted_iota(jnp.int32, out.shape, 0) & (nchunk - 1)
    out_ref[0] = jnp.where(c_idx < nchunk - 1, out, 0.0)


def _compress(xc, pe2, w1k, w1v, w2bd, B, S):
    nchunk = S // CMP_STRIDE
    nrow = CB * nchunk
    width = CMP_STRIDE * D_KV
    kv = lambda i, j: (i, 0, 0)
    return pl.pallas_call(
        _compress_kernel,
        grid=(2, B // CB),
        in_specs=[
            pl.BlockSpec((1, nrow, width), lambda i, j: (i, j, 0)),
            pl.BlockSpec((1, 2, width), kv),
            pl.BlockSpec(w1k.shape, lambda i, j: (0, 0)),
            pl.BlockSpec(w1v.shape, lambda i, j: (0, 0)),
            pl.BlockSpec((1,) + w2bd.shape[1:], kv),
        ],
        out_specs=pl.BlockSpec((1, nrow, D_KV), lambda i, j: (i, j, 0)),
        out_shape=jax.ShapeDtypeStruct((2, B * nchunk, D_KV), F32),
        scratch_shapes=[pltpu.VMEM((width, 2 * N_KV_HEADS * CMP_HIDDEN), BF16)],
        compiler_params=pltpu.CompilerParams(
            dimension_semantics=("arbitrary", "arbitrary"), vmem_limit_bytes=VMEM_LIMIT),
        name="compress",
    )(xc, pe2, w1k, w1v, w2bd)


def _nsa_kernel(qt_ref, cmp_ref, ksel_ref, vselt_ref, kwin_ref, vwint_ref, brgt_ref, ag_ref,
                ovt_ref, out_ref, qpad_ref, pen_ref, o_ref, m_ref, acc_ref, s_ref):
    qi = pl.program_id(1)
    q0 = qi * TQ
    nl = Q_PER_KV * TQ
    nblk = pen_ref.shape[1]
    ncmp = cmp_ref.shape[1]
    groups = range(N_KV_HEADS)
    gsl = [slice(g * HEAD_DIM, (g + 1) * HEAD_DIM) for g in groups]

    def tile4(a):
        return jnp.concatenate([a] * Q_PER_KV, axis=1)

    kc = cmp_ref[0].astype(BF16)
    s_cmp = []
    for g in groups:
        zpad = jnp.zeros((HEAD_DIM, TQ), BF16)
        cols = []
        for r in range(Q_PER_KV):
            hq = g * Q_PER_KV + r
            qh = qt_ref[hq * HEAD_DIM:(hq + 1) * HEAD_DIM, :]
            cols.append(jnp.concatenate([qh, zpad] if g == 0 else [zpad, qh], axis=0))
        qpad = jnp.concatenate(cols, axis=1)
        qpad_ref[g] = qpad
        s_cmp.append(jnp.dot(kc, qpad, preferred_element_type=F32))

    def sel_scores(kt, slot):
        k_tile = ksel_ref[pl.ds(pl.multiple_of(kt * KS, KS), KS), :]
        for g in groups:
            s_ref[slot, g] = jnp.dot(k_tile, qpad_ref[g], preferred_element_type=F32)

    wrows = WINDOW + TQ
    w0 = pl.multiple_of(jnp.maximum(q0 - WINDOW, 0), TQ)
    k_win = kwin_ref[pl.ds(w0, wrows), :]

    n_full = q0 // KS

    ones = jnp.ones((ONES_ROWS, VC), BF16)

    def vt_aug(v_ref, c0, n, g):
        vt = jnp.concatenate([v_ref[c0 + c, gsl[g], :] for c in range(n)], axis=1)
        return jnp.concatenate([vt, jnp.concatenate([ones] * n, axis=1)], axis=0)

    def normalise(res):
        return res[0:HEAD_DIM, :] * (1.0 / res[HEAD_DIM:HEAD_DIM + 1, :])

    c_row = lax.broadcasted_iota(jnp.int32, (ncmp, TQ), 0)
    t_lane = q0 + lax.broadcasted_iota(jnp.int32, (ncmp, TQ), 1)
    cpen = tile4(jnp.where(c_row * CMP_STRIDE + (CMP_BLOCK - 1) <= t_lane, 0.0, NEG))
    vc_t = cmp_ref[1].T
    for g in groups:
        vct = vc_t[gsl[g], :].astype(BF16)
        s = s_cmp[g] + cpen
        m = jnp.max(s, axis=0, keepdims=True)
        m = jnp.where(m > 0.5 * NEG, m, 0.0)
        e = jnp.exp2(s - m)
        den = jnp.sum(e, axis=0, keepdims=True)
        p = e * (1.0 / jnp.where(den > 0.0, den, 1.0))
        o_ref[0, g] = jnp.dot(vct, p.astype(BF16), preferred_element_type=F32)

        psum = p[:, 0:TQ]
        for r in range(1, Q_PER_KV):
            psum = psum + p[:, r * TQ:(r + 1) * TQ]
        imp = jnp.dot(ovt_ref[...], psum, preferred_element_type=F32,
                      precision=lax.Precision.HIGHEST)
        j_row = lax.broadcasted_iota(jnp.int32, (nblk, TQ), 0)
        t_blk = q0 + lax.broadcasted_iota(jnp.int32, (nblk, TQ), 1)
        cur = jnp.right_shift(t_blk, SEL_BLOCK.bit_length() - 1)
        forced = (j_row == 0) | (j_row == cur) | (j_row == cur - 1)
        val = jnp.where(forced, SEL_FORCE, imp)
        val = jnp.where(j_row * SEL_BLOCK <= t_blk, val, -SEL_FORCE)
        rank = jnp.zeros((nblk, TQ), F32)
        for i in range(nblk):
            vi = jnp.broadcast_to(val[i:i + 1, :], (nblk, TQ))
            rank = rank + jnp.where(j_row > i, jnp.where(vi >= val, 1.0, 0.0),
                                    jnp.where(vi > val, 1.0, 0.0))
        chosen = (rank < float(min(SEL_TOPK, nblk))) & (val > -0.5 * SEL_FORCE)
        selpen = jnp.where(chosen, 0.0, NEG)
        for j in range(nblk):
            pen_ref[g, j] = jnp.broadcast_to(selpen[j:j + 1, :], (8, TQ))

    s_win = [jnp.dot(k_win, qpad_ref[g], preferred_element_type=F32) for g in groups]
    key = w0 + lax.broadcasted_iota(jnp.int32, (wrows, TQ), 0)
    t_q = q0 + lax.broadcasted_iota(jnp.int32, (wrows, TQ), 1)
    wpen = tile4(jnp.where(key <= t_q, jnp.where(key > t_q - WINDOW, 0.0, NEG), NEG))
    p_win = []
    for g in groups:
        s = s_win[g] + wpen
        m = jnp.max(s, axis=0, keepdims=True)
        p_win.append(jnp.exp2(s - m).astype(BF16))
    sel_scores(0, n_full & 1)
    for g in groups:
        o_ref[2, g] = normalise(jnp.dot(vt_aug(vwint_ref, w0 // VC, wrows // VC, g), p_win[g],
                                        preferred_element_type=F32))

    m_ref[...] = jnp.full(m_ref.shape, NEG, F32)
    acc_ref[...] = jnp.zeros(acc_ref.shape, F32)
    bpt = KS // SEL_BLOCK
    cpt = KS // VC

    def sel_update(kt, slot, diagonal):
        probs, alphas = [], []
        for g in groups:
            m_prev = m_ref[g]
            if diagonal:
                rows = [jnp.concatenate([pen_ref[g, kt * bpt + jj]] * (SEL_BLOCK // 8), axis=0)
                        for jj in range(bpt)]
                pen = jnp.concatenate(rows, axis=0)
                key = kt * KS + lax.broadcasted_iota(jnp.int32, (KS, TQ), 0)
                t_q = q0 + lax.broadcasted_iota(jnp.int32, (KS, TQ), 1)
                s = s_ref[slot, g] + tile4(pen + jnp.where(key <= t_q, 0.0, NEG))
                m_new = jnp.maximum(m_prev, jnp.max(s, axis=0, keepdims=True))
                p = jnp.exp2(s - m_new)
            else:
                sb = [s_ref[slot, g, jj * SEL_BLOCK:(jj + 1) * SEL_BLOCK, :] for jj in range(bpt)]
                bias = [tile4(pen_ref[g, kt * bpt + jj][0:1, :]) for jj in range(bpt)]
                m_new = m_prev
                for jj in range(bpt):
                    m_new = jnp.maximum(m_new, jnp.max(sb[jj], axis=0, keepdims=True) + bias[jj])
                p = jnp.concatenate([jnp.exp2(sb[jj] + (bias[jj] - m_new)) for jj in range(bpt)],
                                    axis=0)
            probs.append(p.astype(BF16))
            alphas.append(jnp.exp2(m_prev - m_new))
            m_ref[g] = m_new
        for g in groups:
            acc_ref[g] = alphas[g] * acc_ref[g] + jnp.dot(
                vt_aug(vselt_ref, kt * cpt, cpt, g), probs[g], preferred_element_type=F32)

    def sel_body(kt, carry):
        for slot in (0, 1):
            @pl.when(((n_full - kt) & 1) == slot)
            def _(slot=slot):
                sel_scores(kt + 1, 1 - slot)
                sel_update(kt, slot, diagonal=False)
        return carry

    lax.fori_loop(0, n_full, sel_body, 0)
    sel_update(n_full, 0, diagonal=True)
    for g in groups:
        o_ref[1, g] = normalise(acc_ref[g])

    gates = jax.nn.sigmoid(brgt_ref[...])
    group_out = []
    for g in groups:
        heads = []
        for r in range(Q_PER_KV):
            hq = g * Q_PER_KV + r
            ls = slice(r * TQ, (r + 1) * TQ)
            o = gates[hq:hq + 1, :] * o_ref[0, g, :, ls]
            for n in range(1, N_BRANCH):
                o = o + gates[n * N_Q_HEADS + hq:n * N_Q_HEADS + hq + 1, :] * o_ref[n, g, :, ls]
            heads.append(o)
        group_out.append(jnp.concatenate(heads, axis=0).T)

    attn = jnp.concatenate(group_out, axis=1)
    out_ref[...] = (attn * _silu(ag_ref[...])).astype(out_ref.dtype)


def _nsa(qt, cmp_kv, ksel, vselt, kwin, vwint, brgt, ag, ovt, B, S):
    T = B * S
    nq = S // TQ
    qpm = TM // TQ
    nblk = S // SEL_BLOCK
    ncmp = S // CMP_STRIDE
    nl = Q_PER_KV * TQ
    return pl.pallas_call(
        _nsa_kernel,
        grid=(B, nq),
        in_specs=[
            pl.BlockSpec((None, None, D_ATTN, TQ), lambda b, q: (b, q // qpm, 0, q % qpm)),
            pl.BlockSpec((2, ncmp, D_KV), lambda b, q: (0, b, 0)),
            pl.BlockSpec((S, D_KV), lambda b, q: (b, 0)),
            pl.BlockSpec((None, S // VC, D_KV, VC), lambda b, q: (b, 0, 0, 0)),
            pl.BlockSpec((S, D_KV), lambda b, q: (b, 0)),
            pl.BlockSpec((None, S // VC, D_KV, VC), lambda b, q: (b, 0, 0, 0)),
            pl.BlockSpec((None, None, N_GATE_ROWS, TQ), lambda b, q: (b, q // qpm, 0, q % qpm)),
            pl.BlockSpec((TQ, D_ATTN), lambda b, q: (b * nq + q, 0)),
            pl.BlockSpec(ovt.shape, lambda b, q: (0, 0)),
        ],
        out_specs=pl.BlockSpec((TQ, D_ATTN), lambda b, q: (b * nq + q, 0)),
        out_shape=jax.ShapeDtypeStruct((T, D_ATTN), BF16),
        scratch_shapes=[
            pltpu.VMEM((N_KV_HEADS, D_KV, nl), BF16),
            pltpu.VMEM((N_KV_HEADS, nblk, 8, TQ), F32),
            pltpu.VMEM((N_BRANCH, N_KV_HEADS, HEAD_DIM, nl), F32),
            pltpu.VMEM((N_KV_HEADS, 1, nl), F32),
            pltpu.VMEM((N_KV_HEADS, HEAD_DIM + ONES_ROWS, nl), F32),
            pltpu.VMEM((2, N_KV_HEADS, KS, nl), F32),
        ],
        compiler_params=pltpu.CompilerParams(
            dimension_semantics=("arbitrary", "arbitrary"), vmem_limit_bytes=VMEM_LIMIT),
        name="nsa",
    )(qt, cmp_kv, ksel, vselt, kwin, vwint, brgt, ag, ovt)


def _outproj_kernel(x_ref, rnn_ref, attn_ref, wo_ref, nfw_ref, out_ref, wo_bf, *, final_norm):
    @pl.when(pl.program_id(0) == 0)
    def _():
        wo_bf[...] = wo_ref[...].astype(BF16)

    y = x_ref[...]
    y = y + jnp.dot(rnn_ref[...], wo_bf[0:D_RNN, :], preferred_element_type=F32)
    y = y + jnp.dot(attn_ref[...], wo_bf[D_RNN:D_MIX, :], preferred_element_type=F32)
    if final_norm:
        ms = jnp.mean(y * y, axis=-1, keepdims=True)
        y = (y * lax.rsqrt(ms + EPS)) * nfw_ref[...]
    out_ref[...] = y


def _outproj(x2, rnn_out, attn_out, wo, nfw, final_norm):
    T = x2.shape[0]
    row = lambda i: (i, 0)
    return pl.pallas_call(
        functools.partial(_outproj_kernel, final_norm=final_norm),
        grid=(T // TO,),
        in_specs=[
            pl.BlockSpec((TO, D_MODEL), row),
            pl.BlockSpec((TO, D_RNN), row),
            pl.BlockSpec((TO, D_ATTN), row),
            pl.BlockSpec((D_MIX, D_MODEL), lambda i: (0, 0)),
            pl.BlockSpec((1, D_MODEL), lambda i: (0, 0)),
        ],
        out_specs=pl.BlockSpec((TO, D_MODEL), row),
        out_shape=jax.ShapeDtypeStruct((T, D_MODEL), F32),
        scratch_shapes=[pltpu.VMEM((D_MIX, D_MODEL), BF16)],
        compiler_params=pltpu.CompilerParams(
            dimension_semantics=("arbitrary",), vmem_limit_bytes=VMEM_LIMIT),
        name="outproj",
    )(x2, rnn_out, attn_out, wo, nfw)


def _block_diag_halves(wa, wx):
    eye = jnp.eye(RNN_HEADS, dtype=wa.dtype)
    full = lambda w: jnp.einsum('hij,hk->hikj', w, eye).reshape(D_RNN, D_RNN)
    fa, fx = full(wa), full(wx)
    half = D_RNN // 2
    return jnp.stack([
        jnp.concatenate([fa[s:s + half, s:s + half], fx[s:s + half, s:s + half]], axis=1)
        for s in (0, half)]).astype(BF16)


def _compress_weights(pe, w2):
    eye = jnp.eye(N_KV_HEADS, dtype=w2.dtype)
    w2bd = jnp.einsum('nd,ge->gned', w2, eye).reshape(N_KV_HEADS * CMP_HIDDEN, D_KV)
    per = pe.reshape(2, CMP_STRIDE, 1, HEAD_DIM)
    pe2 = jnp.broadcast_to(per, (2, CMP_STRIDE, N_KV_HEADS, HEAD_DIM)).reshape(2, CMP_STRIDE * D_KV)
    return pe2, w2bd.astype(BF16)


def _overlap_t(ncmp_pad, nblk):
    cs = np.arange(ncmp_pad)[None, :] * CMP_STRIDE
    ss = np.arange(nblk)[:, None] * SEL_BLOCK
    ov = np.clip(np.minimum(cs + CMP_BLOCK, ss + SEL_BLOCK) - np.maximum(cs, ss), 0, None)
    return jnp.asarray(ov.astype(np.float32) / CMP_BLOCK)


def kernel(x, norm1_w, w_in, conv_w, conv_b, rg_wa, rg_ba, rg_wx, rg_bx, rg_lambda,
           cmp_k_pe, cmp_k_w1, cmp_k_w2, cmp_v_pe, cmp_v_w1, cmp_v_w2, w_out, normf_w):
    B, S, D = x.shape
    assert D == D_MODEL and w_in.shape[-1] == _D_IN
    assert S % TM == 0 and TS == TM and B % CB == 0
    assert TQ % VC == 0 and WINDOW % TQ == 0 and KS % TQ == 0 and TM % TQ == 0
    assert S % KS == 0 and S >= WINDOW + TQ and (B * S) % TO == 0
    depth = w_in.shape[0]
    T = B * S
    ovt = _overlap_t(S // CMP_STRIDE, S // SEL_BLOCK)
    x2 = x.reshape(T, D)
    for l in range(depth):
        wn = jnp.pad(w_in[l].astype(BF16), ((0, 0), (0, _O_BR + LANE - _D_IN)))
        (rnn_out, kvc, ksel, kwin, ag, qt, vselt, vwint, brgt) = _inproj(
            x2, norm1_w[l].reshape(1, D), wn,
            conv_w[l], conv_b[l].reshape(1, D_RNN), _block_diag_halves(rg_wa[l], rg_wx[l]),
            rg_ba[l].reshape(1, D_RNN), rg_bx[l].reshape(1, D_RNN),
            rg_lambda[l].reshape(1, D_RNN), B, S)

        pk, w2k = _compress_weights(cmp_k_pe[l], cmp_k_w2[l])
        pv, w2v = _compress_weights(cmp_v_pe[l], cmp_v_w2[l])
        cmp_kv = _compress(kvc, jnp.stack([pk, pv]), cmp_k_w1[l], cmp_v_w1[l],
                           jnp.stack([w2k, w2v]), B, S)

        attn_out = _nsa(qt, cmp_kv, ksel, vselt, kwin, vwint, brgt, ag, ovt, B, S)

        x2 = _outproj(x2, rnn_out, attn_out, w_out[l],
                      normf_w.reshape(1, D), final_norm=(l == depth - 1))
    return x2.reshape(B, S, D)
```

```python
import functools

import numpy as np
import jax
import jax.numpy as jnp
from jax import lax
from jax.experimental import pallas as pl
from jax.experimental.pallas import tpu as pltpu

F32 = jnp.float32
BF16 = jnp.bfloat16

D_MODEL = 1024
EPS = 1e-6
D_RNN = 512
RNN_HEADS = 8
RNN_HEAD_DIM = D_RNN // RNN_HEADS
CONV_WIDTH = 4
LRU_C = 8.0
N_Q_HEADS = 8
N_KV_HEADS = 2
HEAD_DIM = 64
Q_PER_KV = N_Q_HEADS // N_KV_HEADS
D_ATTN = N_Q_HEADS * HEAD_DIM
D_KV = N_KV_HEADS * HEAD_DIM
CMP_BLOCK = 32
CMP_STRIDE = 16
CMP_HIDDEN = 256
SEL_BLOCK = 64
SEL_TOPK = 8
SEL_FORCE = 1e9
WINDOW = 512
N_BRANCH = 3
D_MIX = D_RNN + D_ATTN
N_GATE_ROWS = 32

LANE = 128
SCAN_CHUNKS = 8

TM = 1024
TS = 1024
TO = 1024
TQ = 256
KS = 512
VC = 128
TW = 128
CB = 4
NEG = -1e30
LOG2E = 1.4426950408889634
ONES_ROWS = 16
VMEM_V7X = 64 * 1024 * 1024
VMEM_LIMIT = VMEM_V7X - 8 * 1024 * 1024

_O_RX, _O_RG, _O_Q = 0, D_RNN, 2 * D_RNN
_O_KC = _O_Q + D_ATTN
_O_VC = _O_KC + D_KV
_O_KS = _O_VC + D_KV
_O_VS = _O_KS + D_KV
_O_KW = _O_VS + D_KV
_O_VW = _O_KW + D_KV
_O_AG = _O_VW + D_KV
_O_BR = _O_AG + D_ATTN
_D_IN = _O_BR + N_BRANCH * N_Q_HEADS


def _silu(x):
    return x * jax.nn.sigmoid(x)


def _inproj_kernel(x_ref, nw_ref, wn_ref, cw_ref, cb_ref, wg_ref, ba_ref, bx_ref, lam_ref,
                   rnn_ref, kvc_ref, ksel_ref, kwin_ref, ag_ref,
                   qt_ref, vselt_ref, vwint_ref, brgt_ref,
                   kv_scr, rx_scr, rg_scr, tail_ref, a_s, u_s, h_s, p_s, hlast, o_scr):
    @pl.when(pl.program_id(1) == 0)
    def _():
        tail_ref[...] = jnp.zeros(tail_ref.shape, F32)
        hlast[...] = jnp.zeros(hlast.shape, F32)

    x = x_ref[...]
    ms = jnp.mean(x * x, axis=-1, keepdims=True)
    h = ((x * lax.rsqrt(ms + EPS)) * nw_ref[...]).astype(BF16)

    def nat(a, b):
        return jnp.dot(h, wn_ref[:, a:b], preferred_element_type=F32)

    def store_step_major(dst_ref, val):
        steps = TM // SCAN_CHUNKS
        for l in range(D_RNN // LANE):
            for c in range(SCAN_CHUNKS):
                dst_ref[l, pl.ds(c, steps, stride=SCAN_CHUNKS), :] = (
                    val[c * steps:(c + 1) * steps, l * LANE:(l + 1) * LANE])

    store_step_major(rx_scr, nat(_O_RX, _O_RG))
    store_step_major(rg_scr, nat(_O_RG, _O_Q))

    def proj_cmp():
        kv = nat(_O_KC, _O_KS)
        for a in range(2):
            kv_scr[a] = kv[:, a * D_KV:(a + 1) * D_KV]
            for l in range(CMP_STRIDE):
                kvc_ref[a, :, l * D_KV:(l + 1) * D_KV] = (
                    kv_scr[a, pl.ds(l, TM // CMP_STRIDE, stride=CMP_STRIDE), :])

    def proj_kv(lo, k_ref, vt_ref):
        kv = nat(lo, lo + 2 * D_KV)
        k_ref[...] = kv[:, :D_KV].astype(BF16)
        vt = kv[:, D_KV:].T.astype(BF16)
        for c in range(TM // VC):
            vt_ref[c] = vt[:, c * VC:(c + 1) * VC]

    def proj_gates():
        ag_ref[...] = nat(_O_AG, _O_BR)
        brgt_ref[...] = nat(_O_BR, _O_BR + LANE).T[0:N_GATE_ROWS, :]

    def proj_q():
        qt_ref[...] = (nat(_O_Q, _O_KC) * (HEAD_DIM ** -0.5 * LOG2E)).T.astype(BF16)

    projections = ((proj_gates,), (proj_q,),
                   (functools.partial(proj_kv, _O_KS, ksel_ref, vselt_ref),
                    functools.partial(proj_kv, _O_KW, kwin_ref, vwint_ref)),
                   (proj_cmp,))
    wraps = _rglru_wraps(rx_scr, tail_ref)
    rows = TM // len(projections)
    for ci, projs in enumerate(projections):
        _rglru_gates(rx_scr, wraps, ci * rows, (ci + 1) * rows,
                     cw_ref, cb_ref, wg_ref, ba_ref, bx_ref, lam_ref, a_s, u_s)
        for proj in projs:
            proj()

    _rglru_scan(rg_scr, rnn_ref, a_s, u_s, h_s, p_s, hlast, o_scr)


def _inproj(x2, nw, wn, cw, cb, wg, ba, bx, lam, B, S):
    T = B * S
    ns = S // TM
    grid = (B, ns)
    row = lambda b, s: (b * ns + s, 0)
    const2 = lambda b, s: (0, 0)
    out_shape = (
        jax.ShapeDtypeStruct((T, D_RNN), BF16),
        jax.ShapeDtypeStruct((2, T // CMP_STRIDE, CMP_STRIDE * D_KV), F32),
        jax.ShapeDtypeStruct((T, D_KV), BF16),
        jax.ShapeDtypeStruct((T, D_KV), BF16),
        jax.ShapeDtypeStruct((T, D_ATTN), F32),
        jax.ShapeDtypeStruct((B, ns, D_ATTN, TM), BF16),
        jax.ShapeDtypeStruct((B, S // VC, D_KV, VC), BF16),
        jax.ShapeDtypeStruct((B, S // VC, D_KV, VC), BF16),
        jax.ShapeDtypeStruct((B, ns, N_GATE_ROWS, TM), F32),
    )
    out_specs = (
        pl.BlockSpec((TM, D_RNN), row),
        pl.BlockSpec((2, TM // CMP_STRIDE, CMP_STRIDE * D_KV), lambda b, s: (0, b * ns + s, 0)),
        pl.BlockSpec((TM, D_KV), row),
        pl.BlockSpec((TM, D_KV), row),
        pl.BlockSpec((TM, D_ATTN), row),
        pl.BlockSpec((None, None, D_ATTN, TM), lambda b, s: (b, s, 0, 0)),
        pl.BlockSpec((None, TM // VC, D_KV, VC), lambda b, s: (b, s, 0, 0)),
        pl.BlockSpec((None, TM // VC, D_KV, VC), lambda b, s: (b, s, 0, 0)),
        pl.BlockSpec((None, None, N_GATE_ROWS, TM), lambda b, s: (b, s, 0, 0)),
    )
    return pl.pallas_call(
        _inproj_kernel,
        grid=grid,
        in_specs=[
            pl.BlockSpec((TM, D_MODEL), row),
            pl.BlockSpec((1, D_MODEL), const2),
            pl.BlockSpec(wn.shape, const2),
            pl.BlockSpec((CONV_WIDTH, D_RNN), const2),
            pl.BlockSpec((1, D_RNN), const2),
            pl.BlockSpec(wg.shape, lambda b, s: (0, 0, 0)),
            pl.BlockSpec((1, D_RNN), const2),
            pl.BlockSpec((1, D_RNN), const2),
            pl.BlockSpec((1, D_RNN), const2),
        ],
        out_specs=out_specs,
        out_shape=out_shape,
        scratch_shapes=[
            pltpu.VMEM((2, TM, D_KV), F32),
            pltpu.VMEM((D_RNN // LANE, TM, LANE), F32),
            pltpu.VMEM((D_RNN // LANE, TM, LANE), F32),
            pltpu.VMEM(((CONV_WIDTH - 1) * SCAN_CHUNKS, D_RNN), F32),
            pltpu.VMEM((TM, D_RNN), F32),
            pltpu.VMEM((TM, D_RNN), F32),
            pltpu.VMEM((TM, D_RNN), F32),
            pltpu.VMEM((TM, D_RNN), F32),
            pltpu.VMEM((1, D_RNN), F32),
            pltpu.VMEM((D_RNN // LANE, TM, LANE), F32),
        ],
        compiler_params=pltpu.CompilerParams(
            dimension_semantics=("arbitrary", "arbitrary"), vmem_limit_bytes=VMEM_LIMIT),
        name="inproj",
    )(x2, nw, wn, cw, cb, wg, ba, bx, lam)


def _slab_rows(x_ref, lo, hi):
    return jnp.concatenate([x_ref[l, lo:hi, :] for l in range(D_RNN // LANE)], axis=1)


def _rglru_wraps(x_ref, tail_ref):
    ntail = (CONV_WIDTH - 1) * SCAN_CHUNKS
    last = _slab_rows(x_ref, TS - ntail, TS)
    sub = lax.broadcasted_iota(jnp.int32, (SCAN_CHUNKS, D_RNN), 0)
    wraps = []
    for j in range(CONV_WIDTH - 1):
        rows = slice(j * SCAN_CHUNKS, (j + 1) * SCAN_CHUNKS)
        cur = pltpu.roll(last[rows, :], 1, 0)
        prev = pltpu.roll(tail_ref[rows, :], 1, 0)
        wraps.append(jnp.where(sub == 0, prev, cur))
    tail_ref[...] = last
    return wraps


def _rglru_gates(x_ref, wraps, lo, hi, cw_ref, cb_ref, wg_ref, ba_ref, bx_ref, lam_ref, a_s, u_s):
    y = cb_ref[...]
    for k in range(CONV_WIDTH):
        off = (CONV_WIDTH - 1 - k) * SCAN_CHUNKS
        if lo >= off:
            xs = _slab_rows(x_ref, lo - off, hi - off)
        else:
            assert lo == 0
            xs = jnp.concatenate(wraps[len(wraps) - off // SCAN_CHUNKS:]
                                 + [_slab_rows(x_ref, 0, hi - off)], axis=0)
        y = y + xs * cw_ref[k:k + 1, :]

    yb = y.astype(BF16)
    half = D_RNN // 2
    pre = [jnp.dot(yb[:, hh * half:(hh + 1) * half], wg_ref[hh], preferred_element_type=F32)
           for hh in range(2)]
    pre_a = jnp.concatenate([pre[0][:, :half], pre[1][:, :half]], axis=1)
    pre_x = jnp.concatenate([pre[0][:, half:], pre[1][:, half:]], axis=1)
    r = jax.nn.sigmoid(pre_a + ba_ref[...])
    i = jax.nn.sigmoid(pre_x + bx_ref[...])
    lam = lam_ref[...]
    lsig = jnp.minimum(lam, 0.0) - jnp.log1p(jnp.exp(-jnp.abs(lam)))
    log_a = LRU_C * r * lsig
    a = jnp.exp(log_a)
    u = jnp.sqrt(1.0 - a * a) * (i * y)
    a_s[lo:hi, :] = a
    u_s[lo:hi, :] = u


def _rglru_scan(gate_ref, out_ref, a_s, u_s, h_s, p_s, hlast, o_scr):
    nstep = TS // SCAN_CHUNKS
    nslab = D_RNN // LANE
    gate = jnp.concatenate([gate_ref[l] for l in range(nslab)], axis=1)

    def body(j, carry):
        h, p = carry
        blk = pl.multiple_of(j * SCAN_CHUNKS, SCAN_CHUNKS)
        a_j = a_s[pl.ds(blk, SCAN_CHUNKS), :]
        h = a_j * h + u_s[pl.ds(blk, SCAN_CHUNKS), :]
        p = a_j * p
        h_s[pl.ds(blk, SCAN_CHUNKS), :] = h
        p_s[pl.ds(blk, SCAN_CHUNKS), :] = p
        return h, p

    h_end, p_end = lax.fori_loop(
        0, nstep, body,
        (jnp.zeros((SCAN_CHUNKS, D_RNN), F32), jnp.ones((SCAN_CHUNKS, D_RNN), F32)), unroll=8)
    carry = hlast[...]
    h_in = []
    for c in range(SCAN_CHUNKS):
        h_in.append(carry)
        carry = h_end[c:c + 1, :] + p_end[c:c + 1, :] * carry
    hlast[...] = carry
    h_in = jnp.concatenate([jnp.concatenate(h_in, axis=0)] * nstep, axis=0)
    o = (h_s[...] + p_s[...] * h_in) * _silu(gate)

    for l in range(nslab):
        o_scr[l] = o[:, l * LANE:(l + 1) * LANE]
        for c in range(SCAN_CHUNKS):
            out_ref[c * nstep:(c + 1) * nstep, l * LANE:(l + 1) * LANE] = (
                o_scr[l, pl.ds(c, nstep, stride=SCAN_CHUNKS), :].astype(out_ref.dtype))


def _compress_kernel(x_ref, pe_ref, w1k_ref, w1v_ref, w2_ref, out_ref, wbig_ref):
    nrow = x_ref.shape[1]
    nchunk = nrow // CB
    nh = N_KV_HEADS * CMP_HIDDEN

    for which, w1_ref in enumerate((w1k_ref, w1v_ref)):
        @pl.when((pl.program_id(0) == which) & (pl.program_id(1) == 0))
        def _(w1_ref=w1_ref):
            wbig_ref[...] = jnp.zeros(wbig_ref.shape, BF16)
            for half in range(2):
                for l in range(CMP_STRIDE):
                    r0 = (half * CMP_STRIDE + l) * HEAD_DIM
                    blk = w1_ref[r0:r0 + HEAD_DIM, :].astype(BF16)
                    for g in range(N_KV_HEADS):
                        rows = slice(l * D_KV + g * HEAD_DIM, l * D_KV + (g + 1) * HEAD_DIM)
                        cols = slice(half * nh + g * CMP_HIDDEN, half * nh + (g + 1) * CMP_HIDDEN)
                        wbig_ref[rows, cols] = blk

    x = x_ref[0]
    xa = (x + pe_ref[0, 0:1, :]).astype(BF16)
    xb = (x + pe_ref[0, 1:2, :]).astype(BF16)
    first = jnp.dot(xa, wbig_ref[:, :nh], preferred_element_type=F32)
    second = jnp.dot(xb, wbig_ref[:, nh:], preferred_element_type=F32)
    hid = _silu(first + pltpu.roll(second, nrow - 1, 0))
    out = jnp.dot(hid.astype(BF16), w2_ref[0], preferred_element_type=F32)
    c_idx = lax.broadcasted_iota(jnp.int32, out.shape, 0) & (nchunk - 1)
    out_ref[0] = jnp.where(c_idx < nchunk - 1, out, 0.0)


def _compress(xc, pe2, w1k, w1v, w2bd, B, S):
    nchunk = S // CMP_STRIDE
    nrow = CB * nchunk
    width = CMP_STRIDE * D_KV
    kv = lambda i, j: (i, 0, 0)
    return pl.pallas_call(
        _compress_kernel,
        grid=(2, B // CB),
        in_specs=[
            pl.BlockSpec((1, nrow, width), lambda i, j: (i, j, 0)),
            pl.BlockSpec((1, 2, width), kv),
            pl.BlockSpec(w1k.shape, lambda i, j: (0, 0)),
            pl.BlockSpec(w1v.shape, lambda i, j: (0, 0)),
            pl.BlockSpec((1,) + w2bd.shape[1:], kv),
        ],
        out_specs=pl.BlockSpec((1, nrow, D_KV), lambda i, j: (i, j, 0)),
        out_shape=jax.ShapeDtypeStruct((2, B * nchunk, D_KV), F32),
        scratch_shapes=[pltpu.VMEM((width, 2 * N_KV_HEADS * CMP_HIDDEN), BF16)],
        compiler_params=pltpu.CompilerParams(
            dimension_semantics=("arbitrary", "arbitrary"), vmem_limit_bytes=VMEM_LIMIT),
        name="compress",
    )(xc, pe2, w1k, w1v, w2bd)


def _nsa_kernel(qt_ref, cmp_ref, ksel_ref, vselt_ref, kwin_ref, vwint_ref, brgt_ref, ag_ref,
                ovt_ref, out_ref, qpad_ref, pen_ref, o_ref, m_ref, acc_ref, s_ref):
    qi = pl.program_id(1)
    q0 = qi * TQ
    nl = Q_PER_KV * TQ
    nblk = pen_ref.shape[1]
    ncmp = cmp_ref.shape[1]
    groups = range(N_KV_HEADS)
    gsl = [slice(g * HEAD_DIM, (g + 1) * HEAD_DIM) for g in groups]

    def tile4(a):
        return jnp.concatenate([a] * Q_PER_KV, axis=1)

    kc = cmp_ref[0].astype(BF16)
    s_cmp = []
    for g in groups:
        zpad = jnp.zeros((HEAD_DIM, TQ), BF16)
        cols = []
        for r in range(Q_PER_KV):
            hq = g * Q_PER_KV + r
            qh = qt_ref[hq * HEAD_DIM:(hq + 1) * HEAD_DIM, :]
            cols.append(jnp.concatenate([qh, zpad] if g == 0 else [zpad, qh], axis=0))
        qpad = jnp.concatenate(cols, axis=1)
        qpad_ref[g] = qpad
        s_cmp.append(jnp.dot(kc, qpad, preferred_element_type=F32))

    def sel_scores(kt, slot):
        k_tile = ksel_ref[pl.ds(pl.multiple_of(kt * KS, KS), KS), :]
        for g in groups:
            s_ref[slot, g] = jnp.dot(k_tile, qpad_ref[g], preferred_element_type=F32)

    n_full = q0 // KS

    ones = jnp.ones((ONES_ROWS, VC), BF16)

    def vt_aug(v_ref, c0, n, g):
        vt = jnp.concatenate([v_ref[c0 + c, gsl[g], :] for c in range(n)], axis=1)
        return jnp.concatenate([vt, jnp.concatenate([ones] * n, axis=1)], axis=0)

    def normalise(res):
        return res[0:HEAD_DIM, :] * (1.0 / res[HEAD_DIM:HEAD_DIM + 1, :])

    c_row = lax.broadcasted_iota(jnp.int32, (ncmp, TQ), 0)
    t_lane = q0 + lax.broadcasted_iota(jnp.int32, (ncmp, TQ), 1)
    cpen = tile4(jnp.where(c_row * CMP_STRIDE + (CMP_BLOCK - 1) <= t_lane, 0.0, NEG))
    vc_t = cmp_ref[1].T
    for g in groups:
        vct = vc_t[gsl[g], :].astype(BF16)
        s = s_cmp[g] + cpen
        m = jnp.max(s, axis=0, keepdims=True)
        m = jnp.where(m > 0.5 * NEG, m, 0.0)
        e = jnp.exp2(s - m)
        den = jnp.sum(e, axis=0, keepdims=True)
        p = e * (1.0 / jnp.where(den > 0.0, den, 1.0))
        o_ref[0, g] = jnp.dot(vct, p.astype(BF16), preferred_element_type=F32)

        psum = p[:, 0:TQ]
        for r in range(1, Q_PER_KV):
            psum = psum + p[:, r * TQ:(r + 1) * TQ]
        imp = jnp.dot(ovt_ref[...], psum, preferred_element_type=F32,
                      precision=lax.Precision.HIGHEST)
        j_row = lax.broadcasted_iota(jnp.int32, (nblk, TQ), 0)
        t_blk = q0 + lax.broadcasted_iota(jnp.int32, (nblk, TQ), 1)
        cur = jnp.right_shift(t_blk, SEL_BLOCK.bit_length() - 1)
        forced = (j_row == 0) | (j_row == cur) | (j_row == cur - 1)
        val = jnp.where(forced, SEL_FORCE, imp)
        val = jnp.where(j_row * SEL_BLOCK <= t_blk, val, -SEL_FORCE)
        rank = jnp.zeros((nblk, TQ), F32)
        for i in range(nblk):
            vi = jnp.broadcast_to(val[i:i + 1, :], (nblk, TQ))
            rank = rank + jnp.where(j_row > i, jnp.where(vi >= val, 1.0, 0.0),
                                    jnp.where(vi > val, 1.0, 0.0))
        chosen = (rank < float(min(SEL_TOPK, nblk))) & (val > -0.5 * SEL_FORCE)
        selpen = jnp.where(chosen, 0.0, NEG)
        for j in range(nblk):
            pen_ref[g, j] = jnp.broadcast_to(selpen[j:j + 1, :], (8, TQ))

    wrows = WINDOW + TW
    for sub in range(TQ // TW):
        qs = q0 + sub * TW
        w0 = pl.multiple_of(jnp.maximum(qs - WINDOW, 0), TW)
        k_win = kwin_ref[pl.ds(w0, wrows), :]
        key = w0 + lax.broadcasted_iota(jnp.int32, (wrows, TW), 0)
        t_q = qs + lax.broadcasted_iota(jnp.int32, (wrows, TW), 1)
        wpen = tile4(jnp.where(key <= t_q, jnp.where(key > t_q - WINDOW, 0.0, NEG), NEG))
        cols = [slice(r * TQ + sub * TW, r * TQ + (sub + 1) * TW) for r in range(Q_PER_KV)]
        p_win = []
        for g in groups:
            q_sub = jnp.concatenate([qpad_ref[g, :, c] for c in cols], axis=1)
            s = jnp.dot(k_win, q_sub, preferred_element_type=F32) + wpen
            m = jnp.max(s, axis=0, keepdims=True)
            p_win.append(jnp.exp2(s - m).astype(BF16))
        if sub == 0:
            sel_scores(0, n_full & 1)
        for g in groups:
            o_win = normalise(jnp.dot(vt_aug(vwint_ref, w0 // VC, wrows // VC, g), p_win[g],
                                      preferred_element_type=F32))
            for r, c in enumerate(cols):
                o_ref[2, g, :, c] = o_win[:, r * TW:(r + 1) * TW]

    m_ref[...] = jnp.full(m_ref.shape, NEG, F32)
    acc_ref[...] = jnp.zeros(acc_ref.shape, F32)
    bpt = KS // SEL_BLOCK
    cpt = KS // VC

    def sel_update(kt, slot, diag_keys=0):
        nkeys = diag_keys or KS
        probs, alphas = [], []
        for g in groups:
            m_prev = m_ref[g]
            if diag_keys:
                rows = [jnp.concatenate([pen_ref[g, kt * bpt + jj]] * (SEL_BLOCK // 8), axis=0)
                        for jj in range(nkeys // SEL_BLOCK)]
                pen = jnp.concatenate(rows, axis=0)
                key = kt * KS + lax.broadcasted_iota(jnp.int32, (nkeys, TQ), 0)
                t_q = q0 + lax.broadcasted_iota(jnp.int32, (nkeys, TQ), 1)
                s = s_ref[slot, g, 0:nkeys, :] + tile4(pen + jnp.where(key <= t_q, 0.0, NEG))
                m_new = jnp.maximum(m_prev, jnp.max(s, axis=0, keepdims=True))
                p = jnp.exp2(s - m_new)
            else:
                sb = [s_ref[slot, g, jj * SEL_BLOCK:(jj + 1) * SEL_BLOCK, :] for jj in range(bpt)]
                bias = [tile4(pen_ref[g, kt * bpt + jj][0:1, :]) for jj in range(bpt)]
                m_new = m_prev
                for jj in range(bpt):
                    m_new = jnp.maximum(m_new, jnp.max(sb[jj], axis=0, keepdims=True) + bias[jj])
                p = jnp.concatenate([jnp.exp2(sb[jj] + (bias[jj] - m_new)) for jj in range(bpt)],
                                    axis=0)
            probs.append(p.astype(BF16))
            alphas.append(jnp.exp2(m_prev - m_new))
            m_ref[g] = m_new
        for g in groups:
            acc_ref[g] = alphas[g] * acc_ref[g] + jnp.dot(
                vt_aug(vselt_ref, kt * cpt, nkeys // VC, g), probs[g], preferred_element_type=F32)

    def sel_body(kt, carry):
        for slot in (0, 1):
            @pl.when(((n_full - kt) & 1) == slot)
            def _(slot=slot):
                sel_scores(kt + 1, 1 - slot)
                sel_update(kt, slot)
        return carry

    lax.fori_loop(0, n_full, sel_body, 0)
    for nkeys in range(TQ, KS + 1, TQ):
        @pl.when(q0 + TQ - n_full * KS == nkeys)
        def _(nkeys=nkeys):
            sel_update(n_full, 0, diag_keys=nkeys)
    for g in groups:
        o_ref[1, g] = normalise(acc_ref[g])

    gates = jax.nn.sigmoid(brgt_ref[...])
    group_out = []
    for g in groups:
        heads = []
        for r in range(Q_PER_KV):
            hq = g * Q_PER_KV + r
            ls = slice(r * TQ, (r + 1) * TQ)
            o = gates[hq:hq + 1, :] * o_ref[0, g, :, ls]
            for n in range(1, N_BRANCH):
                o = o + gates[n * N_Q_HEADS + hq:n * N_Q_HEADS + hq + 1, :] * o_ref[n, g, :, ls]
            heads.append(o)
        group_out.append(jnp.concatenate(heads, axis=0).T)

    attn = jnp.concatenate(group_out, axis=1)
    out_ref[...] = (attn * _silu(ag_ref[...])).astype(out_ref.dtype)


def _nsa(qt, cmp_kv, ksel, vselt, kwin, vwint, brgt, ag, ovt, B, S):
    T = B * S
    nq = S // TQ
    qpm = TM // TQ
    nblk = S // SEL_BLOCK
    ncmp = S // CMP_STRIDE
    nl = Q_PER_KV * TQ
    return pl.pallas_call(
        _nsa_kernel,
        grid=(B, nq),
        in_specs=[
            pl.BlockSpec((None, None, D_ATTN, TQ), lambda b, q: (b, q // qpm, 0, q % qpm)),
            pl.BlockSpec((2, ncmp, D_KV), lambda b, q: (0, b, 0)),
            pl.BlockSpec((S, D_KV), lambda b, q: (b, 0)),
            pl.BlockSpec((None, S // VC, D_KV, VC), lambda b, q: (b, 0, 0, 0)),
            pl.BlockSpec((S, D_KV), lambda b, q: (b, 0)),
            pl.BlockSpec((None, S // VC, D_KV, VC), lambda b, q: (b, 0, 0, 0)),
            pl.BlockSpec((None, None, N_GATE_ROWS, TQ), lambda b, q: (b, q // qpm, 0, q % qpm)),
            pl.BlockSpec((TQ, D_ATTN), lambda b, q: (b * nq + q, 0)),
            pl.BlockSpec(ovt.shape, lambda b, q: (0, 0)),
        ],
        out_specs=pl.BlockSpec((TQ, D_ATTN), lambda b, q: (b * nq + q, 0)),
        out_shape=jax.ShapeDtypeStruct((T, D_ATTN), BF16),
        scratch_shapes=[
            pltpu.VMEM((N_KV_HEADS, D_KV, nl), BF16),
            pltpu.VMEM((N_KV_HEADS, nblk, 8, TQ), F32),
            pltpu.VMEM((N_BRANCH, N_KV_HEADS, HEAD_DIM, nl), F32),
            pltpu.VMEM((N_KV_HEADS, 1, nl), F32),
            pltpu.VMEM((N_KV_HEADS, HEAD_DIM + ONES_ROWS, nl), F32),
            pltpu.VMEM((2, N_KV_HEADS, KS, nl), F32),
        ],
        compiler_params=pltpu.CompilerParams(
            dimension_semantics=("arbitrary", "arbitrary"), vmem_limit_bytes=VMEM_LIMIT),
        name="nsa",
    )(qt, cmp_kv, ksel, vselt, kwin, vwint, brgt, ag, ovt)


def _outproj_kernel(x_ref, rnn_ref, attn_ref, wo_ref, nfw_ref, out_ref, wo_bf, *, final_norm):
    @pl.when(pl.program_id(0) == 0)
    def _():
        wo_bf[...] = wo_ref[...].astype(BF16)

    y = x_ref[...]
    y = y + jnp.dot(rnn_ref[...], wo_bf[0:D_RNN, :], preferred_element_type=F32)
    y = y + jnp.dot(attn_ref[...], wo_bf[D_RNN:D_MIX, :], preferred_element_type=F32)
    if final_norm:
        ms = jnp.mean(y * y, axis=-1, keepdims=True)
        y = (y * lax.rsqrt(ms + EPS)) * nfw_ref[...]
    out_ref[...] = y


def _outproj(x2, rnn_out, attn_out, wo, nfw, final_norm):
    T = x2.shape[0]
    row = lambda i: (i, 0)
    return pl.pallas_call(
        functools.partial(_outproj_kernel, final_norm=final_norm),
        grid=(T // TO,),
        in_specs=[
            pl.BlockSpec((TO, D_MODEL), row),
            pl.BlockSpec((TO, D_RNN), row),
            pl.BlockSpec((TO, D_ATTN), row),
            pl.BlockSpec((D_MIX, D_MODEL), lambda i: (0, 0)),
            pl.BlockSpec((1, D_MODEL), lambda i: (0, 0)),
        ],
        out_specs=pl.BlockSpec((TO, D_MODEL), row),
        out_shape=jax.ShapeDtypeStruct((T, D_MODEL), F32),
        scratch_shapes=[pltpu.VMEM((D_MIX, D_MODEL), BF16)],
        compiler_params=pltpu.CompilerParams(
            dimension_semantics=("arbitrary",), vmem_limit_bytes=VMEM_LIMIT),
        name="outproj",
    )(x2, rnn_out, attn_out, wo, nfw)


def _block_diag_halves(wa, wx):
    eye = jnp.eye(RNN_HEADS, dtype=wa.dtype)
    full = lambda w: jnp.einsum('hij,hk->hikj', w, eye).reshape(D_RNN, D_RNN)
    fa, fx = full(wa), full(wx)
    half = D_RNN // 2
    return jnp.stack([
        jnp.concatenate([fa[s:s + half, s:s + half], fx[s:s + half, s:s + half]], axis=1)
        for s in (0, half)]).astype(BF16)


def _compress_weights(pe, w2):
    eye = jnp.eye(N_KV_HEADS, dtype=w2.dtype)
    w2bd = jnp.einsum('nd,ge->gned', w2, eye).reshape(N_KV_HEADS * CMP_HIDDEN, D_KV)
    per = pe.reshape(2, CMP_STRIDE, 1, HEAD_DIM)
    pe2 = jnp.broadcast_to(per, (2, CMP_STRIDE, N_KV_HEADS, HEAD_DIM)).reshape(2, CMP_STRIDE * D_KV)
    return pe2, w2bd.astype(BF16)


def _overlap_t(ncmp_pad, nblk):
    cs = np.arange(ncmp_pad)[None, :] * CMP_STRIDE
    ss = np.arange(nblk)[:, None] * SEL_BLOCK
    ov = np.clip(np.minimum(cs + CMP_BLOCK, ss + SEL_BLOCK) - np.maximum(cs, ss), 0, None)
    return jnp.asarray(ov.astype(np.float32) / CMP_BLOCK)


def kernel(x, norm1_w, w_in, conv_w, conv_b, rg_wa, rg_ba, rg_wx, rg_bx, rg_lambda,
           cmp_k_pe, cmp_k_w1, cmp_k_w2, cmp_v_pe, cmp_v_w1, cmp_v_w2, w_out, normf_w):
    B, S, D = x.shape
    assert D == D_MODEL and w_in.shape[-1] == _D_IN
    assert S % TM == 0 and TS == TM and B % CB == 0
    assert TQ % TW == 0 and TW % VC == 0 and WINDOW % TW == 0 and KS % TQ == 0 and TM % TQ == 0
    assert S % KS == 0 and S >= WINDOW + TQ and (B * S) % TO == 0
    depth = w_in.shape[0]
    T = B * S
    ovt = _overlap_t(S // CMP_STRIDE, S // SEL_BLOCK)
    x2 = x.reshape(T, D)
    for l in range(depth):
        wn = jnp.pad(w_in[l].astype(BF16), ((0, 0), (0, _O_BR + LANE - _D_IN)))
        (rnn_out, kvc, ksel, kwin, ag, qt, vselt, vwint, brgt) = _inproj(
            x2, norm1_w[l].reshape(1, D), wn,
            conv_w[l], conv_b[l].reshape(1, D_RNN), _block_diag_halves(rg_wa[l], rg_wx[l]),
            rg_ba[l].reshape(1, D_RNN), rg_bx[l].reshape(1, D_RNN),
            rg_lambda[l].reshape(1, D_RNN), B, S)

        pk, w2k = _compress_weights(cmp_k_pe[l], cmp_k_w2[l])
        pv, w2v = _compress_weights(cmp_v_pe[l], cmp_v_w2[l])
        cmp_kv = _compress(kvc, jnp.stack([pk, pv]), cmp_k_w1[l], cmp_v_w1[l],
                           jnp.stack([w2k, w2v]), B, S)

        attn_out = _nsa(qt, cmp_kv, ksel, vselt, kwin, vwint, brgt, ag, ovt, B, S)

        x2 = _outproj(x2, rnn_out, attn_out, w_out[l],
                      normf_w.reshape(1, D), final_norm=(l == depth - 1))
    return x2.reshape(B, S, D)
```

```python
import functools

import numpy as np
import jax
import jax.numpy as jnp
from jax import lax
from jax.experimental import pallas as pl
from jax.experimental.pallas import tpu as pltpu

F32 = jnp.float32
BF16 = jnp.bfloat16

D_MODEL = 1024
EPS = 1e-6
D_RNN = 512
RNN_HEADS = 8
RNN_HEAD_DIM = D_RNN // RNN_HEADS
CONV_WIDTH = 4
LRU_C = 8.0
N_Q_HEADS = 8
N_KV_HEADS = 2
HEAD_DIM = 64
Q_PER_KV = N_Q_HEADS // N_KV_HEADS
D_ATTN = N_Q_HEADS * HEAD_DIM
D_KV = N_KV_HEADS * HEAD_DIM
CMP_BLOCK = 32
CMP_STRIDE = 16
CMP_HIDDEN = 256
SEL_BLOCK = 64
SEL_TOPK = 8
SEL_FORCE = 1e9
WINDOW = 512
N_BRANCH = 3
D_MIX = D_RNN + D_ATTN
N_GATE_ROWS = 32

LANE = 128
SCAN_CHUNKS = 8

TM = 1024
TS = 1024
TO = 1024
TQ = 512
KS = 512
VC = 128
TW = 128
CB = 4
NEG = -1e30
LOG2E = 1.4426950408889634
ONES_ROWS = 16
VMEM_V7X = 64 * 1024 * 1024
VMEM_LIMIT = VMEM_V7X - 8 * 1024 * 1024

_O_RX, _O_RG, _O_Q = 0, D_RNN, 2 * D_RNN
_O_KC = _O_Q + D_ATTN
_O_VC = _O_KC + D_KV
_O_KS = _O_VC + D_KV
_O_VS = _O_KS + D_KV
_O_KW = _O_VS + D_KV
_O_VW = _O_KW + D_KV
_O_AG = _O_VW + D_KV
_O_BR = _O_AG + D_ATTN
_D_IN = _O_BR + N_BRANCH * N_Q_HEADS


def _silu(x):
    return x * jax.nn.sigmoid(x)


def _inproj_kernel(x_ref, nw_ref, wn_ref, cw_ref, cb_ref, wg_ref, ba_ref, bx_ref, lam_ref,
                   rnn_ref, kvc_ref, ksel_ref, kwin_ref, ag_ref,
                   qt_ref, vselt_ref, vwint_ref, brgt_ref,
                   kv_scr, rx_scr, rg_scr, tail_ref, a_s, u_s, h_s, p_s, hlast, o_scr):
    @pl.when(pl.program_id(1) == 0)
    def _():
        tail_ref[...] = jnp.zeros(tail_ref.shape, F32)
        hlast[...] = jnp.zeros(hlast.shape, F32)

    x = x_ref[...]
    ms = jnp.mean(x * x, axis=-1, keepdims=True)
    h = ((x * lax.rsqrt(ms + EPS)) * nw_ref[...]).astype(BF16)

    def nat(a, b):
        return jnp.dot(h, wn_ref[:, a:b], preferred_element_type=F32)

    def store_step_major(dst_ref, val):
        steps = TM // SCAN_CHUNKS
        for l in range(D_RNN // LANE):
            for c in range(SCAN_CHUNKS):
                dst_ref[l, pl.ds(c, steps, stride=SCAN_CHUNKS), :] = (
                    val[c * steps:(c + 1) * steps, l * LANE:(l + 1) * LANE])

    store_step_major(rx_scr, nat(_O_RX, _O_RG))
    store_step_major(rg_scr, nat(_O_RG, _O_Q))

    def proj_cmp():
        kv = nat(_O_KC, _O_KS)
        for a in range(2):
            kv_scr[a] = kv[:, a * D_KV:(a + 1) * D_KV]
            for l in range(CMP_STRIDE):
                kvc_ref[a, :, l * D_KV:(l + 1) * D_KV] = (
                    kv_scr[a, pl.ds(l, TM // CMP_STRIDE, stride=CMP_STRIDE), :])

    def proj_kv(lo, k_ref, vt_ref):
        kv = nat(lo, lo + 2 * D_KV)
        k_ref[...] = kv[:, :D_KV].astype(BF16)
        vt = kv[:, D_KV:].T.astype(BF16)
        for c in range(TM // VC):
            vt_ref[c] = vt[:, c * VC:(c + 1) * VC]

    def proj_gates():
        ag_ref[...] = nat(_O_AG, _O_BR)
        brgt_ref[...] = nat(_O_BR, _O_BR + LANE).T[0:N_GATE_ROWS, :]

    def proj_q():
        qt_ref[...] = (nat(_O_Q, _O_KC) * (HEAD_DIM ** -0.5 * LOG2E)).T.astype(BF16)

    projections = ((proj_gates,), (proj_q,),
                   (functools.partial(proj_kv, _O_KS, ksel_ref, vselt_ref),
                    functools.partial(proj_kv, _O_KW, kwin_ref, vwint_ref)),
                   (proj_cmp,))
    wraps = _rglru_wraps(rx_scr, tail_ref)
    rows = TM // len(projections)
    for ci, projs in enumerate(projections):
        _rglru_gates(rx_scr, wraps, ci * rows, (ci + 1) * rows,
                     cw_ref, cb_ref, wg_ref, ba_ref, bx_ref, lam_ref, a_s, u_s)
        for proj in projs:
            proj()

    _rglru_scan(rg_scr, rnn_ref, a_s, u_s, h_s, p_s, hlast, o_scr)


def _inproj(x2, nw, wn, cw, cb, wg, ba, bx, lam, B, S):
    T = B * S
    ns = S // TM
    grid = (B, ns)
    row = lambda b, s: (b * ns + s, 0)
    const2 = lambda b, s: (0, 0)
    out_shape = (
        jax.ShapeDtypeStruct((T, D_RNN), BF16),
        jax.ShapeDtypeStruct((2, T // CMP_STRIDE, CMP_STRIDE * D_KV), F32),
        jax.ShapeDtypeStruct((T, D_KV), BF16),
        jax.ShapeDtypeStruct((T, D_KV), BF16),
        jax.ShapeDtypeStruct((T, D_ATTN), F32),
        jax.ShapeDtypeStruct((B, ns, D_ATTN, TM), BF16),
        jax.ShapeDtypeStruct((B, S // VC, D_KV, VC), BF16),
        jax.ShapeDtypeStruct((B, S // VC, D_KV, VC), BF16),
        jax.ShapeDtypeStruct((B, ns, N_GATE_ROWS, TM), F32),
    )
    out_specs = (
        pl.BlockSpec((TM, D_RNN), row),
        pl.BlockSpec((2, TM // CMP_STRIDE, CMP_STRIDE * D_KV), lambda b, s: (0, b * ns + s, 0)),
        pl.BlockSpec((TM, D_KV), row),
        pl.BlockSpec((TM, D_KV), row),
        pl.BlockSpec((TM, D_ATTN), row),
        pl.BlockSpec((None, None, D_ATTN, TM), lambda b, s: (b, s, 0, 0)),
        pl.BlockSpec((None, TM // VC, D_KV, VC), lambda b, s: (b, s, 0, 0)),
        pl.BlockSpec((None, TM // VC, D_KV, VC), lambda b, s: (b, s, 0, 0)),
        pl.BlockSpec((None, None, N_GATE_ROWS, TM), lambda b, s: (b, s, 0, 0)),
    )
    return pl.pallas_call(
        _inproj_kernel,
        grid=grid,
        in_specs=[
            pl.BlockSpec((TM, D_MODEL), row),
            pl.BlockSpec((1, D_MODEL), const2),
            pl.BlockSpec(wn.shape, const2),
            pl.BlockSpec((CONV_WIDTH, D_RNN), const2),
            pl.BlockSpec((1, D_RNN), const2),
            pl.BlockSpec(wg.shape, lambda b, s: (0, 0, 0)),
            pl.BlockSpec((1, D_RNN), const2),
            pl.BlockSpec((1, D_RNN), const2),
            pl.BlockSpec((1, D_RNN), const2),
        ],
        out_specs=out_specs,
        out_shape=out_shape,
        scratch_shapes=[
            pltpu.VMEM((2, TM, D_KV), F32),
            pltpu.VMEM((D_RNN // LANE, TM, LANE), F32),
            pltpu.VMEM((D_RNN // LANE, TM, LANE), F32),
            pltpu.VMEM(((CONV_WIDTH - 1) * SCAN_CHUNKS, D_RNN), F32),
            pltpu.VMEM((TM, D_RNN), F32),
            pltpu.VMEM((TM, D_RNN), F32),
            pltpu.VMEM((TM, D_RNN), F32),
            pltpu.VMEM((TM, D_RNN), F32),
            pltpu.VMEM((1, D_RNN), F32),
            pltpu.VMEM((D_RNN // LANE, TM, LANE), F32),
        ],
        compiler_params=pltpu.CompilerParams(
            dimension_semantics=("arbitrary", "arbitrary"), vmem_limit_bytes=VMEM_LIMIT),
        name="inproj",
    )(x2, nw, wn, cw, cb, wg, ba, bx, lam)


def _slab_rows(x_ref, lo, hi):
    return jnp.concatenate([x_ref[l, lo:hi, :] for l in range(D_RNN // LANE)], axis=1)


def _rglru_wraps(x_ref, tail_ref):
    ntail = (CONV_WIDTH - 1) * SCAN_CHUNKS
    last = _slab_rows(x_ref, TS - ntail, TS)
    sub = lax.broadcasted_iota(jnp.int32, (SCAN_CHUNKS, D_RNN), 0)
    wraps = []
    for j in range(CONV_WIDTH - 1):
        rows = slice(j * SCAN_CHUNKS, (j + 1) * SCAN_CHUNKS)
        cur = pltpu.roll(last[rows, :], 1, 0)
        prev = pltpu.roll(tail_ref[rows, :], 1, 0)
        wraps.append(jnp.where(sub == 0, prev, cur))
    tail_ref[...] = last
    return wraps


def _rglru_gates(x_ref, wraps, lo, hi, cw_ref, cb_ref, wg_ref, ba_ref, bx_ref, lam_ref, a_s, u_s):
    y = cb_ref[...]
    for k in range(CONV_WIDTH):
        off = (CONV_WIDTH - 1 - k) * SCAN_CHUNKS
        if lo >= off:
            xs = _slab_rows(x_ref, lo - off, hi - off)
        else:
            assert lo == 0
            xs = jnp.concatenate(wraps[len(wraps) - off // SCAN_CHUNKS:]
                                 + [_slab_rows(x_ref, 0, hi - off)], axis=0)
        y = y + xs * cw_ref[k:k + 1, :]

    yb = y.astype(BF16)
    half = D_RNN // 2
    pre = [jnp.dot(yb[:, hh * half:(hh + 1) * half], wg_ref[hh], preferred_element_type=F32)
           for hh in range(2)]
    pre_a = jnp.concatenate([pre[0][:, :half], pre[1][:, :half]], axis=1)
    pre_x = jnp.concatenate([pre[0][:, half:], pre[1][:, half:]], axis=1)
    r = jax.nn.sigmoid(pre_a + ba_ref[...])
    i = jax.nn.sigmoid(pre_x + bx_ref[...])
    lam = lam_ref[...]
    lsig = jnp.minimum(lam, 0.0) - jnp.log1p(jnp.exp(-jnp.abs(lam)))
    log_a = LRU_C * r * lsig
    a = jnp.exp(log_a)
    u = jnp.sqrt(1.0 - a * a) * (i * y)
    a_s[lo:hi, :] = a
    u_s[lo:hi, :] = u


def _rglru_scan(gate_ref, out_ref, a_s, u_s, h_s, p_s, hlast, o_scr):
    nstep = TS // SCAN_CHUNKS
    nslab = D_RNN // LANE
    gate = jnp.concatenate([gate_ref[l] for l in range(nslab)], axis=1)

    def body(j, carry):
        h, p = carry
        blk = pl.multiple_of(j * SCAN_CHUNKS, SCAN_CHUNKS)
        a_j = a_s[pl.ds(blk, SCAN_CHUNKS), :]
        h = a_j * h + u_s[pl.ds(blk, SCAN_CHUNKS), :]
        p = a_j * p
        h_s[pl.ds(blk, SCAN_CHUNKS), :] = h
        p_s[pl.ds(blk, SCAN_CHUNKS), :] = p
        return h, p

    h_end, p_end = lax.fori_loop(
        0, nstep, body,
        (jnp.zeros((SCAN_CHUNKS, D_RNN), F32), jnp.ones((SCAN_CHUNKS, D_RNN), F32)), unroll=8)
    carry = hlast[...]
    h_in = []
    for c in range(SCAN_CHUNKS):
        h_in.append(carry)
        carry = h_end[c:c + 1, :] + p_end[c:c + 1, :] * carry
    hlast[...] = carry
    h_in = jnp.concatenate([jnp.concatenate(h_in, axis=0)] * nstep, axis=0)
    o = (h_s[...] + p_s[...] * h_in) * _silu(gate)

    for l in range(nslab):
        o_scr[l] = o[:, l * LANE:(l + 1) * LANE]
        for c in range(SCAN_CHUNKS):
            out_ref[c * nstep:(c + 1) * nstep, l * LANE:(l + 1) * LANE] = (
                o_scr[l, pl.ds(c, nstep, stride=SCAN_CHUNKS), :].astype(out_ref.dtype))


def _compress_kernel(x_ref, pe_ref, w1k_ref, w1v_ref, w2_ref, out_ref, wbig_ref):
    nrow = x_ref.shape[1]
    nchunk = nrow // CB
    nh = N_KV_HEADS * CMP_HIDDEN

    for which, w1_ref in enumerate((w1k_ref, w1v_ref)):
        @pl.when((pl.program_id(0) == which) & (pl.program_id(1) == 0))
        def _(w1_ref=w1_ref):
            wbig_ref[...] = jnp.zeros(wbig_ref.shape, BF16)
            for half in range(2):
                for l in range(CMP_STRIDE):
                    r0 = (half * CMP_STRIDE + l) * HEAD_DIM
                    blk = w1_ref[r0:r0 + HEAD_DIM, :].astype(BF16)
                    for g in range(N_KV_HEADS):
                        rows = slice(l * D_KV + g * HEAD_DIM, l * D_KV + (g + 1) * HEAD_DIM)
                        cols = slice(half * nh + g * CMP_HIDDEN, half * nh + (g + 1) * CMP_HIDDEN)
                        wbig_ref[rows, cols] = blk

    x = x_ref[0]
    xa = (x + pe_ref[0, 0:1, :]).astype(BF16)
    xb = (x + pe_ref[0, 1:2, :]).astype(BF16)
    first = jnp.dot(xa, wbig_ref[:, :nh], preferred_element_type=F32)
    second = jnp.dot(xb, wbig_ref[:, nh:], preferred_element_type=F32)
    hid = _silu(first + pltpu.roll(second, nrow - 1, 0))
    out = jnp.dot(hid.astype(BF16), w2_ref[0], preferred_element_type=F32)
    c_idx = lax.broadcasted_iota(jnp.int32, out.shape, 0) & (nchunk - 1)
    out_ref[0] = jnp.where(c_idx < nchunk - 1, out, 0.0)


def _compress(xc, pe2, w1k, w1v, w2bd, B, S):
    nchunk = S // CMP_STRIDE
    nrow = CB * nchunk
    width = CMP_STRIDE * D_KV
    kv = lambda i, j: (i, 0, 0)
    return pl.pallas_call(
        _compress_kernel,
        grid=(2, B // CB),
        in_specs=[
            pl.BlockSpec((1, nrow, width), lambda i, j: (i, j, 0)),
            pl.BlockSpec((1, 2, width), kv),
            pl.BlockSpec(w1k.shape, lambda i, j: (0, 0)),
            pl.BlockSpec(w1v.shape, lambda i, j: (0, 0)),
            pl.BlockSpec((1,) + w2bd.shape[1:], kv),
        ],
        out_specs=pl.BlockSpec((1, nrow, D_KV), lambda i, j: (i, j, 0)),
        out_shape=jax.ShapeDtypeStruct((2, B * nchunk, D_KV), F32),
        scratch_shapes=[pltpu.VMEM((width, 2 * N_KV_HEADS * CMP_HIDDEN), BF16)],
        compiler_params=pltpu.CompilerParams(
            dimension_semantics=("arbitrary", "arbitrary"), vmem_limit_bytes=VMEM_LIMIT),
        name="compress",
    )(xc, pe2, w1k, w1v, w2bd)


def _nsa_kernel(qt_ref, cmp_ref, ksel_ref, vselt_ref, kwin_ref, vwint_ref, brgt_ref, ag_ref,
                ovt_ref, out_ref, qpad_ref, pen_ref, o_ref, m_ref, acc_ref, s_ref):
    qi = pl.program_id(1)
    q0 = qi * TQ
    nl = Q_PER_KV * TQ
    nblk = pen_ref.shape[1]
    ncmp = cmp_ref.shape[1]
    groups = range(N_KV_HEADS)
    gsl = [slice(g * HEAD_DIM, (g + 1) * HEAD_DIM) for g in groups]

    def tile4(a):
        return jnp.concatenate([a] * Q_PER_KV, axis=1)

    kc = cmp_ref[0].astype(BF16)
    s_cmp = []
    for g in groups:
        zpad = jnp.zeros((HEAD_DIM, TQ), BF16)
        cols = []
        for r in range(Q_PER_KV):
            hq = g * Q_PER_KV + r
            qh = qt_ref[hq * HEAD_DIM:(hq + 1) * HEAD_DIM, :]
            cols.append(jnp.concatenate([qh, zpad] if g == 0 else [zpad, qh], axis=0))
        qpad = jnp.concatenate(cols, axis=1)
        qpad_ref[g] = qpad
        s_cmp.append(jnp.dot(kc, qpad, preferred_element_type=F32))

    def sel_scores(kt, slot):
        k_tile = ksel_ref[pl.ds(pl.multiple_of(kt * KS, KS), KS), :]
        for g in groups:
            s_ref[slot, g] = jnp.dot(k_tile, qpad_ref[g], preferred_element_type=F32)

    n_full = q0 // KS

    ones = jnp.ones((ONES_ROWS, VC), BF16)

    def vt_aug(v_ref, c0, n, g):
        vt = jnp.concatenate([v_ref[c0 + c, gsl[g], :] for c in range(n)], axis=1)
        return jnp.concatenate([vt, jnp.concatenate([ones] * n, axis=1)], axis=0)

    def normalise(res):
        return res[0:HEAD_DIM, :] * (1.0 / res[HEAD_DIM:HEAD_DIM + 1, :])

    c_row = lax.broadcasted_iota(jnp.int32, (ncmp, TQ), 0)
    t_lane = q0 + lax.broadcasted_iota(jnp.int32, (ncmp, TQ), 1)
    cpen = tile4(jnp.where(c_row * CMP_STRIDE + (CMP_BLOCK - 1) <= t_lane, 0.0, NEG))
    vc_t = cmp_ref[1].T
    for g in groups:
        vct = vc_t[gsl[g], :].astype(BF16)
        s = s_cmp[g] + cpen
        m = jnp.max(s, axis=0, keepdims=True)
        m = jnp.where(m > 0.5 * NEG, m, 0.0)
        e = jnp.exp2(s - m)
        den = jnp.sum(e, axis=0, keepdims=True)
        p = e * (1.0 / jnp.where(den > 0.0, den, 1.0))
        o_ref[0, g] = jnp.dot(vct, p.astype(BF16), preferred_element_type=F32)

        psum = p[:, 0:TQ]
        for r in range(1, Q_PER_KV):
            psum = psum + p[:, r * TQ:(r + 1) * TQ]
        imp = jnp.dot(ovt_ref[...], psum, preferred_element_type=F32,
                      precision=lax.Precision.HIGHEST)
        j_row = lax.broadcasted_iota(jnp.int32, (nblk, TQ), 0)
        t_blk = q0 + lax.broadcasted_iota(jnp.int32, (nblk, TQ), 1)
        cur = jnp.right_shift(t_blk, SEL_BLOCK.bit_length() - 1)
        forced = (j_row == 0) | (j_row == cur) | (j_row == cur - 1)
        val = jnp.where(forced, SEL_FORCE, imp)
        val = jnp.where(j_row * SEL_BLOCK <= t_blk, val, -SEL_FORCE)
        rank = jnp.zeros((nblk, TQ), F32)
        for i in range(nblk):
            vi = jnp.broadcast_to(val[i:i + 1, :], (nblk, TQ))
            rank = rank + jnp.where(j_row > i, jnp.where(vi >= val, 1.0, 0.0),
                                    jnp.where(vi > val, 1.0, 0.0))
        chosen = (rank < float(min(SEL_TOPK, nblk))) & (val > -0.5 * SEL_FORCE)
        selpen = jnp.where(chosen, 0.0, NEG)
        for j in range(nblk):
            pen_ref[g, j] = jnp.broadcast_to(selpen[j:j + 1, :], (8, TQ))

    wrows = WINDOW + TW
    for sub in range(TQ // TW):
        qs = q0 + sub * TW
        w0 = pl.multiple_of(jnp.maximum(qs - WINDOW, 0), TW)
        k_win = kwin_ref[pl.ds(w0, wrows), :]
        key = w0 + lax.broadcasted_iota(jnp.int32, (wrows, TW), 0)
        t_q = qs + lax.broadcasted_iota(jnp.int32, (wrows, TW), 1)
        wpen = tile4(jnp.where(key <= t_q, jnp.where(key > t_q - WINDOW, 0.0, NEG), NEG))
        cols = [slice(r * TQ + sub * TW, r * TQ + (sub + 1) * TW) for r in range(Q_PER_KV)]
        p_win = []
        for g in groups:
            q_sub = jnp.concatenate([qpad_ref[g, :, c] for c in cols], axis=1)
            s = jnp.dot(k_win, q_sub, preferred_element_type=F32) + wpen
            m = jnp.max(s, axis=0, keepdims=True)
            p_win.append(jnp.exp2(s - m).astype(BF16))
        if sub == 0:
            sel_scores(0, n_full & 1)
        for g in groups:
            o_win = normalise(jnp.dot(vt_aug(vwint_ref, w0 // VC, wrows // VC, g), p_win[g],
                                      preferred_element_type=F32))
            for r, c in enumerate(cols):
                o_ref[2, g, :, c] = o_win[:, r * TW:(r + 1) * TW]

    m_ref[...] = jnp.full(m_ref.shape, NEG, F32)
    acc_ref[...] = jnp.zeros(acc_ref.shape, F32)
    bpt = KS // SEL_BLOCK
    cpt = KS // VC

    def sel_update(kt, slot, diag_keys=0):
        nkeys = diag_keys or KS
        probs, alphas = [], []
        for g in groups:
            m_prev = m_ref[g]
            if diag_keys:
                rows = [jnp.concatenate([pen_ref[g, kt * bpt + jj]] * (SEL_BLOCK // 8), axis=0)
                        for jj in range(nkeys // SEL_BLOCK)]
                pen = jnp.concatenate(rows, axis=0)
                key = kt * KS + lax.broadcasted_iota(jnp.int32, (nkeys, TQ), 0)
                t_q = q0 + lax.broadcasted_iota(jnp.int32, (nkeys, TQ), 1)
                s = s_ref[slot, g, 0:nkeys, :] + tile4(pen + jnp.where(key <= t_q, 0.0, NEG))
                m_new = jnp.maximum(m_prev, jnp.max(s, axis=0, keepdims=True))
                p = jnp.exp2(s - m_new)
            else:
                sb = [s_ref[slot, g, jj * SEL_BLOCK:(jj + 1) * SEL_BLOCK, :] for jj in range(bpt)]
                bias = [tile4(pen_ref[g, kt * bpt + jj][0:1, :]) for jj in range(bpt)]
                m_new = m_prev
                for jj in range(bpt):
                    m_new = jnp.maximum(m_new, jnp.max(sb[jj], axis=0, keepdims=True) + bias[jj])
                p = jnp.concatenate([jnp.exp2(sb[jj] + (bias[jj] - m_new)) for jj in range(bpt)],
                                    axis=0)
            probs.append(p.astype(BF16))
            alphas.append(jnp.exp2(m_prev - m_new))
            m_ref[g] = m_new
        for g in groups:
            acc_ref[g] = alphas[g] * acc_ref[g] + jnp.dot(
                vt_aug(vselt_ref, kt * cpt, nkeys // VC, g), probs[g], preferred_element_type=F32)

    def sel_body(kt, carry):
        for slot in (0, 1):
            @pl.when(((n_full - kt) & 1) == slot)
            def _(slot=slot):
                sel_scores(kt + 1, 1 - slot)
                sel_update(kt, slot)
        return carry

    lax.fori_loop(0, n_full, sel_body, 0)
    for nkeys in range(TQ, KS + 1, TQ):
        @pl.when(q0 + TQ - n_full * KS == nkeys)
        def _(nkeys=nkeys):
            sel_update(n_full, 0, diag_keys=nkeys)
    for g in groups:
        o_ref[1, g] = normalise(acc_ref[g])

    gates = jax.nn.sigmoid(brgt_ref[...])
    group_out = []
    for g in groups:
        heads = []
        for r in range(Q_PER_KV):
            hq = g * Q_PER_KV + r
            ls = slice(r * TQ, (r + 1) * TQ)
            o = gates[hq:hq + 1, :] * o_ref[0, g, :, ls]
            for n in range(1, N_BRANCH):
                o = o + gates[n * N_Q_HEADS + hq:n * N_Q_HEADS + hq + 1, :] * o_ref[n, g, :, ls]
            heads.append(o)
        group_out.append(jnp.concatenate(heads, axis=0).T)

    attn = jnp.concatenate(group_out, axis=1)
    out_ref[...] = (attn * _silu(ag_ref[...])).astype(out_ref.dtype)


def _nsa(qt, cmp_kv, ksel, vselt, kwin, vwint, brgt, ag, ovt, B, S):
    T = B * S
    nq = S // TQ
    qpm = TM // TQ
    nblk = S // SEL_BLOCK
    ncmp = S // CMP_STRIDE
    nl = Q_PER_KV * TQ
    return pl.pallas_call(
        _nsa_kernel,
        grid=(B, nq),
        in_specs=[
            pl.BlockSpec((None, None, D_ATTN, TQ), lambda b, q: (b, q // qpm, 0, q % qpm)),
            pl.BlockSpec((2, ncmp, D_KV), lambda b, q: (0, b, 0)),
            pl.BlockSpec((S, D_KV), lambda b, q: (b, 0)),
            pl.BlockSpec((None, S // VC, D_KV, VC), lambda b, q: (b, 0, 0, 0)),
            pl.BlockSpec((S, D_KV), lambda b, q: (b, 0)),
            pl.BlockSpec((None, S // VC, D_KV, VC), lambda b, q: (b, 0, 0, 0)),
            pl.BlockSpec((None, None, N_GATE_ROWS, TQ), lambda b, q: (b, q // qpm, 0, q % qpm)),
            pl.BlockSpec((TQ, D_ATTN), lambda b, q: (b * nq + q, 0)),
            pl.BlockSpec(ovt.shape, lambda b, q: (0, 0)),
        ],
        out_specs=pl.BlockSpec((TQ, D_ATTN), lambda b, q: (b * nq + q, 0)),
        out_shape=jax.ShapeDtypeStruct((T, D_ATTN), BF16),
        scratch_shapes=[
            pltpu.VMEM((N_KV_HEADS, D_KV, nl), BF16),
            pltpu.VMEM((N_KV_HEADS, nblk, 8, TQ), F32),
            pltpu.VMEM((N_BRANCH, N_KV_HEADS, HEAD_DIM, nl), F32),
            pltpu.VMEM((N_KV_HEADS, 1, nl), F32),
            pltpu.VMEM((N_KV_HEADS, HEAD_DIM + ONES_ROWS, nl), F32),
            pltpu.VMEM((2, N_KV_HEADS, KS, nl), F32),
        ],
        compiler_params=pltpu.CompilerParams(
            dimension_semantics=("arbitrary", "arbitrary"), vmem_limit_bytes=VMEM_LIMIT),
        name="nsa",
    )(qt, cmp_kv, ksel, vselt, kwin, vwint, brgt, ag, ovt)


def _outproj_kernel(x_ref, rnn_ref, attn_ref, wo_ref, nfw_ref, out_ref, wo_bf, *, final_norm):
    @pl.when(pl.program_id(0) == 0)
    def _():
        wo_bf[...] = wo_ref[...].astype(BF16)

    y = x_ref[...]
    y = y + jnp.dot(rnn_ref[...], wo_bf[0:D_RNN, :], preferred_element_type=F32)
    y = y + jnp.dot(attn_ref[...], wo_bf[D_RNN:D_MIX, :], preferred_element_type=F32)
    if final_norm:
        ms = jnp.mean(y * y, axis=-1, keepdims=True)
        y = (y * lax.rsqrt(ms + EPS)) * nfw_ref[...]
    out_ref[...] = y


def _outproj(x2, rnn_out, attn_out, wo, nfw, final_norm):
    T = x2.shape[0]
    row = lambda i: (i, 0)
    return pl.pallas_call(
        functools.partial(_outproj_kernel, final_norm=final_norm),
        grid=(T // TO,),
        in_specs=[
            pl.BlockSpec((TO, D_MODEL), row),
            pl.BlockSpec((TO, D_RNN), row),
            pl.BlockSpec((TO, D_ATTN), row),
            pl.BlockSpec((D_MIX, D_MODEL), lambda i: (0, 0)),
            pl.BlockSpec((1, D_MODEL), lambda i: (0, 0)),
        ],
        out_specs=pl.BlockSpec((TO, D_MODEL), row),
        out_shape=jax.ShapeDtypeStruct((T, D_MODEL), F32),
        scratch_shapes=[pltpu.VMEM((D_MIX, D_MODEL), BF16)],
        compiler_params=pltpu.CompilerParams(
            dimension_semantics=("arbitrary",), vmem_limit_bytes=VMEM_LIMIT),
        name="outproj",
    )(x2, rnn_out, attn_out, wo, nfw)


def _block_diag_halves(wa, wx):
    eye = jnp.eye(RNN_HEADS, dtype=wa.dtype)
    full = lambda w: jnp.einsum('hij,hk->hikj', w, eye).reshape(D_RNN, D_RNN)
    fa, fx = full(wa), full(wx)
    half = D_RNN // 2
    return jnp.stack([
        jnp.concatenate([fa[s:s + half, s:s + half], fx[s:s + half, s:s + half]], axis=1)
        for s in (0, half)]).astype(BF16)


def _compress_weights(pe, w2):
    eye = jnp.eye(N_KV_HEADS, dtype=w2.dtype)
    w2bd = jnp.einsum('nd,ge->gned', w2, eye).reshape(N_KV_HEADS * CMP_HIDDEN, D_KV)
    per = pe.reshape(2, CMP_STRIDE, 1, HEAD_DIM)
    pe2 = jnp.broadcast_to(per, (2, CMP_STRIDE, N_KV_HEADS, HEAD_DIM)).reshape(2, CMP_STRIDE * D_KV)
    return pe2, w2bd.astype(BF16)


def _overlap_t(ncmp_pad, nblk):
    cs = np.arange(ncmp_pad)[None, :] * CMP_STRIDE
    ss = np.arange(nblk)[:, None] * SEL_BLOCK
    ov = np.clip(np.minimum(cs + CMP_BLOCK, ss + SEL_BLOCK) - np.maximum(cs, ss), 0, None)
    return jnp.asarray(ov.astype(np.float32) / CMP_BLOCK)


def kernel(x, norm1_w, w_in, conv_w, conv_b, rg_wa, rg_ba, rg_wx, rg_bx, rg_lambda,
           cmp_k_pe, cmp_k_w1, cmp_k_w2, cmp_v_pe, cmp_v_w1, cmp_v_w2, w_out, normf_w):
    B, S, D = x.shape
    assert D == D_MODEL and w_in.shape[-1] == _D_IN
    assert S % TM == 0 and TS == TM and B % CB == 0
    assert TQ % TW == 0 and TW % VC == 0 and WINDOW % TW == 0 and KS % TQ == 0 and TM % TQ == 0
    assert S % KS == 0 and S >= WINDOW + TQ and (B * S) % TO == 0
    depth = w_in.shape[0]
    T = B * S
    ovt = _overlap_t(S // CMP_STRIDE, S // SEL_BLOCK)
    x2 = x.reshape(T, D)
    for l in range(depth):
        wn = jnp.pad(w_in[l].astype(BF16), ((0, 0), (0, _O_BR + LANE - _D_IN)))
        (rnn_out, kvc, ksel, kwin, ag, qt, vselt, vwint, brgt) = _inproj(
            x2, norm1_w[l].reshape(1, D), wn,
            conv_w[l], conv_b[l].reshape(1, D_RNN), _block_diag_halves(rg_wa[l], rg_wx[l]),
            rg_ba[l].reshape(1, D_RNN), rg_bx[l].reshape(1, D_RNN),
            rg_lambda[l].reshape(1, D_RNN), B, S)

        pk, w2k = _compress_weights(cmp_k_pe[l], cmp_k_w2[l])
        pv, w2v = _compress_weights(cmp_v_pe[l], cmp_v_w2[l])
        cmp_kv = _compress(kvc, jnp.stack([pk, pv]), cmp_k_w1[l], cmp_v_w1[l],
                           jnp.stack([w2k, w2v]), B, S)

        attn_out = _nsa(qt, cmp_kv, ksel, vselt, kwin, vwint, brgt, ag, ovt, B, S)

        x2 = _outproj(x2, rnn_out, attn_out, w_out[l],
                      normf_w.reshape(1, D), final_norm=(l == depth - 1))
    return x2.reshape(B, S, D)
```

```python
import functools

import numpy as np
import jax
import jax.numpy as jnp
from jax import lax
from jax.experimental import pallas as pl
from jax.experimental.pallas import tpu as pltpu

F32 = jnp.float32
BF16 = jnp.bfloat16

D_MODEL = 1024
EPS = 1e-6
D_RNN = 512
RNN_HEADS = 8
RNN_HEAD_DIM = D_RNN // RNN_HEADS
CONV_WIDTH = 4
LRU_C = 8.0
N_Q_HEADS = 8
N_KV_HEADS = 2
HEAD_DIM = 64
Q_PER_KV = N_Q_HEADS // N_KV_HEADS
D_ATTN = N_Q_HEADS * HEAD_DIM
D_KV = N_KV_HEADS * HEAD_DIM
CMP_BLOCK = 32
CMP_STRIDE = 16
CMP_HIDDEN = 256
SEL_BLOCK = 64
SEL_TOPK = 8
SEL_FORCE = 1e9
WINDOW = 512
N_BRANCH = 3
D_MIX = D_RNN + D_ATTN
N_GATE_ROWS = 32

LANE = 128
SCAN_CHUNKS = 8

TM = 1024
TS = 1024
TO = 1024
TQ = 256
KS = 512
VC = 128
TW = 128
CB = 4
NEG = -1e30
LOG2E = 1.4426950408889634
ONES_ROWS = 16
VMEM_V7X = 64 * 1024 * 1024
VMEM_LIMIT = VMEM_V7X - 8 * 1024 * 1024

_O_RX, _O_RG, _O_Q = 0, D_RNN, 2 * D_RNN
_O_KC = _O_Q + D_ATTN
_O_VC = _O_KC + D_KV
_O_KS = _O_VC + D_KV
_O_VS = _O_KS + D_KV
_O_KW = _O_VS + D_KV
_O_VW = _O_KW + D_KV
_O_AG = _O_VW + D_KV
_O_BR = _O_AG + D_ATTN
_D_IN = _O_BR + N_BRANCH * N_Q_HEADS


def _silu(x):
    return x * jax.nn.sigmoid(x)


def _inproj_kernel(x_ref, nw_ref, wn_ref, wb_ref, cw_ref, cb_ref, wg_ref, ba_ref, bx_ref, lam_ref,
                   rnn_ref, kvc_ref, ksel_ref, kwin_ref, ag_ref,
                   qt_ref, vselt_ref, vwint_ref, brgt_ref,
                   kv_scr, rx_scr, rg_scr, tail_ref, a_s, u_s, h_s, p_s, hlast, o_scr):
    @pl.when(pl.program_id(1) == 0)
    def _():
        tail_ref[...] = jnp.zeros(tail_ref.shape, F32)
        hlast[...] = jnp.zeros(hlast.shape, F32)

    x = x_ref[...]
    ms = jnp.mean(x * x, axis=-1, keepdims=True)
    h = ((x * lax.rsqrt(ms + EPS)) * nw_ref[...]).astype(BF16)

    def nat(a, b):
        return jnp.dot(h, wn_ref[:, a:b], preferred_element_type=F32)

    def store_step_major(dst_ref, val):
        steps = TM // SCAN_CHUNKS
        for l in range(D_RNN // LANE):
            for c in range(SCAN_CHUNKS):
                dst_ref[l, pl.ds(c, steps, stride=SCAN_CHUNKS), :] = (
                    val[c * steps:(c + 1) * steps, l * LANE:(l + 1) * LANE])

    store_step_major(rx_scr, nat(_O_RX, _O_RG))
    store_step_major(rg_scr, nat(_O_RG, _O_Q))

    def proj_cmp():
        kv = nat(_O_KC, _O_KS)
        for a in range(2):
            kv_scr[a] = kv[:, a * D_KV:(a + 1) * D_KV]
            for l in range(CMP_STRIDE):
                kvc_ref[a, :, l * D_KV:(l + 1) * D_KV] = (
                    kv_scr[a, pl.ds(l, TM // CMP_STRIDE, stride=CMP_STRIDE), :])

    def proj_kv(lo, k_ref, vt_ref):
        kv = nat(lo, lo + 2 * D_KV)
        k_ref[...] = kv[:, :D_KV].astype(BF16)
        vt = kv[:, D_KV:].T.astype(BF16)
        for c in range(TM // VC):
            vt_ref[c] = vt[:, c * VC:(c + 1) * VC]

    def proj_gates():
        ag_ref[...] = nat(_O_AG, _O_BR)
        brg = jnp.dot(h, wb_ref[...], preferred_element_type=F32)
        brgt_ref[...] = brg.T[0:N_GATE_ROWS, :]

    def proj_q():
        qt_ref[...] = (nat(_O_Q, _O_KC) * (HEAD_DIM ** -0.5 * LOG2E)).T.astype(BF16)

    projections = ((proj_gates,), (proj_q,),
                   (functools.partial(proj_kv, _O_KS, ksel_ref, vselt_ref),
                    functools.partial(proj_kv, _O_KW, kwin_ref, vwint_ref)),
                   (proj_cmp,))
    wraps = _rglru_wraps(rx_scr, tail_ref)
    rows = TM // len(projections)
    for ci, projs in enumerate(projections):
        _rglru_gates(rx_scr, wraps, ci * rows, (ci + 1) * rows,
                     cw_ref, cb_ref, wg_ref, ba_ref, bx_ref, lam_ref, a_s, u_s)
        for proj in projs:
            proj()

    _rglru_scan(rg_scr, rnn_ref, a_s, u_s, h_s, p_s, hlast, o_scr)


def _inproj(x2, nw, wn, wb, cw, cb, wg, ba, bx, lam, B, S):
    T = B * S
    ns = S // TM
    grid = (B, ns)
    row = lambda b, s: (b * ns + s, 0)
    const2 = lambda b, s: (0, 0)
    out_shape = (
        jax.ShapeDtypeStruct((T, D_RNN), BF16),
        jax.ShapeDtypeStruct((2, T // CMP_STRIDE, CMP_STRIDE * D_KV), F32),
        jax.ShapeDtypeStruct((T, D_KV), BF16),
        jax.ShapeDtypeStruct((T, D_KV), BF16),
        jax.ShapeDtypeStruct((T, D_ATTN), F32),
        jax.ShapeDtypeStruct((B, ns, D_ATTN, TM), BF16),
        jax.ShapeDtypeStruct((B, S // VC, D_KV, VC), BF16),
        jax.ShapeDtypeStruct((B, S // VC, D_KV, VC), BF16),
        jax.ShapeDtypeStruct((B, ns, N_GATE_ROWS, TM), F32),
    )
    out_specs = (
        pl.BlockSpec((TM, D_RNN), row),
        pl.BlockSpec((2, TM // CMP_STRIDE, CMP_STRIDE * D_KV), lambda b, s: (0, b * ns + s, 0)),
        pl.BlockSpec((TM, D_KV), row),
        pl.BlockSpec((TM, D_KV), row),
        pl.BlockSpec((TM, D_ATTN), row),
        pl.BlockSpec((None, None, D_ATTN, TM), lambda b, s: (b, s, 0, 0)),
        pl.BlockSpec((None, TM // VC, D_KV, VC), lambda b, s: (b, s, 0, 0)),
        pl.BlockSpec((None, TM // VC, D_KV, VC), lambda b, s: (b, s, 0, 0)),
        pl.BlockSpec((None, None, N_GATE_ROWS, TM), lambda b, s: (b, s, 0, 0)),
    )
    return pl.pallas_call(
        _inproj_kernel,
        grid=grid,
        in_specs=[
            pl.BlockSpec((TM, D_MODEL), row),
            pl.BlockSpec((1, D_MODEL), const2),
            pl.BlockSpec(wn.shape, const2),
            pl.BlockSpec(wb.shape, const2),
            pl.BlockSpec((CONV_WIDTH, D_RNN), const2),
            pl.BlockSpec((1, D_RNN), const2),
            pl.BlockSpec(wg.shape, lambda b, s: (0, 0, 0)),
            pl.BlockSpec((1, D_RNN), const2),
            pl.BlockSpec((1, D_RNN), const2),
            pl.BlockSpec((1, D_RNN), const2),
        ],
        out_specs=out_specs,
        out_shape=out_shape,
        scratch_shapes=[
            pltpu.VMEM((2, TM, D_KV), F32),
            pltpu.VMEM((D_RNN // LANE, TM, LANE), F32),
            pltpu.VMEM((D_RNN // LANE, TM, LANE), F32),
            pltpu.VMEM(((CONV_WIDTH - 1) * SCAN_CHUNKS, D_RNN), F32),
            pltpu.VMEM((TM, D_RNN), F32),
            pltpu.VMEM((TM, D_RNN), F32),
            pltpu.VMEM((TM, D_RNN), F32),
            pltpu.VMEM((TM, D_RNN), F32),
            pltpu.VMEM((1, D_RNN), F32),
            pltpu.VMEM((D_RNN // LANE, TM, LANE), F32),
        ],
        compiler_params=pltpu.CompilerParams(
            dimension_semantics=("arbitrary", "arbitrary"), vmem_limit_bytes=VMEM_LIMIT),
        name="inproj",
    )(x2, nw, wn, wb, cw, cb, wg, ba, bx, lam)


def _slab_rows(x_ref, lo, hi):
    return jnp.concatenate([x_ref[l, lo:hi, :] for l in range(D_RNN // LANE)], axis=1)


def _rglru_wraps(x_ref, tail_ref):
    ntail = (CONV_WIDTH - 1) * SCAN_CHUNKS
    last = _slab_rows(x_ref, TS - ntail, TS)
    sub = lax.broadcasted_iota(jnp.int32, (SCAN_CHUNKS, D_RNN), 0)
    wraps = []
    for j in range(CONV_WIDTH - 1):
        rows = slice(j * SCAN_CHUNKS, (j + 1) * SCAN_CHUNKS)
        cur = pltpu.roll(last[rows, :], 1, 0)
        prev = pltpu.roll(tail_ref[rows, :], 1, 0)
        wraps.append(jnp.where(sub == 0, prev, cur))
    tail_ref[...] = last
    return wraps


def _rglru_gates(x_ref, wraps, lo, hi, cw_ref, cb_ref, wg_ref, ba_ref, bx_ref, lam_ref, a_s, u_s):
    y = cb_ref[...]
    for k in range(CONV_WIDTH):
        off = (CONV_WIDTH - 1 - k) * SCAN_CHUNKS
        if lo >= off:
            xs = _slab_rows(x_ref, lo - off, hi - off)
        else:
            assert lo == 0
            xs = jnp.concatenate(wraps[len(wraps) - off // SCAN_CHUNKS:]
                                 + [_slab_rows(x_ref, 0, hi - off)], axis=0)
        y = y + xs * cw_ref[k:k + 1, :]

    yb = y.astype(BF16)
    half = D_RNN // 2
    pre = [jnp.dot(yb[:, hh * half:(hh + 1) * half], wg_ref[hh], preferred_element_type=F32)
           for hh in range(2)]
    pre_a = jnp.concatenate([pre[0][:, :half], pre[1][:, :half]], axis=1)
    pre_x = jnp.concatenate([pre[0][:, half:], pre[1][:, half:]], axis=1)
    r = jax.nn.sigmoid(pre_a + ba_ref[...])
    i = jax.nn.sigmoid(pre_x + bx_ref[...])
    lam = lam_ref[...]
    lsig = jnp.minimum(lam, 0.0) - jnp.log1p(jnp.exp(-jnp.abs(lam)))
    log_a = LRU_C * r * lsig
    a = jnp.exp(log_a)
    u = jnp.sqrt(1.0 - a * a) * (i * y)
    a_s[lo:hi, :] = a
    u_s[lo:hi, :] = u


def _rglru_scan(gate_ref, out_ref, a_s, u_s, h_s, p_s, hlast, o_scr):
    nstep = TS // SCAN_CHUNKS
    nslab = D_RNN // LANE
    gate = jnp.concatenate([gate_ref[l] for l in range(nslab)], axis=1)

    def body(j, carry):
        h, p = carry
        blk = pl.multiple_of(j * SCAN_CHUNKS, SCAN_CHUNKS)
        a_j = a_s[pl.ds(blk, SCAN_CHUNKS), :]
        h = a_j * h + u_s[pl.ds(blk, SCAN_CHUNKS), :]
        p = a_j * p
        h_s[pl.ds(blk, SCAN_CHUNKS), :] = h
        p_s[pl.ds(blk, SCAN_CHUNKS), :] = p
        return h, p

    h_end, p_end = lax.fori_loop(
        0, nstep, body,
        (jnp.zeros((SCAN_CHUNKS, D_RNN), F32), jnp.ones((SCAN_CHUNKS, D_RNN), F32)), unroll=8)
    carry = hlast[...]
    h_in = []
    for c in range(SCAN_CHUNKS):
        h_in.append(carry)
        carry = h_end[c:c + 1, :] + p_end[c:c + 1, :] * carry
    hlast[...] = carry
    h_in = jnp.concatenate([jnp.concatenate(h_in, axis=0)] * nstep, axis=0)
    o = (h_s[...] + p_s[...] * h_in) * _silu(gate)

    for l in range(nslab):
        o_scr[l] = o[:, l * LANE:(l + 1) * LANE]
        for c in range(SCAN_CHUNKS):
            out_ref[c * nstep:(c + 1) * nstep, l * LANE:(l + 1) * LANE] = (
                o_scr[l, pl.ds(c, nstep, stride=SCAN_CHUNKS), :].astype(out_ref.dtype))


def _compress_kernel(x_ref, pe_ref, w1k_ref, w1v_ref, w2_ref, out_ref, wbig_ref):
    nrow = x_ref.shape[1]
    nchunk = nrow // CB
    nh = N_KV_HEADS * CMP_HIDDEN

    for which, w1_ref in enumerate((w1k_ref, w1v_ref)):
        @pl.when((pl.program_id(0) == which) & (pl.program_id(1) == 0))
        def _(w1_ref=w1_ref):
            wbig_ref[...] = jnp.zeros(wbig_ref.shape, BF16)
            for half in range(2):
                for l in range(CMP_STRIDE):
                    r0 = (half * CMP_STRIDE + l) * HEAD_DIM
                    blk = w1_ref[r0:r0 + HEAD_DIM, :].astype(BF16)
                    for g in range(N_KV_HEADS):
                        rows = slice(l * D_KV + g * HEAD_DIM, l * D_KV + (g + 1) * HEAD_DIM)
                        cols = slice(half * nh + g * CMP_HIDDEN, half * nh + (g + 1) * CMP_HIDDEN)
                        wbig_ref[rows, cols] = blk

    x = x_ref[0]
    xa = (x + pe_ref[0, 0:1, :]).astype(BF16)
    xb = (x + pe_ref[0, 1:2, :]).astype(BF16)
    first = jnp.dot(xa, wbig_ref[:, :nh], preferred_element_type=F32)
    second = jnp.dot(xb, wbig_ref[:, nh:], preferred_element_type=F32)
    hid = _silu(first + pltpu.roll(second, nrow - 1, 0))
    out = jnp.dot(hid.astype(BF16), w2_ref[0], preferred_element_type=F32)
    c_idx = lax.broadcasted_iota(jnp.int32, out.shape, 0) & (nchunk - 1)
    out_ref[0] = jnp.where(c_idx < nchunk - 1, out, 0.0)


def _compress(xc, pe2, w1k, w1v, w2bd, B, S):
    nchunk = S // CMP_STRIDE
    nrow = CB * nchunk
    width = CMP_STRIDE * D_KV
    kv = lambda i, j: (i, 0, 0)
    return pl.pallas_call(
        _compress_kernel,
        grid=(2, B // CB),
        in_specs=[
            pl.BlockSpec((1, nrow, width), lambda i, j: (i, j, 0)),
            pl.BlockSpec((1, 2, width), kv),
            pl.BlockSpec(w1k.shape, lambda i, j: (0, 0)),
            pl.BlockSpec(w1v.shape, lambda i, j: (0, 0)),
            pl.BlockSpec((1,) + w2bd.shape[1:], kv),
        ],
        out_specs=pl.BlockSpec((1, nrow, D_KV), lambda i, j: (i, j, 0)),
        out_shape=jax.ShapeDtypeStruct((2, B * nchunk, D_KV), F32),
        scratch_shapes=[pltpu.VMEM((width, 2 * N_KV_HEADS * CMP_HIDDEN), BF16)],
        compiler_params=pltpu.CompilerParams(
            dimension_semantics=("arbitrary", "arbitrary"), vmem_limit_bytes=VMEM_LIMIT),
        name="compress",
    )(xc, pe2, w1k, w1v, w2bd)


def _nsa_kernel(qt_ref, cmp_ref, ksel_ref, vselt_ref, kwin_ref, vwint_ref, brgt_ref, ag_ref,
                ovt_ref, out_ref, qpad_ref, pen_ref, o_ref, m_ref, acc_ref, s_ref):
    qi = pl.program_id(1)
    q0 = qi * TQ
    nl = Q_PER_KV * TQ
    nblk = pen_ref.shape[1]
    ncmp = cmp_ref.shape[1]
    groups = range(N_KV_HEADS)
    gsl = [slice(g * HEAD_DIM, (g + 1) * HEAD_DIM) for g in groups]

    def tile4(a):
        return jnp.concatenate([a] * Q_PER_KV, axis=1)

    kc = cmp_ref[0].astype(BF16)
    s_cmp = []
    for g in groups:
        zpad = jnp.zeros((HEAD_DIM, TQ), BF16)
        cols = []
        for r in range(Q_PER_KV):
            hq = g * Q_PER_KV + r
            qh = qt_ref[hq * HEAD_DIM:(hq + 1) * HEAD_DIM, :]
            cols.append(jnp.concatenate([qh, zpad] if g == 0 else [zpad, qh], axis=0))
        qpad = jnp.concatenate(cols, axis=1)
        qpad_ref[g] = qpad
        s_cmp.append(jnp.dot(kc, qpad, preferred_element_type=F32))

    def sel_scores(kt, slot):
        k_tile = ksel_ref[pl.ds(pl.multiple_of(kt * KS, KS), KS), :]
        for g in groups:
            s_ref[slot, g] = jnp.dot(k_tile, qpad_ref[g], preferred_element_type=F32)

    n_full = q0 // KS

    ones = jnp.ones((ONES_ROWS, VC), BF16)

    def vt_aug(v_ref, c0, n, g):
        vt = jnp.concatenate([v_ref[c0 + c, gsl[g], :] for c in range(n)], axis=1)
        return jnp.concatenate([vt, jnp.concatenate([ones] * n, axis=1)], axis=0)

    def normalise(res):
        return res[0:HEAD_DIM, :] * (1.0 / res[HEAD_DIM:HEAD_DIM + 1, :])

    c_row = lax.broadcasted_iota(jnp.int32, (ncmp, TQ), 0)
    t_lane = q0 + lax.broadcasted_iota(jnp.int32, (ncmp, TQ), 1)
    cpen = tile4(jnp.where(c_row * CMP_STRIDE + (CMP_BLOCK - 1) <= t_lane, 0.0, NEG))
    vc_t = cmp_ref[1].T
    for g in groups:
        vct = vc_t[gsl[g], :].astype(BF16)
        s = s_cmp[g] + cpen
        m = jnp.max(s, axis=0, keepdims=True)
        m = jnp.where(m > 0.5 * NEG, m, 0.0)
        e = jnp.exp2(s - m)
        den = jnp.sum(e, axis=0, keepdims=True)
        p = e * (1.0 / jnp.where(den > 0.0, den, 1.0))
        o_ref[0, g] = jnp.dot(vct, p.astype(BF16), preferred_element_type=F32)

        psum = p[:, 0:TQ]
        for r in range(1, Q_PER_KV):
            psum = psum + p[:, r * TQ:(r + 1) * TQ]
        imp = jnp.dot(ovt_ref[...], psum, preferred_element_type=F32,
                      precision=lax.Precision.HIGHEST)
        j_row = lax.broadcasted_iota(jnp.int32, (nblk, TQ), 0)
        t_blk = q0 + lax.broadcasted_iota(jnp.int32, (nblk, TQ), 1)
        cur = jnp.right_shift(t_blk, SEL_BLOCK.bit_length() - 1)
        forced = (j_row == 0) | (j_row == cur) | (j_row == cur - 1)
        val = jnp.where(forced, SEL_FORCE, imp)
        val = jnp.where(j_row * SEL_BLOCK <= t_blk, val, -SEL_FORCE)
        rank = jnp.zeros((nblk, TQ), F32)
        for i in range(nblk):
            vi = jnp.broadcast_to(val[i:i + 1, :], (nblk, TQ))
            rank = rank + jnp.where(j_row > i, jnp.where(vi >= val, 1.0, 0.0),
                                    jnp.where(vi > val, 1.0, 0.0))
        chosen = (rank < float(min(SEL_TOPK, nblk))) & (val > -0.5 * SEL_FORCE)
        selpen = jnp.where(chosen, 0.0, NEG)
        for j in range(nblk):
            pen_ref[g, j] = jnp.broadcast_to(selpen[j:j + 1, :], (8, TQ))

    wrows = WINDOW + TW
    for sub in range(TQ // TW):
        qs = q0 + sub * TW
        w0 = pl.multiple_of(jnp.maximum(qs - WINDOW, 0), TW)
        k_win = kwin_ref[pl.ds(w0, wrows), :]
        key = w0 + lax.broadcasted_iota(jnp.int32, (wrows, TW), 0)
        t_q = qs + lax.broadcasted_iota(jnp.int32, (wrows, TW), 1)
        wpen = tile4(jnp.where(key <= t_q, jnp.where(key > t_q - WINDOW, 0.0, NEG), NEG))
        cols = [slice(r * TQ + sub * TW, r * TQ + (sub + 1) * TW) for r in range(Q_PER_KV)]
        p_win = []
        for g in groups:
            q_sub = jnp.concatenate([qpad_ref[g, :, c] for c in cols], axis=1)
            s = jnp.dot(k_win, q_sub, preferred_element_type=F32) + wpen
            m = jnp.max(s, axis=0, keepdims=True)
            p_win.append(jnp.exp2(s - m).astype(BF16))
        if sub == 0:
            sel_scores(0, n_full & 1)
        for g in groups:
            o_win = normalise(jnp.dot(vt_aug(vwint_ref, w0 // VC, wrows // VC, g), p_win[g],
                                      preferred_element_type=F32))
            for r, c in enumerate(cols):
                o_ref[2, g, :, c] = o_win[:, r * TW:(r + 1) * TW]

    m_ref[...] = jnp.full(m_ref.shape, NEG, F32)
    acc_ref[...] = jnp.zeros(acc_ref.shape, F32)
    bpt = KS // SEL_BLOCK
    cpt = KS // VC

    def sel_update(kt, slot, diag_keys=0):
        nkeys = diag_keys or KS
        probs, alphas = [], []
        for g in groups:
            m_prev = m_ref[g]
            if diag_keys:
                rows = [jnp.concatenate([pen_ref[g, kt * bpt + jj]] * (SEL_BLOCK // 8), axis=0)
                        for jj in range(nkeys // SEL_BLOCK)]
                pen = jnp.concatenate(rows, axis=0)
                key = kt * KS + lax.broadcasted_iota(jnp.int32, (nkeys, TQ), 0)
                t_q = q0 + lax.broadcasted_iota(jnp.int32, (nkeys, TQ), 1)
                s = s_ref[slot, g, 0:nkeys, :] + tile4(pen + jnp.where(key <= t_q, 0.0, NEG))
                m_new = jnp.maximum(m_prev, jnp.max(s, axis=0, keepdims=True))
                p = jnp.exp2(s - m_new)
            else:
                sb = [s_ref[slot, g, jj * SEL_BLOCK:(jj + 1) * SEL_BLOCK, :] for jj in range(bpt)]
                bias = [tile4(pen_ref[g, kt * bpt + jj][0:1, :]) for jj in range(bpt)]
                m_new = m_prev
                for jj in range(bpt):
                    m_new = jnp.maximum(m_new, jnp.max(sb[jj], axis=0, keepdims=True) + bias[jj])
                p = jnp.concatenate([jnp.exp2(sb[jj] + (bias[jj] - m_new)) for jj in range(bpt)],
                                    axis=0)
            probs.append(p.astype(BF16))
            alphas.append(jnp.exp2(m_prev - m_new))
            m_ref[g] = m_new
        for g in groups:
            acc_ref[g] = alphas[g] * acc_ref[g] + jnp.dot(
                vt_aug(vselt_ref, kt * cpt, nkeys // VC, g), probs[g], preferred_element_type=F32)

    def sel_body(kt, carry):
        for slot in (0, 1):
            @pl.when(((n_full - kt) & 1) == slot)
            def _(slot=slot):
                sel_scores(kt + 1, 1 - slot)
                sel_update(kt, slot)
        return carry

    lax.fori_loop(0, n_full, sel_body, 0)
    for nkeys in range(TQ, KS + 1, TQ):
        @pl.when(q0 + TQ - n_full * KS == nkeys)
        def _(nkeys=nkeys):
            sel_update(n_full, 0, diag_keys=nkeys)
    for g in groups:
        o_ref[1, g] = normalise(acc_ref[g])

    gates = jax.nn.sigmoid(brgt_ref[...])
    group_out = []
    for g in groups:
        heads = []
        for r in range(Q_PER_KV):
            hq = g * Q_PER_KV + r
            ls = slice(r * TQ, (r + 1) * TQ)
            o = gates[hq:hq + 1, :] * o_ref[0, g, :, ls]
            for n in range(1, N_BRANCH):
                o = o + gates[n * N_Q_HEADS + hq:n * N_Q_HEADS + hq + 1, :] * o_ref[n, g, :, ls]
            heads.append(o)
        group_out.append(jnp.concatenate(heads, axis=0).T)

    attn = jnp.concatenate(group_out, axis=1)
    out_ref[...] = (attn * _silu(ag_ref[...])).astype(out_ref.dtype)


def _nsa(qt, cmp_kv, ksel, vselt, kwin, vwint, brgt, ag, ovt, B, S):
    T = B * S
    nq = S // TQ
    qpm = TM // TQ
    nblk = S // SEL_BLOCK
    ncmp = S // CMP_STRIDE
    nl = Q_PER_KV * TQ
    return pl.pallas_call(
        _nsa_kernel,
        grid=(B, nq),
        in_specs=[
            pl.BlockSpec((None, None, D_ATTN, TQ), lambda b, q: (b, q // qpm, 0, q % qpm)),
            pl.BlockSpec((2, ncmp, D_KV), lambda b, q: (0, b, 0)),
            pl.BlockSpec((S, D_KV), lambda b, q: (b, 0)),
            pl.BlockSpec((None, S // VC, D_KV, VC), lambda b, q: (b, 0, 0, 0)),
            pl.BlockSpec((S, D_KV), lambda b, q: (b, 0)),
            pl.BlockSpec((None, S // VC, D_KV, VC), lambda b, q: (b, 0, 0, 0)),
            pl.BlockSpec((None, None, N_GATE_ROWS, TQ), lambda b, q: (b, q // qpm, 0, q % qpm)),
            pl.BlockSpec((TQ, D_ATTN), lambda b, q: (b * nq + q, 0)),
            pl.BlockSpec(ovt.shape, lambda b, q: (0, 0)),
        ],
        out_specs=pl.BlockSpec((TQ, D_ATTN), lambda b, q: (b * nq + q, 0)),
        out_shape=jax.ShapeDtypeStruct((T, D_ATTN), BF16),
        scratch_shapes=[
            pltpu.VMEM((N_KV_HEADS, D_KV, nl), BF16),
            pltpu.VMEM((N_KV_HEADS, nblk, 8, TQ), F32),
            pltpu.VMEM((N_BRANCH, N_KV_HEADS, HEAD_DIM, nl), F32),
            pltpu.VMEM((N_KV_HEADS, 1, nl), F32),
            pltpu.VMEM((N_KV_HEADS, HEAD_DIM + ONES_ROWS, nl), F32),
            pltpu.VMEM((2, N_KV_HEADS, KS, nl), F32),
        ],
        compiler_params=pltpu.CompilerParams(
            dimension_semantics=("arbitrary", "arbitrary"), vmem_limit_bytes=VMEM_LIMIT),
        name="nsa",
    )(qt, cmp_kv, ksel, vselt, kwin, vwint, brgt, ag, ovt)


def _outproj_kernel(x_ref, rnn_ref, attn_ref, wo_ref, nfw_ref, out_ref, wo_bf, *, final_norm):
    @pl.when(pl.program_id(0) == 0)
    def _():
        wo_bf[...] = wo_ref[...].astype(BF16)

    y = x_ref[...]
    y = y + jnp.dot(rnn_ref[...], wo_bf[0:D_RNN, :], preferred_element_type=F32)
    y = y + jnp.dot(attn_ref[...], wo_bf[D_RNN:D_MIX, :], preferred_element_type=F32)
    if final_norm:
        ms = jnp.mean(y * y, axis=-1, keepdims=True)
        y = (y * lax.rsqrt(ms + EPS)) * nfw_ref[...]
    out_ref[...] = y


def _outproj(x2, rnn_out, attn_out, wo, nfw, final_norm):
    T = x2.shape[0]
    row = lambda i: (i, 0)
    return pl.pallas_call(
        functools.partial(_outproj_kernel, final_norm=final_norm),
        grid=(T // TO,),
        in_specs=[
            pl.BlockSpec((TO, D_MODEL), row),
            pl.BlockSpec((TO, D_RNN), row),
            pl.BlockSpec((TO, D_ATTN), row),
            pl.BlockSpec((D_MIX, D_MODEL), lambda i: (0, 0)),
            pl.BlockSpec((1, D_MODEL), lambda i: (0, 0)),
        ],
        out_specs=pl.BlockSpec((TO, D_MODEL), row),
        out_shape=jax.ShapeDtypeStruct((T, D_MODEL), F32),
        scratch_shapes=[pltpu.VMEM((D_MIX, D_MODEL), BF16)],
        compiler_params=pltpu.CompilerParams(
            dimension_semantics=("arbitrary",), vmem_limit_bytes=VMEM_LIMIT),
        name="outproj",
    )(x2, rnn_out, attn_out, wo, nfw)


def _block_diag_halves(wa, wx):
    eye = jnp.eye(RNN_HEADS, dtype=wa.dtype)
    full = lambda w: jnp.einsum('hij,hk->hikj', w, eye).reshape(D_RNN, D_RNN)
    fa, fx = full(wa), full(wx)
    half = D_RNN // 2
    return jnp.stack([
        jnp.concatenate([fa[s:s + half, s:s + half], fx[s:s + half, s:s + half]], axis=1)
        for s in (0, half)]).astype(BF16)


def _compress_weights(pe, w2):
    eye = jnp.eye(N_KV_HEADS, dtype=w2.dtype)
    w2bd = jnp.einsum('nd,ge->gned', w2, eye).reshape(N_KV_HEADS * CMP_HIDDEN, D_KV)
    per = pe.reshape(2, CMP_STRIDE, 1, HEAD_DIM)
    pe2 = jnp.broadcast_to(per, (2, CMP_STRIDE, N_KV_HEADS, HEAD_DIM)).reshape(2, CMP_STRIDE * D_KV)
    return pe2, w2bd.astype(BF16)


def _overlap_t(ncmp_pad, nblk):
    cs = np.arange(ncmp_pad)[None, :] * CMP_STRIDE
    ss = np.arange(nblk)[:, None] * SEL_BLOCK
    ov = np.clip(np.minimum(cs + CMP_BLOCK, ss + SEL_BLOCK) - np.maximum(cs, ss), 0, None)
    return jnp.asarray(ov.astype(np.float32) / CMP_BLOCK)


def kernel(x, norm1_w, w_in, conv_w, conv_b, rg_wa, rg_ba, rg_wx, rg_bx, rg_lambda,
           cmp_k_pe, cmp_k_w1, cmp_k_w2, cmp_v_pe, cmp_v_w1, cmp_v_w2, w_out, normf_w):
    B, S, D = x.shape
    assert D == D_MODEL and w_in.shape[-1] == _D_IN
    assert S % TM == 0 and TS == TM and B % CB == 0
    assert TQ % TW == 0 and TW % VC == 0 and WINDOW % TW == 0 and KS % TQ == 0 and TM % TQ == 0
    assert S % KS == 0 and S >= WINDOW + TQ and (B * S) % TO == 0
    depth = w_in.shape[0]
    T = B * S
    ovt = _overlap_t(S // CMP_STRIDE, S // SEL_BLOCK)
    x2 = x.reshape(T, D)
    for l in range(depth):
        wn = w_in[l].astype(BF16)
        wb = jnp.pad(w_in[l][:, _O_BR:], ((0, 0), (0, _O_BR + LANE - _D_IN))).astype(BF16)
        (rnn_out, kvc, ksel, kwin, ag, qt, vselt, vwint, brgt) = _inproj(
            x2, norm1_w[l].reshape(1, D), wn, wb,
            conv_w[l], conv_b[l].reshape(1, D_RNN), _block_diag_halves(rg_wa[l], rg_wx[l]),
            rg_ba[l].reshape(1, D_RNN), rg_bx[l].reshape(1, D_RNN),
            rg_lambda[l].reshape(1, D_RNN), B, S)

        pk, w2k = _compress_weights(cmp_k_pe[l], cmp_k_w2[l])
        pv, w2v = _compress_weights(cmp_v_pe[l], cmp_v_w2[l])
        cmp_kv = _compress(kvc, jnp.stack([pk, pv]), cmp_k_w1[l], cmp_v_w1[l],
                           jnp.stack([w2k, w2v]), B, S)

        attn_out = _nsa(qt, cmp_kv, ksel, vselt, kwin, vwint, brgt, ag, ovt, B, S)

        x2 = _outproj(x2, rnn_out, attn_out, w_out[l],
                      normf_w.reshape(1, D), final_norm=(l == depth - 1))
    return x2.reshape(B, S, D)
```

```python
import functools

import numpy as np
import jax
import jax.numpy as jnp
from jax import lax
from jax.experimental import pallas as pl
from jax.experimental.pallas import tpu as pltpu

F32 = jnp.float32
BF16 = jnp.bfloat16

D_MODEL = 1024
EPS = 1e-6
D_RNN = 512
RNN_HEADS = 8
RNN_HEAD_DIM = D_RNN // RNN_HEADS
CONV_WIDTH = 4
LRU_C = 8.0
N_Q_HEADS = 8
N_KV_HEADS = 2
HEAD_DIM = 64
Q_PER_KV = N_Q_HEADS // N_KV_HEADS
D_ATTN = N_Q_HEADS * HEAD_DIM
D_KV = N_KV_HEADS * HEAD_DIM
CMP_BLOCK = 32
CMP_STRIDE = 16
CMP_HIDDEN = 256
SEL_BLOCK = 64
SEL_TOPK = 8
SEL_FORCE = 1e9
WINDOW = 512
N_BRANCH = 3
D_MIX = D_RNN + D_ATTN
N_GATE_ROWS = 32

LANE = 128
SCAN_CHUNKS = 8

TM = 1024
TS = 1024
TO = 1024
OUT_ROW_CHUNKS = 4
NORM_ROW_CHUNKS = 2
TQ = 256
KS = 512
VC = 128
TW = 128
CB = 4
NEG = -1e30
LOG2E = 1.4426950408889634
ONES_ROWS = 16
VMEM_V7X = 64 * 1024 * 1024
VMEM_LIMIT = VMEM_V7X - 8 * 1024 * 1024

_O_RX, _O_RG, _O_Q = 0, D_RNN, 2 * D_RNN
_O_KC = _O_Q + D_ATTN
_O_VC = _O_KC + D_KV
_O_KS = _O_VC + D_KV
_O_VS = _O_KS + D_KV
_O_KW = _O_VS + D_KV
_O_VW = _O_KW + D_KV
_O_AG = _O_VW + D_KV
_O_BR = _O_AG + D_ATTN
_D_IN = _O_BR + N_BRANCH * N_Q_HEADS


def _silu(x):
    return x * jax.nn.sigmoid(x)


def _inproj_kernel(x_ref, nw_ref, wn_ref, wb_ref, cw_ref, cb_ref, wg_ref, ba_ref, bx_ref, lam_ref,
                   rnn_ref, kvc_ref, ksel_ref, kwin_ref, ag_ref,
                   qt_ref, vselt_ref, vwint_ref, brgt_ref,
                   kv_scr, rx_scr, rg_scr, tail_ref, a_s, u_s, h_s, p_s, hlast, o_scr):
    @pl.when(pl.program_id(1) == 0)
    def _():
        tail_ref[...] = jnp.zeros(tail_ref.shape, F32)
        hlast[...] = jnp.zeros(hlast.shape, F32)

    rows = TM // NORM_ROW_CHUNKS
    h_parts, rx_parts = [], []
    for c in range(NORM_ROW_CHUNKS):
        x = x_ref[c * rows:(c + 1) * rows, :]
        ms = jnp.mean(x * x, axis=-1, keepdims=True)
        h_c = ((x * lax.rsqrt(ms + EPS)) * nw_ref[...]).astype(BF16)
        h_parts.append(h_c)
        rx_parts.append(jnp.dot(h_c, wn_ref[:, _O_RX:_O_RG], preferred_element_type=F32))
    h = jnp.concatenate(h_parts, axis=0)

    def nat(a, b):
        return jnp.dot(h, wn_ref[:, a:b], preferred_element_type=F32)

    def store_step_major(dst_ref, val):
        steps = TM // SCAN_CHUNKS
        for l in range(D_RNN // LANE):
            for c in range(SCAN_CHUNKS):
                dst_ref[l, pl.ds(c, steps, stride=SCAN_CHUNKS), :] = (
                    val[c * steps:(c + 1) * steps, l * LANE:(l + 1) * LANE])

    store_step_major(rx_scr, jnp.concatenate(rx_parts, axis=0))
    store_step_major(rg_scr, nat(_O_RG, _O_Q))

    def proj_cmp():
        kv = nat(_O_KC, _O_KS)
        for a in range(2):
            kv_scr[a] = kv[:, a * D_KV:(a + 1) * D_KV]
            for l in range(CMP_STRIDE):
                kvc_ref[a, :, l * D_KV:(l + 1) * D_KV] = (
                    kv_scr[a, pl.ds(l, TM // CMP_STRIDE, stride=CMP_STRIDE), :])

    def proj_kv(lo, k_ref, vt_ref):
        kv = nat(lo, lo + 2 * D_KV)
        k_ref[...] = kv[:, :D_KV].astype(BF16)
        vt = kv[:, D_KV:].T.astype(BF16)
        for c in range(TM // VC):
            vt_ref[c] = vt[:, c * VC:(c + 1) * VC]

    def proj_gates():
        ag_ref[...] = nat(_O_AG, _O_BR)
        brg = jnp.dot(h, wb_ref[...], preferred_element_type=F32)
        brgt_ref[...] = brg.T[0:N_GATE_ROWS, :]

    def proj_q():
        qt_ref[...] = (nat(_O_Q, _O_KC) * (HEAD_DIM ** -0.5 * LOG2E)).T.astype(BF16)

    projections = ((proj_gates,), (proj_q,),
                   (functools.partial(proj_kv, _O_KS, ksel_ref, vselt_ref),
                    functools.partial(proj_kv, _O_KW, kwin_ref, vwint_ref)),
                   (proj_cmp,))
    wraps = _rglru_wraps(rx_scr, tail_ref)
    rows = TM // len(projections)
    for ci, projs in enumerate(projections):
        _rglru_gates(rx_scr, wraps, ci * rows, (ci + 1) * rows,
                     cw_ref, cb_ref, wg_ref, ba_ref, bx_ref, lam_ref, a_s, u_s)
        for proj in projs:
            proj()

    _rglru_scan(rg_scr, rnn_ref, a_s, u_s, h_s, p_s, hlast, o_scr)


def _inproj(x2, nw, wn, wb, cw, cb, wg, ba, bx, lam, B, S):
    T = B * S
    ns = S // TM
    grid = (B, ns)
    row = lambda b, s: (b * ns + s, 0)
    const2 = lambda b, s: (0, 0)
    out_shape = (
        jax.ShapeDtypeStruct((T, D_RNN), BF16),
        jax.ShapeDtypeStruct((2, T // CMP_STRIDE, CMP_STRIDE * D_KV), F32),
        jax.ShapeDtypeStruct((T, D_KV), BF16),
        jax.ShapeDtypeStruct((T, D_KV), BF16),
        jax.ShapeDtypeStruct((T, D_ATTN), F32),
        jax.ShapeDtypeStruct((B, ns, D_ATTN, TM), BF16),
        jax.ShapeDtypeStruct((B, S // VC, D_KV, VC), BF16),
        jax.ShapeDtypeStruct((B, S // VC, D_KV, VC), BF16),
        jax.ShapeDtypeStruct((B, ns, N_GATE_ROWS, TM), F32),
    )
    out_specs = (
        pl.BlockSpec((TM, D_RNN), row),
        pl.BlockSpec((2, TM // CMP_STRIDE, CMP_STRIDE * D_KV), lambda b, s: (0, b * ns + s, 0)),
        pl.BlockSpec((TM, D_KV), row),
        pl.BlockSpec((TM, D_KV), row),
        pl.BlockSpec((TM, D_ATTN), row),
        pl.BlockSpec((None, None, D_ATTN, TM), lambda b, s: (b, s, 0, 0)),
        pl.BlockSpec((None, TM // VC, D_KV, VC), lambda b, s: (b, s, 0, 0)),
        pl.BlockSpec((None, TM // VC, D_KV, VC), lambda b, s: (b, s, 0, 0)),
        pl.BlockSpec((None, None, N_GATE_ROWS, TM), lambda b, s: (b, s, 0, 0)),
    )
    return pl.pallas_call(
        _inproj_kernel,
        grid=grid,
        in_specs=[
            pl.BlockSpec((TM, D_MODEL), row),
            pl.BlockSpec((1, D_MODEL), const2),
            pl.BlockSpec(wn.shape, const2),
            pl.BlockSpec(wb.shape, const2),
            pl.BlockSpec((CONV_WIDTH, D_RNN), const2),
            pl.BlockSpec((1, D_RNN), const2),
            pl.BlockSpec(wg.shape, lambda b, s: (0, 0, 0)),
            pl.BlockSpec((1, D_RNN), const2),
            pl.BlockSpec((1, D_RNN), const2),
            pl.BlockSpec((1, D_RNN), const2),
        ],
        out_specs=out_specs,
        out_shape=out_shape,
        scratch_shapes=[
            pltpu.VMEM((2, TM, D_KV), F32),
            pltpu.VMEM((D_RNN // LANE, TM, LANE), F32),
            pltpu.VMEM((D_RNN // LANE, TM, LANE), F32),
            pltpu.VMEM(((CONV_WIDTH - 1) * SCAN_CHUNKS, D_RNN), F32),
            pltpu.VMEM((TM, D_RNN), F32),
            pltpu.VMEM((TM, D_RNN), F32),
            pltpu.VMEM((TM, D_RNN), F32),
            pltpu.VMEM((TM, D_RNN), F32),
            pltpu.VMEM((1, D_RNN), F32),
            pltpu.VMEM((D_RNN // LANE, TM, LANE), F32),
        ],
        compiler_params=pltpu.CompilerParams(
            dimension_semantics=("arbitrary", "arbitrary"), vmem_limit_bytes=VMEM_LIMIT),
        name="inproj",
    )(x2, nw, wn, wb, cw, cb, wg, ba, bx, lam)


def _slab_rows(x_ref, lo, hi):
    return jnp.concatenate([x_ref[l, lo:hi, :] for l in range(D_RNN // LANE)], axis=1)


def _rglru_wraps(x_ref, tail_ref):
    ntail = (CONV_WIDTH - 1) * SCAN_CHUNKS
    last = _slab_rows(x_ref, TS - ntail, TS)
    sub = lax.broadcasted_iota(jnp.int32, (SCAN_CHUNKS, D_RNN), 0)
    wraps = []
    for j in range(CONV_WIDTH - 1):
        rows = slice(j * SCAN_CHUNKS, (j + 1) * SCAN_CHUNKS)
        cur = pltpu.roll(last[rows, :], 1, 0)
        prev = pltpu.roll(tail_ref[rows, :], 1, 0)
        wraps.append(jnp.where(sub == 0, prev, cur))
    tail_ref[...] = last
    return wraps


def _rglru_gates(x_ref, wraps, lo, hi, cw_ref, cb_ref, wg_ref, ba_ref, bx_ref, lam_ref, a_s, u_s):
    y = cb_ref[...]
    for k in range(CONV_WIDTH):
        off = (CONV_WIDTH - 1 - k) * SCAN_CHUNKS
        if lo >= off:
            xs = _slab_rows(x_ref, lo - off, hi - off)
        else:
            assert lo == 0
            xs = jnp.concatenate(wraps[len(wraps) - off // SCAN_CHUNKS:]
                                 + [_slab_rows(x_ref, 0, hi - off)], axis=0)
        y = y + xs * cw_ref[k:k + 1, :]

    yb = y.astype(BF16)
    half = D_RNN // 2
    pre = [jnp.dot(yb[:, hh * half:(hh + 1) * half], wg_ref[hh], preferred_element_type=F32)
           for hh in range(2)]
    pre_a = jnp.concatenate([pre[0][:, :half], pre[1][:, :half]], axis=1)
    pre_x = jnp.concatenate([pre[0][:, half:], pre[1][:, half:]], axis=1)
    r = jax.nn.sigmoid(pre_a + ba_ref[...])
    i = jax.nn.sigmoid(pre_x + bx_ref[...])
    lam = lam_ref[...]
    lsig = jnp.minimum(lam, 0.0) - jnp.log1p(jnp.exp(-jnp.abs(lam)))
    log_a = LRU_C * r * lsig
    a = jnp.exp(log_a)
    u = jnp.sqrt(1.0 - a * a) * (i * y)
    a_s[lo:hi, :] = a
    u_s[lo:hi, :] = u


def _rglru_scan(gate_ref, out_ref, a_s, u_s, h_s, p_s, hlast, o_scr):
    nstep = TS // SCAN_CHUNKS
    nslab = D_RNN // LANE
    gate = jnp.concatenate([gate_ref[l] for l in range(nslab)], axis=1)

    def body(j, carry):
        h, p = carry
        blk = pl.multiple_of(j * SCAN_CHUNKS, SCAN_CHUNKS)
        a_j = a_s[pl.ds(blk, SCAN_CHUNKS), :]
        h = a_j * h + u_s[pl.ds(blk, SCAN_CHUNKS), :]
        p = a_j * p
        h_s[pl.ds(blk, SCAN_CHUNKS), :] = h
        p_s[pl.ds(blk, SCAN_CHUNKS), :] = p
        return h, p

    h_end, p_end = lax.fori_loop(
        0, nstep, body,
        (jnp.zeros((SCAN_CHUNKS, D_RNN), F32), jnp.ones((SCAN_CHUNKS, D_RNN), F32)), unroll=8)
    carry = hlast[...]
    h_in = []
    for c in range(SCAN_CHUNKS):
        h_in.append(carry)
        carry = h_end[c:c + 1, :] + p_end[c:c + 1, :] * carry
    hlast[...] = carry
    h_in = jnp.concatenate([jnp.concatenate(h_in, axis=0)] * nstep, axis=0)
    o = (h_s[...] + p_s[...] * h_in) * _silu(gate)

    for l in range(nslab):
        o_scr[l] = o[:, l * LANE:(l + 1) * LANE]
        for c in range(SCAN_CHUNKS):
            out_ref[c * nstep:(c + 1) * nstep, l * LANE:(l + 1) * LANE] = (
                o_scr[l, pl.ds(c, nstep, stride=SCAN_CHUNKS), :].astype(out_ref.dtype))


def _compress_kernel(x_ref, pe_ref, w1k_ref, w1v_ref, w2_ref, out_ref, wbig_ref):
    nrow = x_ref.shape[1]
    nchunk = nrow // CB
    nh = N_KV_HEADS * CMP_HIDDEN

    for which, w1_ref in enumerate((w1k_ref, w1v_ref)):
        @pl.when((pl.program_id(0) == which) & (pl.program_id(1) == 0))
        def _(w1_ref=w1_ref):
            wbig_ref[...] = jnp.zeros(wbig_ref.shape, BF16)
            for half in range(2):
                for l in range(CMP_STRIDE):
                    r0 = (half * CMP_STRIDE + l) * HEAD_DIM
                    blk = w1_ref[r0:r0 + HEAD_DIM, :].astype(BF16)
                    for g in range(N_KV_HEADS):
                        rows = slice(l * D_KV + g * HEAD_DIM, l * D_KV + (g + 1) * HEAD_DIM)
                        cols = slice(half * nh + g * CMP_HIDDEN, half * nh + (g + 1) * CMP_HIDDEN)
                        wbig_ref[rows, cols] = blk

    x = x_ref[0]
    xa = (x + pe_ref[0, 0:1, :]).astype(BF16)
    xb = (x + pe_ref[0, 1:2, :]).astype(BF16)
    first = jnp.dot(xa, wbig_ref[:, :nh], preferred_element_type=F32)
    second = jnp.dot(xb, wbig_ref[:, nh:], preferred_element_type=F32)
    hid = _silu(first + pltpu.roll(second, nrow - 1, 0))
    out = jnp.dot(hid.astype(BF16), w2_ref[0], preferred_element_type=F32)
    c_idx = lax.broadcasted_iota(jnp.int32, out.shape, 0) & (nchunk - 1)
    out_ref[0] = jnp.where(c_idx < nchunk - 1, out, 0.0)


def _compress(xc, pe2, w1k, w1v, w2bd, B, S):
    nchunk = S // CMP_STRIDE
    nrow = CB * nchunk
    width = CMP_STRIDE * D_KV
    kv = lambda i, j: (i, 0, 0)
    return pl.pallas_call(
        _compress_kernel,
        grid=(2, B // CB),
        in_specs=[
            pl.BlockSpec((1, nrow, width), lambda i, j: (i, j, 0)),
            pl.BlockSpec((1, 2, width), kv),
            pl.BlockSpec(w1k.shape, lambda i, j: (0, 0)),
            pl.BlockSpec(w1v.shape, lambda i, j: (0, 0)),
            pl.BlockSpec((1,) + w2bd.shape[1:], kv),
        ],
        out_specs=pl.BlockSpec((1, nrow, D_KV), lambda i, j: (i, j, 0)),
        out_shape=jax.ShapeDtypeStruct((2, B * nchunk, D_KV), F32),
        scratch_shapes=[pltpu.VMEM((width, 2 * N_KV_HEADS * CMP_HIDDEN), BF16)],
        compiler_params=pltpu.CompilerParams(
            dimension_semantics=("arbitrary", "arbitrary"), vmem_limit_bytes=VMEM_LIMIT),
        name="compress",
    )(xc, pe2, w1k, w1v, w2bd)


def _nsa_kernel(qt_ref, cmp_ref, ksel_ref, vselt_ref, kwin_ref, vwint_ref, brgt_ref, ag_ref,
                ovt_ref, out_ref, qpad_ref, pen_ref, o_ref, m_ref, acc_ref, s_ref):
    qi = pl.program_id(1)
    q0 = qi * TQ
    nl = Q_PER_KV * TQ
    nblk = pen_ref.shape[1]
    ncmp = cmp_ref.shape[1]
    groups = range(N_KV_HEADS)
    gsl = [slice(g * HEAD_DIM, (g + 1) * HEAD_DIM) for g in groups]

    def tile4(a):
        return jnp.concatenate([a] * Q_PER_KV, axis=1)

    kc = cmp_ref[0].astype(BF16)
    s_cmp = []
    for g in groups:
        zpad = jnp.zeros((HEAD_DIM, TQ), BF16)
        cols = []
        for r in range(Q_PER_KV):
            hq = g * Q_PER_KV + r
            qh = qt_ref[hq * HEAD_DIM:(hq + 1) * HEAD_DIM, :]
            cols.append(jnp.concatenate([qh, zpad] if g == 0 else [zpad, qh], axis=0))
        qpad = jnp.concatenate(cols, axis=1)
        qpad_ref[g] = qpad
        s_cmp.append(jnp.dot(kc, qpad, preferred_element_type=F32))

    def sel_scores(kt, slot):
        k_tile = ksel_ref[pl.ds(pl.multiple_of(kt * KS, KS), KS), :]
        for g in groups:
            s_ref[slot, g] = jnp.dot(k_tile, qpad_ref[g], preferred_element_type=F32)

    n_full = q0 // KS

    ones = jnp.ones((ONES_ROWS, VC), BF16)

    def vt_aug(v_ref, c0, n, g):
        vt = jnp.concatenate([v_ref[c0 + c, gsl[g], :] for c in range(n)], axis=1)
        return jnp.concatenate([vt, jnp.concatenate([ones] * n, axis=1)], axis=0)

    def normalise(res):
        return res[0:HEAD_DIM, :] * (1.0 / res[HEAD_DIM:HEAD_DIM + 1, :])

    c_row = lax.broadcasted_iota(jnp.int32, (ncmp, TQ), 0)
    t_lane = q0 + lax.broadcasted_iota(jnp.int32, (ncmp, TQ), 1)
    cpen = tile4(jnp.where(c_row * CMP_STRIDE + (CMP_BLOCK - 1) <= t_lane, 0.0, NEG))
    vc_t = cmp_ref[1].T
    for g in groups:
        vct = vc_t[gsl[g], :].astype(BF16)
        s = s_cmp[g] + cpen
        m = jnp.max(s, axis=0, keepdims=True)
        m = jnp.where(m > 0.5 * NEG, m, 0.0)
        e = jnp.exp2(s - m)
        den = jnp.sum(e, axis=0, keepdims=True)
        p = e * (1.0 / jnp.where(den > 0.0, den, 1.0))
        o_ref[0, g] = jnp.dot(vct, p.astype(BF16), preferred_element_type=F32)

        psum = p[:, 0:TQ]
        for r in range(1, Q_PER_KV):
            psum = psum + p[:, r * TQ:(r + 1) * TQ]
        imp = jnp.dot(ovt_ref[...], psum, preferred_element_type=F32,
                      precision=lax.Precision.HIGHEST)
        j_row = lax.broadcasted_iota(jnp.int32, (nblk, TQ), 0)
        t_blk = q0 + lax.broadcasted_iota(jnp.int32, (nblk, TQ), 1)
        cur = jnp.right_shift(t_blk, SEL_BLOCK.bit_length() - 1)
        forced = (j_row == 0) | (j_row == cur) | (j_row == cur - 1)
        val = jnp.where(forced, SEL_FORCE, imp)
        val = jnp.where(j_row * SEL_BLOCK <= t_blk, val, -SEL_FORCE)
        rank = jnp.zeros((nblk, TQ), F32)
        for i in range(nblk):
            vi = jnp.broadcast_to(val[i:i + 1, :], (nblk, TQ))
            rank = rank + jnp.where(j_row > i, jnp.where(vi >= val, 1.0, 0.0),
                                    jnp.where(vi > val, 1.0, 0.0))
        chosen = (rank < float(min(SEL_TOPK, nblk))) & (val > -0.5 * SEL_FORCE)
        selpen = jnp.where(chosen, 0.0, NEG)
        for j in range(nblk):
            pen_ref[g, j] = jnp.broadcast_to(selpen[j:j + 1, :], (8, TQ))

    wrows = WINDOW + TW
    for sub in range(TQ // TW):
        qs = q0 + sub * TW
        w0 = pl.multiple_of(jnp.maximum(qs - WINDOW, 0), TW)
        k_win = kwin_ref[pl.ds(w0, wrows), :]
        key = w0 + lax.broadcasted_iota(jnp.int32, (wrows, TW), 0)
        t_q = qs + lax.broadcasted_iota(jnp.int32, (wrows, TW), 1)
        wpen = tile4(jnp.where(key <= t_q, jnp.where(key > t_q - WINDOW, 0.0, NEG), NEG))
        cols = [slice(r * TQ + sub * TW, r * TQ + (sub + 1) * TW) for r in range(Q_PER_KV)]
        p_win = []
        for g in groups:
            q_sub = jnp.concatenate([qpad_ref[g, :, c] for c in cols], axis=1)
            s = jnp.dot(k_win, q_sub, preferred_element_type=F32) + wpen
            m = jnp.max(s, axis=0, keepdims=True)
            p_win.append(jnp.exp2(s - m).astype(BF16))
        if sub == 0:
            sel_scores(0, n_full & 1)
        for g in groups:
            o_win = normalise(jnp.dot(vt_aug(vwint_ref, w0 // VC, wrows // VC, g), p_win[g],
                                      preferred_element_type=F32))
            for r, c in enumerate(cols):
                o_ref[2, g, :, c] = o_win[:, r * TW:(r + 1) * TW]

    m_ref[...] = jnp.full(m_ref.shape, NEG, F32)
    acc_ref[...] = jnp.zeros(acc_ref.shape, F32)
    bpt = KS // SEL_BLOCK
    cpt = KS // VC

    def sel_update(kt, slot, diag_keys=0):
        nkeys = diag_keys or KS
        probs, alphas = [], []
        for g in groups:
            m_prev = m_ref[g]
            if diag_keys:
                rows = [jnp.concatenate([pen_ref[g, kt * bpt + jj]] * (SEL_BLOCK // 8), axis=0)
                        for jj in range(nkeys // SEL_BLOCK)]
                pen = jnp.concatenate(rows, axis=0)
                key = kt * KS + lax.broadcasted_iota(jnp.int32, (nkeys, TQ), 0)
                t_q = q0 + lax.broadcasted_iota(jnp.int32, (nkeys, TQ), 1)
                s = s_ref[slot, g, 0:nkeys, :] + tile4(pen + jnp.where(key <= t_q, 0.0, NEG))
                m_new = jnp.maximum(m_prev, jnp.max(s, axis=0, keepdims=True))
                p = jnp.exp2(s - m_new)
            else:
                sb = [s_ref[slot, g, jj * SEL_BLOCK:(jj + 1) * SEL_BLOCK, :] for jj in range(bpt)]
                bias = [tile4(pen_ref[g, kt * bpt + jj][0:1, :]) for jj in range(bpt)]
                m_new = m_prev
                for jj in range(bpt):
                    m_new = jnp.maximum(m_new, jnp.max(sb[jj], axis=0, keepdims=True) + bias[jj])
                p = jnp.concatenate([jnp.exp2(sb[jj] + (bias[jj] - m_new)) for jj in range(bpt)],
                                    axis=0)
            probs.append(p.astype(BF16))
            alphas.append(jnp.exp2(m_prev - m_new))
            m_ref[g] = m_new
        for g in groups:
            acc_ref[g] = alphas[g] * acc_ref[g] + jnp.dot(
                vt_aug(vselt_ref, kt * cpt, nkeys // VC, g), probs[g], preferred_element_type=F32)

    def sel_body(kt, carry):
        for slot in (0, 1):
            @pl.when(((n_full - kt) & 1) == slot)
            def _(slot=slot):
                sel_scores(kt + 1, 1 - slot)
                sel_update(kt, slot)
        return carry

    lax.fori_loop(0, n_full, sel_body, 0)
    for nkeys in range(TQ, KS + 1, TQ):
        @pl.when(q0 + TQ - n_full * KS == nkeys)
        def _(nkeys=nkeys):
            sel_update(n_full, 0, diag_keys=nkeys)
    for g in groups:
        o_ref[1, g] = normalise(acc_ref[g])

    gates = jax.nn.sigmoid(brgt_ref[...])
    group_out = []
    for g in groups:
        heads = []
        for r in range(Q_PER_KV):
            hq = g * Q_PER_KV + r
            ls = slice(r * TQ, (r + 1) * TQ)
            o = gates[hq:hq + 1, :] * o_ref[0, g, :, ls]
            for n in range(1, N_BRANCH):
                o = o + gates[n * N_Q_HEADS + hq:n * N_Q_HEADS + hq + 1, :] * o_ref[n, g, :, ls]
            heads.append(o)
        group_out.append(jnp.concatenate(heads, axis=0).T)

    attn = jnp.concatenate(group_out, axis=1)
    out_ref[...] = (attn * _silu(ag_ref[...])).astype(out_ref.dtype)


def _nsa(qt, cmp_kv, ksel, vselt, kwin, vwint, brgt, ag, ovt, B, S):
    T = B * S
    nq = S // TQ
    qpm = TM // TQ
    nblk = S // SEL_BLOCK
    ncmp = S // CMP_STRIDE
    nl = Q_PER_KV * TQ
    return pl.pallas_call(
        _nsa_kernel,
        grid=(B, nq),
        in_specs=[
            pl.BlockSpec((None, None, D_ATTN, TQ), lambda b, q: (b, q // qpm, 0, q % qpm)),
            pl.BlockSpec((2, ncmp, D_KV), lambda b, q: (0, b, 0)),
            pl.BlockSpec((S, D_KV), lambda b, q: (b, 0)),
            pl.BlockSpec((None, S // VC, D_KV, VC), lambda b, q: (b, 0, 0, 0)),
            pl.BlockSpec((S, D_KV), lambda b, q: (b, 0)),
            pl.BlockSpec((None, S // VC, D_KV, VC), lambda b, q: (b, 0, 0, 0)),
            pl.BlockSpec((None, None, N_GATE_ROWS, TQ), lambda b, q: (b, q // qpm, 0, q % qpm)),
            pl.BlockSpec((TQ, D_ATTN), lambda b, q: (b * nq + q, 0)),
            pl.BlockSpec(ovt.shape, lambda b, q: (0, 0)),
        ],
        out_specs=pl.BlockSpec((TQ, D_ATTN), lambda b, q: (b * nq + q, 0)),
        out_shape=jax.ShapeDtypeStruct((T, D_ATTN), BF16),
        scratch_shapes=[
            pltpu.VMEM((N_KV_HEADS, D_KV, nl), BF16),
            pltpu.VMEM((N_KV_HEADS, nblk, 8, TQ), F32),
            pltpu.VMEM((N_BRANCH, N_KV_HEADS, HEAD_DIM, nl), F32),
            pltpu.VMEM((N_KV_HEADS, 1, nl), F32),
            pltpu.VMEM((N_KV_HEADS, HEAD_DIM + ONES_ROWS, nl), F32),
            pltpu.VMEM((2, N_KV_HEADS, KS, nl), F32),
        ],
        compiler_params=pltpu.CompilerParams(
            dimension_semantics=("arbitrary", "arbitrary"), vmem_limit_bytes=VMEM_LIMIT),
        name="nsa",
    )(qt, cmp_kv, ksel, vselt, kwin, vwint, brgt, ag, ovt)


def _outproj_kernel(x_ref, rnn_ref, attn_ref, wo_ref, nfw_ref, out_ref, wo_bf, *, final_norm):
    @pl.when(pl.program_id(0) == 0)
    def _():
        wo_bf[...] = wo_ref[...].astype(BF16)

    rows = TO // OUT_ROW_CHUNKS
    for c in range(OUT_ROW_CHUNKS):
        rs = slice(c * rows, (c + 1) * rows)
        y = x_ref[rs, :]
        y = y + jnp.dot(rnn_ref[rs, :], wo_bf[0:D_RNN, :], preferred_element_type=F32)
        y = y + jnp.dot(attn_ref[rs, :], wo_bf[D_RNN:D_MIX, :], preferred_element_type=F32)
        if final_norm:
            ms = jnp.mean(y * y, axis=-1, keepdims=True)
            y = (y * lax.rsqrt(ms + EPS)) * nfw_ref[...]
        out_ref[rs, :] = y


def _outproj(x2, rnn_out, attn_out, wo, nfw, final_norm):
    T = x2.shape[0]
    row = lambda i: (i, 0)
    return pl.pallas_call(
        functools.partial(_outproj_kernel, final_norm=final_norm),
        grid=(T // TO,),
        in_specs=[
            pl.BlockSpec((TO, D_MODEL), row),
            pl.BlockSpec((TO, D_RNN), row),
            pl.BlockSpec((TO, D_ATTN), row),
            pl.BlockSpec((D_MIX, D_MODEL), lambda i: (0, 0)),
            pl.BlockSpec((1, D_MODEL), lambda i: (0, 0)),
        ],
        out_specs=pl.BlockSpec((TO, D_MODEL), row),
        out_shape=jax.ShapeDtypeStruct((T, D_MODEL), F32),
        scratch_shapes=[pltpu.VMEM((D_MIX, D_MODEL), BF16)],
        compiler_params=pltpu.CompilerParams(
            dimension_semantics=("arbitrary",), vmem_limit_bytes=VMEM_LIMIT),
        name="outproj",
    )(x2, rnn_out, attn_out, wo, nfw)


def _block_diag_halves(wa, wx):
    eye = jnp.eye(RNN_HEADS, dtype=wa.dtype)
    full = lambda w: jnp.einsum('hij,hk->hikj', w, eye).reshape(D_RNN, D_RNN)
    fa, fx = full(wa), full(wx)
    half = D_RNN // 2
    return jnp.stack([
        jnp.concatenate([fa[s:s + half, s:s + half], fx[s:s + half, s:s + half]], axis=1)
        for s in (0, half)]).astype(BF16)


def _compress_weights(pe, w2):
    eye = jnp.eye(N_KV_HEADS, dtype=w2.dtype)
    w2bd = jnp.einsum('nd,ge->gned', w2, eye).reshape(N_KV_HEADS * CMP_HIDDEN, D_KV)
    per = pe.reshape(2, CMP_STRIDE, 1, HEAD_DIM)
    pe2 = jnp.broadcast_to(per, (2, CMP_STRIDE, N_KV_HEADS, HEAD_DIM)).reshape(2, CMP_STRIDE * D_KV)
    return pe2, w2bd.astype(BF16)


def _overlap_t(ncmp_pad, nblk):
    cs = np.arange(ncmp_pad)[None, :] * CMP_STRIDE
    ss = np.arange(nblk)[:, None] * SEL_BLOCK
    ov = np.clip(np.minimum(cs + CMP_BLOCK, ss + SEL_BLOCK) - np.maximum(cs, ss), 0, None)
    return jnp.asarray(ov.astype(np.float32) / CMP_BLOCK)


def kernel(x, norm1_w, w_in, conv_w, conv_b, rg_wa, rg_ba, rg_wx, rg_bx, rg_lambda,
           cmp_k_pe, cmp_k_w1, cmp_k_w2, cmp_v_pe, cmp_v_w1, cmp_v_w2, w_out, normf_w):
    B, S, D = x.shape
    assert D == D_MODEL and w_in.shape[-1] == _D_IN
    assert S % TM == 0 and TS == TM and B % CB == 0
    assert TQ % TW == 0 and TW % VC == 0 and WINDOW % TW == 0 and KS % TQ == 0 and TM % TQ == 0
    assert S % KS == 0 and S >= WINDOW + TQ and (B * S) % TO == 0
    depth = w_in.shape[0]
    T = B * S
    ovt = _overlap_t(S // CMP_STRIDE, S // SEL_BLOCK)
    x2 = x.reshape(T, D)
    for l in range(depth):
        wn = w_in[l].astype(BF16)
        wb = jnp.pad(w_in[l][:, _O_BR:], ((0, 0), (0, _O_BR + LANE - _D_IN))).astype(BF16)
        (rnn_out, kvc, ksel, kwin, ag, qt, vselt, vwint, brgt) = _inproj(
            x2, norm1_w[l].reshape(1, D), wn, wb,
            conv_w[l], conv_b[l].reshape(1, D_RNN), _block_diag_halves(rg_wa[l], rg_wx[l]),
            rg_ba[l].reshape(1, D_RNN), rg_bx[l].reshape(1, D_RNN),
            rg_lambda[l].reshape(1, D_RNN), B, S)

        pk, w2k = _compress_weights(cmp_k_pe[l], cmp_k_w2[l])
        pv, w2v = _compress_weights(cmp_v_pe[l], cmp_v_w2[l])
        cmp_kv = _compress(kvc, jnp.stack([pk, pv]), cmp_k_w1[l], cmp_v_w1[l],
                           jnp.stack([w2k, w2v]), B, S)

        attn_out = _nsa(qt, cmp_kv, ksel, vselt, kwin, vwint, brgt, ag, ovt, B, S)

        x2 = _outproj(x2, rnn_out, attn_out, w_out[l],
                      normf_w.reshape(1, D), final_norm=(l == depth - 1))
    return x2.reshape(B, S, D)
```

```python
import functools

import numpy as np
import jax
import jax.numpy as jnp
from jax import lax
from jax.experimental import pallas as pl
from jax.experimental.pallas import tpu as pltpu

F32 = jnp.float32
BF16 = jnp.bfloat16

D_MODEL = 1024
EPS = 1e-6
D_RNN = 512
RNN_HEADS = 8
RNN_HEAD_DIM = D_RNN // RNN_HEADS
CONV_WIDTH = 4
LRU_C = 8.0
N_Q_HEADS = 8
N_KV_HEADS = 2
HEAD_DIM = 64
Q_PER_KV = N_Q_HEADS // N_KV_HEADS
D_ATTN = N_Q_HEADS * HEAD_DIM
D_KV = N_KV_HEADS * HEAD_DIM
CMP_BLOCK = 32
CMP_STRIDE = 16
CMP_HIDDEN = 256
SEL_BLOCK = 64
SEL_TOPK = 8
SEL_FORCE = 1e9
WINDOW = 512
N_BRANCH = 3
D_MIX = D_RNN + D_ATTN
N_GATE_ROWS = 32

LANE = 128
SCAN_CHUNKS = 8

TM = 1024
TS = 1024
TO = 1024
OUT_ROW_CHUNKS = 4
TQ = 256
KS = 512
VC = 128
TW = 128
CB = 4
NEG = -1e30
LOG2E = 1.4426950408889634
ONES_ROWS = 16
VMEM_V7X = 64 * 1024 * 1024
VMEM_LIMIT = VMEM_V7X - 8 * 1024 * 1024

_O_RX, _O_RG, _O_Q = 0, D_RNN, 2 * D_RNN
_O_KC = _O_Q + D_ATTN
_O_VC = _O_KC + D_KV
_O_KS = _O_VC + D_KV
_O_VS = _O_KS + D_KV
_O_KW = _O_VS + D_KV
_O_VW = _O_KW + D_KV
_O_AG = _O_VW + D_KV
_O_BR = _O_AG + D_ATTN
_D_IN = _O_BR + N_BRANCH * N_Q_HEADS


def _silu(x):
    return x * jax.nn.sigmoid(x)


def _inproj_kernel(x_ref, nw_ref, wn_ref, wb_ref, cw_ref, cb_ref, wg_ref, ba_ref, bx_ref, lam_ref,
                   rnn_ref, kvc_ref, ksel_ref, kwin_ref, ag_ref,
                   qt_ref, vselt_ref, vwint_ref, brgt_ref,
                   kv_scr, rx_scr, rg_scr, tail_ref, a_s, u_s, h_s, p_s, hlast, o_scr):
    @pl.when(pl.program_id(1) == 0)
    def _():
        tail_ref[...] = jnp.zeros(tail_ref.shape, F32)
        hlast[...] = jnp.zeros(hlast.shape, F32)

    x = x_ref[...]
    ms = jnp.mean(x * x, axis=-1, keepdims=True)
    h = ((x * lax.rsqrt(ms + EPS)) * nw_ref[...]).astype(BF16)

    def nat(a, b):
        return jnp.dot(h, wn_ref[:, a:b], preferred_element_type=F32)

    def store_step_major(dst_ref, val):
        steps = TM // SCAN_CHUNKS
        for l in range(D_RNN // LANE):
            for c in range(SCAN_CHUNKS):
                dst_ref[l, pl.ds(c, steps, stride=SCAN_CHUNKS), :] = (
                    val[c * steps:(c + 1) * steps, l * LANE:(l + 1) * LANE])

    store_step_major(rx_scr, nat(_O_RX, _O_RG))
    store_step_major(rg_scr, nat(_O_RG, _O_Q))

    def proj_cmp():
        kv = nat(_O_KC, _O_KS)
        for a in range(2):
            kv_scr[a] = kv[:, a * D_KV:(a + 1) * D_KV]
            for l in range(CMP_STRIDE):
                kvc_ref[a, :, l * D_KV:(l + 1) * D_KV] = (
                    kv_scr[a, pl.ds(l, TM // CMP_STRIDE, stride=CMP_STRIDE), :])

    def proj_kv(lo, k_ref, vt_ref):
        kv = nat(lo, lo + 2 * D_KV)
        k_ref[...] = kv[:, :D_KV].astype(BF16)
        vt = kv[:, D_KV:].T.astype(BF16)
        for c in range(TM // VC):
            vt_ref[c] = vt[:, c * VC:(c + 1) * VC]

    def proj_gates():
        ag_ref[...] = nat(_O_AG, _O_BR)
        brg = jnp.dot(h, wb_ref[...], preferred_element_type=F32)
        brgt_ref[...] = brg.T[0:N_GATE_ROWS, :]

    def proj_q():
        qt_ref[...] = (nat(_O_Q, _O_KC) * (HEAD_DIM ** -0.5 * LOG2E)).T.astype(BF16)

    projections = ((proj_gates,), (proj_q,),
                   (functools.partial(proj_kv, _O_KS, ksel_ref, vselt_ref),
                    functools.partial(proj_kv, _O_KW, kwin_ref, vwint_ref)),
                   (proj_cmp,))
    wraps = _rglru_wraps(rx_scr, tail_ref)
    rows = TM // len(projections)
    for ci, projs in enumerate(projections):
        _rglru_gates(rx_scr, wraps, ci * rows, (ci + 1) * rows,
                     cw_ref, cb_ref, wg_ref, ba_ref, bx_ref, lam_ref, a_s, u_s)
        for proj in projs:
            proj()

    _rglru_scan(rg_scr, rnn_ref, a_s, u_s, h_s, p_s, hlast, o_scr)


def _inproj(x2, nw, wn, wb, cw, cb, wg, ba, bx, lam, B, S):
    T = B * S
    ns = S // TM
    grid = (B, ns)
    row = lambda b, s: (b * ns + s, 0)
    const2 = lambda b, s: (0, 0)
    out_shape = (
        jax.ShapeDtypeStruct((T, D_RNN), BF16),
        jax.ShapeDtypeStruct((2, T // CMP_STRIDE, CMP_STRIDE * D_KV), F32),
        jax.ShapeDtypeStruct((T, D_KV), BF16),
        jax.ShapeDtypeStruct((T, D_KV), BF16),
        jax.ShapeDtypeStruct((T, D_ATTN), F32),
        jax.ShapeDtypeStruct((B, ns, D_ATTN, TM), BF16),
        jax.ShapeDtypeStruct((B, S // VC, D_KV, VC), BF16),
        jax.ShapeDtypeStruct((B, S // VC, D_KV, VC), BF16),
        jax.ShapeDtypeStruct((B, ns, N_GATE_ROWS, TM), F32),
    )
    out_specs = (
        pl.BlockSpec((TM, D_RNN), row),
        pl.BlockSpec((2, TM // CMP_STRIDE, CMP_STRIDE * D_KV), lambda b, s: (0, b * ns + s, 0)),
        pl.BlockSpec((TM, D_KV), row),
        pl.BlockSpec((TM, D_KV), row),
        pl.BlockSpec((TM, D_ATTN), row),
        pl.BlockSpec((None, None, D_ATTN, TM), lambda b, s: (b, s, 0, 0)),
        pl.BlockSpec((None, TM // VC, D_KV, VC), lambda b, s: (b, s, 0, 0)),
        pl.BlockSpec((None, TM // VC, D_KV, VC), lambda b, s: (b, s, 0, 0)),
        pl.BlockSpec((None, None, N_GATE_ROWS, TM), lambda b, s: (b, s, 0, 0)),
    )
    return pl.pallas_call(
        _inproj_kernel,
        grid=grid,
        in_specs=[
            pl.BlockSpec((TM, D_MODEL), row),
            pl.BlockSpec((1, D_MODEL), const2),
            pl.BlockSpec(wn.shape, const2),
            pl.BlockSpec(wb.shape, const2),
            pl.BlockSpec((CONV_WIDTH, D_RNN), const2),
            pl.BlockSpec((1, D_RNN), const2),
            pl.BlockSpec(wg.shape, lambda b, s: (0, 0, 0)),
            pl.BlockSpec((1, D_RNN), const2),
            pl.BlockSpec((1, D_RNN), const2),
            pl.BlockSpec((1, D_RNN), const2),
        ],
        out_specs=out_specs,
        out_shape=out_shape,
        scratch_shapes=[
            pltpu.VMEM((2, TM, D_KV), F32),
            pltpu.VMEM((D_RNN // LANE, TM, LANE), F32),
            pltpu.VMEM((D_RNN // LANE, TM, LANE), F32),
            pltpu.VMEM(((CONV_WIDTH - 1) * SCAN_CHUNKS, D_RNN), F32),
            pltpu.VMEM((TM, D_RNN), F32),
            pltpu.VMEM((TM, D_RNN), F32),
            pltpu.VMEM((TM, D_RNN), F32),
            pltpu.VMEM((TM, D_RNN), F32),
            pltpu.VMEM((1, D_RNN), F32),
            pltpu.VMEM((D_RNN // LANE, TM, LANE), F32),
        ],
        compiler_params=pltpu.CompilerParams(
            dimension_semantics=("arbitrary", "arbitrary"), vmem_limit_bytes=VMEM_LIMIT),
        name="inproj",
    )(x2, nw, wn, wb, cw, cb, wg, ba, bx, lam)


def _slab_rows(x_ref, lo, hi):
    return jnp.concatenate([x_ref[l, lo:hi, :] for l in range(D_RNN // LANE)], axis=1)


def _rglru_wraps(x_ref, tail_ref):
    ntail = (CONV_WIDTH - 1) * SCAN_CHUNKS
    last = _slab_rows(x_ref, TS - ntail, TS)
    sub = lax.broadcasted_iota(jnp.int32, (SCAN_CHUNKS, D_RNN), 0)
    wraps = []
    for j in range(CONV_WIDTH - 1):
        rows = slice(j * SCAN_CHUNKS, (j + 1) * SCAN_CHUNKS)
        cur = pltpu.roll(last[rows, :], 1, 0)
        prev = pltpu.roll(tail_ref[rows, :], 1, 0)
        wraps.append(jnp.where(sub == 0, prev, cur))
    tail_ref[...] = last
    return wraps


def _rglru_gates(x_ref, wraps, lo, hi, cw_ref, cb_ref, wg_ref, ba_ref, bx_ref, lam_ref, a_s, u_s):
    y = cb_ref[...]
    for k in range(CONV_WIDTH):
        off = (CONV_WIDTH - 1 - k) * SCAN_CHUNKS
        if lo >= off:
            xs = _slab_rows(x_ref, lo - off, hi - off)
        else:
            assert lo == 0
            xs = jnp.concatenate(wraps[len(wraps) - off // SCAN_CHUNKS:]
                                 + [_slab_rows(x_ref, 0, hi - off)], axis=0)
        y = y + xs * cw_ref[k:k + 1, :]

    yb = y.astype(BF16)
    half = D_RNN // 2
    pre = [jnp.dot(yb[:, hh * half:(hh + 1) * half], wg_ref[hh], preferred_element_type=F32)
           for hh in range(2)]
    pre_a = jnp.concatenate([pre[0][:, :half], pre[1][:, :half]], axis=1)
    pre_x = jnp.concatenate([pre[0][:, half:], pre[1][:, half:]], axis=1)
    r = jax.nn.sigmoid(pre_a + ba_ref[...])
    i = jax.nn.sigmoid(pre_x + bx_ref[...])
    lam = lam_ref[...]
    lsig = jnp.minimum(lam, 0.0) - jnp.log1p(jnp.exp(-jnp.abs(lam)))
    log_a = LRU_C * r * lsig
    a = jnp.exp(log_a)
    u = jnp.sqrt(1.0 - a * a) * (i * y)
    a_s[lo:hi, :] = a
    u_s[lo:hi, :] = u


def _rglru_scan(gate_ref, out_ref, a_s, u_s, h_s, p_s, hlast, o_scr):
    nstep = TS // SCAN_CHUNKS
    nslab = D_RNN // LANE
    gate = jnp.concatenate([gate_ref[l] for l in range(nslab)], axis=1)

    def body(j, carry):
        h, p = carry
        blk = pl.multiple_of(j * SCAN_CHUNKS, SCAN_CHUNKS)
        a_j = a_s[pl.ds(blk, SCAN_CHUNKS), :]
        h = a_j * h + u_s[pl.ds(blk, SCAN_CHUNKS), :]
        p = a_j * p
        h_s[pl.ds(blk, SCAN_CHUNKS), :] = h
        p_s[pl.ds(blk, SCAN_CHUNKS), :] = p
        return h, p

    h_end, p_end = lax.fori_loop(
        0, nstep, body,
        (jnp.zeros((SCAN_CHUNKS, D_RNN), F32), jnp.ones((SCAN_CHUNKS, D_RNN), F32)), unroll=8)
    carry = hlast[...]
    h_in = []
    for c in range(SCAN_CHUNKS):
        h_in.append(carry)
        carry = h_end[c:c + 1, :] + p_end[c:c + 1, :] * carry
    hlast[...] = carry
    h_in = jnp.concatenate([jnp.concatenate(h_in, axis=0)] * nstep, axis=0)
    o = (h_s[...] + p_s[...] * h_in) * _silu(gate)

    for l in range(nslab):
        o_scr[l] = o[:, l * LANE:(l + 1) * LANE]
        for c in range(SCAN_CHUNKS):
            out_ref[c * nstep:(c + 1) * nstep, l * LANE:(l + 1) * LANE] = (
                o_scr[l, pl.ds(c, nstep, stride=SCAN_CHUNKS), :].astype(out_ref.dtype))


def _compress_kernel(x_ref, pe_ref, w1k_ref, w1v_ref, w2_ref, out_ref, wbig_ref):
    nrow = x_ref.shape[1]
    nchunk = nrow // CB
    nh = N_KV_HEADS * CMP_HIDDEN

    for which, w1_ref in enumerate((w1k_ref, w1v_ref)):
        @pl.when((pl.program_id(0) == which) & (pl.program_id(1) == 0))
        def _(w1_ref=w1_ref):
            wbig_ref[...] = jnp.zeros(wbig_ref.shape, BF16)
            for half in range(2):
                for l in range(CMP_STRIDE):
                    r0 = (half * CMP_STRIDE + l) * HEAD_DIM
                    blk = w1_ref[r0:r0 + HEAD_DIM, :].astype(BF16)
                    for g in range(N_KV_HEADS):
                        rows = slice(l * D_KV + g * HEAD_DIM, l * D_KV + (g + 1) * HEAD_DIM)
                        cols = slice(half * nh + g * CMP_HIDDEN, half * nh + (g + 1) * CMP_HIDDEN)
                        wbig_ref[rows, cols] = blk

    x = x_ref[0]
    xa = (x + pe_ref[0, 0:1, :]).astype(BF16)
    xb = (x + pe_ref[0, 1:2, :]).astype(BF16)
    first = jnp.dot(xa, wbig_ref[:, :nh], preferred_element_type=F32)
    second = jnp.dot(xb, wbig_ref[:, nh:], preferred_element_type=F32)
    hid = _silu(first + pltpu.roll(second, nrow - 1, 0))
    out = jnp.dot(hid.astype(BF16), w2_ref[0], preferred_element_type=F32)
    c_idx = lax.broadcasted_iota(jnp.int32, out.shape, 0) & (nchunk - 1)
    out_ref[0] = jnp.where(c_idx < nchunk - 1, out, 0.0)


def _compress(xc, pe2, w1k, w1v, w2bd, B, S):
    nchunk = S // CMP_STRIDE
    nrow = CB * nchunk
    width = CMP_STRIDE * D_KV
    kv = lambda i, j: (i, 0, 0)
    return pl.pallas_call(
        _compress_kernel,
        grid=(2, B // CB),
        in_specs=[
            pl.BlockSpec((1, nrow, width), lambda i, j: (i, j, 0)),
            pl.BlockSpec((1, 2, width), kv),
            pl.BlockSpec(w1k.shape, lambda i, j: (0, 0)),
            pl.BlockSpec(w1v.shape, lambda i, j: (0, 0)),
            pl.BlockSpec((1,) + w2bd.shape[1:], kv),
        ],
        out_specs=pl.BlockSpec((1, nrow, D_KV), lambda i, j: (i, j, 0)),
        out_shape=jax.ShapeDtypeStruct((2, B * nchunk, D_KV), F32),
        scratch_shapes=[pltpu.VMEM((width, 2 * N_KV_HEADS * CMP_HIDDEN), BF16)],
        compiler_params=pltpu.CompilerParams(
            dimension_semantics=("arbitrary", "arbitrary"), vmem_limit_bytes=VMEM_LIMIT),
        name="compress",
    )(xc, pe2, w1k, w1v, w2bd)


def _nsa_kernel(qt_ref, cmp_ref, ksel_ref, vselt_ref, kwin_ref, vwint_ref, brgt_ref, ag_ref,
                ovt_ref, out_ref, qpad_ref, pen_ref, o_ref, m_ref, acc_ref, s_ref):
    qi = pl.program_id(1)
    q0 = qi * TQ
    nl = Q_PER_KV * TQ
    nblk = pen_ref.shape[1]
    ncmp = cmp_ref.shape[1]
    groups = range(N_KV_HEADS)
    gsl = [slice(g * HEAD_DIM, (g + 1) * HEAD_DIM) for g in groups]

    def tile4(a):
        return jnp.concatenate([a] * Q_PER_KV, axis=1)

    kc = cmp_ref[0].astype(BF16)
    s_cmp = []
    for g in groups:
        zpad = jnp.zeros((HEAD_DIM, TQ), BF16)
        cols = []
        for r in range(Q_PER_KV):
            hq = g * Q_PER_KV + r
            qh = qt_ref[hq * HEAD_DIM:(hq + 1) * HEAD_DIM, :]
            cols.append(jnp.concatenate([qh, zpad] if g == 0 else [zpad, qh], axis=0))
        qpad = jnp.concatenate(cols, axis=1)
        qpad_ref[g] = qpad
        s_cmp.append(jnp.dot(kc, qpad, preferred_element_type=F32))

    def sel_scores(kt, slot):
        k_tile = ksel_ref[pl.ds(pl.multiple_of(kt * KS, KS), KS), :]
        for g in groups:
            s_ref[slot, g] = jnp.dot(k_tile, qpad_ref[g], preferred_element_type=F32)

    n_full = q0 // KS

    ones = jnp.ones((ONES_ROWS, VC), BF16)

    def vt_aug(v_ref, c0, n, g):
        vt = jnp.concatenate([v_ref[c0 + c, gsl[g], :] for c in range(n)], axis=1)
        return jnp.concatenate([vt, jnp.concatenate([ones] * n, axis=1)], axis=0)

    def normalise(res):
        return res[0:HEAD_DIM, :] * (1.0 / res[HEAD_DIM:HEAD_DIM + 1, :])

    c_row = lax.broadcasted_iota(jnp.int32, (ncmp, TQ), 0)
    t_lane = q0 + lax.broadcasted_iota(jnp.int32, (ncmp, TQ), 1)
    cpen = tile4(jnp.where(c_row * CMP_STRIDE + (CMP_BLOCK - 1) <= t_lane, 0.0, NEG))
    vc_t = cmp_ref[1].T
    for g in groups:
        vct = vc_t[gsl[g], :].astype(BF16)
        s = s_cmp[g] + cpen
        m = jnp.max(s, axis=0, keepdims=True)
        m = jnp.where(m > 0.5 * NEG, m, 0.0)
        e = jnp.exp2(s - m)
        den = jnp.sum(e, axis=0, keepdims=True)
        p = e * (1.0 / jnp.where(den > 0.0, den, 1.0))
        o_ref[0, g] = jnp.dot(vct, p.astype(BF16), preferred_element_type=F32)

        psum = p[:, 0:TQ]
        for r in range(1, Q_PER_KV):
            psum = psum + p[:, r * TQ:(r + 1) * TQ]
        imp = jnp.dot(ovt_ref[...], psum, preferred_element_type=F32,
                      precision=lax.Precision.HIGHEST)
        j_row = lax.broadcasted_iota(jnp.int32, (nblk, TQ), 0)
        t_blk = q0 + lax.broadcasted_iota(jnp.int32, (nblk, TQ), 1)
        cur = jnp.right_shift(t_blk, SEL_BLOCK.bit_length() - 1)
        forced = (j_row == 0) | (j_row == cur) | (j_row == cur - 1)
        val = jnp.where(forced, SEL_FORCE, imp)
        val = jnp.where(j_row * SEL_BLOCK <= t_blk, val, -SEL_FORCE)
        rank = jnp.zeros((nblk, TQ), F32)
        for i in range(nblk):
            vi = jnp.broadcast_to(val[i:i + 1, :], (nblk, TQ))
            rank = rank + jnp.where(j_row > i, jnp.where(vi >= val, 1.0, 0.0),
                                    jnp.where(vi > val, 1.0, 0.0))
        chosen = (rank < float(min(SEL_TOPK, nblk))) & (val > -0.5 * SEL_FORCE)
        selpen = jnp.where(chosen, 0.0, NEG)
        for j in range(nblk):
            pen_ref[g, j] = jnp.broadcast_to(selpen[j:j + 1, :], (8, TQ))

    wrows = WINDOW + TW
    for sub in range(TQ // TW):
        qs = q0 + sub * TW
        w0 = pl.multiple_of(jnp.maximum(qs - WINDOW, 0), TW)
        k_win = kwin_ref[pl.ds(w0, wrows), :]
        key = w0 + lax.broadcasted_iota(jnp.int32, (wrows, TW), 0)
        t_q = qs + lax.broadcasted_iota(jnp.int32, (wrows, TW), 1)
        wpen = tile4(jnp.where(key <= t_q, jnp.where(key > t_q - WINDOW, 0.0, NEG), NEG))
        cols = [slice(r * TQ + sub * TW, r * TQ + (sub + 1) * TW) for r in range(Q_PER_KV)]
        p_win = []
        for g in groups:
            q_sub = jnp.concatenate([qpad_ref[g, :, c] for c in cols], axis=1)
            s = jnp.dot(k_win, q_sub, preferred_element_type=F32) + wpen
            m = jnp.max(s, axis=0, keepdims=True)
            p_win.append(jnp.exp2(s - m).astype(BF16))
        if sub == 0:
            sel_scores(0, n_full & 1)
        for g in groups:
            o_win = normalise(jnp.dot(vt_aug(vwint_ref, w0 // VC, wrows // VC, g), p_win[g],
                                      preferred_element_type=F32))
            for r, c in enumerate(cols):
                o_ref[2, g, :, c] = o_win[:, r * TW:(r + 1) * TW]

    m_ref[...] = jnp.full(m_ref.shape, NEG, F32)
    acc_ref[...] = jnp.zeros(acc_ref.shape, F32)
    bpt = KS // SEL_BLOCK
    cpt = KS // VC

    def sel_update(kt, slot, diag_keys=0):
        nkeys = diag_keys or KS
        probs, alphas = [], []
        for g in groups:
            m_prev = m_ref[g]
            if diag_keys:
                rows = [jnp.concatenate([pen_ref[g, kt * bpt + jj]] * (SEL_BLOCK // 8), axis=0)
                        for jj in range(nkeys // SEL_BLOCK)]
                pen = jnp.concatenate(rows, axis=0)
                key = kt * KS + lax.broadcasted_iota(jnp.int32, (nkeys, TQ), 0)
                t_q = q0 + lax.broadcasted_iota(jnp.int32, (nkeys, TQ), 1)
                s = s_ref[slot, g, 0:nkeys, :] + tile4(pen + jnp.where(key <= t_q, 0.0, NEG))
                m_new = jnp.maximum(m_prev, jnp.max(s, axis=0, keepdims=True))
                p = jnp.exp2(s - m_new)
            else:
                sb = [s_ref[slot, g, jj * SEL_BLOCK:(jj + 1) * SEL_BLOCK, :] for jj in range(bpt)]
                bias = [tile4(pen_ref[g, kt * bpt + jj][0:1, :]) for jj in range(bpt)]
                m_new = m_prev
                for jj in range(bpt):
                    m_new = jnp.maximum(m_new, jnp.max(sb[jj], axis=0, keepdims=True) + bias[jj])
                p = jnp.concatenate([jnp.exp2(sb[jj] + (bias[jj] - m_new)) for jj in range(bpt)],
                                    axis=0)
            probs.append(p.astype(BF16))
            alphas.append(jnp.exp2(m_prev - m_new))
            m_ref[g] = m_new
        for g in groups:
            acc_ref[g] = alphas[g] * acc_ref[g] + jnp.dot(
                vt_aug(vselt_ref, kt * cpt, nkeys // VC, g), probs[g], preferred_element_type=F32)

    def sel_body(kt, carry):
        for slot in (0, 1):
            @pl.when(((n_full - kt) & 1) == slot)
            def _(slot=slot):
                sel_scores(kt + 1, 1 - slot)
                sel_update(kt, slot)
        return carry

    lax.fori_loop(0, n_full, sel_body, 0)
    for nkeys in range(TQ, KS + 1, TQ):
        @pl.when(q0 + TQ - n_full * KS == nkeys)
        def _(nkeys=nkeys):
            sel_update(n_full, 0, diag_keys=nkeys)
    for g in groups:
        o_ref[1, g] = normalise(acc_ref[g])

    gates = jax.nn.sigmoid(brgt_ref[...])
    group_out = []
    for g in groups:
        heads = []
        for r in range(Q_PER_KV):
            hq = g * Q_PER_KV + r
            ls = slice(r * TQ, (r + 1) * TQ)
            o = gates[hq:hq + 1, :] * o_ref[0, g, :, ls]
            for n in range(1, N_BRANCH):
                o = o + gates[n * N_Q_HEADS + hq:n * N_Q_HEADS + hq + 1, :] * o_ref[n, g, :, ls]
            heads.append(o)
        group_out.append(jnp.concatenate(heads, axis=0).T)

    attn = jnp.concatenate(group_out, axis=1)
    out_ref[...] = (attn * _silu(ag_ref[...])).astype(out_ref.dtype)


def _nsa(qt, cmp_kv, ksel, vselt, kwin, vwint, brgt, ag, ovt, B, S):
    T = B * S
    nq = S // TQ
    qpm = TM // TQ
    nblk = S // SEL_BLOCK
    ncmp = S // CMP_STRIDE
    nl = Q_PER_KV * TQ
    return pl.pallas_call(
        _nsa_kernel,
        grid=(B, nq),
        in_specs=[
            pl.BlockSpec((None, None, D_ATTN, TQ), lambda b, q: (b, q // qpm, 0, q % qpm)),
            pl.BlockSpec((2, ncmp, D_KV), lambda b, q: (0, b, 0)),
            pl.BlockSpec((S, D_KV), lambda b, q: (b, 0)),
            pl.BlockSpec((None, S // VC, D_KV, VC), lambda b, q: (b, 0, 0, 0)),
            pl.BlockSpec((S, D_KV), lambda b, q: (b, 0)),
            pl.BlockSpec((None, S // VC, D_KV, VC), lambda b, q: (b, 0, 0, 0)),
            pl.BlockSpec((None, None, N_GATE_ROWS, TQ), lambda b, q: (b, q // qpm, 0, q % qpm)),
            pl.BlockSpec((TQ, D_ATTN), lambda b, q: (b * nq + q, 0)),
            pl.BlockSpec(ovt.shape, lambda b, q: (0, 0)),
        ],
        out_specs=pl.BlockSpec((TQ, D_ATTN), lambda b, q: (b * nq + q, 0)),
        out_shape=jax.ShapeDtypeStruct((T, D_ATTN), BF16),
        scratch_shapes=[
            pltpu.VMEM((N_KV_HEADS, D_KV, nl), BF16),
            pltpu.VMEM((N_KV_HEADS, nblk, 8, TQ), F32),
            pltpu.VMEM((N_BRANCH, N_KV_HEADS, HEAD_DIM, nl), F32),
            pltpu.VMEM((N_KV_HEADS, 1, nl), F32),
            pltpu.VMEM((N_KV_HEADS, HEAD_DIM + ONES_ROWS, nl), F32),
            pltpu.VMEM((2, N_KV_HEADS, KS, nl), F32),
        ],
        compiler_params=pltpu.CompilerParams(
            dimension_semantics=("arbitrary", "arbitrary"), vmem_limit_bytes=VMEM_LIMIT),
        name="nsa",
    )(qt, cmp_kv, ksel, vselt, kwin, vwint, brgt, ag, ovt)


def _outproj_kernel(x_ref, rnn_ref, attn_ref, wo_ref, nfw_ref, out_ref, wo_bf, *, final_norm):
    @pl.when(pl.program_id(0) == 0)
    def _():
        wo_bf[...] = wo_ref[...].astype(BF16)

    rows = TO // OUT_ROW_CHUNKS
    for c in range(OUT_ROW_CHUNKS):
        rs = slice(c * rows, (c + 1) * rows)
        mix = jnp.concatenate([rnn_ref[rs, :], attn_ref[rs, :]], axis=1)
        y = x_ref[rs, :] + jnp.dot(mix, wo_bf[...], preferred_element_type=F32)
        if final_norm:
            ms = jnp.mean(y * y, axis=-1, keepdims=True)
            y = (y * lax.rsqrt(ms + EPS)) * nfw_ref[...]
        out_ref[rs, :] = y


def _outproj(x2, rnn_out, attn_out, wo, nfw, final_norm):
    T = x2.shape[0]
    row = lambda i: (i, 0)
    return pl.pallas_call(
        functools.partial(_outproj_kernel, final_norm=final_norm),
        grid=(T // TO,),
        in_specs=[
            pl.BlockSpec((TO, D_MODEL), row),
            pl.BlockSpec((TO, D_RNN), row),
            pl.BlockSpec((TO, D_ATTN), row),
            pl.BlockSpec((D_MIX, D_MODEL), lambda i: (0, 0)),
            pl.BlockSpec((1, D_MODEL), lambda i: (0, 0)),
        ],
        out_specs=pl.BlockSpec((TO, D_MODEL), row),
        out_shape=jax.ShapeDtypeStruct((T, D_MODEL), F32),
        scratch_shapes=[pltpu.VMEM((D_MIX, D_MODEL), BF16)],
        compiler_params=pltpu.CompilerParams(
            dimension_semantics=("arbitrary",), vmem_limit_bytes=VMEM_LIMIT),
        name="outproj",
    )(x2, rnn_out, attn_out, wo, nfw)


def _block_diag_halves(wa, wx):
    eye = jnp.eye(RNN_HEADS, dtype=wa.dtype)
    full = lambda w: jnp.einsum('hij,hk->hikj', w, eye).reshape(D_RNN, D_RNN)
    fa, fx = full(wa), full(wx)
    half = D_RNN // 2
    return jnp.stack([
        jnp.concatenate([fa[s:s + half, s:s + half], fx[s:s + half, s:s + half]], axis=1)
        for s in (0, half)]).astype(BF16)


def _compress_weights(pe, w2):
    eye = jnp.eye(N_KV_HEADS, dtype=w2.dtype)
    w2bd = jnp.einsum('nd,ge->gned', w2, eye).reshape(N_KV_HEADS * CMP_HIDDEN, D_KV)
    per = pe.reshape(2, CMP_STRIDE, 1, HEAD_DIM)
    pe2 = jnp.broadcast_to(per, (2, CMP_STRIDE, N_KV_HEADS, HEAD_DIM)).reshape(2, CMP_STRIDE * D_KV)
    return pe2, w2bd.astype(BF16)


def _overlap_t(ncmp_pad, nblk):
    cs = np.arange(ncmp_pad)[None, :] * CMP_STRIDE
    ss = np.arange(nblk)[:, None] * SEL_BLOCK
    ov = np.clip(np.minimum(cs + CMP_BLOCK, ss + SEL_BLOCK) - np.maximum(cs, ss), 0, None)
    return jnp.asarray(ov.astype(np.float32) / CMP_BLOCK)


def kernel(x, norm1_w, w_in, conv_w, conv_b, rg_wa, rg_ba, rg_wx, rg_bx, rg_lambda,
           cmp_k_pe, cmp_k_w1, cmp_k_w2, cmp_v_pe, cmp_v_w1, cmp_v_w2, w_out, normf_w):
    B, S, D = x.shape
    assert D == D_MODEL and w_in.shape[-1] == _D_IN
    assert S % TM == 0 and TS == TM and B % CB == 0
    assert TQ % TW == 0 and TW % VC == 0 and WINDOW % TW == 0 and KS % TQ == 0 and TM % TQ == 0
    assert S % KS == 0 and S >= WINDOW + TQ and (B * S) % TO == 0
    depth = w_in.shape[0]
    T = B * S
    ovt = _overlap_t(S // CMP_STRIDE, S // SEL_BLOCK)
    x2 = x.reshape(T, D)
    for l in range(depth):
        wn = w_in[l].astype(BF16)
        wb = jnp.pad(w_in[l][:, _O_BR:], ((0, 0), (0, _O_BR + LANE - _D_IN))).astype(BF16)
        (rnn_out, kvc, ksel, kwin, ag, qt, vselt, vwint, brgt) = _inproj(
            x2, norm1_w[l].reshape(1, D), wn, wb,
            conv_w[l], conv_b[l].reshape(1, D_RNN), _block_diag_halves(rg_wa[l], rg_wx[l]),
            rg_ba[l].reshape(1, D_RNN), rg_bx[l].reshape(1, D_RNN),
            rg_lambda[l].reshape(1, D_RNN), B, S)

        pk, w2k = _compress_weights(cmp_k_pe[l], cmp_k_w2[l])
        pv, w2v = _compress_weights(cmp_v_pe[l], cmp_v_w2[l])
        cmp_kv = _compress(kvc, jnp.stack([pk, pv]), cmp_k_w1[l], cmp_v_w1[l],
                           jnp.stack([w2k, w2v]), B, S)

        attn_out = _nsa(qt, cmp_kv, ksel, vselt, kwin, vwint, brgt, ag, ovt, B, S)

        x2 = _outproj(x2, rnn_out, attn_out, w_out[l],
                      normf_w.reshape(1, D), final_norm=(l == depth - 1))
    return x2.reshape(B, S, D)
```

```python
import functools

import numpy as np
import jax
import jax.numpy as jnp
from jax import lax
from jax.experimental import pallas as pl
from jax.experimental.pallas import tpu as pltpu

F32 = jnp.float32
BF16 = jnp.bfloat16

D_MODEL = 1024
EPS = 1e-6
D_RNN = 512
RNN_HEADS = 8
RNN_HEAD_DIM = D_RNN // RNN_HEADS
CONV_WIDTH = 4
LRU_C = 8.0
N_Q_HEADS = 8
N_KV_HEADS = 2
HEAD_DIM = 64
Q_PER_KV = N_Q_HEADS // N_KV_HEADS
D_ATTN = N_Q_HEADS * HEAD_DIM
D_KV = N_KV_HEADS * HEAD_DIM
CMP_BLOCK = 32
CMP_STRIDE = 16
CMP_HIDDEN = 256
SEL_BLOCK = 64
SEL_TOPK = 8
SEL_FORCE = 1e9
WINDOW = 512
N_BRANCH = 3
D_MIX = D_RNN + D_ATTN
N_GATE_ROWS = 32

LANE = 128
SCAN_CHUNKS = 8

TM = 1024
TS = 1024
TO = 1024
OUT_ROW_CHUNKS = 4
TQ = 256
KS = 512
VC = 128
TW = 128
CB = 4
NEG = -1e30
LOG2E = 1.4426950408889634
ONES_ROWS = 16
VMEM_V7X = 64 * 1024 * 1024
VMEM_LIMIT = VMEM_V7X - 8 * 1024 * 1024

_O_RX, _O_RG, _O_Q = 0, D_RNN, 2 * D_RNN
_O_KC = _O_Q + D_ATTN
_O_VC = _O_KC + D_KV
_O_KS = _O_VC + D_KV
_O_VS = _O_KS + D_KV
_O_KW = _O_VS + D_KV
_O_VW = _O_KW + D_KV
_O_AG = _O_VW + D_KV
_O_BR = _O_AG + D_ATTN
_D_IN = _O_BR + N_BRANCH * N_Q_HEADS


def _silu(x):
    return x * jax.nn.sigmoid(x)


def _inproj_kernel(x_ref, nw_ref, w_ref, wb_ref, cw_ref, cb_ref, wg_ref, ba_ref, bx_ref, lam_ref,
                   rnn_ref, kvc_ref, ksel_ref, kwin_ref, ag_ref,
                   qt_ref, vselt_ref, vwint_ref, brgt_ref,
                   wn_ref, kv_scr, rx_scr, rg_scr, tail_ref, a_s, u_s, hlast, o_scr):
    @pl.when((pl.program_id(0) == 0) & (pl.program_id(1) == 0))
    def _():
        wn_ref[...] = w_ref[...].astype(BF16)

    @pl.when(pl.program_id(1) == 0)
    def _():
        tail_ref[...] = jnp.zeros(tail_ref.shape, F32)
        hlast[...] = jnp.zeros(hlast.shape, F32)

    x = x_ref[...]
    ms = jnp.mean(x * x, axis=-1, keepdims=True)
    h = ((x * lax.rsqrt(ms + EPS)) * nw_ref[...]).astype(BF16)

    def nat(a, b):
        return jnp.dot(h, wn_ref[:, a:b], preferred_element_type=F32)

    def store_step_major(dst_ref, val):
        steps = TM // SCAN_CHUNKS
        for l in range(D_RNN // LANE):
            for c in range(SCAN_CHUNKS):
                dst_ref[l, pl.ds(c, steps, stride=SCAN_CHUNKS), :] = (
                    val[c * steps:(c + 1) * steps, l * LANE:(l + 1) * LANE])

    store_step_major(rx_scr, nat(_O_RX, _O_RG))
    store_step_major(rg_scr, nat(_O_RG, _O_Q))

    def proj_cmp():
        kv = nat(_O_KC, _O_KS)
        for a in range(2):
            kv_scr[a] = kv[:, a * D_KV:(a + 1) * D_KV]
            for l in range(CMP_STRIDE):
                kvc_ref[a, :, l * D_KV:(l + 1) * D_KV] = (
                    kv_scr[a, pl.ds(l, TM // CMP_STRIDE, stride=CMP_STRIDE), :])

    def proj_kv(lo, k_ref, vt_ref):
        kv = nat(lo, lo + 2 * D_KV)
        k_ref[...] = kv[:, :D_KV].astype(BF16)
        vt = kv[:, D_KV:].T.astype(BF16)
        for c in range(TM // VC):
            vt_ref[c] = vt[:, c * VC:(c + 1) * VC]

    def proj_gates():
        ag_ref[...] = nat(_O_AG, _O_BR)
        brg = jnp.dot(h, wb_ref[...], preferred_element_type=F32)
        brgt_ref[...] = brg.T[0:N_GATE_ROWS, :]

    def proj_q():
        qt_ref[...] = (nat(_O_Q, _O_KC) * (HEAD_DIM ** -0.5 * LOG2E)).T.astype(BF16)

    projections = ((proj_gates,), (proj_q,),
                   (functools.partial(proj_kv, _O_KS, ksel_ref, vselt_ref),
                    functools.partial(proj_kv, _O_KW, kwin_ref, vwint_ref)),
                   (proj_cmp,))
    wraps = _rglru_wraps(rx_scr, tail_ref)
    rows = TM // len(projections)
    for ci, projs in enumerate(projections):
        _rglru_gates(rx_scr, wraps, ci * rows, (ci + 1) * rows,
                     cw_ref, cb_ref, wg_ref, ba_ref, bx_ref, lam_ref, a_s, u_s)
        for proj in projs:
            proj()

    _rglru_scan(rg_scr, rnn_ref, a_s, u_s, hlast, o_scr)


def _inproj(x2, nw, wn, wb, cw, cb, wg, ba, bx, lam, B, S):
    T = B * S
    ns = S // TM
    grid = (B, ns)
    row = lambda b, s: (b * ns + s, 0)
    const2 = lambda b, s: (0, 0)
    out_shape = (
        jax.ShapeDtypeStruct((T, D_RNN), BF16),
        jax.ShapeDtypeStruct((2, T // CMP_STRIDE, CMP_STRIDE * D_KV), F32),
        jax.ShapeDtypeStruct((T, D_KV), BF16),
        jax.ShapeDtypeStruct((T, D_KV), BF16),
        jax.ShapeDtypeStruct((T, D_ATTN), F32),
        jax.ShapeDtypeStruct((B, ns, D_ATTN, TM), BF16),
        jax.ShapeDtypeStruct((B, S // VC, D_KV, VC), BF16),
        jax.ShapeDtypeStruct((B, S // VC, D_KV, VC), BF16),
        jax.ShapeDtypeStruct((B, ns, N_GATE_ROWS, TM), F32),
    )
    out_specs = (
        pl.BlockSpec((TM, D_RNN), row),
        pl.BlockSpec((2, TM // CMP_STRIDE, CMP_STRIDE * D_KV), lambda b, s: (0, b * ns + s, 0)),
        pl.BlockSpec((TM, D_KV), row),
        pl.BlockSpec((TM, D_KV), row),
        pl.BlockSpec((TM, D_ATTN), row),
        pl.BlockSpec((None, None, D_ATTN, TM), lambda b, s: (b, s, 0, 0)),
        pl.BlockSpec((None, TM // VC, D_KV, VC), lambda b, s: (b, s, 0, 0)),
        pl.BlockSpec((None, TM // VC, D_KV, VC), lambda b, s: (b, s, 0, 0)),
        pl.BlockSpec((None, None, N_GATE_ROWS, TM), lambda b, s: (b, s, 0, 0)),
    )
    return pl.pallas_call(
        _inproj_kernel,
        grid=grid,
        in_specs=[
            pl.BlockSpec((TM, D_MODEL), row),
            pl.BlockSpec((1, D_MODEL), const2),
            pl.BlockSpec(wn.shape, const2),
            pl.BlockSpec(wb.shape, const2),
            pl.BlockSpec((CONV_WIDTH, D_RNN), const2),
            pl.BlockSpec((1, D_RNN), const2),
            pl.BlockSpec(wg.shape, lambda b, s: (0, 0, 0)),
            pl.BlockSpec((1, D_RNN), const2),
            pl.BlockSpec((1, D_RNN), const2),
            pl.BlockSpec((1, D_RNN), const2),
        ],
        out_specs=out_specs,
        out_shape=out_shape,
        scratch_shapes=[
            pltpu.VMEM(wn.shape, BF16),
            pltpu.VMEM((2, TM, D_KV), F32),
            pltpu.VMEM((D_RNN // LANE, TM, LANE), F32),
            pltpu.VMEM((D_RNN // LANE, TM, LANE), F32),
            pltpu.VMEM(((CONV_WIDTH - 1) * SCAN_CHUNKS, D_RNN), F32),
            pltpu.VMEM((TM, D_RNN), F32),
            pltpu.VMEM((TM, D_RNN), F32),
            pltpu.VMEM((1, D_RNN), F32),
            pltpu.VMEM((D_RNN // LANE, TM, LANE), F32),
        ],
        compiler_params=pltpu.CompilerParams(
            dimension_semantics=("arbitrary", "arbitrary"), vmem_limit_bytes=VMEM_LIMIT),
        name="inproj",
    )(x2, nw, wn, wb, cw, cb, wg, ba, bx, lam)


def _slab_rows(x_ref, lo, hi):
    return jnp.concatenate([x_ref[l, lo:hi, :] for l in range(D_RNN // LANE)], axis=1)


def _rglru_wraps(x_ref, tail_ref):
    ntail = (CONV_WIDTH - 1) * SCAN_CHUNKS
    last = _slab_rows(x_ref, TS - ntail, TS)
    sub = lax.broadcasted_iota(jnp.int32, (SCAN_CHUNKS, D_RNN), 0)
    wraps = []
    for j in range(CONV_WIDTH - 1):
        rows = slice(j * SCAN_CHUNKS, (j + 1) * SCAN_CHUNKS)
        cur = pltpu.roll(last[rows, :], 1, 0)
        prev = pltpu.roll(tail_ref[rows, :], 1, 0)
        wraps.append(jnp.where(sub == 0, prev, cur))
    tail_ref[...] = last
    return wraps


def _rglru_gates(x_ref, wraps, lo, hi, cw_ref, cb_ref, wg_ref, ba_ref, bx_ref, lam_ref, a_s, u_s):
    y = cb_ref[...]
    for k in range(CONV_WIDTH):
        off = (CONV_WIDTH - 1 - k) * SCAN_CHUNKS
        if lo >= off:
            xs = _slab_rows(x_ref, lo - off, hi - off)
        else:
            assert lo == 0
            xs = jnp.concatenate(wraps[len(wraps) - off // SCAN_CHUNKS:]
                                 + [_slab_rows(x_ref, 0, hi - off)], axis=0)
        y = y + xs * cw_ref[k:k + 1, :]

    yb = y.astype(BF16)
    half = D_RNN // 2
    pre = [jnp.dot(yb[:, hh * half:(hh + 1) * half], wg_ref[hh], preferred_element_type=F32)
           for hh in range(2)]
    pre_a = jnp.concatenate([pre[0][:, :half], pre[1][:, :half]], axis=1)
    pre_x = jnp.concatenate([pre[0][:, half:], pre[1][:, half:]], axis=1)
    r = jax.nn.sigmoid(pre_a + ba_ref[...])
    i = jax.nn.sigmoid(pre_x + bx_ref[...])
    lam = lam_ref[...]
    lsig = jnp.minimum(lam, 0.0) - jnp.log1p(jnp.exp(-jnp.abs(lam)))
    log_a = LRU_C * r * lsig
    a = jnp.exp(log_a)
    u = jnp.sqrt(1.0 - a * a) * (i * y)
    a_s[lo:hi, :] = a
    u_s[lo:hi, :] = u


def _rglru_scan(gate_ref, out_ref, a_s, u_s, hlast, o_scr):
    nstep = TS // SCAN_CHUNKS
    nslab = D_RNN // LANE
    gate = jnp.concatenate([gate_ref[l] for l in range(nslab)], axis=1)

    def body(j, carry):
        h, p = carry
        blk = pl.multiple_of(j * SCAN_CHUNKS, SCAN_CHUNKS)
        a_j = a_s[pl.ds(blk, SCAN_CHUNKS), :]
        h = a_j * h + u_s[pl.ds(blk, SCAN_CHUNKS), :]
        p = a_j * p
        u_s[pl.ds(blk, SCAN_CHUNKS), :] = h
        a_s[pl.ds(blk, SCAN_CHUNKS), :] = p
        return h, p

    h_end, p_end = lax.fori_loop(
        0, nstep, body,
        (jnp.zeros((SCAN_CHUNKS, D_RNN), F32), jnp.ones((SCAN_CHUNKS, D_RNN), F32)), unroll=8)
    carry = hlast[...]
    h_in = []
    for c in range(SCAN_CHUNKS):
        h_in.append(carry)
        carry = h_end[c:c + 1, :] + p_end[c:c + 1, :] * carry
    hlast[...] = carry
    h_in = jnp.concatenate([jnp.concatenate(h_in, axis=0)] * nstep, axis=0)
    o = (u_s[...] + a_s[...] * h_in) * _silu(gate)

    for l in range(nslab):
        o_scr[l] = o[:, l * LANE:(l + 1) * LANE]
        for c in range(SCAN_CHUNKS):
            out_ref[c * nstep:(c + 1) * nstep, l * LANE:(l + 1) * LANE] = (
                o_scr[l, pl.ds(c, nstep, stride=SCAN_CHUNKS), :].astype(out_ref.dtype))


def _compress_kernel(x_ref, pe_ref, w1k_ref, w1v_ref, w2_ref, out_ref, wbig_ref):
    nrow = x_ref.shape[1]
    nchunk = nrow // CB
    nh = N_KV_HEADS * CMP_HIDDEN

    for which, w1_ref in enumerate((w1k_ref, w1v_ref)):
        @pl.when((pl.program_id(0) == which) & (pl.program_id(1) == 0))
        def _(w1_ref=w1_ref):
            wbig_ref[...] = jnp.zeros(wbig_ref.shape, BF16)
            for half in range(2):
                for l in range(CMP_STRIDE):
                    r0 = (half * CMP_STRIDE + l) * HEAD_DIM
                    blk = w1_ref[r0:r0 + HEAD_DIM, :].astype(BF16)
                    for g in range(N_KV_HEADS):
                        rows = slice(l * D_KV + g * HEAD_DIM, l * D_KV + (g + 1) * HEAD_DIM)
                        cols = slice(half * nh + g * CMP_HIDDEN, half * nh + (g + 1) * CMP_HIDDEN)
                        wbig_ref[rows, cols] = blk

    x = x_ref[0]
    xa = (x + pe_ref[0, 0:1, :]).astype(BF16)
    xb = (x + pe_ref[0, 1:2, :]).astype(BF16)
    first = jnp.dot(xa, wbig_ref[:, :nh], preferred_element_type=F32)
    second = jnp.dot(xb, wbig_ref[:, nh:], preferred_element_type=F32)
    hid = _silu(first + pltpu.roll(second, nrow - 1, 0))
    out = jnp.dot(hid.astype(BF16), w2_ref[0], preferred_element_type=F32)
    c_idx = lax.broadcasted_iota(jnp.int32, out.shape, 0) & (nchunk - 1)
    out_ref[0] = jnp.where(c_idx < nchunk - 1, out, 0.0)


def _compress(xc, pe2, w1k, w1v, w2bd, B, S):
    nchunk = S // CMP_STRIDE
    nrow = CB * nchunk
    width = CMP_STRIDE * D_KV
    kv = lambda i, j: (i, 0, 0)
    return pl.pallas_call(
        _compress_kernel,
        grid=(2, B // CB),
        in_specs=[
            pl.BlockSpec((1, nrow, width), lambda i, j: (i, j, 0)),
            pl.BlockSpec((1, 2, width), kv),
            pl.BlockSpec(w1k.shape, lambda i, j: (0, 0)),
            pl.BlockSpec(w1v.shape, lambda i, j: (0, 0)),
            pl.BlockSpec((1,) + w2bd.shape[1:], kv),
        ],
        out_specs=pl.BlockSpec((1, nrow, D_KV), lambda i, j: (i, j, 0)),
        out_shape=jax.ShapeDtypeStruct((2, B * nchunk, D_KV), F32),
        scratch_shapes=[pltpu.VMEM((width, 2 * N_KV_HEADS * CMP_HIDDEN), BF16)],
        compiler_params=pltpu.CompilerParams(
            dimension_semantics=("arbitrary", "arbitrary"), vmem_limit_bytes=VMEM_LIMIT),
        name="compress",
    )(xc, pe2, w1k, w1v, w2bd)


def _nsa_kernel(qt_ref, cmp_ref, ksel_ref, vselt_ref, kwin_ref, vwint_ref, brgt_ref, ag_ref,
                ovt_ref, out_ref, qpad_ref, pen_ref, o_ref, m_ref, acc_ref, s_ref):
    qi = pl.program_id(1)
    q0 = qi * TQ
    nl = Q_PER_KV * TQ
    nblk = pen_ref.shape[1]
    ncmp = cmp_ref.shape[1]
    groups = range(N_KV_HEADS)
    gsl = [slice(g * HEAD_DIM, (g + 1) * HEAD_DIM) for g in groups]

    def tile4(a):
        return jnp.concatenate([a] * Q_PER_KV, axis=1)

    kc = cmp_ref[0].astype(BF16)
    s_cmp = []
    for g in groups:
        zpad = jnp.zeros((HEAD_DIM, TQ), BF16)
        cols = []
        for r in range(Q_PER_KV):
            hq = g * Q_PER_KV + r
            qh = qt_ref[hq * HEAD_DIM:(hq + 1) * HEAD_DIM, :]
            cols.append(jnp.concatenate([qh, zpad] if g == 0 else [zpad, qh], axis=0))
        qpad = jnp.concatenate(cols, axis=1)
        qpad_ref[g] = qpad
        s_cmp.append(jnp.dot(kc, qpad, preferred_element_type=F32))

    def sel_scores(kt, slot):
        k_tile = ksel_ref[pl.ds(pl.multiple_of(kt * KS, KS), KS), :]
        for g in groups:
            s_ref[slot, g] = jnp.dot(k_tile, qpad_ref[g], preferred_element_type=F32)

    n_full = q0 // KS

    ones = jnp.ones((ONES_ROWS, VC), BF16)

    def vt_aug(v_ref, c0, n, g):
        vt = jnp.concatenate([v_ref[c0 + c, gsl[g], :] for c in range(n)], axis=1)
        return jnp.concatenate([vt, jnp.concatenate([ones] * n, axis=1)], axis=0)

    def normalise(res):
        return res[0:HEAD_DIM, :] * (1.0 / res[HEAD_DIM:HEAD_DIM + 1, :])

    c_row = lax.broadcasted_iota(jnp.int32, (ncmp, TQ), 0)
    t_lane = q0 + lax.broadcasted_iota(jnp.int32, (ncmp, TQ), 1)
    cpen = tile4(jnp.where(c_row * CMP_STRIDE + (CMP_BLOCK - 1) <= t_lane, 0.0, NEG))
    vc_t = cmp_ref[1].T
    for g in groups:
        vct = vc_t[gsl[g], :].astype(BF16)
        s = s_cmp[g] + cpen
        m = jnp.max(s, axis=0, keepdims=True)
        m = jnp.where(m > 0.5 * NEG, m, 0.0)
        e = jnp.exp2(s - m)
        den = jnp.sum(e, axis=0, keepdims=True)
        p = e * (1.0 / jnp.where(den > 0.0, den, 1.0))
        o_ref[0, g] = jnp.dot(vct, p.astype(BF16), preferred_element_type=F32)

        psum = p[:, 0:TQ]
        for r in range(1, Q_PER_KV):
            psum = psum + p[:, r * TQ:(r + 1) * TQ]
        imp = jnp.dot(ovt_ref[...], psum, preferred_element_type=F32,
                      precision=lax.Precision.HIGHEST)
        j_row = lax.broadcasted_iota(jnp.int32, (nblk, TQ), 0)
        t_blk = q0 + lax.broadcasted_iota(jnp.int32, (nblk, TQ), 1)
        cur = jnp.right_shift(t_blk, SEL_BLOCK.bit_length() - 1)
        forced = (j_row == 0) | (j_row == cur) | (j_row == cur - 1)
        val = jnp.where(forced, SEL_FORCE, imp)
        val = jnp.where(j_row * SEL_BLOCK <= t_blk, val, -SEL_FORCE)
        rank = jnp.zeros((nblk, TQ), F32)
        for i in range(nblk):
            vi = jnp.broadcast_to(val[i:i + 1, :], (nblk, TQ))
            rank = rank + jnp.where(j_row > i, jnp.where(vi >= val, 1.0, 0.0),
                                    jnp.where(vi > val, 1.0, 0.0))
        chosen = (rank < float(min(SEL_TOPK, nblk))) & (val > -0.5 * SEL_FORCE)
        selpen = jnp.where(chosen, 0.0, NEG)
        for j in range(nblk):
            pen_ref[g, j] = jnp.broadcast_to(selpen[j:j + 1, :], (8, TQ))

    wrows = WINDOW + TW
    for sub in range(TQ // TW):
        qs = q0 + sub * TW
        w0 = pl.multiple_of(jnp.maximum(qs - WINDOW, 0), TW)
        k_win = kwin_ref[pl.ds(w0, wrows), :]
        key = w0 + lax.broadcasted_iota(jnp.int32, (wrows, TW), 0)
        t_q = qs + lax.broadcasted_iota(jnp.int32, (wrows, TW), 1)
        wpen = tile4(jnp.where(key <= t_q, jnp.where(key > t_q - WINDOW, 0.0, NEG), NEG))
        cols = [slice(r * TQ + sub * TW, r * TQ + (sub + 1) * TW) for r in range(Q_PER_KV)]
        p_win = []
        for g in groups:
            q_sub = jnp.concatenate([qpad_ref[g, :, c] for c in cols], axis=1)
            s = jnp.dot(k_win, q_sub, preferred_element_type=F32) + wpen
            m = jnp.max(s, axis=0, keepdims=True)
            p_win.append(jnp.exp2(s - m).astype(BF16))
        if sub == 0:
            sel_scores(0, n_full & 1)
        for g in groups:
            o_win = normalise(jnp.dot(vt_aug(vwint_ref, w0 // VC, wrows // VC, g), p_win[g],
                                      preferred_element_type=F32))
            for r, c in enumerate(cols):
                o_ref[2, g, :, c] = o_win[:, r * TW:(r + 1) * TW]

    m_ref[...] = jnp.full(m_ref.shape, NEG, F32)
    acc_ref[...] = jnp.zeros(acc_ref.shape, F32)
    bpt = KS // SEL_BLOCK
    cpt = KS // VC

    def sel_update(kt, slot, diag_keys=0):
        nkeys = diag_keys or KS
        probs, alphas = [], []
        for g in groups:
            m_prev = m_ref[g]
            if diag_keys:
                rows = [jnp.concatenate([pen_ref[g, kt * bpt + jj]] * (SEL_BLOCK // 8), axis=0)
                        for jj in range(nkeys // SEL_BLOCK)]
                pen = jnp.concatenate(rows, axis=0)
                key = kt * KS + lax.broadcasted_iota(jnp.int32, (nkeys, TQ), 0)
                t_q = q0 + lax.broadcasted_iota(jnp.int32, (nkeys, TQ), 1)
                s = s_ref[slot, g, 0:nkeys, :] + tile4(pen + jnp.where(key <= t_q, 0.0, NEG))
                m_new = jnp.maximum(m_prev, jnp.max(s, axis=0, keepdims=True))
                p = jnp.exp2(s - m_new)
            else:
                sb = [s_ref[slot, g, jj * SEL_BLOCK:(jj + 1) * SEL_BLOCK, :] for jj in range(bpt)]
                bias = [tile4(pen_ref[g, kt * bpt + jj][0:1, :]) for jj in range(bpt)]
                m_new = m_prev
                for jj in range(bpt):
                    m_new = jnp.maximum(m_new, jnp.max(sb[jj], axis=0, keepdims=True) + bias[jj])
                p = jnp.concatenate([jnp.exp2(sb[jj] + (bias[jj] - m_new)) for jj in range(bpt)],
                                    axis=0)
            probs.append(p.astype(BF16))
            alphas.append(jnp.exp2(m_prev - m_new))
            m_ref[g] = m_new
        for g in groups:
            acc_ref[g] = alphas[g] * acc_ref[g] + jnp.dot(
                vt_aug(vselt_ref, kt * cpt, nkeys // VC, g), probs[g], preferred_element_type=F32)

    def sel_body(kt, carry):
        for slot in (0, 1):
            @pl.when(((n_full - kt) & 1) == slot)
            def _(slot=slot):
                sel_scores(kt + 1, 1 - slot)
                sel_update(kt, slot)
        return carry

    lax.fori_loop(0, n_full, sel_body, 0)
    for nkeys in range(TQ, KS + 1, TQ):
        @pl.when(q0 + TQ - n_full * KS == nkeys)
        def _(nkeys=nkeys):
            sel_update(n_full, 0, diag_keys=nkeys)
    for g in groups:
        o_ref[1, g] = normalise(acc_ref[g])

    gates = jax.nn.sigmoid(brgt_ref[...])
    group_out = []
    for g in groups:
        heads = []
        for r in range(Q_PER_KV):
            hq = g * Q_PER_KV + r
            ls = slice(r * TQ, (r + 1) * TQ)
            o = gates[hq:hq + 1, :] * o_ref[0, g, :, ls]
            for n in range(1, N_BRANCH):
                o = o + gates[n * N_Q_HEADS + hq:n * N_Q_HEADS + hq + 1, :] * o_ref[n, g, :, ls]
            heads.append(o)
        group_out.append(jnp.concatenate(heads, axis=0).T)

    attn = jnp.concatenate(group_out, axis=1)
    out_ref[...] = (attn * _silu(ag_ref[...])).astype(out_ref.dtype)


def _nsa(qt, cmp_kv, ksel, vselt, kwin, vwint, brgt, ag, ovt, B, S):
    T = B * S
    nq = S // TQ
    qpm = TM // TQ
    nblk = S // SEL_BLOCK
    ncmp = S // CMP_STRIDE
    nl = Q_PER_KV * TQ
    return pl.pallas_call(
        _nsa_kernel,
        grid=(B, nq),
        in_specs=[
            pl.BlockSpec((None, None, D_ATTN, TQ), lambda b, q: (b, q // qpm, 0, q % qpm)),
            pl.BlockSpec((2, ncmp, D_KV), lambda b, q: (0, b, 0)),
            pl.BlockSpec((S, D_KV), lambda b, q: (b, 0)),
            pl.BlockSpec((None, S // VC, D_KV, VC), lambda b, q: (b, 0, 0, 0)),
            pl.BlockSpec((S, D_KV), lambda b, q: (b, 0)),
            pl.BlockSpec((None, S // VC, D_KV, VC), lambda b, q: (b, 0, 0, 0)),
            pl.BlockSpec((None, None, N_GATE_ROWS, TQ), lambda b, q: (b, q // qpm, 0, q % qpm)),
            pl.BlockSpec((TQ, D_ATTN), lambda b, q: (b * nq + q, 0)),
            pl.BlockSpec(ovt.shape, lambda b, q: (0, 0)),
        ],
        out_specs=pl.BlockSpec((TQ, D_ATTN), lambda b, q: (b * nq + q, 0)),
        out_shape=jax.ShapeDtypeStruct((T, D_ATTN), BF16),
        scratch_shapes=[
            pltpu.VMEM((N_KV_HEADS, D_KV, nl), BF16),
            pltpu.VMEM((N_KV_HEADS, nblk, 8, TQ), F32),
            pltpu.VMEM((N_BRANCH, N_KV_HEADS, HEAD_DIM, nl), F32),
            pltpu.VMEM((N_KV_HEADS, 1, nl), F32),
            pltpu.VMEM((N_KV_HEADS, HEAD_DIM + ONES_ROWS, nl), F32),
            pltpu.VMEM((2, N_KV_HEADS, KS, nl), F32),
        ],
        compiler_params=pltpu.CompilerParams(
            dimension_semantics=("arbitrary", "arbitrary"), vmem_limit_bytes=VMEM_LIMIT),
        name="nsa",
    )(qt, cmp_kv, ksel, vselt, kwin, vwint, brgt, ag, ovt)


def _outproj_kernel(x_ref, rnn_ref, attn_ref, wo_ref, nfw_ref, out_ref, wo_bf, *, final_norm):
    @pl.when(pl.program_id(0) == 0)
    def _():
        wo_bf[...] = wo_ref[...].astype(BF16)

    rows = TO // OUT_ROW_CHUNKS
    for c in range(OUT_ROW_CHUNKS):
        rs = slice(c * rows, (c + 1) * rows)
        mix = jnp.concatenate([rnn_ref[rs, :], attn_ref[rs, :]], axis=1)
        y = x_ref[rs, :] + jnp.dot(mix, wo_bf[...], preferred_element_type=F32)
        if final_norm:
            ms = jnp.mean(y * y, axis=-1, keepdims=True)
            y = (y * lax.rsqrt(ms + EPS)) * nfw_ref[...]
        out_ref[rs, :] = y


def _outproj(x2, rnn_out, attn_out, wo, nfw, final_norm):
    T = x2.shape[0]
    row = lambda i: (i, 0)
    return pl.pallas_call(
        functools.partial(_outproj_kernel, final_norm=final_norm),
        grid=(T // TO,),
        in_specs=[
            pl.BlockSpec((TO, D_MODEL), row),
            pl.BlockSpec((TO, D_RNN), row),
            pl.BlockSpec((TO, D_ATTN), row),
            pl.BlockSpec((D_MIX, D_MODEL), lambda i: (0, 0)),
            pl.BlockSpec((1, D_MODEL), lambda i: (0, 0)),
        ],
        out_specs=pl.BlockSpec((TO, D_MODEL), row),
        out_shape=jax.ShapeDtypeStruct((T, D_MODEL), F32),
        scratch_shapes=[pltpu.VMEM((D_MIX, D_MODEL), BF16)],
        compiler_params=pltpu.CompilerParams(
            dimension_semantics=("arbitrary",), vmem_limit_bytes=VMEM_LIMIT),
        name="outproj",
    )(x2, rnn_out, attn_out, wo, nfw)


def _block_diag_halves(wa, wx):
    eye = jnp.eye(RNN_HEADS, dtype=wa.dtype)
    full = lambda w: jnp.einsum('hij,hk->hikj', w, eye).reshape(D_RNN, D_RNN)
    fa, fx = full(wa), full(wx)
    half = D_RNN // 2
    return jnp.stack([
        jnp.concatenate([fa[s:s + half, s:s + half], fx[s:s + half, s:s + half]], axis=1)
        for s in (0, half)]).astype(BF16)


def _compress_weights(pe, w2):
    eye = jnp.eye(N_KV_HEADS, dtype=w2.dtype)
    w2bd = jnp.einsum('nd,ge->gned', w2, eye).reshape(N_KV_HEADS * CMP_HIDDEN, D_KV)
    per = pe.reshape(2, CMP_STRIDE, 1, HEAD_DIM)
    pe2 = jnp.broadcast_to(per, (2, CMP_STRIDE, N_KV_HEADS, HEAD_DIM)).reshape(2, CMP_STRIDE * D_KV)
    return pe2, w2bd.astype(BF16)


def _overlap_t(ncmp_pad, nblk):
    cs = np.arange(ncmp_pad)[None, :] * CMP_STRIDE
    ss = np.arange(nblk)[:, None] * SEL_BLOCK
    ov = np.clip(np.minimum(cs + CMP_BLOCK, ss + SEL_BLOCK) - np.maximum(cs, ss), 0, None)
    return jnp.asarray(ov.astype(np.float32) / CMP_BLOCK)


def kernel(x, norm1_w, w_in, conv_w, conv_b, rg_wa, rg_ba, rg_wx, rg_bx, rg_lambda,
           cmp_k_pe, cmp_k_w1, cmp_k_w2, cmp_v_pe, cmp_v_w1, cmp_v_w2, w_out, normf_w):
    B, S, D = x.shape
    assert D == D_MODEL and w_in.shape[-1] == _D_IN
    assert S % TM == 0 and TS == TM and B % CB == 0
    assert TQ % TW == 0 and TW % VC == 0 and WINDOW % TW == 0 and KS % TQ == 0 and TM % TQ == 0
    assert S % KS == 0 and S >= WINDOW + TQ and (B * S) % TO == 0
    depth = w_in.shape[0]
    T = B * S
    ovt = _overlap_t(S // CMP_STRIDE, S // SEL_BLOCK)
    x2 = x.reshape(T, D)
    for l in range(depth):
        wn = w_in[l]
        wb = jnp.pad(w_in[l][:, _O_BR:], ((0, 0), (0, _O_BR + LANE - _D_IN))).astype(BF16)
        (rnn_out, kvc, ksel, kwin, ag, qt, vselt, vwint, brgt) = _inproj(
            x2, norm1_w[l].reshape(1, D), wn, wb,
            conv_w[l], conv_b[l].reshape(1, D_RNN), _block_diag_halves(rg_wa[l], rg_wx[l]),
            rg_ba[l].reshape(1, D_RNN), rg_bx[l].reshape(1, D_RNN),
            rg_lambda[l].reshape(1, D_RNN), B, S)

        pk, w2k = _compress_weights(cmp_k_pe[l], cmp_k_w2[l])
        pv, w2v = _compress_weights(cmp_v_pe[l], cmp_v_w2[l])
        cmp_kv = _compress(kvc, jnp.stack([pk, pv]), cmp_k_w1[l], cmp_v_w1[l],
                           jnp.stack([w2k, w2v]), B, S)

        attn_out = _nsa(qt, cmp_kv, ksel, vselt, kwin, vwint, brgt, ag, ovt, B, S)

        x2 = _outproj(x2, rnn_out, attn_out, w_out[l],
                      normf_w.reshape(1, D), final_norm=(l == depth - 1))
    return x2.reshape(B, S, D)
```

```python
import functools

import numpy as np
import jax
import jax.numpy as jnp
from jax import lax
from jax.experimental import pallas as pl
from jax.experimental.pallas import tpu as pltpu

F32 = jnp.float32
BF16 = jnp.bfloat16

D_MODEL = 1024
EPS = 1e-6
D_RNN = 512
RNN_HEADS = 8
RNN_HEAD_DIM = D_RNN // RNN_HEADS
CONV_WIDTH = 4
LRU_C = 8.0
N_Q_HEADS = 8
N_KV_HEADS = 2
HEAD_DIM = 64
Q_PER_KV = N_Q_HEADS // N_KV_HEADS
D_ATTN = N_Q_HEADS * HEAD_DIM
D_KV = N_KV_HEADS * HEAD_DIM
CMP_BLOCK = 32
CMP_STRIDE = 16
CMP_HIDDEN = 256
SEL_BLOCK = 64
SEL_TOPK = 8
SEL_FORCE = 1e9
WINDOW = 512
N_BRANCH = 3
D_MIX = D_RNN + D_ATTN
N_GATE_ROWS = 32

LANE = 128
SCAN_CHUNKS = 8

TM = 1024
TS = 1024
TO = 1024
OUT_ROW_CHUNKS = 4
TQ = 256
KS = 512
VC = 128
TW = 128
CB = 4
NEG = -1e30
LOG2E = 1.4426950408889634
ONES_ROWS = 16
VMEM_V7X = 64 * 1024 * 1024
VMEM_LIMIT = VMEM_V7X - 8 * 1024 * 1024

_O_RX, _O_RG, _O_Q = 0, D_RNN, 2 * D_RNN
_O_KC = _O_Q + D_ATTN
_O_VC = _O_KC + D_KV
_O_KS = _O_VC + D_KV
_O_VS = _O_KS + D_KV
_O_KW = _O_VS + D_KV
_O_VW = _O_KW + D_KV
_O_AG = _O_VW + D_KV
_O_BR = _O_AG + D_ATTN
_D_IN = _O_BR + N_BRANCH * N_Q_HEADS


def _silu(x):
    return x * jax.nn.sigmoid(x)


def _inproj_kernel(x_ref, nw_ref, wn_ref, wb_ref, cw_ref, cb_ref, wg_ref, ba_ref, bx_ref, lam_ref,
                   rnn_ref, kvc_ref, ksel_ref, kwin_ref, ag_ref,
                   qt_ref, vselt_ref, vwint_ref, brgt_ref,
                   kv_scr, rx_scr, rg_scr, tail_ref, a_s, u_s, h_s, p_s, hlast, o_scr):
    @pl.when(pl.program_id(1) == 0)
    def _():
        tail_ref[...] = jnp.zeros(tail_ref.shape, F32)
        hlast[...] = jnp.zeros(hlast.shape, F32)

    x = x_ref[...]
    ms = jnp.mean(x * x, axis=-1, keepdims=True)
    h = ((x * lax.rsqrt(ms + EPS)) * nw_ref[...]).astype(BF16)

    def nat(a, b):
        return jnp.dot(h, wn_ref[:, a:b], preferred_element_type=F32)

    def store_step_major(dst_ref, val):
        steps = TM // SCAN_CHUNKS
        for l in range(D_RNN // LANE):
            for c in range(SCAN_CHUNKS):
                dst_ref[l, pl.ds(c, steps, stride=SCAN_CHUNKS), :] = (
                    val[c * steps:(c + 1) * steps, l * LANE:(l + 1) * LANE])

    store_step_major(rx_scr, nat(_O_RX, _O_RG))
    store_step_major(rg_scr, nat(_O_RG, _O_Q))

    def proj_cmp():
        kv = nat(_O_KC, _O_KS)
        for a in range(2):
            kv_scr[a] = kv[:, a * D_KV:(a + 1) * D_KV]
            for l in range(CMP_STRIDE):
                kvc_ref[a, :, l * D_KV:(l + 1) * D_KV] = (
                    kv_scr[a, pl.ds(l, TM // CMP_STRIDE, stride=CMP_STRIDE), :])

    def proj_kv(lo, k_ref, vt_ref):
        kv = nat(lo, lo + 2 * D_KV)
        k_ref[...] = kv[:, :D_KV].astype(BF16)
        vt = kv[:, D_KV:].T.astype(BF16)
        for c in range(TM // VC):
            vt_ref[c] = vt[:, c * VC:(c + 1) * VC]

    def proj_gates():
        ag_ref[...] = nat(_O_AG, _O_BR)
        brg = jnp.dot(h, wb_ref[...], preferred_element_type=F32)
        brgt_ref[...] = brg.T[0:N_GATE_ROWS, :]

    def proj_q():
        qt_ref[...] = (nat(_O_Q, _O_KC) * (HEAD_DIM ** -0.5 * LOG2E)).T.astype(BF16)

    projections = ((proj_gates,), (proj_q,),
                   (functools.partial(proj_kv, _O_KS, ksel_ref, vselt_ref),
                    functools.partial(proj_kv, _O_KW, kwin_ref, vwint_ref)),
                   (proj_cmp,))
    wraps = _rglru_wraps(rx_scr, tail_ref)
    rows = TM // len(projections)
    for ci, projs in enumerate(projections):
        _rglru_gates(rx_scr, wraps, ci * rows, (ci + 1) * rows,
                     cw_ref, cb_ref, wg_ref, ba_ref, bx_ref, lam_ref, a_s, u_s)
        for proj in projs:
            proj()

    _rglru_scan(rg_scr, rnn_ref, a_s, u_s, h_s, p_s, hlast, o_scr)


def _inproj(x2, nw, wn, wb, cw, cb, wg, ba, bx, lam, B, S):
    T = B * S
    ns = S // TM
    grid = (B, ns)
    row = lambda b, s: (b * ns + s, 0)
    const2 = lambda b, s: (0, 0)
    out_shape = (
        jax.ShapeDtypeStruct((T, D_RNN), BF16),
        jax.ShapeDtypeStruct((2, T // CMP_STRIDE, CMP_STRIDE * D_KV), F32),
        jax.ShapeDtypeStruct((T, D_KV), BF16),
        jax.ShapeDtypeStruct((T, D_KV), BF16),
        jax.ShapeDtypeStruct((T, D_ATTN), F32),
        jax.ShapeDtypeStruct((B, ns, D_ATTN, TM), BF16),
        jax.ShapeDtypeStruct((B, S // VC, D_KV, VC), BF16),
        jax.ShapeDtypeStruct((B, S // VC, D_KV, VC), BF16),
        jax.ShapeDtypeStruct((B, ns, N_GATE_ROWS, TM), F32),
    )
    out_specs = (
        pl.BlockSpec((TM, D_RNN), row),
        pl.BlockSpec((2, TM // CMP_STRIDE, CMP_STRIDE * D_KV), lambda b, s: (0, b * ns + s, 0)),
        pl.BlockSpec((TM, D_KV), row),
        pl.BlockSpec((TM, D_KV), row),
        pl.BlockSpec((TM, D_ATTN), row),
        pl.BlockSpec((None, None, D_ATTN, TM), lambda b, s: (b, s, 0, 0)),
        pl.BlockSpec((None, TM // VC, D_KV, VC), lambda b, s: (b, s, 0, 0)),
        pl.BlockSpec((None, TM // VC, D_KV, VC), lambda b, s: (b, s, 0, 0)),
        pl.BlockSpec((None, None, N_GATE_ROWS, TM), lambda b, s: (b, s, 0, 0)),
    )
    return pl.pallas_call(
        _inproj_kernel,
        grid=grid,
        in_specs=[
            pl.BlockSpec((TM, D_MODEL), row),
            pl.BlockSpec((1, D_MODEL), const2),
            pl.BlockSpec(wn.shape, const2),
            pl.BlockSpec(wb.shape, const2),
            pl.BlockSpec((CONV_WIDTH, D_RNN), const2),
            pl.BlockSpec((1, D_RNN), const2),
            pl.BlockSpec(wg.shape, lambda b, s: (0, 0, 0)),
            pl.BlockSpec((1, D_RNN), const2),
            pl.BlockSpec((1, D_RNN), const2),
            pl.BlockSpec((1, D_RNN), const2),
        ],
        out_specs=out_specs,
        out_shape=out_shape,
        scratch_shapes=[
            pltpu.VMEM((2, TM, D_KV), F32),
            pltpu.VMEM((D_RNN // LANE, TM, LANE), F32),
            pltpu.VMEM((D_RNN // LANE, TM, LANE), F32),
            pltpu.VMEM(((CONV_WIDTH - 1) * SCAN_CHUNKS, D_RNN), F32),
            pltpu.VMEM((TM, D_RNN), F32),
            pltpu.VMEM((TM, D_RNN), F32),
            pltpu.VMEM((TM, D_RNN), F32),
            pltpu.VMEM((TM, D_RNN), F32),
            pltpu.VMEM((1, D_RNN), F32),
            pltpu.VMEM((D_RNN // LANE, TM, LANE), F32),
        ],
        compiler_params=pltpu.CompilerParams(
            dimension_semantics=("arbitrary", "arbitrary"), vmem_limit_bytes=VMEM_LIMIT),
        name="inproj",
    )(x2, nw, wn, wb, cw, cb, wg, ba, bx, lam)


def _slab_rows(x_ref, lo, hi):
    return jnp.concatenate([x_ref[l, lo:hi, :] for l in range(D_RNN // LANE)], axis=1)


def _rglru_wraps(x_ref, tail_ref):
    ntail = (CONV_WIDTH - 1) * SCAN_CHUNKS
    last = _slab_rows(x_ref, TS - ntail, TS)
    sub = lax.broadcasted_iota(jnp.int32, (SCAN_CHUNKS, D_RNN), 0)
    wraps = []
    for j in range(CONV_WIDTH - 1):
        rows = slice(j * SCAN_CHUNKS, (j + 1) * SCAN_CHUNKS)
        cur = pltpu.roll(last[rows, :], 1, 0)
        prev = pltpu.roll(tail_ref[rows, :], 1, 0)
        wraps.append(jnp.where(sub == 0, prev, cur))
    tail_ref[...] = last
    return wraps


def _rglru_gates(x_ref, wraps, lo, hi, cw_ref, cb_ref, wg_ref, ba_ref, bx_ref, lam_ref, a_s, u_s):
    y = cb_ref[...]
    for k in range(CONV_WIDTH):
        off = (CONV_WIDTH - 1 - k) * SCAN_CHUNKS
        if lo >= off:
            xs = _slab_rows(x_ref, lo - off, hi - off)
        else:
            assert lo == 0
            xs = jnp.concatenate(wraps[len(wraps) - off // SCAN_CHUNKS:]
                                 + [_slab_rows(x_ref, 0, hi - off)], axis=0)
        y = y + xs * cw_ref[k:k + 1, :]

    yb = y.astype(BF16)
    half = D_RNN // 2
    pre = [jnp.dot(yb[:, hh * half:(hh + 1) * half], wg_ref[hh], preferred_element_type=F32)
           for hh in range(2)]
    pre_a = jnp.concatenate([pre[0][:, :half], pre[1][:, :half]], axis=1)
    pre_x = jnp.concatenate([pre[0][:, half:], pre[1][:, half:]], axis=1)
    r = jax.nn.sigmoid(pre_a + ba_ref[...])
    i = jax.nn.sigmoid(pre_x + bx_ref[...])
    lam = lam_ref[...]
    lsig = jnp.minimum(lam, 0.0) - jnp.log1p(jnp.exp(-jnp.abs(lam)))
    log_a = LRU_C * r * lsig
    a = jnp.exp(log_a)
    u = jnp.sqrt(1.0 - a * a) * (i * y)
    a_s[lo:hi, :] = a
    u_s[lo:hi, :] = u


def _rglru_scan(gate_ref, out_ref, a_s, u_s, h_s, p_s, hlast, o_scr):
    nstep = TS // SCAN_CHUNKS
    nslab = D_RNN // LANE
    gate = jnp.concatenate([gate_ref[l] for l in range(nslab)], axis=1)

    def body(j, carry):
        h, p = carry
        blk = pl.multiple_of(j * SCAN_CHUNKS, SCAN_CHUNKS)
        a_j = a_s[pl.ds(blk, SCAN_CHUNKS), :]
        h = a_j * h + u_s[pl.ds(blk, SCAN_CHUNKS), :]
        p = a_j * p
        h_s[pl.ds(blk, SCAN_CHUNKS), :] = h
        p_s[pl.ds(blk, SCAN_CHUNKS), :] = p
        return h, p

    h_end, p_end = lax.fori_loop(
        0, nstep, body,
        (jnp.zeros((SCAN_CHUNKS, D_RNN), F32), jnp.ones((SCAN_CHUNKS, D_RNN), F32)), unroll=8)
    carry = hlast[...]
    h_in = []
    for c in range(SCAN_CHUNKS):
        h_in.append(carry)
        carry = h_end[c:c + 1, :] + p_end[c:c + 1, :] * carry
    hlast[...] = carry
    h_in = jnp.concatenate([jnp.concatenate(h_in, axis=0)] * nstep, axis=0)
    o = (h_s[...] + p_s[...] * h_in) * _silu(gate)

    for l in range(nslab):
        o_scr[l] = o[:, l * LANE:(l + 1) * LANE]
        for c in range(SCAN_CHUNKS):
            out_ref[c * nstep:(c + 1) * nstep, l * LANE:(l + 1) * LANE] = (
                o_scr[l, pl.ds(c, nstep, stride=SCAN_CHUNKS), :].astype(out_ref.dtype))


def _compress_kernel(x_ref, pe_ref, w1k_ref, w1v_ref, w2_ref, out_ref, wbig_ref):
    nrow = x_ref.shape[1]
    nchunk = nrow // CB
    nh = N_KV_HEADS * CMP_HIDDEN

    for which, w1_ref in enumerate((w1k_ref, w1v_ref)):
        @pl.when((pl.program_id(0) == which) & (pl.program_id(1) == 0))
        def _(w1_ref=w1_ref):
            wbig_ref[...] = jnp.zeros(wbig_ref.shape, BF16)
            for half in range(2):
                for l in range(CMP_STRIDE):
                    r0 = (half * CMP_STRIDE + l) * HEAD_DIM
                    blk = w1_ref[r0:r0 + HEAD_DIM, :].astype(BF16)
                    for g in range(N_KV_HEADS):
                        rows = slice(l * D_KV + g * HEAD_DIM, l * D_KV + (g + 1) * HEAD_DIM)
                        cols = slice(half * nh + g * CMP_HIDDEN, half * nh + (g + 1) * CMP_HIDDEN)
                        wbig_ref[rows, cols] = blk

    x = x_ref[0]
    xa = (x + pe_ref[0, 0:1, :]).astype(BF16)
    xb = (x + pe_ref[0, 1:2, :]).astype(BF16)
    first = jnp.dot(xa, wbig_ref[:, :nh], preferred_element_type=F32)
    second = jnp.dot(xb, wbig_ref[:, nh:], preferred_element_type=F32)
    hid = _silu(first + pltpu.roll(second, nrow - 1, 0))
    out = jnp.dot(hid.astype(BF16), w2_ref[0], preferred_element_type=F32)
    c_idx = lax.broadcasted_iota(jnp.int32, out.shape, 0) & (nchunk - 1)
    out_ref[0] = jnp.where(c_idx < nchunk - 1, out, 0.0)


def _compress(xc, pe2, w1k, w1v, w2bd, B, S):
    nchunk = S // CMP_STRIDE
    nrow = CB * nchunk
    width = CMP_STRIDE * D_KV
    kv = lambda i, j: (i, 0, 0)
    return pl.pallas_call(
        _compress_kernel,
        grid=(2, B // CB),
        in_specs=[
            pl.BlockSpec((1, nrow, width), lambda i, j: (i, j, 0)),
            pl.BlockSpec((1, 2, width), kv),
            pl.BlockSpec(w1k.shape, lambda i, j: (0, 0)),
            pl.BlockSpec(w1v.shape, lambda i, j: (0, 0)),
            pl.BlockSpec((1,) + w2bd.shape[1:], kv),
        ],
        out_specs=pl.BlockSpec((1, nrow, D_KV), lambda i, j: (i, j, 0)),
        out_shape=jax.ShapeDtypeStruct((2, B * nchunk, D_KV), F32),
        scratch_shapes=[pltpu.VMEM((width, 2 * N_KV_HEADS * CMP_HIDDEN), BF16)],
        compiler_params=pltpu.CompilerParams(
            dimension_semantics=("arbitrary", "arbitrary"), vmem_limit_bytes=VMEM_LIMIT),
        name="compress",
    )(xc, pe2, w1k, w1v, w2bd)


def _nsa_kernel(qt_ref, cmp_ref, ksel_ref, vselt_ref, kwin_ref, vwint_ref, brgt_ref, ag_ref,
                ovt_ref, out_ref, qpad_ref, pen_ref, o_ref, m_ref, acc_ref, s_ref):
    qi = pl.program_id(1)
    q0 = qi * TQ
    nl = Q_PER_KV * TQ
    nblk = pen_ref.shape[1]
    ncmp = cmp_ref.shape[1]
    groups = range(N_KV_HEADS)
    gsl = [slice(g * HEAD_DIM, (g + 1) * HEAD_DIM) for g in groups]

    def tile4(a):
        return jnp.concatenate([a] * Q_PER_KV, axis=1)

    kc = cmp_ref[0].astype(BF16)
    s_cmp = []
    for g in groups:
        zpad = jnp.zeros((HEAD_DIM, TQ), BF16)
        cols = []
        for r in range(Q_PER_KV):
            hq = g * Q_PER_KV + r
            qh = qt_ref[hq * HEAD_DIM:(hq + 1) * HEAD_DIM, :]
            cols.append(jnp.concatenate([qh, zpad] if g == 0 else [zpad, qh], axis=0))
        qpad = jnp.concatenate(cols, axis=1)
        qpad_ref[g] = qpad
        s_cmp.append(jnp.dot(kc, qpad, preferred_element_type=F32))

    def sel_scores(kt, slot):
        k_tile = ksel_ref[pl.ds(pl.multiple_of(kt * KS, KS), KS), :]
        for g in groups:
            s_ref[slot, g] = jnp.dot(k_tile, qpad_ref[g], preferred_element_type=F32)

    n_full = q0 // KS

    ones = jnp.ones((ONES_ROWS, VC), BF16)

    def vt_aug(v_ref, chunks, g):
        vt = jnp.concatenate([v_ref[c, gsl[g], :] for c in chunks], axis=1)
        return jnp.concatenate([vt, jnp.concatenate([ones] * len(chunks), axis=1)], axis=0)

    def normalise(res):
        return res[0:HEAD_DIM, :] * (1.0 / res[HEAD_DIM:HEAD_DIM + 1, :])

    c_row = lax.broadcasted_iota(jnp.int32, (ncmp, TQ), 0)
    t_lane = q0 + lax.broadcasted_iota(jnp.int32, (ncmp, TQ), 1)
    cpen = tile4(jnp.where(c_row * CMP_STRIDE + (CMP_BLOCK - 1) <= t_lane, 0.0, NEG))
    vc_t = cmp_ref[1].T
    for g in groups:
        vct = vc_t[gsl[g], :].astype(BF16)
        s = s_cmp[g] + cpen
        m = jnp.max(s, axis=0, keepdims=True)
        m = jnp.where(m > 0.5 * NEG, m, 0.0)
        e = jnp.exp2(s - m)
        den = jnp.sum(e, axis=0, keepdims=True)
        p = e * (1.0 / jnp.where(den > 0.0, den, 1.0))
        o_ref[0, g] = jnp.dot(vct, p.astype(BF16), preferred_element_type=F32)

        psum = p[:, 0:TQ]
        for r in range(1, Q_PER_KV):
            psum = psum + p[:, r * TQ:(r + 1) * TQ]
        imp = jnp.dot(ovt_ref[...], psum, preferred_element_type=F32,
                      precision=lax.Precision.HIGHEST)
        j_row = lax.broadcasted_iota(jnp.int32, (nblk, TQ), 0)
        t_blk = q0 + lax.broadcasted_iota(jnp.int32, (nblk, TQ), 1)
        cur = jnp.right_shift(t_blk, SEL_BLOCK.bit_length() - 1)
        forced = (j_row == 0) | (j_row == cur) | (j_row == cur - 1)
        val = jnp.where(forced, SEL_FORCE, imp)
        val = jnp.where(j_row * SEL_BLOCK <= t_blk, val, -SEL_FORCE)
        rank = jnp.zeros((nblk, TQ), F32)
        for i in range(nblk):
            vi = jnp.broadcast_to(val[i:i + 1, :], (nblk, TQ))
            rank = rank + jnp.where(j_row > i, jnp.where(vi >= val, 1.0, 0.0),
                                    jnp.where(vi > val, 1.0, 0.0))
        chosen = (rank < float(min(SEL_TOPK, nblk))) & (val > -0.5 * SEL_FORCE)
        selpen = jnp.where(chosen, 0.0, NEG)
        for j in range(nblk):
            pen_ref[g, j] = jnp.broadcast_to(selpen[j:j + 1, :], (8, TQ))

    nchunk = WINDOW // TW + 1
    row_w = lax.broadcasted_iota(jnp.int32, (TW, TW), 0)
    lane_w = lax.broadcasted_iota(jnp.int32, (TW, TW), 1)
    edge_pen = {0: tile4(jnp.where(row_w > lane_w, 0.0, NEG)),
                nchunk - 1: tile4(jnp.where(row_w <= lane_w, 0.0, NEG))}
    for sub in range(TQ // TW):
        qs = q0 + sub * TW
        starts = [qs - WINDOW + c * TW for c in range(nchunk)]
        rows0 = [pl.multiple_of(jnp.maximum(st, 0), TW) for st in starts]
        bias = [jnp.where(st >= 0, 0.0, NEG) for st in starts]
        k_win = jnp.concatenate([kwin_ref[pl.ds(r0, TW), :] for r0 in rows0], axis=0)
        cols = [slice(r * TQ + sub * TW, r * TQ + (sub + 1) * TW) for r in range(Q_PER_KV)]
        p_win = []
        for g in groups:
            q_sub = jnp.concatenate([qpad_ref[g, :, c] for c in cols], axis=1)
            s = jnp.dot(k_win, q_sub, preferred_element_type=F32)
            parts = []
            for c in range(nchunk):
                s_c = s[c * TW:(c + 1) * TW, :]
                parts.append(s_c + edge_pen[c] if c in edge_pen else s_c)
            m = jnp.max(parts[-1], axis=0, keepdims=True)
            for c in range(nchunk - 1):
                m = jnp.maximum(m, jnp.max(parts[c], axis=0, keepdims=True) + bias[c])
            p_win.append(jnp.concatenate(
                [jnp.exp2(parts[c] + (bias[c] - m)) for c in range(nchunk)], axis=0).astype(BF16))
        if sub == 0:
            sel_scores(0, n_full & 1)
        for g in groups:
            o_win = normalise(jnp.dot(vt_aug(vwint_ref, [r0 // VC for r0 in rows0], g), p_win[g],
                                      preferred_element_type=F32))
            for r, c in enumerate(cols):
                o_ref[2, g, :, c] = o_win[:, r * TW:(r + 1) * TW]

    m_ref[...] = jnp.full(m_ref.shape, NEG, F32)
    acc_ref[...] = jnp.zeros(acc_ref.shape, F32)
    bpt = KS // SEL_BLOCK
    cpt = KS // VC

    def sel_update(kt, slot, diag_keys=0):
        nkeys = diag_keys or KS
        probs, alphas = [], []
        for g in groups:
            m_prev = m_ref[g]
            if diag_keys:
                rows = [jnp.concatenate([pen_ref[g, kt * bpt + jj]] * (SEL_BLOCK // 8), axis=0)
                        for jj in range(nkeys // SEL_BLOCK)]
                pen = jnp.concatenate(rows, axis=0)
                key = kt * KS + lax.broadcasted_iota(jnp.int32, (nkeys, TQ), 0)
                t_q = q0 + lax.broadcasted_iota(jnp.int32, (nkeys, TQ), 1)
                s = s_ref[slot, g, 0:nkeys, :] + tile4(pen + jnp.where(key <= t_q, 0.0, NEG))
                m_new = jnp.maximum(m_prev, jnp.max(s, axis=0, keepdims=True))
                p = jnp.exp2(s - m_new)
            else:
                sb = [s_ref[slot, g, jj * SEL_BLOCK:(jj + 1) * SEL_BLOCK, :] for jj in range(bpt)]
                bias = [tile4(pen_ref[g, kt * bpt + jj][0:1, :]) for jj in range(bpt)]
                m_new = m_prev
                for jj in range(bpt):
                    m_new = jnp.maximum(m_new, jnp.max(sb[jj], axis=0, keepdims=True) + bias[jj])
                p = jnp.concatenate([jnp.exp2(sb[jj] + (bias[jj] - m_new)) for jj in range(bpt)],
                                    axis=0)
            probs.append(p.astype(BF16))
            alphas.append(jnp.exp2(m_prev - m_new))
            m_ref[g] = m_new
        for g in groups:
            acc_ref[g] = alphas[g] * acc_ref[g] + jnp.dot(
                vt_aug(vselt_ref, [kt * cpt + c for c in range(nkeys // VC)], g), probs[g],
                preferred_element_type=F32)

    def sel_body(kt, carry):
        for slot in (0, 1):
            @pl.when(((n_full - kt) & 1) == slot)
            def _(slot=slot):
                sel_scores(kt + 1, 1 - slot)
                sel_update(kt, slot)
        return carry

    lax.fori_loop(0, n_full, sel_body, 0)
    for nkeys in range(TQ, KS + 1, TQ):
        @pl.when(q0 + TQ - n_full * KS == nkeys)
        def _(nkeys=nkeys):
            sel_update(n_full, 0, diag_keys=nkeys)
    for g in groups:
        o_ref[1, g] = normalise(acc_ref[g])

    gates = jax.nn.sigmoid(brgt_ref[...])
    group_out = []
    for g in groups:
        heads = []
        for r in range(Q_PER_KV):
            hq = g * Q_PER_KV + r
            ls = slice(r * TQ, (r + 1) * TQ)
            o = gates[hq:hq + 1, :] * o_ref[0, g, :, ls]
            for n in range(1, N_BRANCH):
                o = o + gates[n * N_Q_HEADS + hq:n * N_Q_HEADS + hq + 1, :] * o_ref[n, g, :, ls]
            heads.append(o)
        group_out.append(jnp.concatenate(heads, axis=0).T)

    attn = jnp.concatenate(group_out, axis=1)
    out_ref[...] = (attn * _silu(ag_ref[...])).astype(out_ref.dtype)


def _nsa(qt, cmp_kv, ksel, vselt, kwin, vwint, brgt, ag, ovt, B, S):
    T = B * S
    nq = S // TQ
    qpm = TM // TQ
    nblk = S // SEL_BLOCK
    ncmp = S // CMP_STRIDE
    nl = Q_PER_KV * TQ
    return pl.pallas_call(
        _nsa_kernel,
        grid=(B, nq),
        in_specs=[
            pl.BlockSpec((None, None, D_ATTN, TQ), lambda b, q: (b, q // qpm, 0, q % qpm)),
            pl.BlockSpec((2, ncmp, D_KV), lambda b, q: (0, b, 0)),
            pl.BlockSpec((S, D_KV), lambda b, q: (b, 0)),
            pl.BlockSpec((None, S // VC, D_KV, VC), lambda b, q: (b, 0, 0, 0)),
            pl.BlockSpec((S, D_KV), lambda b, q: (b, 0)),
            pl.BlockSpec((None, S // VC, D_KV, VC), lambda b, q: (b, 0, 0, 0)),
            pl.BlockSpec((None, None, N_GATE_ROWS, TQ), lambda b, q: (b, q // qpm, 0, q % qpm)),
            pl.BlockSpec((TQ, D_ATTN), lambda b, q: (b * nq + q, 0)),
            pl.BlockSpec(ovt.shape, lambda b, q: (0, 0)),
        ],
        out_specs=pl.BlockSpec((TQ, D_ATTN), lambda b, q: (b * nq + q, 0)),
        out_shape=jax.ShapeDtypeStruct((T, D_ATTN), BF16),
        scratch_shapes=[
            pltpu.VMEM((N_KV_HEADS, D_KV, nl), BF16),
            pltpu.VMEM((N_KV_HEADS, nblk, 8, TQ), F32),
            pltpu.VMEM((N_BRANCH, N_KV_HEADS, HEAD_DIM, nl), F32),
            pltpu.VMEM((N_KV_HEADS, 1, nl), F32),
            pltpu.VMEM((N_KV_HEADS, HEAD_DIM + ONES_ROWS, nl), F32),
            pltpu.VMEM((2, N_KV_HEADS, KS, nl), F32),
        ],
        compiler_params=pltpu.CompilerParams(
            dimension_semantics=("arbitrary", "arbitrary"), vmem_limit_bytes=VMEM_LIMIT),
        name="nsa",
    )(qt, cmp_kv, ksel, vselt, kwin, vwint, brgt, ag, ovt)


def _outproj_kernel(x_ref, rnn_ref, attn_ref, wo_ref, nfw_ref, out_ref, wo_bf, *, final_norm):
    @pl.when(pl.program_id(0) == 0)
    def _():
        wo_bf[...] = wo_ref[...].astype(BF16)

    rows = TO // OUT_ROW_CHUNKS
    for c in range(OUT_ROW_CHUNKS):
        rs = slice(c * rows, (c + 1) * rows)
        mix = jnp.concatenate([rnn_ref[rs, :], attn_ref[rs, :]], axis=1)
        y = x_ref[rs, :] + jnp.dot(mix, wo_bf[...], preferred_element_type=F32)
        if final_norm:
            ms = jnp.mean(y * y, axis=-1, keepdims=True)
            y = (y * lax.rsqrt(ms + EPS)) * nfw_ref[...]
        out_ref[rs, :] = y


def _outproj(x2, rnn_out, attn_out, wo, nfw, final_norm):
    T = x2.shape[0]
    row = lambda i: (i, 0)
    return pl.pallas_call(
        functools.partial(_outproj_kernel, final_norm=final_norm),
        grid=(T // TO,),
        in_specs=[
            pl.BlockSpec((TO, D_MODEL), row),
            pl.BlockSpec((TO, D_RNN), row),
            pl.BlockSpec((TO, D_ATTN), row),
            pl.BlockSpec((D_MIX, D_MODEL), lambda i: (0, 0)),
            pl.BlockSpec((1, D_MODEL), lambda i: (0, 0)),
        ],
        out_specs=pl.BlockSpec((TO, D_MODEL), row),
        out_shape=jax.ShapeDtypeStruct((T, D_MODEL), F32),
        scratch_shapes=[pltpu.VMEM((D_MIX, D_MODEL), BF16)],
        compiler_params=pltpu.CompilerParams(
            dimension_semantics=("arbitrary",), vmem_limit_bytes=VMEM_LIMIT),
        name="outproj",
    )(x2, rnn_out, attn_out, wo, nfw)


def _block_diag_halves(wa, wx):
    eye = jnp.eye(RNN_HEADS, dtype=wa.dtype)
    full = lambda w: jnp.einsum('hij,hk->hikj', w, eye).reshape(D_RNN, D_RNN)
    fa, fx = full(wa), full(wx)
    half = D_RNN // 2
    return jnp.stack([
        jnp.concatenate([fa[s:s + half, s:s + half], fx[s:s + half, s:s + half]], axis=1)
        for s in (0, half)]).astype(BF16)


def _compress_weights(pe, w2):
    eye = jnp.eye(N_KV_HEADS, dtype=w2.dtype)
    w2bd = jnp.einsum('nd,ge->gned', w2, eye).reshape(N_KV_HEADS * CMP_HIDDEN, D_KV)
    per = pe.reshape(2, CMP_STRIDE, 1, HEAD_DIM)
    pe2 = jnp.broadcast_to(per, (2, CMP_STRIDE, N_KV_HEADS, HEAD_DIM)).reshape(2, CMP_STRIDE * D_KV)
    return pe2, w2bd.astype(BF16)


def _overlap_t(ncmp_pad, nblk):
    cs = np.arange(ncmp_pad)[None, :] * CMP_STRIDE
    ss = np.arange(nblk)[:, None] * SEL_BLOCK
    ov = np.clip(np.minimum(cs + CMP_BLOCK, ss + SEL_BLOCK) - np.maximum(cs, ss), 0, None)
    return jnp.asarray(ov.astype(np.float32) / CMP_BLOCK)


def kernel(x, norm1_w, w_in, conv_w, conv_b, rg_wa, rg_ba, rg_wx, rg_bx, rg_lambda,
           cmp_k_pe, cmp_k_w1, cmp_k_w2, cmp_v_pe, cmp_v_w1, cmp_v_w2, w_out, normf_w):
    B, S, D = x.shape
    assert D == D_MODEL and w_in.shape[-1] == _D_IN
    assert S % TM == 0 and TS == TM and B % CB == 0
    assert TQ % TW == 0 and TW % VC == 0 and WINDOW % TW == 0 and KS % TQ == 0 and TM % TQ == 0
    assert S % KS == 0 and S >= WINDOW + TQ and (B * S) % TO == 0
    depth = w_in.shape[0]
    T = B * S
    ovt = _overlap_t(S // CMP_STRIDE, S // SEL_BLOCK)
    x2 = x.reshape(T, D)
    for l in range(depth):
        wn = w_in[l].astype(BF16)
        wb = jnp.pad(w_in[l][:, _O_BR:], ((0, 0), (0, _O_BR + LANE - _D_IN))).astype(BF16)
        (rnn_out, kvc, ksel, kwin, ag, qt, vselt, vwint, brgt) = _inproj(
            x2, norm1_w[l].reshape(1, D), wn, wb,
            conv_w[l], conv_b[l].reshape(1, D_RNN), _block_diag_halves(rg_wa[l], rg_wx[l]),
            rg_ba[l].reshape(1, D_RNN), rg_bx[l].reshape(1, D_RNN),
            rg_lambda[l].reshape(1, D_RNN), B, S)

        pk, w2k = _compress_weights(cmp_k_pe[l], cmp_k_w2[l])
        pv, w2v = _compress_weights(cmp_v_pe[l], cmp_v_w2[l])
        cmp_kv = _compress(kvc, jnp.stack([pk, pv]), cmp_k_w1[l], cmp_v_w1[l],
                           jnp.stack([w2k, w2v]), B, S)

        attn_out = _nsa(qt, cmp_kv, ksel, vselt, kwin, vwint, brgt, ag, ovt, B, S)

        x2 = _outproj(x2, rnn_out, attn_out, w_out[l],
                      normf_w.reshape(1, D), final_norm=(l == depth - 1))
    return x2.reshape(B, S, D)
```

```python
import functools

import numpy as np
import jax
import jax.numpy as jnp
from jax import lax
from jax.experimental import pallas as pl
from jax.experimental.pallas import tpu as pltpu

F32 = jnp.float32
BF16 = jnp.bfloat16

D_MODEL = 1024
EPS = 1e-6
D_RNN = 512
RNN_HEADS = 8
RNN_HEAD_DIM = D_RNN // RNN_HEADS
CONV_WIDTH = 4
LRU_C = 8.0
N_Q_HEADS = 8
N_KV_HEADS = 2
HEAD_DIM = 64
Q_PER_KV = N_Q_HEADS // N_KV_HEADS
D_ATTN = N_Q_HEADS * HEAD_DIM
D_KV = N_KV_HEADS * HEAD_DIM
CMP_BLOCK = 32
CMP_STRIDE = 16
CMP_HIDDEN = 256
SEL_BLOCK = 64
SEL_TOPK = 8
SEL_FORCE = 1e9
WINDOW = 512
N_BRANCH = 3
D_MIX = D_RNN + D_ATTN
N_GATE_ROWS = 32

LANE = 128
SCAN_CHUNKS = 8

TM = 1024
TS = 1024
TO = 1024
OUT_ROW_CHUNKS = 4
TQ = 256
KS = 512
VC = 128
TW = 128
CB = 4
NEG = -1e30
LOG2E = 1.4426950408889634
ONES_ROWS = 16
VMEM_V7X = 64 * 1024 * 1024
VMEM_LIMIT = VMEM_V7X - 8 * 1024 * 1024

_O_RX, _O_RG, _O_Q = 0, D_RNN, 2 * D_RNN
_O_KC = _O_Q + D_ATTN
_O_VC = _O_KC + D_KV
_O_KS = _O_VC + D_KV
_O_VS = _O_KS + D_KV
_O_KW = _O_VS + D_KV
_O_VW = _O_KW + D_KV
_O_AG = _O_VW + D_KV
_O_BR = _O_AG + D_ATTN
_D_IN = _O_BR + N_BRANCH * N_Q_HEADS


def _silu(x):
    return x * jax.nn.sigmoid(x)


def _inproj_kernel(x_ref, nw_ref, wn_ref, wb_ref, cw_ref, cb_ref, wg_ref, ba_ref, bx_ref, lam_ref,
                   rnn_ref, kvc_ref, ksel_ref, kwin_ref, ag_ref,
                   qt_ref, vselt_ref, vwint_ref, brgt_ref,
                   kv_scr, rx_scr, rg_scr, tail_ref, a_s, u_s, h_s, p_s, hlast, o_scr):
    @pl.when(pl.program_id(1) == 0)
    def _():
        tail_ref[...] = jnp.zeros(tail_ref.shape, F32)
        hlast[...] = jnp.zeros(hlast.shape, F32)

    x = x_ref[...]
    ms = jnp.mean(x * x, axis=-1, keepdims=True)
    h = ((x * lax.rsqrt(ms + EPS)) * nw_ref[...]).astype(BF16)

    def nat(a, b):
        return jnp.dot(h, wn_ref[:, a:b], preferred_element_type=F32)

    def store_step_major(dst_ref, val):
        steps = TM // SCAN_CHUNKS
        for l in range(D_RNN // LANE):
            for c in range(SCAN_CHUNKS):
                dst_ref[l, pl.ds(c, steps, stride=SCAN_CHUNKS), :] = (
                    val[c * steps:(c + 1) * steps, l * LANE:(l + 1) * LANE])

    store_step_major(rx_scr, nat(_O_RX, _O_RG))
    store_step_major(rg_scr, nat(_O_RG, _O_Q))

    def proj_cmp():
        kv = nat(_O_KC, _O_KS)
        for a in range(2):
            kv_scr[a] = kv[:, a * D_KV:(a + 1) * D_KV]
            for l in range(CMP_STRIDE):
                kvc_ref[a, :, l * D_KV:(l + 1) * D_KV] = (
                    kv_scr[a, pl.ds(l, TM // CMP_STRIDE, stride=CMP_STRIDE), :])

    def proj_kv(lo, k_ref, vt_ref):
        kv = nat(lo, lo + 2 * D_KV)
        k_ref[...] = kv[:, :D_KV].astype(BF16)
        vt = kv[:, D_KV:].T.astype(BF16)
        for c in range(TM // VC):
            vt_ref[c] = vt[:, c * VC:(c + 1) * VC]

    def proj_gates():
        ag_ref[...] = nat(_O_AG, _O_BR)
        brg = jnp.dot(h, wb_ref[...], preferred_element_type=F32)
        brgt_ref[...] = brg.T[0:N_GATE_ROWS, :]

    def proj_q():
        qt_ref[...] = (nat(_O_Q, _O_KC) * (HEAD_DIM ** -0.5 * LOG2E)).T.astype(BF16)

    projections = ((proj_gates,), (proj_q,),
                   (functools.partial(proj_kv, _O_KS, ksel_ref, vselt_ref),
                    functools.partial(proj_kv, _O_KW, kwin_ref, vwint_ref)),
                   (proj_cmp,))
    wraps = _rglru_wraps(rx_scr, tail_ref)
    rows = TM // len(projections)
    for ci, projs in enumerate(projections):
        _rglru_gates(rx_scr, wraps, ci * rows, (ci + 1) * rows,
                     cw_ref, cb_ref, wg_ref, ba_ref, bx_ref, lam_ref, a_s, u_s)
        for proj in projs:
            proj()

    _rglru_scan(rg_scr, rnn_ref, a_s, u_s, h_s, p_s, hlast, o_scr)


def _inproj(x2, nw, wn, wb, cw, cb, wg, ba, bx, lam, B, S):
    T = B * S
    ns = S // TM
    grid = (B, ns)
    row = lambda b, s: (b * ns + s, 0)
    const2 = lambda b, s: (0, 0)
    out_shape = (
        jax.ShapeDtypeStruct((T, D_RNN), BF16),
        jax.ShapeDtypeStruct((2, T // CMP_STRIDE, CMP_STRIDE * D_KV), F32),
        jax.ShapeDtypeStruct((T, D_KV), BF16),
        jax.ShapeDtypeStruct((T, D_KV), BF16),
        jax.ShapeDtypeStruct((T, D_ATTN), F32),
        jax.ShapeDtypeStruct((B, ns, D_ATTN, TM), BF16),
        jax.ShapeDtypeStruct((B, S // VC, D_KV, VC), BF16),
        jax.ShapeDtypeStruct((B, S // VC, D_KV, VC), BF16),
        jax.ShapeDtypeStruct((B, ns, N_GATE_ROWS, TM), F32),
    )
    out_specs = (
        pl.BlockSpec((TM, D_RNN), row),
        pl.BlockSpec((2, TM // CMP_STRIDE, CMP_STRIDE * D_KV), lambda b, s: (0, b * ns + s, 0)),
        pl.BlockSpec((TM, D_KV), row),
        pl.BlockSpec((TM, D_KV), row),
        pl.BlockSpec((TM, D_ATTN), row),
        pl.BlockSpec((None, None, D_ATTN, TM), lambda b, s: (b, s, 0, 0)),
        pl.BlockSpec((None, TM // VC, D_KV, VC), lambda b, s: (b, s, 0, 0)),
        pl.BlockSpec((None, TM // VC, D_KV, VC), lambda b, s: (b, s, 0, 0)),
        pl.BlockSpec((None, None, N_GATE_ROWS, TM), lambda b, s: (b, s, 0, 0)),
    )
    return pl.pallas_call(
        _inproj_kernel,
        grid=grid,
        in_specs=[
            pl.BlockSpec((TM, D_MODEL), row),
            pl.BlockSpec((1, D_MODEL), const2),
            pl.BlockSpec(wn.shape, const2),
            pl.BlockSpec(wb.shape, const2),
            pl.BlockSpec((CONV_WIDTH, D_RNN), const2),
            pl.BlockSpec((1, D_RNN), const2),
            pl.BlockSpec(wg.shape, lambda b, s: (0, 0, 0)),
            pl.BlockSpec((1, D_RNN), const2),
            pl.BlockSpec((1, D_RNN), const2),
            pl.BlockSpec((1, D_RNN), const2),
        ],
        out_specs=out_specs,
        out_shape=out_shape,
        scratch_shapes=[
            pltpu.VMEM((2, TM, D_KV), F32),
            pltpu.VMEM((D_RNN // LANE, TM, LANE), F32),
            pltpu.VMEM((D_RNN // LANE, TM, LANE), F32),
            pltpu.VMEM(((CONV_WIDTH - 1) * SCAN_CHUNKS, D_RNN), F32),
            pltpu.VMEM((TM, D_RNN), F32),
            pltpu.VMEM((TM, D_RNN), F32),
            pltpu.VMEM((TM, D_RNN), F32),
            pltpu.VMEM((TM, D_RNN), F32),
            pltpu.VMEM((1, D_RNN), F32),
            pltpu.VMEM((D_RNN // LANE, TM, LANE), F32),
        ],
        compiler_params=pltpu.CompilerParams(
            dimension_semantics=("arbitrary", "arbitrary"), vmem_limit_bytes=VMEM_LIMIT),
        name="inproj",
    )(x2, nw, wn, wb, cw, cb, wg, ba, bx, lam)


def _slab_rows(x_ref, lo, hi):
    return jnp.concatenate([x_ref[l, lo:hi, :] for l in range(D_RNN // LANE)], axis=1)


def _rglru_wraps(x_ref, tail_ref):
    ntail = (CONV_WIDTH - 1) * SCAN_CHUNKS
    last = _slab_rows(x_ref, TS - ntail, TS)
    sub = lax.broadcasted_iota(jnp.int32, (SCAN_CHUNKS, D_RNN), 0)
    wraps = []
    for j in range(CONV_WIDTH - 1):
        rows = slice(j * SCAN_CHUNKS, (j + 1) * SCAN_CHUNKS)
        cur = pltpu.roll(last[rows, :], 1, 0)
        prev = pltpu.roll(tail_ref[rows, :], 1, 0)
        wraps.append(jnp.where(sub == 0, prev, cur))
    tail_ref[...] = last
    return wraps


def _rglru_gates(x_ref, wraps, lo, hi, cw_ref, cb_ref, wg_ref, ba_ref, bx_ref, lam_ref, a_s, u_s):
    y = cb_ref[...]
    for k in range(CONV_WIDTH):
        off = (CONV_WIDTH - 1 - k) * SCAN_CHUNKS
        if lo >= off:
            xs = _slab_rows(x_ref, lo - off, hi - off)
        else:
            assert lo == 0
            xs = jnp.concatenate(wraps[len(wraps) - off // SCAN_CHUNKS:]
                                 + [_slab_rows(x_ref, 0, hi - off)], axis=0)
        y = y + xs * cw_ref[k:k + 1, :]

    yb = y.astype(BF16)
    half = D_RNN // 2
    pre = [jnp.dot(yb[:, hh * half:(hh + 1) * half], wg_ref[hh], preferred_element_type=F32)
           for hh in range(2)]
    pre_a = jnp.concatenate([pre[0][:, :half], pre[1][:, :half]], axis=1)
    pre_x = jnp.concatenate([pre[0][:, half:], pre[1][:, half:]], axis=1)
    r = jax.nn.sigmoid(pre_a + ba_ref[...])
    i = jax.nn.sigmoid(pre_x + bx_ref[...])
    lam = lam_ref[...]
    lsig = jnp.minimum(lam, 0.0) - jnp.log1p(jnp.exp(-jnp.abs(lam)))
    a = jnp.exp2(r * ((LRU_C * LOG2E) * lsig))
    u = jnp.sqrt(1.0 - a * a) * (i * y)
    a_s[lo:hi, :] = a
    u_s[lo:hi, :] = u


def _rglru_scan(gate_ref, out_ref, a_s, u_s, h_s, p_s, hlast, o_scr):
    nstep = TS // SCAN_CHUNKS
    nslab = D_RNN // LANE
    gate = jnp.concatenate([gate_ref[l] for l in range(nslab)], axis=1)

    def body(j, carry):
        h, p = carry
        blk = pl.multiple_of(j * SCAN_CHUNKS, SCAN_CHUNKS)
        a_j = a_s[pl.ds(blk, SCAN_CHUNKS), :]
        h = a_j * h + u_s[pl.ds(blk, SCAN_CHUNKS), :]
        p = a_j * p
        h_s[pl.ds(blk, SCAN_CHUNKS), :] = h
        p_s[pl.ds(blk, SCAN_CHUNKS), :] = p
        return h, p

    h_end, p_end = lax.fori_loop(
        0, nstep, body,
        (jnp.zeros((SCAN_CHUNKS, D_RNN), F32), jnp.ones((SCAN_CHUNKS, D_RNN), F32)), unroll=8)
    carry = hlast[...]
    h_in = []
    for c in range(SCAN_CHUNKS):
        h_in.append(carry)
        carry = h_end[c:c + 1, :] + p_end[c:c + 1, :] * carry
    hlast[...] = carry
    h_in = jnp.concatenate([jnp.concatenate(h_in, axis=0)] * nstep, axis=0)
    o = (h_s[...] + p_s[...] * h_in) * _silu(gate)

    for l in range(nslab):
        o_scr[l] = o[:, l * LANE:(l + 1) * LANE]
        for c in range(SCAN_CHUNKS):
            out_ref[c * nstep:(c + 1) * nstep, l * LANE:(l + 1) * LANE] = (
                o_scr[l, pl.ds(c, nstep, stride=SCAN_CHUNKS), :].astype(out_ref.dtype))


def _compress_kernel(x_ref, pe_ref, w1k_ref, w1v_ref, w2_ref, out_ref, wbig_ref):
    nrow = x_ref.shape[1]
    nchunk = nrow // CB
    nh = N_KV_HEADS * CMP_HIDDEN

    for which, w1_ref in enumerate((w1k_ref, w1v_ref)):
        @pl.when((pl.program_id(0) == which) & (pl.program_id(1) == 0))
        def _(w1_ref=w1_ref):
            wbig_ref[...] = jnp.zeros(wbig_ref.shape, BF16)
            for half in range(2):
                for l in range(CMP_STRIDE):
                    r0 = (half * CMP_STRIDE + l) * HEAD_DIM
                    blk = w1_ref[r0:r0 + HEAD_DIM, :].astype(BF16)
                    for g in range(N_KV_HEADS):
                        rows = slice(l * D_KV + g * HEAD_DIM, l * D_KV + (g + 1) * HEAD_DIM)
                        cols = slice(half * nh + g * CMP_HIDDEN, half * nh + (g + 1) * CMP_HIDDEN)
                        wbig_ref[rows, cols] = blk

    x = x_ref[0]
    xa = (x + pe_ref[0, 0:1, :]).astype(BF16)
    xb = (x + pe_ref[0, 1:2, :]).astype(BF16)
    first = jnp.dot(xa, wbig_ref[:, :nh], preferred_element_type=F32)
    second = jnp.dot(xb, wbig_ref[:, nh:], preferred_element_type=F32)
    hid = _silu(first + pltpu.roll(second, nrow - 1, 0))
    out = jnp.dot(hid.astype(BF16), w2_ref[0], preferred_element_type=F32)
    c_idx = lax.broadcasted_iota(jnp.int32, out.shape, 0) & (nchunk - 1)
    out_ref[0] = jnp.where(c_idx < nchunk - 1, out, 0.0)


def _compress(xc, pe2, w1k, w1v, w2bd, B, S):
    nchunk = S // CMP_STRIDE
    nrow = CB * nchunk
    width = CMP_STRIDE * D_KV
    kv = lambda i, j: (i, 0, 0)
    return pl.pallas_call(
        _compress_kernel,
        grid=(2, B // CB),
        in_specs=[
            pl.BlockSpec((1, nrow, width), lambda i, j: (i, j, 0)),
            pl.BlockSpec((1, 2, width), kv),
            pl.BlockSpec(w1k.shape, lambda i, j: (0, 0)),
            pl.BlockSpec(w1v.shape, lambda i, j: (0, 0)),
            pl.BlockSpec((1,) + w2bd.shape[1:], kv),
        ],
        out_specs=pl.BlockSpec((1, nrow, D_KV), lambda i, j: (i, j, 0)),
        out_shape=jax.ShapeDtypeStruct((2, B * nchunk, D_KV), F32),
        scratch_shapes=[pltpu.VMEM((width, 2 * N_KV_HEADS * CMP_HIDDEN), BF16)],
        compiler_params=pltpu.CompilerParams(
            dimension_semantics=("arbitrary", "arbitrary"), vmem_limit_bytes=VMEM_LIMIT),
        name="compress",
    )(xc, pe2, w1k, w1v, w2bd)


def _nsa_kernel(qt_ref, cmp_ref, ksel_ref, vselt_ref, kwin_ref, vwint_ref, brgt_ref, ag_ref,
                ovt_ref, out_ref, qpad_ref, pen_ref, o_ref, m_ref, acc_ref, s_ref):
    qi = pl.program_id(1)
    q0 = qi * TQ
    nl = Q_PER_KV * TQ
    nblk = pen_ref.shape[1]
    ncmp = cmp_ref.shape[1]
    groups = range(N_KV_HEADS)
    gsl = [slice(g * HEAD_DIM, (g + 1) * HEAD_DIM) for g in groups]

    def tile4(a):
        return jnp.concatenate([a] * Q_PER_KV, axis=1)

    kc = cmp_ref[0].astype(BF16)
    s_cmp = []
    for g in groups:
        zpad = jnp.zeros((HEAD_DIM, TQ), BF16)
        cols = []
        for r in range(Q_PER_KV):
            hq = g * Q_PER_KV + r
            qh = qt_ref[hq * HEAD_DIM:(hq + 1) * HEAD_DIM, :]
            cols.append(jnp.concatenate([qh, zpad] if g == 0 else [zpad, qh], axis=0))
        qpad = jnp.concatenate(cols, axis=1)
        qpad_ref[g] = qpad
        s_cmp.append(jnp.dot(kc, qpad, preferred_element_type=F32))

    def sel_scores(kt, slot):
        k_tile = ksel_ref[pl.ds(pl.multiple_of(kt * KS, KS), KS), :]
        for g in groups:
            s_ref[slot, g] = jnp.dot(k_tile, qpad_ref[g], preferred_element_type=F32)

    n_full = q0 // KS

    ones = jnp.ones((ONES_ROWS, VC), BF16)

    def vt_aug(v_ref, chunks, g):
        vt = jnp.concatenate([v_ref[c, gsl[g], :] for c in chunks], axis=1)
        return jnp.concatenate([vt, jnp.concatenate([ones] * len(chunks), axis=1)], axis=0)

    def normalise(res):
        return res[0:HEAD_DIM, :] * (1.0 / res[HEAD_DIM:HEAD_DIM + 1, :])

    c_row = lax.broadcasted_iota(jnp.int32, (ncmp, TQ), 0)
    t_lane = q0 + lax.broadcasted_iota(jnp.int32, (ncmp, TQ), 1)
    cpen = tile4(jnp.where(c_row * CMP_STRIDE + (CMP_BLOCK - 1) <= t_lane, 0.0, NEG))
    vc_t = cmp_ref[1].T
    for g in groups:
        vct = vc_t[gsl[g], :].astype(BF16)
        s = s_cmp[g] + cpen
        m = jnp.max(s, axis=0, keepdims=True)
        m = jnp.where(m > 0.5 * NEG, m, 0.0)
        e = jnp.exp2(s - m)
        den = jnp.sum(e, axis=0, keepdims=True)
        p = e * (1.0 / jnp.where(den > 0.0, den, 1.0))
        o_ref[0, g] = jnp.dot(vct, p.astype(BF16), preferred_element_type=F32)

        psum = p[:, 0:TQ]
        for r in range(1, Q_PER_KV):
            psum = psum + p[:, r * TQ:(r + 1) * TQ]
        imp = jnp.dot(ovt_ref[...], psum, preferred_element_type=F32,
                      precision=lax.Precision.HIGHEST)
        j_row = lax.broadcasted_iota(jnp.int32, (nblk, TQ), 0)
        t_blk = q0 + lax.broadcasted_iota(jnp.int32, (nblk, TQ), 1)
        cur = jnp.right_shift(t_blk, SEL_BLOCK.bit_length() - 1)
        forced = (j_row == 0) | (j_row == cur) | (j_row == cur - 1)
        val = jnp.where(forced, SEL_FORCE, imp)
        val = jnp.where(j_row * SEL_BLOCK <= t_blk, val, -SEL_FORCE)
        rank = jnp.zeros((nblk, TQ), F32)
        for i in range(nblk):
            vi = jnp.broadcast_to(val[i:i + 1, :], (nblk, TQ))
            rank = rank + jnp.where(j_row > i, jnp.where(vi >= val, 1.0, 0.0),
                                    jnp.where(vi > val, 1.0, 0.0))
        chosen = (rank < float(min(SEL_TOPK, nblk))) & (val > -0.5 * SEL_FORCE)
        selpen = jnp.where(chosen, 0.0, NEG)
        for j in range(nblk):
            pen_ref[g, j] = jnp.broadcast_to(selpen[j:j + 1, :], (8, TQ))

    nchunk = WINDOW // TW + 1
    row_w = lax.broadcasted_iota(jnp.int32, (TW, TW), 0)
    lane_w = lax.broadcasted_iota(jnp.int32, (TW, TW), 1)
    edge_pen = {0: tile4(jnp.where(row_w > lane_w, 0.0, NEG)),
                nchunk - 1: tile4(jnp.where(row_w <= lane_w, 0.0, NEG))}
    for sub in range(TQ // TW):
        qs = q0 + sub * TW
        starts = [qs - WINDOW + c * TW for c in range(nchunk)]
        rows0 = [pl.multiple_of(jnp.maximum(st, 0), TW) for st in starts]
        bias = [jnp.where(st >= 0, 0.0, NEG) for st in starts]
        k_win = jnp.concatenate([kwin_ref[pl.ds(r0, TW), :] for r0 in rows0], axis=0)
        cols = [slice(r * TQ + sub * TW, r * TQ + (sub + 1) * TW) for r in range(Q_PER_KV)]
        p_win = []
        for g in groups:
            q_sub = jnp.concatenate([qpad_ref[g, :, c] for c in cols], axis=1)
            s = jnp.dot(k_win, q_sub, preferred_element_type=F32)
            parts = []
            for c in range(nchunk):
                s_c = s[c * TW:(c + 1) * TW, :]
                parts.append(s_c + edge_pen[c] if c in edge_pen else s_c)
            m = jnp.max(parts[-1], axis=0, keepdims=True)
            for c in range(nchunk - 1):
                m = jnp.maximum(m, jnp.max(parts[c], axis=0, keepdims=True) + bias[c])
            p_win.append(jnp.concatenate(
                [jnp.exp2(parts[c] + (bias[c] - m)) for c in range(nchunk)], axis=0).astype(BF16))
        if sub == 0:
            sel_scores(0, n_full & 1)
        for g in groups:
            o_win = normalise(jnp.dot(vt_aug(vwint_ref, [r0 // VC for r0 in rows0], g), p_win[g],
                                      preferred_element_type=F32))
            for r, c in enumerate(cols):
                o_ref[2, g, :, c] = o_win[:, r * TW:(r + 1) * TW]

    m_ref[...] = jnp.full(m_ref.shape, NEG, F32)
    acc_ref[...] = jnp.zeros(acc_ref.shape, F32)
    bpt = KS // SEL_BLOCK
    cpt = KS // VC

    def sel_update(kt, slot, diag_keys=0):
        nkeys = diag_keys or KS
        probs, alphas = [], []
        for g in groups:
            m_prev = m_ref[g]
            if diag_keys:
                rows = [jnp.concatenate([pen_ref[g, kt * bpt + jj]] * (SEL_BLOCK // 8), axis=0)
                        for jj in range(nkeys // SEL_BLOCK)]
                pen = jnp.concatenate(rows, axis=0)
                key = kt * KS + lax.broadcasted_iota(jnp.int32, (nkeys, TQ), 0)
                t_q = q0 + lax.broadcasted_iota(jnp.int32, (nkeys, TQ), 1)
                s = s_ref[slot, g, 0:nkeys, :] + tile4(pen + jnp.where(key <= t_q, 0.0, NEG))
                m_new = jnp.maximum(m_prev, jnp.max(s, axis=0, keepdims=True))
                p = jnp.exp2(s - m_new)
            else:
                sb = [s_ref[slot, g, jj * SEL_BLOCK:(jj + 1) * SEL_BLOCK, :] for jj in range(bpt)]
                bias = [tile4(pen_ref[g, kt * bpt + jj][0:1, :]) for jj in range(bpt)]
                m_new = m_prev
                for jj in range(bpt):
                    m_new = jnp.maximum(m_new, jnp.max(sb[jj], axis=0, keepdims=True) + bias[jj])
                p = jnp.concatenate([jnp.exp2(sb[jj] + (bias[jj] - m_new)) for jj in range(bpt)],
                                    axis=0)
            probs.append(p.astype(BF16))
            alphas.append(jnp.exp2(m_prev - m_new))
            m_ref[g] = m_new
        for g in groups:
            acc_ref[g] = alphas[g] * acc_ref[g] + jnp.dot(
                vt_aug(vselt_ref, [kt * cpt + c for c in range(nkeys // VC)], g), probs[g],
                preferred_element_type=F32)

    def sel_body(kt, carry):
        for slot in (0, 1):
            @pl.when(((n_full - kt) & 1) == slot)
            def _(slot=slot):
                sel_scores(kt + 1, 1 - slot)
                sel_update(kt, slot)
        return carry

    lax.fori_loop(0, n_full, sel_body, 0)
    for nkeys in range(TQ, KS + 1, TQ):
        @pl.when(q0 + TQ - n_full * KS == nkeys)
        def _(nkeys=nkeys):
            sel_update(n_full, 0, diag_keys=nkeys)
    for g in groups:
        o_ref[1, g] = normalise(acc_ref[g])

    gates = jax.nn.sigmoid(brgt_ref[...])
    group_out = []
    for g in groups:
        heads = []
        for r in range(Q_PER_KV):
            hq = g * Q_PER_KV + r
            ls = slice(r * TQ, (r + 1) * TQ)
            o = gates[hq:hq + 1, :] * o_ref[0, g, :, ls]
            for n in range(1, N_BRANCH):
                o = o + gates[n * N_Q_HEADS + hq:n * N_Q_HEADS + hq + 1, :] * o_ref[n, g, :, ls]
            heads.append(o)
        group_out.append(jnp.concatenate(heads, axis=0).T)

    attn = jnp.concatenate(group_out, axis=1)
    out_ref[...] = (attn * _silu(ag_ref[...])).astype(out_ref.dtype)


def _nsa(qt, cmp_kv, ksel, vselt, kwin, vwint, brgt, ag, ovt, B, S):
    T = B * S
    nq = S // TQ
    qpm = TM // TQ
    nblk = S // SEL_BLOCK
    ncmp = S // CMP_STRIDE
    nl = Q_PER_KV * TQ
    return pl.pallas_call(
        _nsa_kernel,
        grid=(B, nq),
        in_specs=[
            pl.BlockSpec((None, None, D_ATTN, TQ), lambda b, q: (b, q // qpm, 0, q % qpm)),
            pl.BlockSpec((2, ncmp, D_KV), lambda b, q: (0, b, 0)),
            pl.BlockSpec((S, D_KV), lambda b, q: (b, 0)),
            pl.BlockSpec((None, S // VC, D_KV, VC), lambda b, q: (b, 0, 0, 0)),
            pl.BlockSpec((S, D_KV), lambda b, q: (b, 0)),
            pl.BlockSpec((None, S // VC, D_KV, VC), lambda b, q: (b, 0, 0, 0)),
            pl.BlockSpec((None, None, N_GATE_ROWS, TQ), lambda b, q: (b, q // qpm, 0, q % qpm)),
            pl.BlockSpec((TQ, D_ATTN), lambda b, q: (b * nq + q, 0)),
            pl.BlockSpec(ovt.shape, lambda b, q: (0, 0)),
        ],
        out_specs=pl.BlockSpec((TQ, D_ATTN), lambda b, q: (b * nq + q, 0)),
        out_shape=jax.ShapeDtypeStruct((T, D_ATTN), BF16),
        scratch_shapes=[
            pltpu.VMEM((N_KV_HEADS, D_KV, nl), BF16),
            pltpu.VMEM((N_KV_HEADS, nblk, 8, TQ), F32),
            pltpu.VMEM((N_BRANCH, N_KV_HEADS, HEAD_DIM, nl), F32),
            pltpu.VMEM((N_KV_HEADS, 1, nl), F32),
            pltpu.VMEM((N_KV_HEADS, HEAD_DIM + ONES_ROWS, nl), F32),
            pltpu.VMEM((2, N_KV_HEADS, KS, nl), F32),
        ],
        compiler_params=pltpu.CompilerParams(
            dimension_semantics=("arbitrary", "arbitrary"), vmem_limit_bytes=VMEM_LIMIT),
        name="nsa",
    )(qt, cmp_kv, ksel, vselt, kwin, vwint, brgt, ag, ovt)


def _outproj_kernel(x_ref, rnn_ref, attn_ref, wo_ref, nfw_ref, out_ref, wo_bf, *, final_norm):
    @pl.when(pl.program_id(0) == 0)
    def _():
        wo_bf[...] = wo_ref[...].astype(BF16)

    rows = TO // OUT_ROW_CHUNKS
    for c in range(OUT_ROW_CHUNKS):
        rs = slice(c * rows, (c + 1) * rows)
        mix = jnp.concatenate([rnn_ref[rs, :], attn_ref[rs, :]], axis=1)
        y = x_ref[rs, :] + jnp.dot(mix, wo_bf[...], preferred_element_type=F32)
        if final_norm:
            ms = jnp.mean(y * y, axis=-1, keepdims=True)
            y = (y * lax.rsqrt(ms + EPS)) * nfw_ref[...]
        out_ref[rs, :] = y


def _outproj(x2, rnn_out, attn_out, wo, nfw, final_norm):
    T = x2.shape[0]
    row = lambda i: (i, 0)
    return pl.pallas_call(
        functools.partial(_outproj_kernel, final_norm=final_norm),
        grid=(T // TO,),
        in_specs=[
            pl.BlockSpec((TO, D_MODEL), row),
            pl.BlockSpec((TO, D_RNN), row),
            pl.BlockSpec((TO, D_ATTN), row),
            pl.BlockSpec((D_MIX, D_MODEL), lambda i: (0, 0)),
            pl.BlockSpec((1, D_MODEL), lambda i: (0, 0)),
        ],
        out_specs=pl.BlockSpec((TO, D_MODEL), row),
        out_shape=jax.ShapeDtypeStruct((T, D_MODEL), F32),
        scratch_shapes=[pltpu.VMEM((D_MIX, D_MODEL), BF16)],
        compiler_params=pltpu.CompilerParams(
            dimension_semantics=("arbitrary",), vmem_limit_bytes=VMEM_LIMIT),
        name="outproj",
    )(x2, rnn_out, attn_out, wo, nfw)


def _block_diag_halves(wa, wx):
    eye = jnp.eye(RNN_HEADS, dtype=wa.dtype)
    full = lambda w: jnp.einsum('hij,hk->hikj', w, eye).reshape(D_RNN, D_RNN)
    fa, fx = full(wa), full(wx)
    half = D_RNN // 2
    return jnp.stack([
        jnp.concatenate([fa[s:s + half, s:s + half], fx[s:s + half, s:s + half]], axis=1)
        for s in (0, half)]).astype(BF16)


def _compress_weights(pe, w2):
    eye = jnp.eye(N_KV_HEADS, dtype=w2.dtype)
    w2bd = jnp.einsum('nd,ge->gned', w2, eye).reshape(N_KV_HEADS * CMP_HIDDEN, D_KV)
    per = pe.reshape(2, CMP_STRIDE, 1, HEAD_DIM)
    pe2 = jnp.broadcast_to(per, (2, CMP_STRIDE, N_KV_HEADS, HEAD_DIM)).reshape(2, CMP_STRIDE * D_KV)
    return pe2, w2bd.astype(BF16)


def _overlap_t(ncmp_pad, nblk):
    cs = np.arange(ncmp_pad)[None, :] * CMP_STRIDE
    ss = np.arange(nblk)[:, None] * SEL_BLOCK
    ov = np.clip(np.minimum(cs + CMP_BLOCK, ss + SEL_BLOCK) - np.maximum(cs, ss), 0, None)
    return jnp.asarray(ov.astype(np.float32) / CMP_BLOCK)


def kernel(x, norm1_w, w_in, conv_w, conv_b, rg_wa, rg_ba, rg_wx, rg_bx, rg_lambda,
           cmp_k_pe, cmp_k_w1, cmp_k_w2, cmp_v_pe, cmp_v_w1, cmp_v_w2, w_out, normf_w):
    B, S, D = x.shape
    assert D == D_MODEL and w_in.shape[-1] == _D_IN
    assert S % TM == 0 and TS == TM and B % CB == 0
    assert TQ % TW == 0 and TW % VC == 0 and WINDOW % TW == 0 and KS % TQ == 0 and TM % TQ == 0
    assert S % KS == 0 and S >= WINDOW + TQ and (B * S) % TO == 0
    depth = w_in.shape[0]
    T = B * S
    ovt = _overlap_t(S // CMP_STRIDE, S // SEL_BLOCK)
    x2 = x.reshape(T, D)
    for l in range(depth):
        wn = w_in[l].astype(BF16)
        wb = jnp.pad(w_in[l][:, _O_BR:], ((0, 0), (0, _O_BR + LANE - _D_IN))).astype(BF16)
        (rnn_out, kvc, ksel, kwin, ag, qt, vselt, vwint, brgt) = _inproj(
            x2, norm1_w[l].reshape(1, D), wn, wb,
            conv_w[l], conv_b[l].reshape(1, D_RNN), _block_diag_halves(rg_wa[l], rg_wx[l]),
            rg_ba[l].reshape(1, D_RNN), rg_bx[l].reshape(1, D_RNN),
            rg_lambda[l].reshape(1, D_RNN), B, S)

        pk, w2k = _compress_weights(cmp_k_pe[l], cmp_k_w2[l])
        pv, w2v = _compress_weights(cmp_v_pe[l], cmp_v_w2[l])
        cmp_kv = _compress(kvc, jnp.stack([pk, pv]), cmp_k_w1[l], cmp_v_w1[l],
                           jnp.stack([w2k, w2v]), B, S)

        attn_out = _nsa(qt, cmp_kv, ksel, vselt, kwin, vwint, brgt, ag, ovt, B, S)

        x2 = _outproj(x2, rnn_out, attn_out, w_out[l],
                      normf_w.reshape(1, D), final_norm=(l == depth - 1))
    return x2.reshape(B, S, D)
```

```python
import functools

import numpy as np
import jax
import jax.numpy as jnp
from jax import lax
from jax.experimental import pallas as pl
from jax.experimental.pallas import tpu as pltpu

F32 = jnp.float32
BF16 = jnp.bfloat16

D_MODEL = 1024
EPS = 1e-6
D_RNN = 512
RNN_HEADS = 8
RNN_HEAD_DIM = D_RNN // RNN_HEADS
CONV_WIDTH = 4
LRU_C = 8.0
N_Q_HEADS = 8
N_KV_HEADS = 2
HEAD_DIM = 64
Q_PER_KV = N_Q_HEADS // N_KV_HEADS
D_ATTN = N_Q_HEADS * HEAD_DIM
D_KV = N_KV_HEADS * HEAD_DIM
CMP_BLOCK = 32
CMP_STRIDE = 16
CMP_HIDDEN = 256
SEL_BLOCK = 64
SEL_TOPK = 8
SEL_FORCE = 1e9
WINDOW = 512
N_BRANCH = 3
D_MIX = D_RNN + D_ATTN
N_GATE_ROWS = 32

LANE = 128
SCAN_CHUNKS = 8

TM = 1024
TS = 1024
TO = 1024
OUT_ROW_CHUNKS = 4
TQ = 256
KS = 512
VC = 128
TW = 128
CMP_ALIGN = 64
CB = 4
NEG = -1e30
LOG2E = 1.4426950408889634
ONES_ROWS = 16
VMEM_V7X = 64 * 1024 * 1024
VMEM_LIMIT = VMEM_V7X - 8 * 1024 * 1024

_O_RX, _O_RG, _O_Q = 0, D_RNN, 2 * D_RNN
_O_KC = _O_Q + D_ATTN
_O_VC = _O_KC + D_KV
_O_KS = _O_VC + D_KV
_O_VS = _O_KS + D_KV
_O_KW = _O_VS + D_KV
_O_VW = _O_KW + D_KV
_O_AG = _O_VW + D_KV
_O_BR = _O_AG + D_ATTN
_D_IN = _O_BR + N_BRANCH * N_Q_HEADS


def _silu(x):
    return x * jax.nn.sigmoid(x)


def _inproj_kernel(x_ref, nw_ref, wn_ref, wb_ref, cw_ref, cb_ref, wg_ref, ba_ref, bx_ref, lam_ref,
                   rnn_ref, kvc_ref, ksel_ref, kwin_ref, ag_ref,
                   qt_ref, vselt_ref, vwint_ref, brgt_ref,
                   kv_scr, rx_scr, rg_scr, tail_ref, a_s, u_s, h_s, p_s, hlast, o_scr):
    @pl.when(pl.program_id(1) == 0)
    def _():
        tail_ref[...] = jnp.zeros(tail_ref.shape, F32)
        hlast[...] = jnp.zeros(hlast.shape, F32)

    x = x_ref[...]
    ms = jnp.mean(x * x, axis=-1, keepdims=True)
    h = ((x * lax.rsqrt(ms + EPS)) * nw_ref[...]).astype(BF16)

    def nat(a, b):
        return jnp.dot(h, wn_ref[:, a:b], preferred_element_type=F32)

    def store_step_major(dst_ref, val):
        steps = TM // SCAN_CHUNKS
        for l in range(D_RNN // LANE):
            for c in range(SCAN_CHUNKS):
                dst_ref[l, pl.ds(c, steps, stride=SCAN_CHUNKS), :] = (
                    val[c * steps:(c + 1) * steps, l * LANE:(l + 1) * LANE])

    store_step_major(rx_scr, nat(_O_RX, _O_RG))
    store_step_major(rg_scr, nat(_O_RG, _O_Q))

    def proj_cmp():
        kv = nat(_O_KC, _O_KS)
        for a in range(2):
            kv_scr[a] = kv[:, a * D_KV:(a + 1) * D_KV]
            for l in range(CMP_STRIDE):
                kvc_ref[a, :, l * D_KV:(l + 1) * D_KV] = (
                    kv_scr[a, pl.ds(l, TM // CMP_STRIDE, stride=CMP_STRIDE), :])

    def proj_kv(lo, k_ref, vt_ref):
        kv = nat(lo, lo + 2 * D_KV)
        k_ref[...] = kv[:, :D_KV].astype(BF16)
        vt = kv[:, D_KV:].T.astype(BF16)
        for c in range(TM // VC):
            vt_ref[c] = vt[:, c * VC:(c + 1) * VC]

    def proj_gates():
        ag_ref[...] = nat(_O_AG, _O_BR)
        brg = jnp.dot(h, wb_ref[...], preferred_element_type=F32)
        brgt_ref[...] = brg.T[0:N_GATE_ROWS, :]

    def proj_q():
        qt_ref[...] = (nat(_O_Q, _O_KC) * (HEAD_DIM ** -0.5 * LOG2E)).T.astype(BF16)

    projections = ((proj_gates,), (proj_q,),
                   (functools.partial(proj_kv, _O_KS, ksel_ref, vselt_ref),
                    functools.partial(proj_kv, _O_KW, kwin_ref, vwint_ref)),
                   (proj_cmp,))
    wraps = _rglru_wraps(rx_scr, tail_ref)
    rows = TM // len(projections)
    for ci, projs in enumerate(projections):
        _rglru_gates(rx_scr, wraps, ci * rows, (ci + 1) * rows,
                     cw_ref, cb_ref, wg_ref, ba_ref, bx_ref, lam_ref, a_s, u_s)
        for proj in projs:
            proj()

    _rglru_scan(rg_scr, rnn_ref, a_s, u_s, h_s, p_s, hlast, o_scr)


def _inproj(x2, nw, wn, wb, cw, cb, wg, ba, bx, lam, B, S):
    T = B * S
    ns = S // TM
    grid = (B, ns)
    row = lambda b, s: (b * ns + s, 0)
    const2 = lambda b, s: (0, 0)
    out_shape = (
        jax.ShapeDtypeStruct((T, D_RNN), BF16),
        jax.ShapeDtypeStruct((2, T // CMP_STRIDE, CMP_STRIDE * D_KV), F32),
        jax.ShapeDtypeStruct((T, D_KV), BF16),
        jax.ShapeDtypeStruct((T, D_KV), BF16),
        jax.ShapeDtypeStruct((T, D_ATTN), F32),
        jax.ShapeDtypeStruct((B, ns, D_ATTN, TM), BF16),
        jax.ShapeDtypeStruct((B, S // VC, D_KV, VC), BF16),
        jax.ShapeDtypeStruct((B, S // VC, D_KV, VC), BF16),
        jax.ShapeDtypeStruct((B, ns, N_GATE_ROWS, TM), F32),
    )
    out_specs = (
        pl.BlockSpec((TM, D_RNN), row),
        pl.BlockSpec((2, TM // CMP_STRIDE, CMP_STRIDE * D_KV), lambda b, s: (0, b * ns + s, 0)),
        pl.BlockSpec((TM, D_KV), row),
        pl.BlockSpec((TM, D_KV), row),
        pl.BlockSpec((TM, D_ATTN), row),
        pl.BlockSpec((None, None, D_ATTN, TM), lambda b, s: (b, s, 0, 0)),
        pl.BlockSpec((None, TM // VC, D_KV, VC), lambda b, s: (b, s, 0, 0)),
        pl.BlockSpec((None, TM // VC, D_KV, VC), lambda b, s: (b, s, 0, 0)),
        pl.BlockSpec((None, None, N_GATE_ROWS, TM), lambda b, s: (b, s, 0, 0)),
    )
    return pl.pallas_call(
        _inproj_kernel,
        grid=grid,
        in_specs=[
            pl.BlockSpec((TM, D_MODEL), row),
            pl.BlockSpec((1, D_MODEL), const2),
            pl.BlockSpec(wn.shape, const2),
            pl.BlockSpec(wb.shape, const2),
            pl.BlockSpec((CONV_WIDTH, D_RNN), const2),
            pl.BlockSpec((1, D_RNN), const2),
            pl.BlockSpec(wg.shape, lambda b, s: (0, 0, 0)),
            pl.BlockSpec((1, D_RNN), const2),
            pl.BlockSpec((1, D_RNN), const2),
            pl.BlockSpec((1, D_RNN), const2),
        ],
        out_specs=out_specs,
        out_shape=out_shape,
        scratch_shapes=[
            pltpu.VMEM((2, TM, D_KV), F32),
            pltpu.VMEM((D_RNN // LANE, TM, LANE), F32),
            pltpu.VMEM((D_RNN // LANE, TM, LANE), F32),
            pltpu.VMEM(((CONV_WIDTH - 1) * SCAN_CHUNKS, D_RNN), F32),
            pltpu.VMEM((TM, D_RNN), F32),
            pltpu.VMEM((TM, D_RNN), F32),
            pltpu.VMEM((TM, D_RNN), F32),
            pltpu.VMEM((TM, D_RNN), F32),
            pltpu.VMEM((1, D_RNN), F32),
            pltpu.VMEM((D_RNN // LANE, TM, LANE), F32),
        ],
        compiler_params=pltpu.CompilerParams(
            dimension_semantics=("arbitrary", "arbitrary"), vmem_limit_bytes=VMEM_LIMIT),
        name="inproj",
    )(x2, nw, wn, wb, cw, cb, wg, ba, bx, lam)


def _slab_rows(x_ref, lo, hi):
    return jnp.concatenate([x_ref[l, lo:hi, :] for l in range(D_RNN // LANE)], axis=1)


def _rglru_wraps(x_ref, tail_ref):
    ntail = (CONV_WIDTH - 1) * SCAN_CHUNKS
    last = _slab_rows(x_ref, TS - ntail, TS)
    sub = lax.broadcasted_iota(jnp.int32, (SCAN_CHUNKS, D_RNN), 0)
    wraps = []
    for j in range(CONV_WIDTH - 1):
        rows = slice(j * SCAN_CHUNKS, (j + 1) * SCAN_CHUNKS)
        cur = pltpu.roll(last[rows, :], 1, 0)
        prev = pltpu.roll(tail_ref[rows, :], 1, 0)
        wraps.append(jnp.where(sub == 0, prev, cur))
    tail_ref[...] = last
    return wraps


def _rglru_gates(x_ref, wraps, lo, hi, cw_ref, cb_ref, wg_ref, ba_ref, bx_ref, lam_ref, a_s, u_s):
    y = cb_ref[...]
    for k in range(CONV_WIDTH):
        off = (CONV_WIDTH - 1 - k) * SCAN_CHUNKS
        if lo >= off:
            xs = _slab_rows(x_ref, lo - off, hi - off)
        else:
            assert lo == 0
            xs = jnp.concatenate(wraps[len(wraps) - off // SCAN_CHUNKS:]
                                 + [_slab_rows(x_ref, 0, hi - off)], axis=0)
        y = y + xs * cw_ref[k:k + 1, :]

    yb = y.astype(BF16)
    half = D_RNN // 2
    pre = [jnp.dot(yb[:, hh * half:(hh + 1) * half], wg_ref[hh], preferred_element_type=F32)
           for hh in range(2)]
    pre_a = jnp.concatenate([pre[0][:, :half], pre[1][:, :half]], axis=1)
    pre_x = jnp.concatenate([pre[0][:, half:], pre[1][:, half:]], axis=1)
    r = jax.nn.sigmoid(pre_a + ba_ref[...])
    i = jax.nn.sigmoid(pre_x + bx_ref[...])
    lam = lam_ref[...]
    lsig = jnp.minimum(lam, 0.0) - jnp.log1p(jnp.exp(-jnp.abs(lam)))
    a = jnp.exp2(r * ((LRU_C * LOG2E) * lsig))
    u = jnp.sqrt(1.0 - a * a) * (i * y)
    a_s[lo:hi, :] = a
    u_s[lo:hi, :] = u


def _rglru_scan(gate_ref, out_ref, a_s, u_s, h_s, p_s, hlast, o_scr):
    nstep = TS // SCAN_CHUNKS
    nslab = D_RNN // LANE
    gate = jnp.concatenate([gate_ref[l] for l in range(nslab)], axis=1)

    def body(j, carry):
        h, p = carry
        blk = pl.multiple_of(j * SCAN_CHUNKS, SCAN_CHUNKS)
        a_j = a_s[pl.ds(blk, SCAN_CHUNKS), :]
        h = a_j * h + u_s[pl.ds(blk, SCAN_CHUNKS), :]
        p = a_j * p
        h_s[pl.ds(blk, SCAN_CHUNKS), :] = h
        p_s[pl.ds(blk, SCAN_CHUNKS), :] = p
        return h, p

    h_end, p_end = lax.fori_loop(
        0, nstep, body,
        (jnp.zeros((SCAN_CHUNKS, D_RNN), F32), jnp.ones((SCAN_CHUNKS, D_RNN), F32)), unroll=8)
    carry = hlast[...]
    h_in = []
    for c in range(SCAN_CHUNKS):
        h_in.append(carry)
        carry = h_end[c:c + 1, :] + p_end[c:c + 1, :] * carry
    hlast[...] = carry
    h_in = jnp.concatenate([jnp.concatenate(h_in, axis=0)] * nstep, axis=0)
    o = (h_s[...] + p_s[...] * h_in) * _silu(gate)

    for l in range(nslab):
        o_scr[l] = o[:, l * LANE:(l + 1) * LANE]
        for c in range(SCAN_CHUNKS):
            out_ref[c * nstep:(c + 1) * nstep, l * LANE:(l + 1) * LANE] = (
                o_scr[l, pl.ds(c, nstep, stride=SCAN_CHUNKS), :].astype(out_ref.dtype))


def _compress_kernel(x_ref, pe_ref, w1k_ref, w1v_ref, w2_ref, out_ref, wbig_ref):
    nrow = x_ref.shape[1]
    nchunk = nrow // CB
    nh = N_KV_HEADS * CMP_HIDDEN

    for which, w1_ref in enumerate((w1k_ref, w1v_ref)):
        @pl.when((pl.program_id(0) == which) & (pl.program_id(1) == 0))
        def _(w1_ref=w1_ref):
            wbig_ref[...] = jnp.zeros(wbig_ref.shape, BF16)
            for half in range(2):
                for l in range(CMP_STRIDE):
                    r0 = (half * CMP_STRIDE + l) * HEAD_DIM
                    blk = w1_ref[r0:r0 + HEAD_DIM, :].astype(BF16)
                    for g in range(N_KV_HEADS):
                        rows = slice(l * D_KV + g * HEAD_DIM, l * D_KV + (g + 1) * HEAD_DIM)
                        cols = slice(half * nh + g * CMP_HIDDEN, half * nh + (g + 1) * CMP_HIDDEN)
                        wbig_ref[rows, cols] = blk

    x = x_ref[0]
    xa = (x + pe_ref[0, 0:1, :]).astype(BF16)
    xb = (x + pe_ref[0, 1:2, :]).astype(BF16)
    first = jnp.dot(xa, wbig_ref[:, :nh], preferred_element_type=F32)
    second = jnp.dot(xb, wbig_ref[:, nh:], preferred_element_type=F32)
    hid = _silu(first + pltpu.roll(second, nrow - 1, 0))
    out = jnp.dot(hid.astype(BF16), w2_ref[0], preferred_element_type=F32)
    c_idx = lax.broadcasted_iota(jnp.int32, out.shape, 0) & (nchunk - 1)
    out_ref[0] = jnp.where(c_idx < nchunk - 1, out, 0.0)


def _compress(xc, pe2, w1k, w1v, w2bd, B, S):
    nchunk = S // CMP_STRIDE
    nrow = CB * nchunk
    width = CMP_STRIDE * D_KV
    kv = lambda i, j: (i, 0, 0)
    return pl.pallas_call(
        _compress_kernel,
        grid=(2, B // CB),
        in_specs=[
            pl.BlockSpec((1, nrow, width), lambda i, j: (i, j, 0)),
            pl.BlockSpec((1, 2, width), kv),
            pl.BlockSpec(w1k.shape, lambda i, j: (0, 0)),
            pl.BlockSpec(w1v.shape, lambda i, j: (0, 0)),
            pl.BlockSpec((1,) + w2bd.shape[1:], kv),
        ],
        out_specs=pl.BlockSpec((1, nrow, D_KV), lambda i, j: (i, j, 0)),
        out_shape=jax.ShapeDtypeStruct((2, B * nchunk, D_KV), F32),
        scratch_shapes=[pltpu.VMEM((width, 2 * N_KV_HEADS * CMP_HIDDEN), BF16)],
        compiler_params=pltpu.CompilerParams(
            dimension_semantics=("arbitrary", "arbitrary"), vmem_limit_bytes=VMEM_LIMIT),
        name="compress",
    )(xc, pe2, w1k, w1v, w2bd)


def _nsa_kernel(qt_ref, cmp_ref, ksel_ref, vselt_ref, kwin_ref, vwint_ref, brgt_ref, ag_ref,
                ovt_ref, out_ref, qpad_ref, pen_ref, o_ref, m_ref, acc_ref, s_ref, *, nq):
    qi = pl.program_id(1)
    q0 = qi * TQ
    nl = Q_PER_KV * TQ
    nblk = pen_ref.shape[1]
    ncmp = cmp_ref.shape[1]
    groups = range(N_KV_HEADS)
    gsl = [slice(g * HEAD_DIM, (g + 1) * HEAD_DIM) for g in groups]

    def tile4(a):
        return jnp.concatenate([a] * Q_PER_KV, axis=1)

    def sel_scores(kt, slot):
        k_tile = ksel_ref[pl.ds(pl.multiple_of(kt * KS, KS), KS), :]
        for g in groups:
            s_ref[slot, g] = jnp.dot(k_tile, qpad_ref[g], preferred_element_type=F32)

    n_full = q0 // KS

    ones = jnp.ones((ONES_ROWS, VC), BF16)

    def vt_aug(v_ref, chunks, g):
        vt = jnp.concatenate([v_ref[c, gsl[g], :] for c in chunks], axis=1)
        return jnp.concatenate([vt, jnp.concatenate([ones] * len(chunks), axis=1)], axis=0)

    def normalise(res):
        return res[0:HEAD_DIM, :] * (1.0 / res[HEAD_DIM:HEAD_DIM + 1, :])

    def first_block(ncmp_use, qi_static):
        kc = cmp_ref[0, 0:ncmp_use, :].astype(BF16)
        s_cmp = []
        for g in groups:
            zpad = jnp.zeros((HEAD_DIM, TQ), BF16)
            cols = []
            for r in range(Q_PER_KV):
                hq = g * Q_PER_KV + r
                qh = qt_ref[hq * HEAD_DIM:(hq + 1) * HEAD_DIM, :]
                cols.append(jnp.concatenate([qh, zpad] if g == 0 else [zpad, qh], axis=0))
            qpad = jnp.concatenate(cols, axis=1)
            qpad_ref[g] = qpad
            s_cmp.append(jnp.dot(kc, qpad, preferred_element_type=F32))

        c_row = lax.broadcasted_iota(jnp.int32, (ncmp_use, TQ), 0)
        t_lane = q0 + lax.broadcasted_iota(jnp.int32, (ncmp_use, TQ), 1)
        cpen = tile4(jnp.where(c_row * CMP_STRIDE + (CMP_BLOCK - 1) <= t_lane, 0.0, NEG))
        vc_t = cmp_ref[1, 0:ncmp_use, :].T
        for g in groups:
            vct = vc_t[gsl[g], :].astype(BF16)
            s = s_cmp[g] + cpen
            m = jnp.max(s, axis=0, keepdims=True)
            m = jnp.where(m > 0.5 * NEG, m, 0.0)
            e = jnp.exp2(s - m)
            den = jnp.sum(e, axis=0, keepdims=True)
            p = e * (1.0 / jnp.where(den > 0.0, den, 1.0))
            o_ref[0, g] = jnp.dot(vct, p.astype(BF16), preferred_element_type=F32)

            psum = p[:, 0:TQ]
            for r in range(1, Q_PER_KV):
                psum = psum + p[:, r * TQ:(r + 1) * TQ]
            if ncmp_use < ncmp:
                psum = jnp.concatenate([psum, jnp.zeros((ncmp - ncmp_use, TQ), F32)], axis=0)
            imp = jnp.dot(ovt_ref[...], psum, preferred_element_type=F32,
                          precision=lax.Precision.HIGHEST)
            j_row = lax.broadcasted_iota(jnp.int32, (nblk, TQ), 0)
            t_blk = q0 + lax.broadcasted_iota(jnp.int32, (nblk, TQ), 1)
            cur = jnp.right_shift(t_blk, SEL_BLOCK.bit_length() - 1)
            forced = (j_row == 0) | (j_row == cur) | (j_row == cur - 1)
            val = jnp.where(forced, SEL_FORCE, imp)
            val = jnp.where(j_row * SEL_BLOCK <= t_blk, val, -SEL_FORCE)
            rank = jnp.zeros((nblk, TQ), F32)
            for i in range(nblk):
                vi = jnp.broadcast_to(val[i:i + 1, :], (nblk, TQ))
                rank = rank + jnp.where(j_row > i, jnp.where(vi >= val, 1.0, 0.0),
                                        jnp.where(vi > val, 1.0, 0.0))
            chosen = (rank < float(min(SEL_TOPK, nblk))) & (val > -0.5 * SEL_FORCE)
            selpen = jnp.where(chosen, 0.0, NEG)
            for j in range(nblk):
                pen_ref[g, j] = jnp.broadcast_to(selpen[j:j + 1, :], (8, TQ))

        nchunk = WINDOW // TW + 1
        row_w = lax.broadcasted_iota(jnp.int32, (TW, TW), 0)
        lane_w = lax.broadcasted_iota(jnp.int32, (TW, TW), 1)
        edge_pen = {0: tile4(jnp.where(row_w > lane_w, 0.0, NEG)),
                    nchunk - 1: tile4(jnp.where(row_w <= lane_w, 0.0, NEG))}
        for sub in range(TQ // TW):
            qs = q0 + sub * TW
            first = 0 if qi_static is None else max(0, nchunk - 1 - (qi_static * TQ + sub * TW) // TW)
            chunks = range(first, nchunk)
            rows0 = [pl.multiple_of(qs - WINDOW + c * TW, TW) for c in chunks]
            k_win = jnp.concatenate([kwin_ref[pl.ds(r0, TW), :] for r0 in rows0], axis=0)
            cols = [slice(r * TQ + sub * TW, r * TQ + (sub + 1) * TW) for r in range(Q_PER_KV)]
            p_win = []
            for g in groups:
                q_sub = jnp.concatenate([qpad_ref[g, :, c] for c in cols], axis=1)
                s = jnp.dot(k_win, q_sub, preferred_element_type=F32)
                parts = []
                for n, c in enumerate(chunks):
                    s_c = s[n * TW:(n + 1) * TW, :]
                    parts.append(s_c + edge_pen[c] if c in edge_pen else s_c)
                m = jnp.max(parts[-1], axis=0, keepdims=True)
                for part in parts[:-1]:
                    m = jnp.maximum(m, jnp.max(part, axis=0, keepdims=True))
                p_win.append(jnp.concatenate([jnp.exp2(part - m) for part in parts],
                                             axis=0).astype(BF16))
            if sub == 0:
                sel_scores(0, n_full & 1)
            for g in groups:
                o_win = normalise(jnp.dot(vt_aug(vwint_ref, [r0 // VC for r0 in rows0], g),
                                          p_win[g], preferred_element_type=F32))
                for r, c in enumerate(cols):
                    o_ref[2, g, :, c] = o_win[:, r * TW:(r + 1) * TW]

    def cmp_rows(tile):
        return min(ncmp, -(-((tile + 1) * TQ // CMP_STRIDE) // CMP_ALIGN) * CMP_ALIGN)

    n_partial = min(nq, WINDOW // TQ)
    for tile in range(n_partial):
        pl.when(qi == tile)(functools.partial(first_block, cmp_rows(tile), tile))
    lo = n_partial
    while lo < nq:
        hi = lo
        while hi + 1 < nq and cmp_rows(hi + 1) == cmp_rows(lo):
            hi += 1
        pl.when((qi >= lo) & (qi <= hi))(functools.partial(first_block, cmp_rows(lo), None))
        lo = hi + 1

    m_ref[...] = jnp.full(m_ref.shape, NEG, F32)
    acc_ref[...] = jnp.zeros(acc_ref.shape, F32)
    bpt = KS // SEL_BLOCK
    cpt = KS // VC

    def sel_update(kt, slot, diag_keys=0):
        nkeys = diag_keys or KS
        probs, alphas = [], []
        for g in groups:
            m_prev = m_ref[g]
            if diag_keys:
                rows = [jnp.concatenate([pen_ref[g, kt * bpt + jj]] * (SEL_BLOCK // 8), axis=0)
                        for jj in range(nkeys // SEL_BLOCK)]
                pen = jnp.concatenate(rows, axis=0)
                key = kt * KS + lax.broadcasted_iota(jnp.int32, (nkeys, TQ), 0)
                t_q = q0 + lax.broadcasted_iota(jnp.int32, (nkeys, TQ), 1)
                s = s_ref[slot, g, 0:nkeys, :] + tile4(pen + jnp.where(key <= t_q, 0.0, NEG))
                m_new = jnp.maximum(m_prev, jnp.max(s, axis=0, keepdims=True))
                p = jnp.exp2(s - m_new)
            else:
                sb = [s_ref[slot, g, jj * SEL_BLOCK:(jj + 1) * SEL_BLOCK, :] for jj in range(bpt)]
                bias = [tile4(pen_ref[g, kt * bpt + jj][0:1, :]) for jj in range(bpt)]
                m_new = m_prev
                for jj in range(bpt):
                    m_new = jnp.maximum(m_new, jnp.max(sb[jj], axis=0, keepdims=True) + bias[jj])
                p = jnp.concatenate([jnp.exp2(sb[jj] + (bias[jj] - m_new)) for jj in range(bpt)],
                                    axis=0)
            probs.append(p.astype(BF16))
            alphas.append(jnp.exp2(m_prev - m_new))
            m_ref[g] = m_new
        for g in groups:
            acc_ref[g] = alphas[g] * acc_ref[g] + jnp.dot(
                vt_aug(vselt_ref, [kt * cpt + c for c in range(nkeys // VC)], g), probs[g],
                preferred_element_type=F32)

    def sel_body(kt, carry):
        for slot in (0, 1):
            @pl.when(((n_full - kt) & 1) == slot)
            def _(slot=slot):
                sel_scores(kt + 1, 1 - slot)
                sel_update(kt, slot)
        return carry

    lax.fori_loop(0, n_full, sel_body, 0)
    for nkeys in range(TQ, KS + 1, TQ):
        @pl.when(q0 + TQ - n_full * KS == nkeys)
        def _(nkeys=nkeys):
            sel_update(n_full, 0, diag_keys=nkeys)
    for g in groups:
        o_ref[1, g] = normalise(acc_ref[g])

    gates = jax.nn.sigmoid(brgt_ref[...])
    group_out = []
    for g in groups:
        heads = []
        for r in range(Q_PER_KV):
            hq = g * Q_PER_KV + r
            ls = slice(r * TQ, (r + 1) * TQ)
            o = gates[hq:hq + 1, :] * o_ref[0, g, :, ls]
            for n in range(1, N_BRANCH):
                o = o + gates[n * N_Q_HEADS + hq:n * N_Q_HEADS + hq + 1, :] * o_ref[n, g, :, ls]
            heads.append(o)
        group_out.append(jnp.concatenate(heads, axis=0).T)

    attn = jnp.concatenate(group_out, axis=1)
    out_ref[...] = (attn * _silu(ag_ref[...])).astype(out_ref.dtype)


def _nsa(qt, cmp_kv, ksel, vselt, kwin, vwint, brgt, ag, ovt, B, S):
    T = B * S
    nq = S // TQ
    qpm = TM // TQ
    nblk = S // SEL_BLOCK
    ncmp = S // CMP_STRIDE
    nl = Q_PER_KV * TQ
    return pl.pallas_call(
        functools.partial(_nsa_kernel, nq=nq),
        grid=(B, nq),
        in_specs=[
            pl.BlockSpec((None, None, D_ATTN, TQ), lambda b, q: (b, q // qpm, 0, q % qpm)),
            pl.BlockSpec((2, ncmp, D_KV), lambda b, q: (0, b, 0)),
            pl.BlockSpec((S, D_KV), lambda b, q: (b, 0)),
            pl.BlockSpec((None, S // VC, D_KV, VC), lambda b, q: (b, 0, 0, 0)),
            pl.BlockSpec((S, D_KV), lambda b, q: (b, 0)),
            pl.BlockSpec((None, S // VC, D_KV, VC), lambda b, q: (b, 0, 0, 0)),
            pl.BlockSpec((None, None, N_GATE_ROWS, TQ), lambda b, q: (b, q // qpm, 0, q % qpm)),
            pl.BlockSpec((TQ, D_ATTN), lambda b, q: (b * nq + q, 0)),
            pl.BlockSpec(ovt.shape, lambda b, q: (0, 0)),
        ],
        out_specs=pl.BlockSpec((TQ, D_ATTN), lambda b, q: (b * nq + q, 0)),
        out_shape=jax.ShapeDtypeStruct((T, D_ATTN), BF16),
        scratch_shapes=[
            pltpu.VMEM((N_KV_HEADS, D_KV, nl), BF16),
            pltpu.VMEM((N_KV_HEADS, nblk, 8, TQ), F32),
            pltpu.VMEM((N_BRANCH, N_KV_HEADS, HEAD_DIM, nl), F32),
            pltpu.VMEM((N_KV_HEADS, 1, nl), F32),
            pltpu.VMEM((N_KV_HEADS, HEAD_DIM + ONES_ROWS, nl), F32),
            pltpu.VMEM((2, N_KV_HEADS, KS, nl), F32),
        ],
        compiler_params=pltpu.CompilerParams(
            dimension_semantics=("arbitrary", "arbitrary"), vmem_limit_bytes=VMEM_LIMIT),
        name="nsa",
    )(qt, cmp_kv, ksel, vselt, kwin, vwint, brgt, ag, ovt)


def _outproj_kernel(x_ref, rnn_ref, attn_ref, wo_ref, nfw_ref, out_ref, wo_bf, *, final_norm):
    @pl.when(pl.program_id(0) == 0)
    def _():
        wo_bf[...] = wo_ref[...].astype(BF16)

    rows = TO // OUT_ROW_CHUNKS
    for c in range(OUT_ROW_CHUNKS):
        rs = slice(c * rows, (c + 1) * rows)
        mix = jnp.concatenate([rnn_ref[rs, :], attn_ref[rs, :]], axis=1)
        y = x_ref[rs, :] + jnp.dot(mix, wo_bf[...], preferred_element_type=F32)
        if final_norm:
            ms = jnp.mean(y * y, axis=-1, keepdims=True)
            y = (y * lax.rsqrt(ms + EPS)) * nfw_ref[...]
        out_ref[rs, :] = y


def _outproj(x2, rnn_out, attn_out, wo, nfw, final_norm):
    T = x2.shape[0]
    row = lambda i: (i, 0)
    return pl.pallas_call(
        functools.partial(_outproj_kernel, final_norm=final_norm),
        grid=(T // TO,),
        in_specs=[
            pl.BlockSpec((TO, D_MODEL), row),
            pl.BlockSpec((TO, D_RNN), row),
            pl.BlockSpec((TO, D_ATTN), row),
            pl.BlockSpec((D_MIX, D_MODEL), lambda i: (0, 0)),
            pl.BlockSpec((1, D_MODEL), lambda i: (0, 0)),
        ],
        out_specs=pl.BlockSpec((TO, D_MODEL), row),
        out_shape=jax.ShapeDtypeStruct((T, D_MODEL), F32),
        scratch_shapes=[pltpu.VMEM((D_MIX, D_MODEL), BF16)],
        compiler_params=pltpu.CompilerParams(
            dimension_semantics=("arbitrary",), vmem_limit_bytes=VMEM_LIMIT),
        name="outproj",
    )(x2, rnn_out, attn_out, wo, nfw)


def _block_diag_halves(wa, wx):
    eye = jnp.eye(RNN_HEADS, dtype=wa.dtype)
    full = lambda w: jnp.einsum('hij,hk->hikj', w, eye).reshape(D_RNN, D_RNN)
    fa, fx = full(wa), full(wx)
    half = D_RNN // 2
    return jnp.stack([
        jnp.concatenate([fa[s:s + half, s:s + half], fx[s:s + half, s:s + half]], axis=1)
        for s in (0, half)]).astype(BF16)


def _compress_weights(pe, w2):
    eye = jnp.eye(N_KV_HEADS, dtype=w2.dtype)
    w2bd = jnp.einsum('nd,ge->gned', w2, eye).reshape(N_KV_HEADS * CMP_HIDDEN, D_KV)
    per = pe.reshape(2, CMP_STRIDE, 1, HEAD_DIM)
    pe2 = jnp.broadcast_to(per, (2, CMP_STRIDE, N_KV_HEADS, HEAD_DIM)).reshape(2, CMP_STRIDE * D_KV)
    return pe2, w2bd.astype(BF16)


def _overlap_t(ncmp_pad, nblk):
    cs = np.arange(ncmp_pad)[None, :] * CMP_STRIDE
    ss = np.arange(nblk)[:, None] * SEL_BLOCK
    ov = np.clip(np.minimum(cs + CMP_BLOCK, ss + SEL_BLOCK) - np.maximum(cs, ss), 0, None)
    return jnp.asarray(ov.astype(np.float32) / CMP_BLOCK)


def kernel(x, norm1_w, w_in, conv_w, conv_b, rg_wa, rg_ba, rg_wx, rg_bx, rg_lambda,
           cmp_k_pe, cmp_k_w1, cmp_k_w2, cmp_v_pe, cmp_v_w1, cmp_v_w2, w_out, normf_w):
    B, S, D = x.shape
    assert D == D_MODEL and w_in.shape[-1] == _D_IN
    assert S % TM == 0 and TS == TM and B % CB == 0
    assert TQ % TW == 0 and TW % VC == 0 and WINDOW % TW == 0 and KS % TQ == 0 and TM % TQ == 0
    assert S % KS == 0 and S >= WINDOW + TQ and (B * S) % TO == 0
    depth = w_in.shape[0]
    T = B * S
    ovt = _overlap_t(S // CMP_STRIDE, S // SEL_BLOCK)
    x2 = x.reshape(T, D)
    for l in range(depth):
        wn = w_in[l].astype(BF16)
        wb = jnp.pad(w_in[l][:, _O_BR:], ((0, 0), (0, _O_BR + LANE - _D_IN))).astype(BF16)
        (rnn_out, kvc, ksel, kwin, ag, qt, vselt, vwint, brgt) = _inproj(
            x2, norm1_w[l].reshape(1, D), wn, wb,
            conv_w[l], conv_b[l].reshape(1, D_RNN), _block_diag_halves(rg_wa[l], rg_wx[l]),
            rg_ba[l].reshape(1, D_RNN), rg_bx[l].reshape(1, D_RNN),
            rg_lambda[l].reshape(1, D_RNN), B, S)

        pk, w2k = _compress_weights(cmp_k_pe[l], cmp_k_w2[l])
        pv, w2v = _compress_weights(cmp_v_pe[l], cmp_v_w2[l])
        cmp_kv = _compress(kvc, jnp.stack([pk, pv]), cmp_k_w1[l], cmp_v_w1[l],
                           jnp.stack([w2k, w2v]), B, S)

        attn_out = _nsa(qt, cmp_kv, ksel, vselt, kwin, vwint, brgt, ag, ovt, B, S)

        x2 = _outproj(x2, rnn_out, attn_out, w_out[l],
                      normf_w.reshape(1, D), final_norm=(l == depth - 1))
    return x2.reshape(B, S, D)
```

```python
import functools

import numpy as np
import jax
import jax.numpy as jnp
from jax import lax
from jax.experimental import pallas as pl
from jax.experimental.pallas import tpu as pltpu

F32 = jnp.float32
BF16 = jnp.bfloat16

D_MODEL = 1024
EPS = 1e-6
D_RNN = 512
RNN_HEADS = 8
RNN_HEAD_DIM = D_RNN // RNN_HEADS
CONV_WIDTH = 4
LRU_C = 8.0
N_Q_HEADS = 8
N_KV_HEADS = 2
HEAD_DIM = 64
Q_PER_KV = N_Q_HEADS // N_KV_HEADS
D_ATTN = N_Q_HEADS * HEAD_DIM
D_KV = N_KV_HEADS * HEAD_DIM
CMP_BLOCK = 32
CMP_STRIDE = 16
CMP_HIDDEN = 256
SEL_BLOCK = 64
SEL_TOPK = 8
SEL_FORCE = 1e9
WINDOW = 512
N_BRANCH = 3
D_MIX = D_RNN + D_ATTN
N_GATE_ROWS = 32

LANE = 128
SCAN_CHUNKS = 8

TM = 1024
TS = 1024
TO = 1024
OUT_ROW_CHUNKS = 4
TQ = 256
KS = 512
VC = 128
TW = 128
CMP_ALIGN = 32
CB = 4
NEG = -1e30
LOG2E = 1.4426950408889634
ONES_ROWS = 16
VMEM_V7X = 64 * 1024 * 1024
VMEM_LIMIT = VMEM_V7X - 8 * 1024 * 1024

_O_RX, _O_RG, _O_Q = 0, D_RNN, 2 * D_RNN
_O_KC = _O_Q + D_ATTN
_O_VC = _O_KC + D_KV
_O_KS = _O_VC + D_KV
_O_VS = _O_KS + D_KV
_O_KW = _O_VS + D_KV
_O_VW = _O_KW + D_KV
_O_AG = _O_VW + D_KV
_O_BR = _O_AG + D_ATTN
_D_IN = _O_BR + N_BRANCH * N_Q_HEADS


def _silu(x):
    return x * jax.nn.sigmoid(x)


def _inproj_kernel(x_ref, nw_ref, wn_ref, wb_ref, cw_ref, cb_ref, wg_ref, ba_ref, bx_ref, lam_ref,
                   rnn_ref, kvc_ref, ksel_ref, kwin_ref, ag_ref,
                   qt_ref, vselt_ref, vwint_ref, brgt_ref,
                   kv_scr, rx_scr, rg_scr, tail_ref, a_s, u_s, h_s, p_s, hlast, o_scr):
    @pl.when(pl.program_id(1) == 0)
    def _():
        tail_ref[...] = jnp.zeros(tail_ref.shape, F32)
        hlast[...] = jnp.zeros(hlast.shape, F32)

    x = x_ref[...]
    ms = jnp.mean(x * x, axis=-1, keepdims=True)
    h = ((x * lax.rsqrt(ms + EPS)) * nw_ref[...]).astype(BF16)

    def nat(a, b):
        return jnp.dot(h, wn_ref[:, a:b], preferred_element_type=F32)

    def store_step_major(dst_ref, val):
        steps = TM // SCAN_CHUNKS
        for l in range(D_RNN // LANE):
            for c in range(SCAN_CHUNKS):
                dst_ref[l, pl.ds(c, steps, stride=SCAN_CHUNKS), :] = (
                    val[c * steps:(c + 1) * steps, l * LANE:(l + 1) * LANE])

    store_step_major(rx_scr, nat(_O_RX, _O_RG))
    store_step_major(rg_scr, nat(_O_RG, _O_Q))

    def proj_cmp():
        kv = nat(_O_KC, _O_KS)
        for a in range(2):
            kv_scr[a] = kv[:, a * D_KV:(a + 1) * D_KV]
            for l in range(CMP_STRIDE):
                kvc_ref[a, :, l * D_KV:(l + 1) * D_KV] = (
                    kv_scr[a, pl.ds(l, TM // CMP_STRIDE, stride=CMP_STRIDE), :])

    def proj_kv(lo, k_ref, vt_ref):
        kv = nat(lo, lo + 2 * D_KV)
        k_ref[...] = kv[:, :D_KV].astype(BF16)
        vt = kv[:, D_KV:].T.astype(BF16)
        for c in range(TM // VC):
            vt_ref[c] = vt[:, c * VC:(c + 1) * VC]

    def proj_gates():
        ag_ref[...] = nat(_O_AG, _O_BR)
        brg = jnp.dot(h, wb_ref[...], preferred_element_type=F32)
        brgt_ref[...] = brg.T[0:N_GATE_ROWS, :]

    def proj_q():
        qt_ref[...] = (nat(_O_Q, _O_KC) * (HEAD_DIM ** -0.5 * LOG2E)).T.astype(BF16)

    projections = ((proj_gates,), (proj_q,),
                   (functools.partial(proj_kv, _O_KS, ksel_ref, vselt_ref),
                    functools.partial(proj_kv, _O_KW, kwin_ref, vwint_ref)),
                   (proj_cmp,))
    wraps = _rglru_wraps(rx_scr, tail_ref)
    rows = TM // len(projections)
    for ci, projs in enumerate(projections):
        _rglru_gates(rx_scr, wraps, ci * rows, (ci + 1) * rows,
                     cw_ref, cb_ref, wg_ref, ba_ref, bx_ref, lam_ref, a_s, u_s)
        for proj in projs:
            proj()

    _rglru_scan(rg_scr, rnn_ref, a_s, u_s, h_s, p_s, hlast, o_scr)


def _inproj(x2, nw, wn, wb, cw, cb, wg, ba, bx, lam, B, S):
    T = B * S
    ns = S // TM
    grid = (B, ns)
    row = lambda b, s: (b * ns + s, 0)
    const2 = lambda b, s: (0, 0)
    out_shape = (
        jax.ShapeDtypeStruct((T, D_RNN), BF16),
        jax.ShapeDtypeStruct((2, T // CMP_STRIDE, CMP_STRIDE * D_KV), F32),
        jax.ShapeDtypeStruct((T, D_KV), BF16),
        jax.ShapeDtypeStruct((T, D_KV), BF16),
        jax.ShapeDtypeStruct((T, D_ATTN), F32),
        jax.ShapeDtypeStruct((B, ns, D_ATTN, TM), BF16),
        jax.ShapeDtypeStruct((B, S // VC, D_KV, VC), BF16),
        jax.ShapeDtypeStruct((B, S // VC, D_KV, VC), BF16),
        jax.ShapeDtypeStruct((B, ns, N_GATE_ROWS, TM), F32),
    )
    out_specs = (
        pl.BlockSpec((TM, D_RNN), row),
        pl.BlockSpec((2, TM // CMP_STRIDE, CMP_STRIDE * D_KV), lambda b, s: (0, b * ns + s, 0)),
        pl.BlockSpec((TM, D_KV), row),
        pl.BlockSpec((TM, D_KV), row),
        pl.BlockSpec((TM, D_ATTN), row),
        pl.BlockSpec((None, None, D_ATTN, TM), lambda b, s: (b, s, 0, 0)),
        pl.BlockSpec((None, TM // VC, D_KV, VC), lambda b, s: (b, s, 0, 0)),
        pl.BlockSpec((None, TM // VC, D_KV, VC), lambda b, s: (b, s, 0, 0)),
        pl.BlockSpec((None, None, N_GATE_ROWS, TM), lambda b, s: (b, s, 0, 0)),
    )
    return pl.pallas_call(
        _inproj_kernel,
        grid=grid,
        in_specs=[
            pl.BlockSpec((TM, D_MODEL), row),
            pl.BlockSpec((1, D_MODEL), const2),
            pl.BlockSpec(wn.shape, const2),
            pl.BlockSpec(wb.shape, const2),
            pl.BlockSpec((CONV_WIDTH, D_RNN), const2),
            pl.BlockSpec((1, D_RNN), const2),
            pl.BlockSpec(wg.shape, lambda b, s: (0, 0, 0)),
            pl.BlockSpec((1, D_RNN), const2),
            pl.BlockSpec((1, D_RNN), const2),
            pl.BlockSpec((1, D_RNN), const2),
        ],
        out_specs=out_specs,
        out_shape=out_shape,
        scratch_shapes=[
            pltpu.VMEM((2, TM, D_KV), F32),
            pltpu.VMEM((D_RNN // LANE, TM, LANE), F32),
            pltpu.VMEM((D_RNN // LANE, TM, LANE), F32),
            pltpu.VMEM(((CONV_WIDTH - 1) * SCAN_CHUNKS, D_RNN), F32),
            pltpu.VMEM((TM, D_RNN), F32),
            pltpu.VMEM((TM, D_RNN), F32),
            pltpu.VMEM((TM, D_RNN), F32),
            pltpu.VMEM((TM, D_RNN), F32),
            pltpu.VMEM((1, D_RNN), F32),
            pltpu.VMEM((D_RNN // LANE, TM, LANE), F32),
        ],
        compiler_params=pltpu.CompilerParams(
            dimension_semantics=("arbitrary", "arbitrary"), vmem_limit_bytes=VMEM_LIMIT),
        name="inproj",
    )(x2, nw, wn, wb, cw, cb, wg, ba, bx, lam)


def _slab_rows(x_ref, lo, hi):
    return jnp.concatenate([x_ref[l, lo:hi, :] for l in range(D_RNN // LANE)], axis=1)


def _rglru_wraps(x_ref, tail_ref):
    ntail = (CONV_WIDTH - 1) * SCAN_CHUNKS
    last = _slab_rows(x_ref, TS - ntail, TS)
    sub = lax.broadcasted_iota(jnp.int32, (SCAN_CHUNKS, D_RNN), 0)
    wraps = []
    for j in range(CONV_WIDTH - 1):
        rows = slice(j * SCAN_CHUNKS, (j + 1) * SCAN_CHUNKS)
        cur = pltpu.roll(last[rows, :], 1, 0)
        prev = pltpu.roll(tail_ref[rows, :], 1, 0)
        wraps.append(jnp.where(sub == 0, prev, cur))
    tail_ref[...] = last
    return wraps


def _rglru_gates(x_ref, wraps, lo, hi, cw_ref, cb_ref, wg_ref, ba_ref, bx_ref, lam_ref, a_s, u_s):
    y = cb_ref[...]
    for k in range(CONV_WIDTH):
        off = (CONV_WIDTH - 1 - k) * SCAN_CHUNKS
        if lo >= off:
            xs = _slab_rows(x_ref, lo - off, hi - off)
        else:
            assert lo == 0
            xs = jnp.concatenate(wraps[len(wraps) - off // SCAN_CHUNKS:]
                                 + [_slab_rows(x_ref, 0, hi - off)], axis=0)
        y = y + xs * cw_ref[k:k + 1, :]

    yb = y.astype(BF16)
    half = D_RNN // 2
    pre = [jnp.dot(yb[:, hh * half:(hh + 1) * half], wg_ref[hh], preferred_element_type=F32)
           for hh in range(2)]
    pre_a = jnp.concatenate([pre[0][:, :half], pre[1][:, :half]], axis=1)
    pre_x = jnp.concatenate([pre[0][:, half:], pre[1][:, half:]], axis=1)
    r = jax.nn.sigmoid(pre_a + ba_ref[...])
    i = jax.nn.sigmoid(pre_x + bx_ref[...])
    lam = lam_ref[...]
    lsig = jnp.minimum(lam, 0.0) - jnp.log1p(jnp.exp(-jnp.abs(lam)))
    a = jnp.exp2(r * ((LRU_C * LOG2E) * lsig))
    u = jnp.sqrt(1.0 - a * a) * (i * y)
    a_s[lo:hi, :] = a
    u_s[lo:hi, :] = u


def _rglru_scan(gate_ref, out_ref, a_s, u_s, h_s, p_s, hlast, o_scr):
    nstep = TS // SCAN_CHUNKS
    nslab = D_RNN // LANE
    gate = jnp.concatenate([gate_ref[l] for l in range(nslab)], axis=1)

    def body(j, carry):
        h, p = carry
        blk = pl.multiple_of(j * SCAN_CHUNKS, SCAN_CHUNKS)
        a_j = a_s[pl.ds(blk, SCAN_CHUNKS), :]
        h = a_j * h + u_s[pl.ds(blk, SCAN_CHUNKS), :]
        p = a_j * p
        h_s[pl.ds(blk, SCAN_CHUNKS), :] = h
        p_s[pl.ds(blk, SCAN_CHUNKS), :] = p
        return h, p

    h_end, p_end = lax.fori_loop(
        0, nstep, body,
        (jnp.zeros((SCAN_CHUNKS, D_RNN), F32), jnp.ones((SCAN_CHUNKS, D_RNN), F32)), unroll=8)
    carry = hlast[...]
    h_in = []
    for c in range(SCAN_CHUNKS):
        h_in.append(carry)
        carry = h_end[c:c + 1, :] + p_end[c:c + 1, :] * carry
    hlast[...] = carry
    h_in = jnp.concatenate([jnp.concatenate(h_in, axis=0)] * nstep, axis=0)
    o = (h_s[...] + p_s[...] * h_in) * _silu(gate)

    for l in range(nslab):
        o_scr[l] = o[:, l * LANE:(l + 1) * LANE]
        for c in range(SCAN_CHUNKS):
            out_ref[c * nstep:(c + 1) * nstep, l * LANE:(l + 1) * LANE] = (
                o_scr[l, pl.ds(c, nstep, stride=SCAN_CHUNKS), :].astype(out_ref.dtype))


def _compress_kernel(x_ref, pe_ref, w1k_ref, w1v_ref, w2_ref, out_ref, w1_bf):
    nrow = x_ref.shape[1]
    nchunk = nrow // CB
    half = CMP_STRIDE * HEAD_DIM

    for which, w1_ref in enumerate((w1k_ref, w1v_ref)):
        @pl.when((pl.program_id(0) == which) & (pl.program_id(1) == 0))
        def _(w1_ref=w1_ref):
            w1_bf[...] = w1_ref[...].astype(BF16)

    x = x_ref[0]
    x2 = jnp.concatenate(
        [jnp.concatenate([x[:, l * D_KV + g * HEAD_DIM:l * D_KV + (g + 1) * HEAD_DIM]
                          for l in range(CMP_STRIDE)], axis=1)
         for g in range(N_KV_HEADS)], axis=0)
    xa = (x2 + pe_ref[0, 0:1, :]).astype(BF16)
    xb = (x2 + pe_ref[0, 1:2, :]).astype(BF16)
    first = jnp.dot(xa, w1_bf[0:half, :], preferred_element_type=F32)
    second = jnp.dot(xb, w1_bf[half:2 * half, :], preferred_element_type=F32)
    hid = _silu(first + pltpu.roll(second, N_KV_HEADS * nrow - 1, 0))
    out2 = jnp.dot(hid.astype(BF16), w2_ref[0], preferred_element_type=F32)
    out = jnp.concatenate([out2[g * nrow:(g + 1) * nrow, :] for g in range(N_KV_HEADS)], axis=1)
    c_idx = lax.broadcasted_iota(jnp.int32, out.shape, 0) & (nchunk - 1)
    out_ref[0] = jnp.where(c_idx < nchunk - 1, out, 0.0)


def _compress(xc, pe2, w1k, w1v, w2, B, S):
    nchunk = S // CMP_STRIDE
    nrow = CB * nchunk
    width = CMP_STRIDE * D_KV
    kv = lambda i, j: (i, 0, 0)
    return pl.pallas_call(
        _compress_kernel,
        grid=(2, B // CB),
        in_specs=[
            pl.BlockSpec((1, nrow, width), lambda i, j: (i, j, 0)),
            pl.BlockSpec((1,) + pe2.shape[1:], kv),
            pl.BlockSpec(w1k.shape, lambda i, j: (0, 0)),
            pl.BlockSpec(w1v.shape, lambda i, j: (0, 0)),
            pl.BlockSpec((1,) + w2.shape[1:], kv),
        ],
        out_specs=pl.BlockSpec((1, nrow, D_KV), lambda i, j: (i, j, 0)),
        out_shape=jax.ShapeDtypeStruct((2, B * nchunk, D_KV), F32),
        scratch_shapes=[pltpu.VMEM(w1k.shape, BF16)],
        compiler_params=pltpu.CompilerParams(
            dimension_semantics=("arbitrary", "arbitrary"), vmem_limit_bytes=VMEM_LIMIT),
        name="compress",
    )(xc, pe2, w1k, w1v, w2)


def _nsa_kernel(qt_ref, cmp_ref, ksel_ref, vselt_ref, kwin_ref, vwint_ref, brgt_ref, ag_ref,
                ovt_ref, out_ref, qpad_ref, pen_ref, o_ref, m_ref, acc_ref, s_ref, *, nq):
    qi = pl.program_id(1)
    q0 = qi * TQ
    nl = Q_PER_KV * TQ
    nblk = pen_ref.shape[1]
    ncmp = cmp_ref.shape[1]
    groups = range(N_KV_HEADS)
    gsl = [slice(g * HEAD_DIM, (g + 1) * HEAD_DIM) for g in groups]

    def tile4(a):
        return jnp.concatenate([a] * Q_PER_KV, axis=1)

    def sel_scores(kt, slot, nkeys=KS):
        k_tile = ksel_ref[pl.ds(pl.multiple_of(kt * KS, KS), nkeys), :]
        for g in groups:
            s_ref[slot, g, 0:nkeys, :] = jnp.dot(k_tile, qpad_ref[g], preferred_element_type=F32)

    n_full = q0 // KS

    ones = jnp.ones((ONES_ROWS, VC), BF16)

    def vt_aug(v_ref, chunks, g):
        vt = jnp.concatenate([v_ref[c, gsl[g], :] for c in chunks], axis=1)
        return jnp.concatenate([vt, jnp.concatenate([ones] * len(chunks), axis=1)], axis=0)

    def normalise(res):
        return res[0:HEAD_DIM, :] * (1.0 / res[HEAD_DIM:HEAD_DIM + 1, :])

    def first_block(ncmp_use, qi_static):
        kc = cmp_ref[0, 0:ncmp_use, :].astype(BF16)
        s_cmp = []
        for g in groups:
            zpad = jnp.zeros((HEAD_DIM, TQ), BF16)
            cols = []
            for r in range(Q_PER_KV):
                hq = g * Q_PER_KV + r
                qh = qt_ref[hq * HEAD_DIM:(hq + 1) * HEAD_DIM, :]
                cols.append(jnp.concatenate([qh, zpad] if g == 0 else [zpad, qh], axis=0))
            qpad = jnp.concatenate(cols, axis=1)
            qpad_ref[g] = qpad
            s_cmp.append(jnp.dot(kc, qpad, preferred_element_type=F32))

        c_row = lax.broadcasted_iota(jnp.int32, (ncmp_use, TQ), 0)
        t_lane = q0 + lax.broadcasted_iota(jnp.int32, (ncmp_use, TQ), 1)
        cpen = tile4(jnp.where(c_row * CMP_STRIDE + (CMP_BLOCK - 1) <= t_lane, 0.0, NEG))
        vc_t = cmp_ref[1, 0:ncmp_use, :].T
        for g in groups:
            vct = vc_t[gsl[g], :].astype(BF16)
            s = s_cmp[g] + cpen
            m = jnp.max(s, axis=0, keepdims=True)
            m = jnp.where(m > 0.5 * NEG, m, 0.0)
            e = jnp.exp2(s - m)
            den = jnp.sum(e, axis=0, keepdims=True)
            p = e * (1.0 / jnp.where(den > 0.0, den, 1.0))
            o_ref[0, g] = jnp.dot(vct, p.astype(BF16), preferred_element_type=F32)

            psum = p[:, 0:TQ]
            for r in range(1, Q_PER_KV):
                psum = psum + p[:, r * TQ:(r + 1) * TQ]
            if ncmp_use < ncmp:
                psum = jnp.concatenate([psum, jnp.zeros((ncmp - ncmp_use, TQ), F32)], axis=0)
            imp = jnp.dot(ovt_ref[...], psum, preferred_element_type=F32,
                          precision=lax.Precision.HIGHEST)
            j_row = lax.broadcasted_iota(jnp.int32, (nblk, TQ), 0)
            t_blk = q0 + lax.broadcasted_iota(jnp.int32, (nblk, TQ), 1)
            cur = jnp.right_shift(t_blk, SEL_BLOCK.bit_length() - 1)
            forced = (j_row == 0) | (j_row == cur) | (j_row == cur - 1)
            val = jnp.where(forced, SEL_FORCE, imp)
            val = jnp.where(j_row * SEL_BLOCK <= t_blk, val, -SEL_FORCE)
            rank = jnp.zeros((nblk, TQ), F32)
            for i in range(nblk):
                vi = jnp.broadcast_to(val[i:i + 1, :], (nblk, TQ))
                rank = rank + jnp.where(j_row > i, jnp.where(vi >= val, 1.0, 0.0),
                                        jnp.where(vi > val, 1.0, 0.0))
            chosen = (rank < float(min(SEL_TOPK, nblk))) & (val > -0.5 * SEL_FORCE)
            selpen = jnp.where(chosen, 0.0, NEG)
            for j in range(nblk):
                pen_ref[g, j] = jnp.broadcast_to(selpen[j:j + 1, :], (8, TQ))

        nchunk = WINDOW // TW + 1
        row_w = lax.broadcasted_iota(jnp.int32, (TW, TW), 0)
        lane_w = lax.broadcasted_iota(jnp.int32, (TW, TW), 1)
        edge_pen = {0: tile4(jnp.where(row_w > lane_w, 0.0, NEG)),
                    nchunk - 1: tile4(jnp.where(row_w <= lane_w, 0.0, NEG))}
        for sub in range(TQ // TW):
            qs = q0 + sub * TW
            first = 0 if qi_static is None else max(0, nchunk - 1 - (qi_static * TQ + sub * TW) // TW)
            chunks = range(first, nchunk)
            rows0 = [pl.multiple_of(qs - WINDOW + c * TW, TW) for c in chunks]
            k_win = jnp.concatenate([kwin_ref[pl.ds(r0, TW), :] for r0 in rows0], axis=0)
            cols = [slice(r * TQ + sub * TW, r * TQ + (sub + 1) * TW) for r in range(Q_PER_KV)]
            p_win = []
            for g in groups:
                q_sub = jnp.concatenate([qpad_ref[g, :, c] for c in cols], axis=1)
                s = jnp.dot(k_win, q_sub, preferred_element_type=F32)
                parts = []
                for n, c in enumerate(chunks):
                    s_c = s[n * TW:(n + 1) * TW, :]
                    parts.append(s_c + edge_pen[c] if c in edge_pen else s_c)
                m = jnp.max(parts[-1], axis=0, keepdims=True)
                for part in parts[:-1]:
                    m = jnp.maximum(m, jnp.max(part, axis=0, keepdims=True))
                p_win.append(jnp.concatenate([jnp.exp2(part - m) for part in parts],
                                             axis=0).astype(BF16))
            if sub == 0:
                first_keys = KS if qi_static is None else min(KS, (qi_static + 1) * TQ)
                sel_scores(0, n_full & 1, first_keys)
            for g in groups:
                o_win = normalise(jnp.dot(vt_aug(vwint_ref, [r0 // VC for r0 in rows0], g),
                                          p_win[g], preferred_element_type=F32))
                for r, c in enumerate(cols):
                    o_ref[2, g, :, c] = o_win[:, r * TW:(r + 1) * TW]

    def cmp_rows(tile):
        return min(ncmp, -(-((tile + 1) * TQ // CMP_STRIDE) // CMP_ALIGN) * CMP_ALIGN)

    n_partial = min(nq, WINDOW // TQ)
    for tile in range(n_partial):
        pl.when(qi == tile)(functools.partial(first_block, cmp_rows(tile), tile))
    lo = n_partial
    while lo < nq:
        hi = lo
        while hi + 1 < nq and cmp_rows(hi + 1) == cmp_rows(lo):
            hi += 1
        pl.when((qi >= lo) & (qi <= hi))(functools.partial(first_block, cmp_rows(lo), None))
        lo = hi + 1

    m_ref[...] = jnp.full(m_ref.shape, NEG, F32)
    acc_ref[...] = jnp.zeros(acc_ref.shape, F32)
    bpt = KS // SEL_BLOCK
    cpt = KS // VC

    def sel_update(kt, slot, diag_keys=0):
        nkeys = diag_keys or KS
        probs, alphas = [], []
        for g in groups:
            m_prev = m_ref[g]
            if diag_keys:
                rows = [jnp.concatenate([pen_ref[g, kt * bpt + jj]] * (SEL_BLOCK // 8), axis=0)
                        for jj in range(nkeys // SEL_BLOCK)]
                pen = jnp.concatenate(rows, axis=0)
                key = kt * KS + lax.broadcasted_iota(jnp.int32, (nkeys, TQ), 0)
                t_q = q0 + lax.broadcasted_iota(jnp.int32, (nkeys, TQ), 1)
                s = s_ref[slot, g, 0:nkeys, :] + tile4(pen + jnp.where(key <= t_q, 0.0, NEG))
                m_new = jnp.maximum(m_prev, jnp.max(s, axis=0, keepdims=True))
                p = jnp.exp2(s - m_new)
            else:
                sb = [s_ref[slot, g, jj * SEL_BLOCK:(jj + 1) * SEL_BLOCK, :] for jj in range(bpt)]
                bias = [tile4(pen_ref[g, kt * bpt + jj][0:1, :]) for jj in range(bpt)]
                m_new = m_prev
                for jj in range(bpt):
                    m_new = jnp.maximum(m_new, jnp.max(sb[jj], axis=0, keepdims=True) + bias[jj])
                p = jnp.concatenate([jnp.exp2(sb[jj] + (bias[jj] - m_new)) for jj in range(bpt)],
                                    axis=0)
            probs.append(p.astype(BF16))
            alphas.append(jnp.exp2(m_prev - m_new))
            m_ref[g] = m_new
        for g in groups:
            acc_ref[g] = alphas[g] * acc_ref[g] + jnp.dot(
                vt_aug(vselt_ref, [kt * cpt + c for c in range(nkeys // VC)], g), probs[g],
                preferred_element_type=F32)

    last_keys = q0 + TQ - n_full * KS

    def sel_body(kt, carry):
        next_keys = jnp.where(kt + 1 == n_full, last_keys, KS)
        for slot in (0, 1):
            for nkeys in range(TQ, KS + 1, TQ):
                @pl.when((((n_full - kt) & 1) == slot) & (next_keys == nkeys))
                def _(slot=slot, nkeys=nkeys):
                    sel_scores(kt + 1, 1 - slot, nkeys)
                    sel_update(kt, slot)
        return carry

    lax.fori_loop(0, n_full, sel_body, 0)
    for nkeys in range(TQ, KS + 1, TQ):
        @pl.when(last_keys == nkeys)
        def _(nkeys=nkeys):
            sel_update(n_full, 0, diag_keys=nkeys)
    for g in groups:
        o_ref[1, g] = normalise(acc_ref[g])

    gates = jax.nn.sigmoid(brgt_ref[...])
    group_out = []
    for g in groups:
        heads = []
        for r in range(Q_PER_KV):
            hq = g * Q_PER_KV + r
            ls = slice(r * TQ, (r + 1) * TQ)
            o = gates[hq:hq + 1, :] * o_ref[0, g, :, ls]
            for n in range(1, N_BRANCH):
                o = o + gates[n * N_Q_HEADS + hq:n * N_Q_HEADS + hq + 1, :] * o_ref[n, g, :, ls]
            heads.append(o)
        group_out.append(jnp.concatenate(heads, axis=0).T)

    attn = jnp.concatenate(group_out, axis=1)
    out_ref[...] = (attn * _silu(ag_ref[...])).astype(out_ref.dtype)


def _nsa(qt, cmp_kv, ksel, vselt, kwin, vwint, brgt, ag, ovt, B, S):
    T = B * S
    nq = S // TQ
    qpm = TM // TQ
    nblk = S // SEL_BLOCK
    ncmp = S // CMP_STRIDE
    nl = Q_PER_KV * TQ
    return pl.pallas_call(
        functools.partial(_nsa_kernel, nq=nq),
        grid=(B, nq),
        in_specs=[
            pl.BlockSpec((None, None, D_ATTN, TQ), lambda b, q: (b, q // qpm, 0, q % qpm)),
            pl.BlockSpec((2, ncmp, D_KV), lambda b, q: (0, b, 0)),
            pl.BlockSpec((S, D_KV), lambda b, q: (b, 0)),
            pl.BlockSpec((None, S // VC, D_KV, VC), lambda b, q: (b, 0, 0, 0)),
            pl.BlockSpec((S, D_KV), lambda b, q: (b, 0)),
            pl.BlockSpec((None, S // VC, D_KV, VC), lambda b, q: (b, 0, 0, 0)),
            pl.BlockSpec((None, None, N_GATE_ROWS, TQ), lambda b, q: (b, q // qpm, 0, q % qpm)),
            pl.BlockSpec((TQ, D_ATTN), lambda b, q: (b * nq + q, 0)),
            pl.BlockSpec(ovt.shape, lambda b, q: (0, 0)),
        ],
        out_specs=pl.BlockSpec((TQ, D_ATTN), lambda b, q: (b * nq + q, 0)),
        out_shape=jax.ShapeDtypeStruct((T, D_ATTN), BF16),
        scratch_shapes=[
            pltpu.VMEM((N_KV_HEADS, D_KV, nl), BF16),
            pltpu.VMEM((N_KV_HEADS, nblk, 8, TQ), F32),
            pltpu.VMEM((N_BRANCH, N_KV_HEADS, HEAD_DIM, nl), F32),
            pltpu.VMEM((N_KV_HEADS, 1, nl), F32),
            pltpu.VMEM((N_KV_HEADS, HEAD_DIM + ONES_ROWS, nl), F32),
            pltpu.VMEM((2, N_KV_HEADS, KS, nl), F32),
        ],
        compiler_params=pltpu.CompilerParams(
            dimension_semantics=("arbitrary", "arbitrary"), vmem_limit_bytes=VMEM_LIMIT),
        name="nsa",
    )(qt, cmp_kv, ksel, vselt, kwin, vwint, brgt, ag, ovt)


def _outproj_kernel(x_ref, rnn_ref, attn_ref, wo_ref, nfw_ref, out_ref, wo_bf, *, final_norm):
    @pl.when(pl.program_id(0) == 0)
    def _():
        wo_bf[...] = wo_ref[...].astype(BF16)

    rows = TO // OUT_ROW_CHUNKS
    for c in range(OUT_ROW_CHUNKS):
        rs = slice(c * rows, (c + 1) * rows)
        mix = jnp.concatenate([rnn_ref[rs, :], attn_ref[rs, :]], axis=1)
        y = x_ref[rs, :] + jnp.dot(mix, wo_bf[...], preferred_element_type=F32)
        if final_norm:
            ms = jnp.mean(y * y, axis=-1, keepdims=True)
            y = (y * lax.rsqrt(ms + EPS)) * nfw_ref[...]
        out_ref[rs, :] = y


def _outproj(x2, rnn_out, attn_out, wo, nfw, final_norm):
    T = x2.shape[0]
    row = lambda i: (i, 0)
    return pl.pallas_call(
        functools.partial(_outproj_kernel, final_norm=final_norm),
        grid=(T // TO,),
        in_specs=[
            pl.BlockSpec((TO, D_MODEL), row),
            pl.BlockSpec((TO, D_RNN), row),
            pl.BlockSpec((TO, D_ATTN), row),
            pl.BlockSpec((D_MIX, D_MODEL), lambda i: (0, 0)),
            pl.BlockSpec((1, D_MODEL), lambda i: (0, 0)),
        ],
        out_specs=pl.BlockSpec((TO, D_MODEL), row),
        out_shape=jax.ShapeDtypeStruct((T, D_MODEL), F32),
        scratch_shapes=[pltpu.VMEM((D_MIX, D_MODEL), BF16)],
        compiler_params=pltpu.CompilerParams(
            dimension_semantics=("arbitrary",), vmem_limit_bytes=VMEM_LIMIT),
        name="outproj",
    )(x2, rnn_out, attn_out, wo, nfw)


def _block_diag_halves(wa, wx):
    eye = jnp.eye(RNN_HEADS, dtype=wa.dtype)
    full = lambda w: jnp.einsum('hij,hk->hikj', w, eye).reshape(D_RNN, D_RNN)
    fa, fx = full(wa), full(wx)
    half = D_RNN // 2
    return jnp.stack([
        jnp.concatenate([fa[s:s + half, s:s + half], fx[s:s + half, s:s + half]], axis=1)
        for s in (0, half)]).astype(BF16)


def _overlap_t(ncmp_pad, nblk):
    cs = np.arange(ncmp_pad)[None, :] * CMP_STRIDE
    ss = np.arange(nblk)[:, None] * SEL_BLOCK
    ov = np.clip(np.minimum(cs + CMP_BLOCK, ss + SEL_BLOCK) - np.maximum(cs, ss), 0, None)
    return jnp.asarray(ov.astype(np.float32) / CMP_BLOCK)


def kernel(x, norm1_w, w_in, conv_w, conv_b, rg_wa, rg_ba, rg_wx, rg_bx, rg_lambda,
           cmp_k_pe, cmp_k_w1, cmp_k_w2, cmp_v_pe, cmp_v_w1, cmp_v_w2, w_out, normf_w):
    B, S, D = x.shape
    assert D == D_MODEL and w_in.shape[-1] == _D_IN
    assert S % TM == 0 and TS == TM and B % CB == 0
    assert TQ % TW == 0 and TW % VC == 0 and WINDOW % TW == 0 and KS % TQ == 0 and TM % TQ == 0
    assert S % KS == 0 and S >= WINDOW + TQ and (B * S) % TO == 0
    depth = w_in.shape[0]
    T = B * S
    ovt = _overlap_t(S // CMP_STRIDE, S // SEL_BLOCK)
    x2 = x.reshape(T, D)
    for l in range(depth):
        wn = w_in[l].astype(BF16)
        wb = jnp.pad(w_in[l][:, _O_BR:], ((0, 0), (0, _O_BR + LANE - _D_IN))).astype(BF16)
        (rnn_out, kvc, ksel, kwin, ag, qt, vselt, vwint, brgt) = _inproj(
            x2, norm1_w[l].reshape(1, D), wn, wb,
            conv_w[l], conv_b[l].reshape(1, D_RNN), _block_diag_halves(rg_wa[l], rg_wx[l]),
            rg_ba[l].reshape(1, D_RNN), rg_bx[l].reshape(1, D_RNN),
            rg_lambda[l].reshape(1, D_RNN), B, S)

        pe2 = jnp.stack([cmp_k_pe[l], cmp_v_pe[l]]).reshape(2, 2, CMP_STRIDE * HEAD_DIM)
        w2 = jnp.stack([cmp_k_w2[l], cmp_v_w2[l]]).astype(BF16)
        cmp_kv = _compress(kvc, pe2, cmp_k_w1[l], cmp_v_w1[l], w2, B, S)

        attn_out = _nsa(qt, cmp_kv, ksel, vselt, kwin, vwint, brgt, ag, ovt, B, S)

        x2 = _outproj(x2, rnn_out, attn_out, w_out[l],
                      normf_w.reshape(1, D), final_norm=(l == depth - 1))
    return x2.reshape(B, S, D)
```

```python
import functools

import numpy as np
import jax
import jax.numpy as jnp
from jax import lax
from jax.experimental import pallas as pl
from jax.experimental.pallas import tpu as pltpu

F32 = jnp.float32
BF16 = jnp.bfloat16

D_MODEL = 1024
EPS = 1e-6
D_RNN = 512
RNN_HEADS = 8
RNN_HEAD_DIM = D_RNN // RNN_HEADS
CONV_WIDTH = 4
LRU_C = 8.0
N_Q_HEADS = 8
N_KV_HEADS = 2
HEAD_DIM = 64
Q_PER_KV = N_Q_HEADS // N_KV_HEADS
D_ATTN = N_Q_HEADS * HEAD_DIM
D_KV = N_KV_HEADS * HEAD_DIM
CMP_BLOCK = 32
CMP_STRIDE = 16
CMP_HIDDEN = 256
SEL_BLOCK = 64
SEL_TOPK = 8
SEL_FORCE = 1e9
WINDOW = 512
N_BRANCH = 3
D_MIX = D_RNN + D_ATTN
N_GATE_ROWS = 32

LANE = 128
SCAN_CHUNKS = 8

TM = 1024
TS = 1024
TO = 1024
OUT_ROW_CHUNKS = 4
OUT_RING = 3
TQ = 256
KS = 512
VC = 128
TW = 128
CMP_ALIGN = 32
CB = 4
NEG = -1e30
LOG2E = 1.4426950408889634
ONES_ROWS = 16
VMEM_V7X = 64 * 1024 * 1024
VMEM_LIMIT = VMEM_V7X - 8 * 1024 * 1024

_O_RX, _O_RG, _O_Q = 0, D_RNN, 2 * D_RNN
_O_KC = _O_Q + D_ATTN
_O_VC = _O_KC + D_KV
_O_KS = _O_VC + D_KV
_O_VS = _O_KS + D_KV
_O_KW = _O_VS + D_KV
_O_VW = _O_KW + D_KV
_O_AG = _O_VW + D_KV
_O_BR = _O_AG + D_ATTN
_D_IN = _O_BR + N_BRANCH * N_Q_HEADS


def _silu(x):
    return x * jax.nn.sigmoid(x)


def _inproj_kernel(x_ref, nw_ref, wn_ref, wb_ref, cw_ref, cb_ref, wg_ref, ba_ref, bx_ref, lam_ref,
                   rnn_ref, kvc_ref, ksel_ref, kwin_ref, ag_ref,
                   qt_ref, vselt_ref, vwint_ref, brgt_ref,
                   kv_scr, rx_scr, rg_scr, tail_ref, a_s, u_s, h_s, p_s, hlast, o_scr):
    @pl.when(pl.program_id(1) == 0)
    def _():
        tail_ref[...] = jnp.zeros(tail_ref.shape, F32)
        hlast[...] = jnp.zeros(hlast.shape, F32)

    x = x_ref[...]
    ms = jnp.mean(x * x, axis=-1, keepdims=True)
    h = ((x * lax.rsqrt(ms + EPS)) * nw_ref[...]).astype(BF16)

    def nat(a, b):
        return jnp.dot(h, wn_ref[:, a:b], preferred_element_type=F32)

    def store_step_major(dst_ref, val):
        steps = TM // SCAN_CHUNKS
        for l in range(D_RNN // LANE):
            for c in range(SCAN_CHUNKS):
                dst_ref[l, pl.ds(c, steps, stride=SCAN_CHUNKS), :] = (
                    val[c * steps:(c + 1) * steps, l * LANE:(l + 1) * LANE])

    store_step_major(rx_scr, nat(_O_RX, _O_RG))
    store_step_major(rg_scr, nat(_O_RG, _O_Q))

    def proj_cmp():
        kv = nat(_O_KC, _O_KS)
        for a in range(2):
            kv_scr[a] = kv[:, a * D_KV:(a + 1) * D_KV]
            for l in range(CMP_STRIDE):
                kvc_ref[a, :, l * D_KV:(l + 1) * D_KV] = (
                    kv_scr[a, pl.ds(l, TM // CMP_STRIDE, stride=CMP_STRIDE), :])

    def proj_kv(lo, k_ref, vt_ref):
        kv = nat(lo, lo + 2 * D_KV)
        k_ref[...] = kv[:, :D_KV].astype(BF16)
        vt = kv[:, D_KV:].T.astype(BF16)
        for c in range(TM // VC):
            vt_ref[c] = vt[:, c * VC:(c + 1) * VC]

    def proj_gates():
        ag_ref[...] = nat(_O_AG, _O_BR)
        brg = jnp.dot(h, wb_ref[...], preferred_element_type=F32)
        brgt_ref[...] = brg.T[0:N_GATE_ROWS, :]

    def proj_q():
        qt_ref[...] = (nat(_O_Q, _O_KC) * (HEAD_DIM ** -0.5 * LOG2E)).T.astype(BF16)

    projections = ((proj_gates,), (proj_q,),
                   (functools.partial(proj_kv, _O_KS, ksel_ref, vselt_ref),
                    functools.partial(proj_kv, _O_KW, kwin_ref, vwint_ref)),
                   (proj_cmp,))
    wraps = _rglru_wraps(rx_scr, tail_ref)
    rows = TM // len(projections)
    for ci, projs in enumerate(projections):
        _rglru_gates(rx_scr, wraps, ci * rows, (ci + 1) * rows,
                     cw_ref, cb_ref, wg_ref, ba_ref, bx_ref, lam_ref, a_s, u_s)
        for proj in projs:
            proj()

    _rglru_scan(rg_scr, rnn_ref, a_s, u_s, h_s, p_s, hlast, o_scr)


def _inproj(x2, nw, wn, wb, cw, cb, wg, ba, bx, lam, B, S):
    T = B * S
    ns = S // TM
    grid = (B, ns)
    row = lambda b, s: (b * ns + s, 0)
    const2 = lambda b, s: (0, 0)
    out_shape = (
        jax.ShapeDtypeStruct((T, D_RNN), BF16),
        jax.ShapeDtypeStruct((2, T // CMP_STRIDE, CMP_STRIDE * D_KV), F32),
        jax.ShapeDtypeStruct((T, D_KV), BF16),
        jax.ShapeDtypeStruct((T, D_KV), BF16),
        jax.ShapeDtypeStruct((T, D_ATTN), F32),
        jax.ShapeDtypeStruct((B, ns, D_ATTN, TM), BF16),
        jax.ShapeDtypeStruct((B, S // VC, D_KV, VC), BF16),
        jax.ShapeDtypeStruct((B, S // VC, D_KV, VC), BF16),
        jax.ShapeDtypeStruct((B, ns, N_GATE_ROWS, TM), F32),
    )
    out_specs = (
        pl.BlockSpec((TM, D_RNN), row),
        pl.BlockSpec((2, TM // CMP_STRIDE, CMP_STRIDE * D_KV), lambda b, s: (0, b * ns + s, 0)),
        pl.BlockSpec((TM, D_KV), row),
        pl.BlockSpec((TM, D_KV), row),
        pl.BlockSpec((TM, D_ATTN), row),
        pl.BlockSpec((None, None, D_ATTN, TM), lambda b, s: (b, s, 0, 0)),
        pl.BlockSpec((None, TM // VC, D_KV, VC), lambda b, s: (b, s, 0, 0)),
        pl.BlockSpec((None, TM // VC, D_KV, VC), lambda b, s: (b, s, 0, 0)),
        pl.BlockSpec((None, None, N_GATE_ROWS, TM), lambda b, s: (b, s, 0, 0)),
    )
    return pl.pallas_call(
        _inproj_kernel,
        grid=grid,
        in_specs=[
            pl.BlockSpec((TM, D_MODEL), row),
            pl.BlockSpec((1, D_MODEL), const2),
            pl.BlockSpec(wn.shape, const2),
            pl.BlockSpec(wb.shape, const2),
            pl.BlockSpec((CONV_WIDTH, D_RNN), const2),
            pl.BlockSpec((1, D_RNN), const2),
            pl.BlockSpec(wg.shape, lambda b, s: (0, 0, 0)),
            pl.BlockSpec((1, D_RNN), const2),
            pl.BlockSpec((1, D_RNN), const2),
            pl.BlockSpec((1, D_RNN), const2),
        ],
        out_specs=out_specs,
        out_shape=out_shape,
        scratch_shapes=[
            pltpu.VMEM((2, TM, D_KV), F32),
            pltpu.VMEM((D_RNN // LANE, TM, LANE), F32),
            pltpu.VMEM((D_RNN // LANE, TM, LANE), F32),
            pltpu.VMEM(((CONV_WIDTH - 1) * SCAN_CHUNKS, D_RNN), F32),
            pltpu.VMEM((TM, D_RNN), F32),
            pltpu.VMEM((TM, D_RNN), F32),
            pltpu.VMEM((TM, D_RNN), F32),
            pltpu.VMEM((TM, D_RNN), F32),
            pltpu.VMEM((1, D_RNN), F32),
            pltpu.VMEM((D_RNN // LANE, TM, LANE), F32),
        ],
        compiler_params=pltpu.CompilerParams(
            dimension_semantics=("arbitrary", "arbitrary"), vmem_limit_bytes=VMEM_LIMIT),
        name="inproj",
    )(x2, nw, wn, wb, cw, cb, wg, ba, bx, lam)


def _slab_rows(x_ref, lo, hi):
    return jnp.concatenate([x_ref[l, lo:hi, :] for l in range(D_RNN // LANE)], axis=1)


def _rglru_wraps(x_ref, tail_ref):
    ntail = (CONV_WIDTH - 1) * SCAN_CHUNKS
    last = _slab_rows(x_ref, TS - ntail, TS)
    sub = lax.broadcasted_iota(jnp.int32, (SCAN_CHUNKS, D_RNN), 0)
    wraps = []
    for j in range(CONV_WIDTH - 1):
        rows = slice(j * SCAN_CHUNKS, (j + 1) * SCAN_CHUNKS)
        cur = pltpu.roll(last[rows, :], 1, 0)
        prev = pltpu.roll(tail_ref[rows, :], 1, 0)
        wraps.append(jnp.where(sub == 0, prev, cur))
    tail_ref[...] = last
    return wraps


def _rglru_gates(x_ref, wraps, lo, hi, cw_ref, cb_ref, wg_ref, ba_ref, bx_ref, lam_ref, a_s, u_s):
    y = cb_ref[...]
    for k in range(CONV_WIDTH):
        off = (CONV_WIDTH - 1 - k) * SCAN_CHUNKS
        if lo >= off:
            xs = _slab_rows(x_ref, lo - off, hi - off)
        else:
            assert lo == 0
            xs = jnp.concatenate(wraps[len(wraps) - off // SCAN_CHUNKS:]
                                 + [_slab_rows(x_ref, 0, hi - off)], axis=0)
        y = y + xs * cw_ref[k:k + 1, :]

    yb = y.astype(BF16)
    half = D_RNN // 2
    pre = [jnp.dot(yb[:, hh * half:(hh + 1) * half], wg_ref[hh], preferred_element_type=F32)
           for hh in range(2)]
    pre_a = jnp.concatenate([pre[0][:, :half], pre[1][:, :half]], axis=1)
    pre_x = jnp.concatenate([pre[0][:, half:], pre[1][:, half:]], axis=1)
    r = jax.nn.sigmoid(pre_a + ba_ref[...])
    i = jax.nn.sigmoid(pre_x + bx_ref[...])
    lam = lam_ref[...]
    lsig = jnp.minimum(lam, 0.0) - jnp.log1p(jnp.exp(-jnp.abs(lam)))
    a = jnp.exp2(r * ((LRU_C * LOG2E) * lsig))
    u = jnp.sqrt(1.0 - a * a) * (i * y)
    a_s[lo:hi, :] = a
    u_s[lo:hi, :] = u


def _rglru_scan(gate_ref, out_ref, a_s, u_s, h_s, p_s, hlast, o_scr):
    nstep = TS // SCAN_CHUNKS
    nslab = D_RNN // LANE
    gate = jnp.concatenate([gate_ref[l] for l in range(nslab)], axis=1)

    def body(j, carry):
        h, p = carry
        blk = pl.multiple_of(j * SCAN_CHUNKS, SCAN_CHUNKS)
        a_j = a_s[pl.ds(blk, SCAN_CHUNKS), :]
        h = a_j * h + u_s[pl.ds(blk, SCAN_CHUNKS), :]
        p = a_j * p
        h_s[pl.ds(blk, SCAN_CHUNKS), :] = h
        p_s[pl.ds(blk, SCAN_CHUNKS), :] = p
        return h, p

    h_end, p_end = lax.fori_loop(
        0, nstep, body,
        (jnp.zeros((SCAN_CHUNKS, D_RNN), F32), jnp.ones((SCAN_CHUNKS, D_RNN), F32)), unroll=8)
    carry = hlast[...]
    h_in = []
    for c in range(SCAN_CHUNKS):
        h_in.append(carry)
        carry = h_end[c:c + 1, :] + p_end[c:c + 1, :] * carry
    hlast[...] = carry
    h_in = jnp.concatenate([jnp.concatenate(h_in, axis=0)] * nstep, axis=0)
    o = (h_s[...] + p_s[...] * h_in) * _silu(gate)

    for l in range(nslab):
        o_scr[l] = o[:, l * LANE:(l + 1) * LANE]
        for c in range(SCAN_CHUNKS):
            out_ref[c * nstep:(c + 1) * nstep, l * LANE:(l + 1) * LANE] = (
                o_scr[l, pl.ds(c, nstep, stride=SCAN_CHUNKS), :].astype(out_ref.dtype))


def _compress_kernel(x_ref, pe_ref, w1k_ref, w1v_ref, w2_ref, out_ref, w1_bf):
    nrow = x_ref.shape[1]
    nchunk = nrow // CB
    half = CMP_STRIDE * HEAD_DIM

    for which, w1_ref in enumerate((w1k_ref, w1v_ref)):
        @pl.when((pl.program_id(0) == which) & (pl.program_id(1) == 0))
        def _(w1_ref=w1_ref):
            w1_bf[...] = w1_ref[...].astype(BF16)

    x = x_ref[0]
    x2 = jnp.concatenate(
        [jnp.concatenate([x[:, l * D_KV + g * HEAD_DIM:l * D_KV + (g + 1) * HEAD_DIM]
                          for l in range(CMP_STRIDE)], axis=1)
         for g in range(N_KV_HEADS)], axis=0)
    xa = (x2 + pe_ref[0, 0:1, :]).astype(BF16)
    xb = (x2 + pe_ref[0, 1:2, :]).astype(BF16)
    first = jnp.dot(xa, w1_bf[0:half, :], preferred_element_type=F32)
    second = jnp.dot(xb, w1_bf[half:2 * half, :], preferred_element_type=F32)
    hid = _silu(first + pltpu.roll(second, N_KV_HEADS * nrow - 1, 0))
    out2 = jnp.dot(hid.astype(BF16), w2_ref[0], preferred_element_type=F32)
    out = jnp.concatenate([out2[g * nrow:(g + 1) * nrow, :] for g in range(N_KV_HEADS)], axis=1)
    c_idx = lax.broadcasted_iota(jnp.int32, out.shape, 0) & (nchunk - 1)
    out_ref[0] = jnp.where(c_idx < nchunk - 1, out, 0.0)


def _compress(xc, pe2, w1k, w1v, w2, B, S):
    nchunk = S // CMP_STRIDE
    nrow = CB * nchunk
    width = CMP_STRIDE * D_KV
    kv = lambda i, j: (i, 0, 0)
    return pl.pallas_call(
        _compress_kernel,
        grid=(2, B // CB),
        in_specs=[
            pl.BlockSpec((1, nrow, width), lambda i, j: (i, j, 0)),
            pl.BlockSpec((1,) + pe2.shape[1:], kv),
            pl.BlockSpec(w1k.shape, lambda i, j: (0, 0)),
            pl.BlockSpec(w1v.shape, lambda i, j: (0, 0)),
            pl.BlockSpec((1,) + w2.shape[1:], kv),
        ],
        out_specs=pl.BlockSpec((1, nrow, D_KV), lambda i, j: (i, j, 0)),
        out_shape=jax.ShapeDtypeStruct((2, B * nchunk, D_KV), F32),
        scratch_shapes=[pltpu.VMEM(w1k.shape, BF16)],
        compiler_params=pltpu.CompilerParams(
            dimension_semantics=("arbitrary", "arbitrary"), vmem_limit_bytes=VMEM_LIMIT),
        name="compress",
    )(xc, pe2, w1k, w1v, w2)


def _nsa_kernel(qt_ref, cmp_ref, ksel_ref, vselt_ref, kwin_ref, vwint_ref, brgt_ref, ag_ref,
                ovt_ref, out_ref, qpad_ref, pen_ref, o_ref, m_ref, acc_ref, s_ref, *, nq):
    qi = pl.program_id(1)
    q0 = qi * TQ
    nl = Q_PER_KV * TQ
    nblk = pen_ref.shape[1]
    ncmp = cmp_ref.shape[1]
    groups = range(N_KV_HEADS)
    gsl = [slice(g * HEAD_DIM, (g + 1) * HEAD_DIM) for g in groups]

    def tile4(a):
        return jnp.concatenate([a] * Q_PER_KV, axis=1)

    def sel_scores(kt, slot, nkeys=KS):
        k_tile = ksel_ref[pl.ds(pl.multiple_of(kt * KS, KS), nkeys), :]
        for g in groups:
            s_ref[slot, g, 0:nkeys, :] = jnp.dot(k_tile, qpad_ref[g], preferred_element_type=F32)

    n_full = q0 // KS

    ones = jnp.ones((ONES_ROWS, VC), BF16)

    def vt_aug(v_ref, chunks, g):
        vt = jnp.concatenate([v_ref[c, gsl[g], :] for c in chunks], axis=1)
        return jnp.concatenate([vt, jnp.concatenate([ones] * len(chunks), axis=1)], axis=0)

    def normalise(res):
        return res[0:HEAD_DIM, :] * (1.0 / res[HEAD_DIM:HEAD_DIM + 1, :])

    def first_block(ncmp_use, qi_static):
        kc = cmp_ref[0, 0:ncmp_use, :].astype(BF16)
        s_cmp = []
        for g in groups:
            zpad = jnp.zeros((HEAD_DIM, TQ), BF16)
            cols = []
            for r in range(Q_PER_KV):
                hq = g * Q_PER_KV + r
                qh = qt_ref[hq * HEAD_DIM:(hq + 1) * HEAD_DIM, :]
                cols.append(jnp.concatenate([qh, zpad] if g == 0 else [zpad, qh], axis=0))
            qpad = jnp.concatenate(cols, axis=1)
            qpad_ref[g] = qpad
            s_cmp.append(jnp.dot(kc, qpad, preferred_element_type=F32))

        c_row = lax.broadcasted_iota(jnp.int32, (ncmp_use, TQ), 0)
        t_lane = q0 + lax.broadcasted_iota(jnp.int32, (ncmp_use, TQ), 1)
        cpen = tile4(jnp.where(c_row * CMP_STRIDE + (CMP_BLOCK - 1) <= t_lane, 0.0, NEG))
        vc_t = cmp_ref[1, 0:ncmp_use, :].T
        for g in groups:
            vct = vc_t[gsl[g], :].astype(BF16)
            s = s_cmp[g] + cpen
            m = jnp.max(s, axis=0, keepdims=True)
            m = jnp.where(m > 0.5 * NEG, m, 0.0)
            e = jnp.exp2(s - m)
            den = jnp.sum(e, axis=0, keepdims=True)
            p = e * (1.0 / jnp.where(den > 0.0, den, 1.0))
            o_ref[0, g] = jnp.dot(vct, p.astype(BF16), preferred_element_type=F32)

            psum = p[:, 0:TQ]
            for r in range(1, Q_PER_KV):
                psum = psum + p[:, r * TQ:(r + 1) * TQ]
            if ncmp_use < ncmp:
                psum = jnp.concatenate([psum, jnp.zeros((ncmp - ncmp_use, TQ), F32)], axis=0)
            imp = jnp.dot(ovt_ref[...], psum, preferred_element_type=F32,
                          precision=lax.Precision.HIGHEST)
            j_row = lax.broadcasted_iota(jnp.int32, (nblk, TQ), 0)
            t_blk = q0 + lax.broadcasted_iota(jnp.int32, (nblk, TQ), 1)
            cur = jnp.right_shift(t_blk, SEL_BLOCK.bit_length() - 1)
            forced = (j_row == 0) | (j_row == cur) | (j_row == cur - 1)
            val = jnp.where(forced, SEL_FORCE, imp)
            val = jnp.where(j_row * SEL_BLOCK <= t_blk, val, -SEL_FORCE)
            rank = jnp.zeros((nblk, TQ), F32)
            for i in range(nblk):
                vi = jnp.broadcast_to(val[i:i + 1, :], (nblk, TQ))
                rank = rank + jnp.where(j_row > i, jnp.where(vi >= val, 1.0, 0.0),
                                        jnp.where(vi > val, 1.0, 0.0))
            chosen = (rank < float(min(SEL_TOPK, nblk))) & (val > -0.5 * SEL_FORCE)
            selpen = jnp.where(chosen, 0.0, NEG)
            for j in range(nblk):
                pen_ref[g, j] = jnp.broadcast_to(selpen[j:j + 1, :], (8, TQ))

        nchunk = WINDOW // TW + 1
        row_w = lax.broadcasted_iota(jnp.int32, (TW, TW), 0)
        lane_w = lax.broadcasted_iota(jnp.int32, (TW, TW), 1)
        edge_pen = {0: tile4(jnp.where(row_w > lane_w, 0.0, NEG)),
                    nchunk - 1: tile4(jnp.where(row_w <= lane_w, 0.0, NEG))}
        for sub in range(TQ // TW):
            qs = q0 + sub * TW
            first = 0 if qi_static is None else max(0, nchunk - 1 - (qi_static * TQ + sub * TW) // TW)
            chunks = range(first, nchunk)
            rows0 = [pl.multiple_of(qs - WINDOW + c * TW, TW) for c in chunks]
            k_win = jnp.concatenate([kwin_ref[pl.ds(r0, TW), :] for r0 in rows0], axis=0)
            cols = [slice(r * TQ + sub * TW, r * TQ + (sub + 1) * TW) for r in range(Q_PER_KV)]
            p_win = []
            for g in groups:
                q_sub = jnp.concatenate([qpad_ref[g, :, c] for c in cols], axis=1)
                s = jnp.dot(k_win, q_sub, preferred_element_type=F32)
                parts = []
                for n, c in enumerate(chunks):
                    s_c = s[n * TW:(n + 1) * TW, :]
                    parts.append(s_c + edge_pen[c] if c in edge_pen else s_c)
                m = jnp.max(parts[-1], axis=0, keepdims=True)
                for part in parts[:-1]:
                    m = jnp.maximum(m, jnp.max(part, axis=0, keepdims=True))
                p_win.append(jnp.concatenate([jnp.exp2(part - m) for part in parts],
                                             axis=0).astype(BF16))
            if sub == 0:
                first_keys = KS if qi_static is None else min(KS, (qi_static + 1) * TQ)
                sel_scores(0, n_full & 1, first_keys)
            for g in groups:
                o_win = normalise(jnp.dot(vt_aug(vwint_ref, [r0 // VC for r0 in rows0], g),
                                          p_win[g], preferred_element_type=F32))
                for r, c in enumerate(cols):
                    o_ref[2, g, :, c] = o_win[:, r * TW:(r + 1) * TW]

    def cmp_rows(tile):
        return min(ncmp, -(-((tile + 1) * TQ // CMP_STRIDE) // CMP_ALIGN) * CMP_ALIGN)

    n_partial = min(nq, WINDOW // TQ)
    for tile in range(n_partial):
        pl.when(qi == tile)(functools.partial(first_block, cmp_rows(tile), tile))
    lo = n_partial
    while lo < nq:
        hi = lo
        while hi + 1 < nq and cmp_rows(hi + 1) == cmp_rows(lo):
            hi += 1
        pl.when((qi >= lo) & (qi <= hi))(functools.partial(first_block, cmp_rows(lo), None))
        lo = hi + 1

    m_ref[...] = jnp.full(m_ref.shape, NEG, F32)
    acc_ref[...] = jnp.zeros(acc_ref.shape, F32)
    bpt = KS // SEL_BLOCK
    cpt = KS // VC

    def sel_update(kt, slot, diag_keys=0):
        nkeys = diag_keys or KS
        probs, alphas = [], []
        for g in groups:
            m_prev = m_ref[g]
            if diag_keys:
                rows = [jnp.concatenate([pen_ref[g, kt * bpt + jj]] * (SEL_BLOCK // 8), axis=0)
                        for jj in range(nkeys // SEL_BLOCK)]
                pen = jnp.concatenate(rows, axis=0)
                key = kt * KS + lax.broadcasted_iota(jnp.int32, (nkeys, TQ), 0)
                t_q = q0 + lax.broadcasted_iota(jnp.int32, (nkeys, TQ), 1)
                s = s_ref[slot, g, 0:nkeys, :] + tile4(pen + jnp.where(key <= t_q, 0.0, NEG))
                m_new = jnp.maximum(m_prev, jnp.max(s, axis=0, keepdims=True))
                p = jnp.exp2(s - m_new)
            else:
                sb = [s_ref[slot, g, jj * SEL_BLOCK:(jj + 1) * SEL_BLOCK, :] for jj in range(bpt)]
                bias = [tile4(pen_ref[g, kt * bpt + jj][0:1, :]) for jj in range(bpt)]
                m_new = m_prev
                for jj in range(bpt):
                    m_new = jnp.maximum(m_new, jnp.max(sb[jj], axis=0, keepdims=True) + bias[jj])
                p = jnp.concatenate([jnp.exp2(sb[jj] + (bias[jj] - m_new)) for jj in range(bpt)],
                                    axis=0)
            probs.append(p.astype(BF16))
            alphas.append(jnp.exp2(m_prev - m_new))
            m_ref[g] = m_new
        for g in groups:
            acc_ref[g] = alphas[g] * acc_ref[g] + jnp.dot(
                vt_aug(vselt_ref, [kt * cpt + c for c in range(nkeys // VC)], g), probs[g],
                preferred_element_type=F32)

    last_keys = q0 + TQ - n_full * KS

    def sel_body(kt, carry):
        next_keys = jnp.where(kt + 1 == n_full, last_keys, KS)
        for slot in (0, 1):
            for nkeys in range(TQ, KS + 1, TQ):
                @pl.when((((n_full - kt) & 1) == slot) & (next_keys == nkeys))
                def _(slot=slot, nkeys=nkeys):
                    sel_scores(kt + 1, 1 - slot, nkeys)
                    sel_update(kt, slot)
        return carry

    lax.fori_loop(0, n_full, sel_body, 0)
    for nkeys in range(TQ, KS + 1, TQ):
        @pl.when(last_keys == nkeys)
        def _(nkeys=nkeys):
            sel_update(n_full, 0, diag_keys=nkeys)
    for g in groups:
        o_ref[1, g] = normalise(acc_ref[g])

    gates = jax.nn.sigmoid(brgt_ref[...])
    group_out = []
    for g in groups:
        heads = []
        for r in range(Q_PER_KV):
            hq = g * Q_PER_KV + r
            ls = slice(r * TQ, (r + 1) * TQ)
            o = gates[hq:hq + 1, :] * o_ref[0, g, :, ls]
            for n in range(1, N_BRANCH):
                o = o + gates[n * N_Q_HEADS + hq:n * N_Q_HEADS + hq + 1, :] * o_ref[n, g, :, ls]
            heads.append(o)
        group_out.append(jnp.concatenate(heads, axis=0).T)

    attn = jnp.concatenate(group_out, axis=1)
    out_ref[...] = (attn * _silu(ag_ref[...])).astype(out_ref.dtype)


def _nsa(qt, cmp_kv, ksel, vselt, kwin, vwint, brgt, ag, ovt, B, S):
    T = B * S
    nq = S // TQ
    qpm = TM // TQ
    nblk = S // SEL_BLOCK
    ncmp = S // CMP_STRIDE
    nl = Q_PER_KV * TQ
    return pl.pallas_call(
        functools.partial(_nsa_kernel, nq=nq),
        grid=(B, nq),
        in_specs=[
            pl.BlockSpec((None, None, D_ATTN, TQ), lambda b, q: (b, q // qpm, 0, q % qpm)),
            pl.BlockSpec((2, ncmp, D_KV), lambda b, q: (0, b, 0)),
            pl.BlockSpec((S, D_KV), lambda b, q: (b, 0)),
            pl.BlockSpec((None, S // VC, D_KV, VC), lambda b, q: (b, 0, 0, 0)),
            pl.BlockSpec((S, D_KV), lambda b, q: (b, 0)),
            pl.BlockSpec((None, S // VC, D_KV, VC), lambda b, q: (b, 0, 0, 0)),
            pl.BlockSpec((None, None, N_GATE_ROWS, TQ), lambda b, q: (b, q // qpm, 0, q % qpm)),
            pl.BlockSpec((TQ, D_ATTN), lambda b, q: (b * nq + q, 0)),
            pl.BlockSpec(ovt.shape, lambda b, q: (0, 0)),
        ],
        out_specs=pl.BlockSpec((TQ, D_ATTN), lambda b, q: (b * nq + q, 0)),
        out_shape=jax.ShapeDtypeStruct((T, D_ATTN), BF16),
        scratch_shapes=[
            pltpu.VMEM((N_KV_HEADS, D_KV, nl), BF16),
            pltpu.VMEM((N_KV_HEADS, nblk, 8, TQ), F32),
            pltpu.VMEM((N_BRANCH, N_KV_HEADS, HEAD_DIM, nl), F32),
            pltpu.VMEM((N_KV_HEADS, 1, nl), F32),
            pltpu.VMEM((N_KV_HEADS, HEAD_DIM + ONES_ROWS, nl), F32),
            pltpu.VMEM((2, N_KV_HEADS, KS, nl), F32),
        ],
        compiler_params=pltpu.CompilerParams(
            dimension_semantics=("arbitrary", "arbitrary"), vmem_limit_bytes=VMEM_LIMIT),
        name="nsa",
    )(qt, cmp_kv, ksel, vselt, kwin, vwint, brgt, ag, ovt)


def _outproj_kernel(x_hbm, rnn_hbm, attn_hbm, wo_ref, nfw_ref, out_ref,
                    wo_bf, x_ring, rnn_ring, attn_ring, sems, *, final_norm, nsteps):
    i = pl.program_id(0)
    streams = ((x_hbm, x_ring), (rnn_hbm, rnn_ring), (attn_hbm, attn_ring))

    def tile_copies(step, slot):
        rows = pl.ds(pl.multiple_of(step * TO, TO), TO)
        return [pltpu.make_async_copy(src.at[rows, :], ring.at[slot], sems.at[k, slot])
                for k, (src, ring) in enumerate(streams)]

    @pl.when(i == 0)
    def _():
        for step in range(OUT_RING - 1):
            for cp in tile_copies(step, step):
                cp.start()
        wo_bf[...] = wo_ref[...].astype(BF16)

    def tile(slot):
        ahead = OUT_RING - 1

        @pl.when(i + ahead < nsteps)
        def _():
            for cp in tile_copies(i + ahead, (slot + ahead) % OUT_RING):
                cp.start()

        for cp in tile_copies(i, slot):
            cp.wait()
        rows = TO // OUT_ROW_CHUNKS
        for c in range(OUT_ROW_CHUNKS):
            rs = slice(c * rows, (c + 1) * rows)
            mix = jnp.concatenate([rnn_ring[slot, rs, :], attn_ring[slot, rs, :]], axis=1)
            y = x_ring[slot, rs, :] + jnp.dot(mix, wo_bf[...], preferred_element_type=F32)
            if final_norm:
                ms = jnp.mean(y * y, axis=-1, keepdims=True)
                y = (y * lax.rsqrt(ms + EPS)) * nfw_ref[...]
            out_ref[rs, :] = y

    for slot in range(OUT_RING):
        pl.when(i % OUT_RING == slot)(functools.partial(tile, slot))


def _outproj(x2, rnn_out, attn_out, wo, nfw, final_norm):
    T = x2.shape[0]
    nsteps = T // TO
    assert nsteps >= OUT_RING - 1
    row = lambda i: (i, 0)
    hbm = pl.BlockSpec(memory_space=pl.ANY)
    return pl.pallas_call(
        functools.partial(_outproj_kernel, final_norm=final_norm, nsteps=nsteps),
        grid=(nsteps,),
        in_specs=[
            hbm, hbm, hbm,
            pl.BlockSpec((D_MIX, D_MODEL), lambda i: (0, 0)),
            pl.BlockSpec((1, D_MODEL), lambda i: (0, 0)),
        ],
        out_specs=pl.BlockSpec((TO, D_MODEL), row),
        out_shape=jax.ShapeDtypeStruct((T, D_MODEL), F32),
        scratch_shapes=[
            pltpu.VMEM((D_MIX, D_MODEL), BF16),
            pltpu.VMEM((OUT_RING, TO, D_MODEL), x2.dtype),
            pltpu.VMEM((OUT_RING, TO, D_RNN), rnn_out.dtype),
            pltpu.VMEM((OUT_RING, TO, D_ATTN), attn_out.dtype),
            pltpu.SemaphoreType.DMA((3, OUT_RING)),
        ],
        compiler_params=pltpu.CompilerParams(
            dimension_semantics=("arbitrary",), vmem_limit_bytes=VMEM_LIMIT),
        name="outproj",
    )(x2, rnn_out, attn_out, wo, nfw)


def _block_diag_halves(wa, wx):
    eye = jnp.eye(RNN_HEADS, dtype=wa.dtype)
    full = lambda w: jnp.einsum('hij,hk->hikj', w, eye).reshape(D_RNN, D_RNN)
    fa, fx = full(wa), full(wx)
    half = D_RNN // 2
    return jnp.stack([
        jnp.concatenate([fa[s:s + half, s:s + half], fx[s:s + half, s:s + half]], axis=1)
        for s in (0, half)]).astype(BF16)


def _overlap_t(ncmp_pad, nblk):
    cs = np.arange(ncmp_pad)[None, :] * CMP_STRIDE
    ss = np.arange(nblk)[:, None] * SEL_BLOCK
    ov = np.clip(np.minimum(cs + CMP_BLOCK, ss + SEL_BLOCK) - np.maximum(cs, ss), 0, None)
    return jnp.asarray(ov.astype(np.float32) / CMP_BLOCK)


def kernel(x, norm1_w, w_in, conv_w, conv_b, rg_wa, rg_ba, rg_wx, rg_bx, rg_lambda,
           cmp_k_pe, cmp_k_w1, cmp_k_w2, cmp_v_pe, cmp_v_w1, cmp_v_w2, w_out, normf_w):
    B, S, D = x.shape
    assert D == D_MODEL and w_in.shape[-1] == _D_IN
    assert S % TM == 0 and TS == TM and B % CB == 0
    assert TQ % TW == 0 and TW % VC == 0 and WINDOW % TW == 0 and KS % TQ == 0 and TM % TQ == 0
    assert S % KS == 0 and S >= WINDOW + TQ and (B * S) % TO == 0
    depth = w_in.shape[0]
    T = B * S
    ovt = _overlap_t(S // CMP_STRIDE, S // SEL_BLOCK)
    x2 = x.reshape(T, D)
    for l in range(depth):
        wn = w_in[l].astype(BF16)
        wb = jnp.pad(w_in[l][:, _O_BR:], ((0, 0), (0, _O_BR + LANE - _D_IN))).astype(BF16)
        (rnn_out, kvc, ksel, kwin, ag, qt, vselt, vwint, brgt) = _inproj(
            x2, norm1_w[l].reshape(1, D), wn, wb,
            conv_w[l], conv_b[l].reshape(1, D_RNN), _block_diag_halves(rg_wa[l], rg_wx[l]),
            rg_ba[l].reshape(1, D_RNN), rg_bx[l].reshape(1, D_RNN),
            rg_lambda[l].reshape(1, D_RNN), B, S)

        pe2 = jnp.stack([cmp_k_pe[l], cmp_v_pe[l]]).reshape(2, 2, CMP_STRIDE * HEAD_DIM)
        w2 = jnp.stack([cmp_k_w2[l], cmp_v_w2[l]]).astype(BF16)
        cmp_kv = _compress(kvc, pe2, cmp_k_w1[l], cmp_v_w1[l], w2, B, S)

        attn_out = _nsa(qt, cmp_kv, ksel, vselt, kwin, vwint, brgt, ag, ovt, B, S)

        x2 = _outproj(x2, rnn_out, attn_out, w_out[l],
                      normf_w.reshape(1, D), final_norm=(l == depth - 1))
    return x2.reshape(B, S, D)
```
